```python
import math
import jax, jax.numpy as jnp
from jax import lax
import numpy as np


D_MODEL = 1024
BATCH = 8
SEQ = 2048
DEPTH = 2
DEC_BATCH = 128
DEC_SEQ = 4
PAST_LEN = 16384
PAGE_SIZE = 128

N_HEADS_A = 8
N_KV_A = 2
HEAD_DIM = 64
GQA_GROUP = N_HEADS_A // N_KV_A
WINDOW = 128
ATTN_WIDTH = N_HEADS_A * HEAD_DIM
KV_WIDTH = N_KV_A * HEAD_DIM
N_BUCKETS = 32
MAX_DISTANCE = 128
POOL_WINDOWS = (2, 4, 8, 16)
N_POOL_GROUPS = len(POOL_WINDOWS)
POOL_GROUP = 128
POOL_WIDTH = N_POOL_GROUPS * POOL_GROUP
POOL_BUF = max(POOL_WINDOWS) - 1
IN_EVEN = ATTN_WIDTH + 2 * KV_WIDTH + POOL_WIDTH
CHUNK = 128
SGU_WIDTH = 1024
SGU_GROUPS = 4
SGU_GROUP_W = SGU_WIDTH // SGU_GROUPS
D_FF = 2816
N_EVEN = (DEPTH + 1) // 2
N_ODD = DEPTH // 2
EPS = 1e-6
NEG = -1e30

kernel_name = 'hybrid_swa_pool_sgu_macaron_step'


def rmsnorm(x, g):
    xf = x.astype(jnp.float32)
    y = xf * lax.rsqrt(jnp.mean(xf * xf, axis=-1, keepdims=True) + EPS)
    return (y * g.astype(jnp.float32)).astype(x.dtype)


def macaron_half(x, g, wg, wu, wd):
    h = rmsnorm(x, g)
    return x + 0.5 * ((jax.nn.silu(h @ wg) * (h @ wu)) @ wd)


def t5_bucket(dist):
    n = jnp.maximum(dist, 0)
    max_exact = N_BUCKETS // 2
    nf = jnp.maximum(n, 1).astype(jnp.float32)
    large = max_exact + (jnp.log(nf / max_exact) / math.log(MAX_DISTANCE / max_exact)
                         * (N_BUCKETS - max_exact)).astype(jnp.int32)
    large = jnp.minimum(large, N_BUCKETS - 1)
    return jnp.where(n < max_exact, n, large)


def band_attention(q, k, v, dist, valid, sink, rel_bias):
    B, N, Q = q.shape[:3]
    S = k.shape[2]
    qg = q.reshape(B, N, Q, N_KV_A, GQA_GROUP, HEAD_DIM)
    logits = jnp.einsum('bnqkgd,bnskd->bnkgqs', qg, k).astype(jnp.float32) * (HEAD_DIM ** -0.5)
    bias = rel_bias[t5_bucket(dist)].astype(jnp.float32)
    bias = bias.transpose(2, 0, 1).reshape(N_KV_A, GQA_GROUP, Q, S)
    logits = jnp.where(valid[None, :, None, None], logits + bias, NEG)
    s = sink.astype(jnp.float32).reshape(N_KV_A, GQA_GROUP, 1, 1)
    m = jnp.maximum(jnp.max(logits, axis=-1, keepdims=True), s)
    p = jnp.exp(logits - m)
    p = p / (jnp.sum(p, axis=-1, keepdims=True) + jnp.exp(s - m))
    out = jnp.einsum('bnkgqs,bnskd->bnqkgd', p.astype(v.dtype), v)
    return out.reshape(B, N, Q, ATTN_WIDTH)


def attn_prompt(q, k, v, sink, rel_bias):
    B, S = q.shape[:2]
    nb = S // WINDOW
    qb = q.reshape(B, nb, WINDOW, N_HEADS_A, HEAD_DIM)

    def band(t):
        tp = jnp.concatenate([jnp.zeros_like(t[:, :WINDOW]), t], axis=1)
        tp = tp.reshape(B, nb + 1, WINDOW, N_KV_A, HEAD_DIM)
        return jnp.concatenate([tp[:, :-1], tp[:, 1:]], axis=2)

    kb, vb = band(k), band(v)
    qi = jnp.arange(WINDOW)[:, None] + WINDOW
    kj = jnp.arange(2 * WINDOW)[None, :]
    dist = qi - kj
    blk = jnp.arange(nb)[:, None, None]
    valid = (dist >= 0) & (dist < WINDOW) & (blk * WINDOW + kj - WINDOW >= 0)
    out = band_attention(qb, kb, vb, dist, valid, sink, rel_bias)
    return out.reshape(B, S, ATTN_WIDTH)


def attn_sample(q, k, v, buf_k, buf_v, sink, rel_bias):
    B, T = q.shape[:2]
    kk = jnp.concatenate([buf_k.astype(k.dtype), k], axis=1)
    vv = jnp.concatenate([buf_v.astype(v.dtype), v], axis=1)
    dist = (WINDOW + jnp.arange(T))[:, None] - jnp.arange(WINDOW + T)[None, :]
    valid = ((dist >= 0) & (dist < WINDOW))[None]
    out = band_attention(q[:, None], kk[:, None], vv[:, None], dist, valid, sink, rel_bias)
    return out.reshape(B, T, ATTN_WIDTH), kk[:, -WINDOW:], vv[:, -WINDOW:]


def pool_mix(u, ctx, pos0, w_pool, scale):
    B, T = u.shape[:2]
    z = jnp.concatenate([ctx.astype(u.dtype), u], axis=1)
    zf = z.astype(jnp.float32)
    cs = jnp.concatenate([jnp.zeros_like(zf[:, :1]), jnp.cumsum(zf, axis=1)], axis=1)
    end = cs[:, POOL_BUF + 1:]
    pos = pos0 + jnp.arange(T)
    means = []
    for g, w in enumerate(POOL_WINDOWS):
        sl = slice(g * POOL_GROUP, (g + 1) * POOL_GROUP)
        start = cs[:, POOL_BUF + 1 - w: POOL_BUF + 1 - w + T, sl]
        cnt = jnp.minimum(pos + 1, w).astype(jnp.float32)[None, :, None]
        means.append((end[..., sl] - start) / cnt)
    d = jnp.concatenate(means, axis=-1) - zf[:, POOL_BUF:]
    d = d.astype(u.dtype).reshape(B, T, N_POOL_GROUPS, POOL_GROUP)
    y = jnp.einsum('btgc,gcd->btgd', d, w_pool).reshape(B, T, POOL_WIDTH)
    return y * scale, z[:, -POOL_BUF:]


def even_mixer(h, prompt, buf_k, buf_v, buf_pool, w_in, w_out, sink, rel_bias, w_pool, pool_scale):
    B, T = h.shape[:2]
    proj = h @ w_in
    q, k, v, up = jnp.split(proj, [ATTN_WIDTH, ATTN_WIDTH + KV_WIDTH, ATTN_WIDTH + 2 * KV_WIDTH], axis=-1)
    q = q.reshape(B, T, N_HEADS_A, HEAD_DIM)
    k = k.reshape(B, T, N_KV_A, HEAD_DIM)
    v = v.reshape(B, T, N_KV_A, HEAD_DIM)
    if prompt:
        a = attn_prompt(q, k, v, sink, rel_bias)
        nk, nv = k[:, -WINDOW:], v[:, -WINDOW:]
        p, npool = pool_mix(up, jnp.zeros((B, POOL_BUF, POOL_WIDTH), up.dtype), 0, w_pool, pool_scale)
    else:
        a, nk, nv = attn_sample(q, k, v, buf_k, buf_v, sink, rel_bias)
        p, npool = pool_mix(up, buf_pool, PAST_LEN, w_pool, pool_scale)
    y = jnp.concatenate([a, p], axis=-1) @ w_out
    return y, nk, nv, npool


def odd_mixer(h, w_in, g_v, w_s, b_s, w_out):
    B, T = h.shape[:2]
    uv = jax.nn.gelu(h @ w_in)
    u, v = jnp.split(uv, 2, axis=-1)
    v = rmsnorm(v, g_v)
    L = min(T, CHUNK)
    nc = T // L
    vc = v.reshape(B, nc, L, SGU_GROUPS, SGU_GROUP_W)
    w = jnp.tril(w_s[:, :L, :L])
    mixed = jnp.einsum('gts,bnsgc->bntgc', w, vc) + b_s[:, :L].T[:, :, None]
    y = (u * mixed.reshape(B, T, SGU_WIDTH)) @ w_out
    return y, v


def setup_inputs(seed: int = 0) -> dict:
    key = jax.random.key(seed)
    ks = jax.random.split(key, 21)

    def nrm(k, shape, s):
        return jax.random.normal(k, shape, jnp.float32) * s

    return {
        'x_prompt': nrm(ks[0], (BATCH, SEQ, D_MODEL), 1.0),
        'x_sample': nrm(ks[1], (DEC_BATCH, DEC_SEQ, D_MODEL), 1.0),
        'state_win_k': nrm(ks[2], (N_EVEN, DEC_BATCH, WINDOW, N_KV_A, HEAD_DIM), 1.0),
        'state_win_v': nrm(ks[3], (N_EVEN, DEC_BATCH, WINDOW, N_KV_A, HEAD_DIM), 1.0),
        'state_pool': nrm(ks[4], (N_EVEN, DEC_BATCH, POOL_BUF, POOL_WIDTH), 1.0),
        'rel_bias': nrm(ks[5], (N_BUCKETS, N_HEADS_A), 0.5),
        'norm_gains': 1.0 + nrm(ks[6], (DEPTH, 3, D_MODEL), 0.05),
        'final_gain': 1.0 + nrm(ks[7], (D_MODEL,), 0.05),
        'ffn_gate': nrm(ks[8], (DEPTH, 2, D_MODEL, D_FF), D_MODEL ** -0.5),
        'ffn_up': nrm(ks[9], (DEPTH, 2, D_MODEL, D_FF), D_MODEL ** -0.5),
        'ffn_down': nrm(ks[10], (DEPTH, 2, D_FF, D_MODEL), D_FF ** -0.5),
        'w_in_even': nrm(ks[11], (N_EVEN, D_MODEL, IN_EVEN), D_MODEL ** -0.5),
        'w_out_even': nrm(ks[12], (N_EVEN, ATTN_WIDTH + POOL_WIDTH, D_MODEL), (ATTN_WIDTH + POOL_WIDTH) ** -0.5),
        'attn_sinks': nrm(ks[13], (N_EVEN, N_HEADS_A), 1.0),
        'w_pool': nrm(ks[14], (N_EVEN, N_POOL_GROUPS, POOL_GROUP, POOL_GROUP), POOL_GROUP ** -0.5),
        'pool_scale': 1.0 + nrm(ks[15], (N_EVEN, POOL_WIDTH), 0.1),
        'w_in_odd': nrm(ks[16], (N_ODD, D_MODEL, 2 * SGU_WIDTH), D_MODEL ** -0.5),
        'sgu_norm': 1.0 + nrm(ks[17], (N_ODD, SGU_WIDTH), 0.05),
        'w_spatial': nrm(ks[18], (N_ODD, SGU_GROUPS, CHUNK, CHUNK), CHUNK ** -0.5),
        'b_spatial': 1.0 + nrm(ks[19], (N_ODD, SGU_GROUPS, CHUNK), 0.1),
        'w_out_odd': nrm(ks[20], (N_ODD, SGU_WIDTH, D_MODEL), SGU_WIDTH ** -0.5),
    }


def reference(x_prompt, x_sample, state_win_k, state_win_v, state_pool, rel_bias, norm_gains, final_gain,
              ffn_gate, ffn_up, ffn_down, w_in_even, w_out_even, attn_sinks, w_pool, pool_scale,
              w_in_odd, sgu_norm, w_spatial, b_spatial, w_out_odd):
    xp, xs = x_prompt, x_sample
    kp_l, vp_l, pp_l, ks_l, vs_l, ps_l, sv_l = [], [], [], [], [], [], []
    for l in range(DEPTH):
        fa = (norm_gains[l, 0], ffn_gate[l, 0], ffn_up[l, 0], ffn_down[l, 0])
        fb = (norm_gains[l, 2], ffn_gate[l, 1], ffn_up[l, 1], ffn_down[l, 1])
        xp, xs = macaron_half(xp, *fa), macaron_half(xs, *fa)
        hp, hs = rmsnorm(xp, norm_gains[l, 1]), rmsnorm(xs, norm_gains[l, 1])
        if l % 2 == 0:
            e = l // 2
            wts = (w_in_even[e], w_out_even[e], attn_sinks[e], rel_bias, w_pool[e], pool_scale[e])
            yp, kp, vp, pp = even_mixer(hp, True, None, None, None, *wts)
            ys, k_s, v_s, p_s = even_mixer(hs, False, state_win_k[e], state_win_v[e], state_pool[e], *wts)
            kp_l.append(kp); vp_l.append(vp); pp_l.append(pp)
            ks_l.append(k_s); vs_l.append(v_s); ps_l.append(p_s)
        else:
            o = l // 2
            wts = (w_in_odd[o], sgu_norm[o], w_spatial[o], b_spatial[o], w_out_odd[o])
            yp, _ = odd_mixer(hp, *wts)
            ys, sv = odd_mixer(hs, *wts)
            sv_l.append(sv)
        xp, xs = xp + yp, xs + ys
        xp, xs = macaron_half(xp, *fb), macaron_half(xs, *fb)
    y_prompt = rmsnorm(xp, final_gain)
    y_sample = rmsnorm(xs, final_gain)
    new_win_k_prompt = jnp.stack(kp_l)
    new_win_v_prompt = jnp.stack(vp_l)
    new_pool_prompt = jnp.stack(pp_l)
    new_win_k_sample = jnp.stack(ks_l)
    new_win_v_sample = jnp.stack(vs_l)
    new_pool_sample = jnp.stack(ps_l)
    new_sgu_v_sample = jnp.stack(sv_l)
    return (y_prompt, y_sample, new_win_k_prompt, new_win_v_prompt, new_pool_prompt,
            new_win_k_sample, new_win_v_sample, new_pool_sample, new_sgu_v_sample)
```

```python
import functools
import math

import numpy as np
import jax
import jax.numpy as jnp
from jax import lax
from jax.experimental import pallas as pl
from jax.experimental.pallas import tpu as pltpu

F32 = jnp.float32
BF16 = jnp.bfloat16

D_MODEL = 1024
BATCH = 8
SEQ = 2048
DEPTH = 2
DEC_BATCH = 128
DEC_SEQ = 4
PAST_LEN = 16384
N_HEADS_A = 8
N_KV_A = 2
HEAD_DIM = 64
GQA_GROUP = N_HEADS_A // N_KV_A
WINDOW = 128
ATTN_WIDTH = N_HEADS_A * HEAD_DIM
KV_WIDTH = N_KV_A * HEAD_DIM
N_BUCKETS = 32
MAX_DISTANCE = 128
POOL_WINDOWS = (2, 4, 8, 16)
N_POOL_GROUPS = len(POOL_WINDOWS)
POOL_GROUP = 128
POOL_WIDTH = N_POOL_GROUPS * POOL_GROUP
POOL_BUF = max(POOL_WINDOWS) - 1
IN_EVEN = ATTN_WIDTH + 2 * KV_WIDTH + POOL_WIDTH
CHUNK = 128
SGU_WIDTH = 1024
SGU_GROUPS = 4
SGU_GROUP_W = SGU_WIDTH // SGU_GROUPS
D_FF = 2816
EPS = 1e-6
NEG = -1e30

LANES = 128
SUBLANES = 8
VMEM_LIMIT_BYTES = 56 * 1024 * 1024

PROMPT_TOK = BATCH * SEQ
SAMPLE_TOK = DEC_BATCH * DEC_SEQ
N_TOK = PROMPT_TOK + SAMPLE_TOK

TM = 512
PROMPT_BLOCKS = PROMPT_TOK // TM
TILES_PER_SEQ = SEQ // TM
QBLK_PER_TILE = TM // WINDOW
POOL_CTX = 32
SEQ_GROUP = 8
N_SEQ_GROUPS = DEC_BATCH // SEQ_GROUP
STATE_KEYS = SEQ_GROUP * WINDOW
NEW_KEYS = SEQ_GROUP * DEC_SEQ
SAMPLE_KEYS = 1152
SAMPLE_ROWS = DEC_SEQ * SEQ_GROUP


def _t5_bucket_np(dist):
    n = np.maximum(dist, 0)
    max_exact = N_BUCKETS // 2
    nf = np.maximum(n, 1).astype(np.float32)
    large = max_exact + (np.log(nf / np.float32(max_exact)) / np.float32(math.log(MAX_DISTANCE / max_exact))
                         * np.float32(N_BUCKETS - max_exact)).astype(np.int32)
    large = np.minimum(large, N_BUCKETS - 1)
    return np.where(n < max_exact, n, large).astype(np.int32)


def _prompt_bucket_table():
    qi = np.arange(WINDOW)[:, None] + WINDOW
    kj = np.arange(2 * WINDOW)[None, :]
    dist = qi - kj
    valid = (dist >= 0) & (dist < WINDOW)
    return np.where(valid, _t5_bucket_np(dist), -1).astype(np.int32)


def _sample_bucket_table():
    t = (np.arange(SAMPLE_ROWS) // SEQ_GROUP)[:, None]
    b = (np.arange(SAMPLE_ROWS) % SEQ_GROUP)[:, None]
    col = np.arange(SAMPLE_KEYS)[None, :]
    is_state = col < STATE_KEYS
    is_new = (col >= STATE_KEYS) & (col < STATE_KEYS + NEW_KEYS)
    kb = np.where(is_state, col // WINDOW, (col - STATE_KEYS) % SEQ_GROUP)
    kpos = np.where(is_state, col % WINDOW, WINDOW + (col - STATE_KEYS) // SEQ_GROUP)
    dist = (WINDOW + t) - kpos
    valid = (is_state | is_new) & (kb == b) & (dist >= 0) & (dist < WINDOW)
    return np.where(valid, _t5_bucket_np(dist), -1).astype(np.int32)


def _rms(x, g):
    y = x * lax.rsqrt(jnp.mean(x * x, axis=-1, keepdims=True) + EPS)
    return y * g


def _bias_from_buckets(bucket, rb_ref, h):
    t = jnp.zeros(bucket.shape, F32)
    for bkt in range(N_BUCKETS):
        t = jnp.where(bucket == bkt, rb_ref[bkt, h], t)
    return t


def _softmax_pv(s, valid, bias, sink, v):
    s = jnp.where(valid, s + bias, NEG)
    m = jnp.maximum(jnp.max(s, axis=-1, keepdims=True), sink)
    p = jnp.exp(s - m)
    denom = jnp.sum(p, axis=-1, keepdims=True) + jnp.exp(sink - m)
    o = jnp.dot(p.astype(BF16), v, preferred_element_type=F32)
    return o / denom


def _ffn_kernel(*refs, final_norm, aliased_dst):
    if aliased_dst:
        refs = refs[1:]
    if final_norm:
        x_ref, g_ref, wg_ref, wu_ref, wd_ref, fg_ref, o_ref = refs
    else:
        x_ref, g_ref, wg_ref, wu_ref, wd_ref, o_ref = refs
    x = x_ref[...]
    h = _rms(x, g_ref[...]).astype(BF16)
    gate = jnp.dot(h, wg_ref[...], preferred_element_type=F32)
    up = jnp.dot(h, wu_ref[...], preferred_element_type=F32)
    act = (jax.nn.silu(gate) * up).astype(BF16)
    y = jnp.dot(act, wd_ref[...], preferred_element_type=F32)
    out = x + 0.5 * y
    if final_norm:
        out = _rms(out, fg_ref[...])
    o_ref[...] = out


def _resident(shape):
    return pl.BlockSpec(shape, lambda *_: (0,) * len(shape), pipeline_mode=pl.Buffered(1))


def _ffn(x_in, in_off, nblk, gain, wg, wu, wd, *, out_rows, out_off, dst=None, final_gain=None):
    in_place = dst is x_in
    aliased_dst = dst is not None and not in_place
    args, in_specs = [], []
    if aliased_dst:
        args.append(dst)
        in_specs.append(pl.BlockSpec(memory_space=pl.ANY))
    args += [x_in, gain.reshape(1, D_MODEL), wg, wu, wd]
    in_specs += [
        pl.BlockSpec((TM, D_MODEL), lambda i: (i + in_off, 0)),
        _resident((1, D_MODEL)),
        _resident((D_MODEL, D_FF)),
        _resident((D_MODEL, D_FF)),
        _resident((D_FF, D_MODEL)),
    ]
    if final_gain is not None:
        args.append(final_gain.reshape(1, D_MODEL))
        in_specs.append(_resident((1, D_MODEL)))
    aliases = {}
    if aliased_dst:
        aliases = {0: 0}
    elif in_place:
        aliases = {0: 0}
    return pl.pallas_call(
        functools.partial(_ffn_kernel, final_norm=final_gain is not None, aliased_dst=aliased_dst),
        out_shape=jax.ShapeDtypeStruct((out_rows, D_MODEL), F32),
        grid=(nblk,),
        in_specs=in_specs,
        out_specs=pl.BlockSpec((TM, D_MODEL), lambda i: (i + out_off, 0)),
        input_output_aliases=aliases,
        compiler_params=pltpu.CompilerParams(
            dimension_semantics=("arbitrary",), vmem_limit_bytes=VMEM_LIMIT_BYTES),
        name="macaron_half",
    )(*args)


def _even_prompt_kernel(x_ref, g_ref, win_ref, wout_ref, wpool_ref, pscale_ref, bucket_ref, rb_ref, sink_ref,
                        o_ref, klast_ref, vlast_ref, plast_ref,
                        kbuf, kbuf_r, vbuf, vbuf_r, zb, s2b, s4b, s8b, bias_ref, cat_ref):
    b = pl.program_id(0)
    s = pl.program_id(1)
    last = pl.num_programs(1) - 1

    @pl.when((b == 0) & (s == 0))
    def _():
        bucket = bucket_ref[...]
        for h in range(N_HEADS_A):
            bias_ref[h] = _bias_from_buckets(bucket, rb_ref, h)

    @pl.when(s == 0)
    def _():
        for buf in (kbuf, kbuf_r, vbuf, vbuf_r):
            buf[0:WINDOW, :] = jnp.zeros((WINDOW, KV_WIDTH), BF16)
        zb[0:POOL_CTX, :] = jnp.zeros((POOL_CTX, POOL_WIDTH), F32)

    @pl.when(s > 0)
    def _():
        for buf in (kbuf, kbuf_r, vbuf, vbuf_r):
            buf[0:WINDOW, :] = buf[TM:TM + WINDOW, :]
        zb[0:POOL_CTX, :] = zb[TM:TM + POOL_CTX, :]

    x = x_ref[...]
    h = _rms(x, g_ref[...]).astype(BF16)
    proj = jnp.dot(h, win_ref[...], preferred_element_type=F32)
    q = proj[:, 0:ATTN_WIDTH] * (HEAD_DIM ** -0.5)
    k = proj[:, ATTN_WIDTH:ATTN_WIDTH + KV_WIDTH]
    v = proj[:, ATTN_WIDTH + KV_WIDTH:ATTN_WIDTH + 2 * KV_WIDTH]
    up = proj[:, ATTN_WIDTH + 2 * KV_WIDTH:IN_EVEN]

    qlane = lax.broadcasted_iota(jnp.int32, (TM, ATTN_WIDTH), 1) % LANES
    q_half = (jnp.where(qlane < HEAD_DIM, q, 0.0).astype(BF16),
              jnp.where(qlane >= HEAD_DIM, q, 0.0).astype(BF16))
    kbuf[WINDOW:, :] = k.astype(BF16)
    vbuf[WINDOW:, :] = v.astype(BF16)
    kbuf_r[WINDOW:, :] = pltpu.roll(k, HEAD_DIM, 1).astype(BF16)
    vbuf_r[WINDOW:, :] = pltpu.roll(v, HEAD_DIM, 1).astype(BF16)

    bucket = bucket_ref[...]
    col = lax.broadcasted_iota(jnp.int32, (WINDOW, 2 * WINDOW), 1)
    olane = lax.broadcasted_iota(jnp.int32, (WINDOW, LANES), 1)
    for i in range(QBLK_PER_TILE):
        valid = bucket >= 0
        if i == 0:
            valid = valid & (col >= jnp.where(s > 0, 0, WINDOW))
        rows = slice(i * WINDOW, (i + 1) * WINDOW)
        keys = slice(i * WINDOW, (i + 2) * WINDOW)
        for j in range(N_HEADS_A // 2):
            halves = []
            for half in range(2):
                hd = 2 * j + half
                kvh = hd // GQA_GROUP
                kk = (kbuf if kvh == half else kbuf_r)[keys, :]
                vv = (vbuf if kvh == half else vbuf_r)[keys, :]
                qh = q_half[half][rows, j * LANES:(j + 1) * LANES]
                sc = lax.dot_general(qh, kk, (((1,), (1,)), ((), ())), preferred_element_type=F32)
                halves.append(_softmax_pv(sc, valid, bias_ref[hd], sink_ref[hd], vv))
            cat_ref[rows, j * LANES:(j + 1) * LANES] = jnp.where(
                olane < HEAD_DIM, halves[0], halves[1]).astype(BF16)

    zb[POOL_CTX:, :] = up
    n = TM + POOL_CTX
    s2b[8:, :] = zb[8:n, :] + zb[7:n - 1, :]
    s4b[16:, :] = s2b[16:n, POOL_GROUP:] + s2b[14:n - 2, POOL_GROUP:]
    s8b[24:, :] = s4b[24:n, POOL_GROUP:] + s4b[20:n - 4, POOL_GROUP:]
    s16 = s8b[32:n, POOL_GROUP:] + s8b[24:n - 8, POOL_GROUP:]
    sums = (s2b[POOL_CTX:, 0:POOL_GROUP], s4b[POOL_CTX:, 0:POOL_GROUP], s8b[POOL_CTX:, 0:POOL_GROUP], s16)
    pos1 = lax.broadcasted_iota(jnp.int32, (TM, POOL_GROUP), 0) + s * TM + 1
    for g, w in enumerate(POOL_WINDOWS):
        lanes = slice(g * POOL_GROUP, (g + 1) * POOL_GROUP)
        cnt = jnp.minimum(pos1, w).astype(F32)
        d = (sums[g] / cnt - up[:, lanes]).astype(BF16)
        y = jnp.dot(d, wpool_ref[g], preferred_element_type=F32) * pscale_ref[:, lanes]
        cat_ref[:, ATTN_WIDTH + g * POOL_GROUP:ATTN_WIDTH + (g + 1) * POOL_GROUP] = y.astype(BF16)

    o_ref[...] = x + jnp.dot(cat_ref[...], wout_ref[...], preferred_element_type=F32)

    @pl.when(s == last)
    def _():
        klast_ref[0] = k[TM - WINDOW:, :]
        vlast_ref[0] = v[TM - WINDOW:, :]
        plast_ref[0] = up[TM - 2 * SUBLANES:, :]


def _even_prompt(x, gain, w_in, w_out, w_pool, pool_scale, rel_bias, sinks):
    bucket = jnp.asarray(_prompt_bucket_table())
    smem = pl.BlockSpec(memory_space=pltpu.SMEM)
    return pl.pallas_call(
        _even_prompt_kernel,
        out_shape=(
            jax.ShapeDtypeStruct((N_TOK, D_MODEL), F32),
            jax.ShapeDtypeStruct((BATCH, WINDOW, KV_WIDTH), F32),
            jax.ShapeDtypeStruct((BATCH, WINDOW, KV_WIDTH), F32),
            jax.ShapeDtypeStruct((BATCH, 2 * SUBLANES, POOL_WIDTH), F32),
        ),
        grid=(BATCH, TILES_PER_SEQ),
        in_specs=[
            pl.BlockSpec((TM, D_MODEL), lambda b, s: (b * TILES_PER_SEQ + s, 0)),
            _resident((1, D_MODEL)),
            _resident((D_MODEL, IN_EVEN)),
            _resident((ATTN_WIDTH + POOL_WIDTH, D_MODEL)),
            _resident((N_POOL_GROUPS, POOL_GROUP, POOL_GROUP)),
            _resident((1, POOL_WIDTH)),
            _resident((WINDOW, 2 * WINDOW)),
            smem,
            smem,
        ],
        out_specs=(
            pl.BlockSpec((TM, D_MODEL), lambda b, s: (b * TILES_PER_SEQ + s, 0)),
            pl.BlockSpec((1, WINDOW, KV_WIDTH), lambda b, s: (b, 0, 0)),
            pl.BlockSpec((1, WINDOW, KV_WIDTH), lambda b, s: (b, 0, 0)),
            pl.BlockSpec((1, 2 * SUBLANES, POOL_WIDTH), lambda b, s: (b, 0, 0)),
        ),
        scratch_shapes=[
            pltpu.VMEM((WINDOW + TM, KV_WIDTH), BF16),
            pltpu.VMEM((WINDOW + TM, KV_WIDTH), BF16),
            pltpu.VMEM((WINDOW + TM, KV_WIDTH), BF16),
            pltpu.VMEM((WINDOW + TM, KV_WIDTH), BF16),
            pltpu.VMEM((POOL_CTX + TM, POOL_WIDTH), F32),
            pltpu.VMEM((POOL_CTX + TM, POOL_WIDTH), F32),
            pltpu.VMEM((POOL_CTX + TM, POOL_WIDTH - POOL_GROUP), F32),
            pltpu.VMEM((POOL_CTX + TM, POOL_WIDTH - 2 * POOL_GROUP), F32),
            pltpu.VMEM((N_HEADS_A, WINDOW, 2 * WINDOW), F32),
            pltpu.VMEM((TM, ATTN_WIDTH + POOL_WIDTH), BF16),
        ],
        input_output_aliases={0: 0},
        compiler_params=pltpu.CompilerParams(
            dimension_semantics=("arbitrary", "arbitrary"), vmem_limit_bytes=VMEM_LIMIT_BYTES),
        name="even_mixer_prompt",
    )(x, gain.reshape(1, D_MODEL), w_in, w_out, w_pool, pool_scale.reshape(1, POOL_WIDTH), bucket,
      rel_bias, sinks)


def _even_sample_kernel(x_ref, g_ref, win_ref, wout_ref, wpool_ref, pscale_ref, bucket_ref, rb_ref, sink_ref,
                        sk_ref, sv_ref, spool_ref,
                        o_ref, knew_ref, vnew_ref, upnew_ref,
                        qsel, kall, vall, bias_ref, attn_f32, cat_ref):
    g = pl.program_id(0)
    last = pl.num_programs(0) - 1
    olane = lax.broadcasted_iota(jnp.int32, (SAMPLE_ROWS, LANES), 1)

    @pl.when(g == 0)
    def _():
        bucket = bucket_ref[...]
        for h in range(N_HEADS_A):
            bias_ref[h] = _bias_from_buckets(bucket, rb_ref, h)
        x = x_ref[...]
        h = _rms(x, g_ref[...]).astype(BF16)
        proj = jnp.dot(h, win_ref[...], preferred_element_type=F32)
        q = proj[:, 0:ATTN_WIDTH] * (HEAD_DIM ** -0.5)
        knew_ref[...] = proj[:, ATTN_WIDTH:ATTN_WIDTH + KV_WIDTH]
        vnew_ref[...] = proj[:, ATTN_WIDTH + KV_WIDTH:ATTN_WIDTH + 2 * KV_WIDTH]
        upnew_ref[...] = proj[:, ATTN_WIDTH + 2 * KV_WIDTH:IN_EVEN]
        qlane = lax.broadcasted_iota(jnp.int32, (SAMPLE_TOK, LANES), 1)
        for hd in range(N_HEADS_A):
            j, half, kvh = hd // 2, hd % 2, hd // GQA_GROUP
            blk = q[:, j * LANES:(j + 1) * LANES]
            if half != kvh:
                blk = pltpu.roll(blk, HEAD_DIM, 1)
            keep = (qlane < HEAD_DIM) if kvh == 0 else (qlane >= HEAD_DIM)
            qsel[hd] = jnp.where(keep, blk, 0.0)
        kall[STATE_KEYS:, :] = jnp.zeros((SAMPLE_KEYS - STATE_KEYS, KV_WIDTH), BF16)
        vall[STATE_KEYS:, :] = jnp.zeros((SAMPLE_KEYS - STATE_KEYS, KV_WIDTH), BF16)

    kall[0:STATE_KEYS, :] = sk_ref[...].reshape(STATE_KEYS, KV_WIDTH).astype(BF16)
    vall[0:STATE_KEYS, :] = sv_ref[...].reshape(STATE_KEYS, KV_WIDTH).astype(BF16)
    row0 = pl.multiple_of(g * SEQ_GROUP, SEQ_GROUP)
    knew = [knew_ref[pl.ds(t * DEC_BATCH + row0, SEQ_GROUP), :] for t in range(DEC_SEQ)]
    vnew = [vnew_ref[pl.ds(t * DEC_BATCH + row0, SEQ_GROUP), :] for t in range(DEC_SEQ)]
    kall[STATE_KEYS:STATE_KEYS + NEW_KEYS, :] = jnp.concatenate(knew, axis=0).astype(BF16)
    vall[STATE_KEYS:STATE_KEYS + NEW_KEYS, :] = jnp.concatenate(vnew, axis=0).astype(BF16)

    qg = jnp.concatenate(
        [qsel[hd, pl.ds(t * DEC_BATCH + row0, SEQ_GROUP), :] for hd in range(N_HEADS_A) for t in range(DEC_SEQ)],
        axis=0).astype(BF16)
    sc = lax.dot_general(qg, kall[...], (((1,), (1,)), ((), ())), preferred_element_type=F32)
    valid = jnp.concatenate([bucket_ref[...]] * N_HEADS_A, axis=0) >= 0
    bias = bias_ref[...].reshape(N_HEADS_A * SAMPLE_ROWS, SAMPLE_KEYS)
    sink = jnp.concatenate([jnp.full((SAMPLE_ROWS, 1), sink_ref[hd], F32) for hd in range(N_HEADS_A)], axis=0)
    o_all = _softmax_pv(sc, valid, bias, sink, vall[...])
    for j in range(N_HEADS_A // 2):
        halves = []
        for half in range(2):
            hd = 2 * j + half
            o = o_all[hd * SAMPLE_ROWS:(hd + 1) * SAMPLE_ROWS, :]
            if hd // GQA_GROUP != half:
                o = pltpu.roll(o, HEAD_DIM, 1)
            halves.append(o)
        blk = jnp.where(olane < HEAD_DIM, halves[0], halves[1])
        for t in range(DEC_SEQ):
            attn_f32[pl.ds(t * DEC_BATCH + row0, SEQ_GROUP), j * LANES:(j + 1) * LANES] = (
                blk[t * SEQ_GROUP:(t + 1) * SEQ_GROUP, :])

    @pl.when(g == last)
    def _():
        cat_ref[:, 0:ATTN_WIDTH] = attn_f32[...].astype(BF16)
        for t in range(DEC_SEQ):
            rows = slice(t * DEC_BATCH, (t + 1) * DEC_BATCH)
            for gi, w in enumerate(POOL_WINDOWS):
                lanes = slice(gi * POOL_GROUP, (gi + 1) * POOL_GROUP)
                acc = upnew_ref[rows, lanes]
                for back in range(1, w):
                    tt = t - back
                    if tt >= 0:
                        acc = acc + upnew_ref[tt * DEC_BATCH:(tt + 1) * DEC_BATCH, lanes]
                    else:
                        acc = acc + spool_ref[POOL_BUF + tt, :, lanes]
                cnt = float(min(PAST_LEN + t + 1, w))
                d = (acc / cnt - upnew_ref[rows, lanes]).astype(BF16)
                y = jnp.dot(d, wpool_ref[gi], preferred_element_type=F32) * pscale_ref[:, lanes]
                cat_ref[rows, ATTN_WIDTH + gi * POOL_GROUP:ATTN_WIDTH + (gi + 1) * POOL_GROUP] = y.astype(BF16)
        o_ref[...] = x_ref[...] + jnp.dot(cat_ref[...], wout_ref[...], preferred_element_type=F32)


def _even_sample(x, gain, w_in, w_out, w_pool, pool_scale, rel_bias, sinks, state_k, state_v, state_pool_t):
    bucket = jnp.asarray(_sample_bucket_table())
    smem = pl.BlockSpec(memory_space=pltpu.SMEM)
    sample_block = PROMPT_TOK // SAMPLE_TOK
    return pl.pallas_call(
        _even_sample_kernel,
        out_shape=(
            jax.ShapeDtypeStruct((N_TOK, D_MODEL), F32),
            jax.ShapeDtypeStruct((SAMPLE_TOK, KV_WIDTH), F32),
            jax.ShapeDtypeStruct((SAMPLE_TOK, KV_WIDTH), F32),
            jax.ShapeDtypeStruct((SAMPLE_TOK, POOL_WIDTH), F32),
        ),
        grid=(N_SEQ_GROUPS,),
        in_specs=[
            pl.BlockSpec((SAMPLE_TOK, D_MODEL), lambda g: (sample_block, 0)),
            _resident((1, D_MODEL)),
            _resident((D_MODEL, IN_EVEN)),
            _resident((ATTN_WIDTH + POOL_WIDTH, D_MODEL)),
            _resident((N_POOL_GROUPS, POOL_GROUP, POOL_GROUP)),
            _resident((1, POOL_WIDTH)),
            _resident((SAMPLE_ROWS, SAMPLE_KEYS)),
            smem,
            smem,
            pl.BlockSpec((SEQ_GROUP, WINDOW, KV_WIDTH), lambda g: (g, 0, 0)),
            pl.BlockSpec((SEQ_GROUP, WINDOW, KV_WIDTH), lambda g: (g, 0, 0)),
            _resident((POOL_BUF, DEC_BATCH, POOL_WIDTH)),
        ],
        out_specs=(
            pl.BlockSpec((SAMPLE_TOK, D_MODEL), lambda g: (sample_block, 0)),
            pl.BlockSpec((SAMPLE_TOK, KV_WIDTH), lambda g: (0, 0)),
            pl.BlockSpec((SAMPLE_TOK, KV_WIDTH), lambda g: (0, 0)),
            pl.BlockSpec((SAMPLE_TOK, POOL_WIDTH), lambda g: (0, 0)),
        ),
        scratch_shapes=[
            pltpu.VMEM((N_HEADS_A, SAMPLE_TOK, LANES), F32),
            pltpu.VMEM((SAMPLE_KEYS, KV_WIDTH), BF16),
            pltpu.VMEM((SAMPLE_KEYS, KV_WIDTH), BF16),
            pltpu.VMEM((N_HEADS_A, SAMPLE_ROWS, SAMPLE_KEYS), F32),
            pltpu.VMEM((SAMPLE_TOK, ATTN_WIDTH), F32),
            pltpu.VMEM((SAMPLE_TOK, ATTN_WIDTH + POOL_WIDTH), BF16),
        ],
        input_output_aliases={0: 0},
        compiler_params=pltpu.CompilerParams(
            dimension_semantics=("arbitrary",), vmem_limit_bytes=VMEM_LIMIT_BYTES),
        name="even_mixer_sample",
    )(x, gain.reshape(1, D_MODEL), w_in, w_out, w_pool, pool_scale.reshape(1, POOL_WIDTH), bucket,
      rel_bias, sinks, state_k, state_v, state_pool_t)


def _sgu_front(x, g_ref, win_ref, gv_ref):
    h = _rms(x, g_ref[...]).astype(BF16)
    uv = jax.nn.gelu(jnp.dot(h, win_ref[...], preferred_element_type=F32))
    return uv[:, 0:SGU_WIDTH], _rms(uv[:, SGU_WIDTH:], gv_ref[...])


def _odd_prompt_kernel(x_ref, g_ref, win_ref, gv_ref, ws_ref, bs_ref, wout_ref, o_ref, gated_ref):
    x = x_ref[...]
    u, v = _sgu_front(x, g_ref, win_ref, gv_ref)
    vb = v.astype(BF16)
    r = lax.broadcasted_iota(jnp.int32, (CHUNK, CHUNK), 0)
    c = lax.broadcasted_iota(jnp.int32, (CHUNK, CHUNK), 1)
    for gi in range(SGU_GROUPS):
        w = jnp.where(r >= c, ws_ref[gi], 0.0).astype(BF16)
        bias = jnp.concatenate([bs_ref[gi]] * (SGU_GROUP_W // LANES), axis=1)
        lanes = slice(gi * SGU_GROUP_W, (gi + 1) * SGU_GROUP_W)
        for ci in range(TM // CHUNK):
            rows = slice(ci * CHUNK, (ci + 1) * CHUNK)
            mixed = jnp.dot(w, vb[rows, lanes], preferred_element_type=F32) + bias
            gated_ref[rows, lanes] = (u[rows, lanes] * mixed).astype(BF16)
    o_ref[...] = x + jnp.dot(gated_ref[...], wout_ref[...], preferred_element_type=F32)


def _odd_prompt(x, gain, w_in, g_v, w_s, b_s, w_out):
    bias = jnp.broadcast_to(b_s[:, :, None], (SGU_GROUPS, CHUNK, LANES))
    return pl.pallas_call(
        _odd_prompt_kernel,
        out_shape=jax.ShapeDtypeStruct((N_TOK, D_MODEL), F32),
        grid=(PROMPT_BLOCKS,),
        in_specs=[
            pl.BlockSpec((TM, D_MODEL), lambda i: (i, 0)),
            _resident((1, D_MODEL)),
            _resident((D_MODEL, 2 * SGU_WIDTH)),
            _resident((1, SGU_WIDTH)),
            _resident((SGU_GROUPS, CHUNK, CHUNK)),
            _resident((SGU_GROUPS, CHUNK, LANES)),
            _resident((SGU_WIDTH, D_MODEL)),
        ],
        out_specs=pl.BlockSpec((TM, D_MODEL), lambda i: (i, 0)),
        scratch_shapes=[pltpu.VMEM((TM, SGU_WIDTH), BF16)],
        input_output_aliases={0: 0},
        compiler_params=pltpu.CompilerParams(
            dimension_semantics=("arbitrary",), vmem_limit_bytes=VMEM_LIMIT_BYTES),
        name="odd_mixer_prompt",
    )(x, gain.reshape(1, D_MODEL), w_in, g_v.reshape(1, SGU_WIDTH), w_s, bias, w_out)


def _odd_sample_kernel(x_ref, g_ref, win_ref, gv_ref, coef_ref, bias_ref, wout_ref, o_ref, v_ref):
    x = x_ref[...]
    u, v = _sgu_front(x, g_ref, win_ref, gv_ref)
    v_ref[...] = v
    gated = []
    for t in range(DEC_SEQ):
        mixed = bias_ref[t:t + 1, :]
        for s in range(t + 1):
            mixed = mixed + coef_ref[t, s:s + 1, :] * v[s * DEC_BATCH:(s + 1) * DEC_BATCH, :]
        gated.append((u[t * DEC_BATCH:(t + 1) * DEC_BATCH, :] * mixed).astype(BF16))
    o_ref[...] = x + jnp.dot(jnp.concatenate(gated, axis=0), wout_ref[...], preferred_element_type=F32)


def _odd_sample(x, gain, w_in, g_v, w_s, b_s, w_out):
    coef = jnp.repeat(jnp.transpose(w_s[:, :DEC_SEQ, :DEC_SEQ], (1, 2, 0)), SGU_GROUP_W, axis=-1)
    bias = jnp.repeat(b_s[:, :DEC_SEQ].T, SGU_GROUP_W, axis=-1)
    sample_block = PROMPT_TOK // SAMPLE_TOK
    return pl.pallas_call(
        _odd_sample_kernel,
        out_shape=(
            jax.ShapeDtypeStruct((N_TOK, D_MODEL), F32),
            jax.ShapeDtypeStruct((SAMPLE_TOK, SGU_WIDTH), F32),
        ),
        grid=(1,),
        in_specs=[
            pl.BlockSpec((SAMPLE_TOK, D_MODEL), lambda i: (sample_block, 0)),
            _resident((1, D_MODEL)),
            _resident((D_MODEL, 2 * SGU_WIDTH)),
            _resident((1, SGU_WIDTH)),
            _resident((DEC_SEQ, DEC_SEQ, SGU_WIDTH)),
            _resident((DEC_SEQ, SGU_WIDTH)),
            _resident((SGU_WIDTH, D_MODEL)),
        ],
        out_specs=(
            pl.BlockSpec((SAMPLE_TOK, D_MODEL), lambda i: (sample_block, 0)),
            pl.BlockSpec((SAMPLE_TOK, SGU_WIDTH), lambda i: (0, 0)),
        ),
        input_output_aliases={0: 0},
        compiler_params=pltpu.CompilerParams(
            dimension_semantics=("arbitrary",), vmem_limit_bytes=VMEM_LIMIT_BYTES),
        name="odd_mixer_sample",
    )(x, gain.reshape(1, D_MODEL), w_in, g_v.reshape(1, SGU_WIDTH), coef, bias, w_out)


def _to_step_major(a):
    return jnp.swapaxes(a, 0, 1).reshape(SAMPLE_TOK, a.shape[-1])


def _to_batch_major(a):
    return jnp.swapaxes(a.reshape(DEC_SEQ, DEC_BATCH, a.shape[-1]), 0, 1)


def kernel(x_prompt, x_sample, state_win_k, state_win_v, state_pool, rel_bias, norm_gains, final_gain,
           ffn_gate, ffn_up, ffn_down, w_in_even, w_out_even, attn_sinks, w_pool, pool_scale,
           w_in_odd, sgu_norm, w_spatial, b_spatial, w_out_odd):
    xp = x_prompt.reshape(PROMPT_TOK, D_MODEL)
    xs = _to_step_major(x_sample)
    sample_block = PROMPT_TOK // TM

    kp_l, vp_l, pp_l, ks_l, vs_l, ps_l, sv_l = [], [], [], [], [], [], []
    x = None
    y_prompt = y_sample = None
    for l in range(DEPTH):
        fa = (norm_gains[l, 0], ffn_gate[l, 0].astype(BF16), ffn_up[l, 0].astype(BF16), ffn_down[l, 0].astype(BF16))
        fb = (norm_gains[l, 2], ffn_gate[l, 1].astype(BF16), ffn_up[l, 1].astype(BF16), ffn_down[l, 1].astype(BF16))
        if l == 0:
            x = _ffn(xp, 0, PROMPT_BLOCKS, *fa, out_rows=N_TOK, out_off=0)
            x = _ffn(xs, 0, 1, *fa, out_rows=N_TOK, out_off=sample_block, dst=x)
        else:
            x = _ffn(x, 0, PROMPT_BLOCKS + 1, *fa, out_rows=N_TOK, out_off=0, dst=x)
        g1 = norm_gains[l, 1]
        if l % 2 == 0:
            e = l // 2
            w_in, w_out = w_in_even[e].astype(BF16), w_out_even[e].astype(BF16)
            wp = w_pool[e].astype(BF16)
            x, kp, vp, pp = _even_prompt(x, g1, w_in, w_out, wp, pool_scale[e], rel_bias, attn_sinks[e])
            sk = state_win_k[e].reshape(DEC_BATCH, WINDOW, KV_WIDTH)
            sv = state_win_v[e].reshape(DEC_BATCH, WINDOW, KV_WIDTH)
            x, kn, vn, un = _even_sample(x, g1, w_in, w_out, wp, pool_scale[e], rel_bias, attn_sinks[e],
                                         sk, sv, jnp.swapaxes(state_pool[e], 0, 1))
            kp_l.append(kp.reshape(BATCH, WINDOW, N_KV_A, HEAD_DIM))
            vp_l.append(vp.reshape(BATCH, WINDOW, N_KV_A, HEAD_DIM))
            pp_l.append(pp[:, 2 * SUBLANES - POOL_BUF:, :])
            ks_l.append(jnp.concatenate(
                [state_win_k[e][:, DEC_SEQ:], _to_batch_major(kn).reshape(DEC_BATCH, DEC_SEQ, N_KV_A, HEAD_DIM)],
                axis=1))
            vs_l.append(jnp.concatenate(
                [state_win_v[e][:, DEC_SEQ:], _to_batch_major(vn).reshape(DEC_BATCH, DEC_SEQ, N_KV_A, HEAD_DIM)],
                axis=1))
            ps_l.append(jnp.concatenate([state_pool[e][:, DEC_SEQ:], _to_batch_major(un)], axis=1))
        else:
            o = l // 2
            w_in, w_out = w_in_odd[o].astype(BF16), w_out_odd[o].astype(BF16)
            x = _odd_prompt(x, g1, w_in, sgu_norm[o], w_spatial[o], b_spatial[o], w_out)
            x, sv_new = _odd_sample(x, g1, w_in, sgu_norm[o], w_spatial[o], b_spatial[o], w_out)
            sv_l.append(_to_batch_major(sv_new))
        if l < DEPTH - 1:
            x = _ffn(x, 0, PROMPT_BLOCKS + 1, *fb, out_rows=N_TOK, out_off=0, dst=x)
        else:
            y_prompt = _ffn(x, 0, PROMPT_BLOCKS, *fb, out_rows=PROMPT_TOK, out_off=0, final_gain=final_gain)
            y_sample = _ffn(x, sample_block, 1, *fb, out_rows=SAMPLE_TOK, out_off=0, final_gain=final_gain)

    return (y_prompt.reshape(BATCH, SEQ, D_MODEL), _to_batch_major(y_sample),
            jnp.stack(kp_l), jnp.stack(vp_l), jnp.stack(pp_l),
            jnp.stack(ks_l), jnp.stack(vs_l), jnp.stack(ps_l), jnp.stack(sv_l))
```

```python
import functools
import math

import numpy as np
import jax
import jax.numpy as jnp
from jax import lax
from jax.experimental import pallas as pl
from jax.experimental.pallas import tpu as pltpu

F32 = jnp.float32
BF16 = jnp.bfloat16

D_MODEL = 1024
BATCH = 8
SEQ = 2048
DEPTH = 2
DEC_BATCH = 128
DEC_SEQ = 4
PAST_LEN = 16384
N_HEADS_A = 8
N_KV_A = 2
HEAD_DIM = 64
GQA_GROUP = N_HEADS_A // N_KV_A
WINDOW = 128
ATTN_WIDTH = N_HEADS_A * HEAD_DIM
KV_WIDTH = N_KV_A * HEAD_DIM
N_BUCKETS = 32
MAX_DISTANCE = 128
POOL_WINDOWS = (2, 4, 8, 16)
N_POOL_GROUPS = len(POOL_WINDOWS)
POOL_GROUP = 128
POOL_WIDTH = N_POOL_GROUPS * POOL_GROUP
POOL_BUF = max(POOL_WINDOWS) - 1
IN_EVEN = ATTN_WIDTH + 2 * KV_WIDTH + POOL_WIDTH
CHUNK = 128
SGU_WIDTH = 1024
SGU_GROUPS = 4
SGU_GROUP_W = SGU_WIDTH // SGU_GROUPS
D_FF = 2816
EPS = 1e-6
NEG = -1e30

LANES = 128
SUBLANES = 8
VMEM_LIMIT_BYTES = 56 * 1024 * 1024

PROMPT_TOK = BATCH * SEQ
SAMPLE_TOK = DEC_BATCH * DEC_SEQ
N_TOK = PROMPT_TOK + SAMPLE_TOK

TM = SAMPLE_TOK
assert PROMPT_TOK % TM == 0
PROMPT_BLOCKS = PROMPT_TOK // TM
TILES_PER_SEQ = SEQ // TM
QBLK_PER_TILE = TM // WINDOW
FF_CHUNK = 256
N_FF_CHUNKS = D_FF // FF_CHUNK
POOL_CTX = 32
SEQ_GROUP = 8
N_SEQ_GROUPS = DEC_BATCH // SEQ_GROUP
STATE_KEYS = SEQ_GROUP * WINDOW
NEW_KEYS = SEQ_GROUP * DEC_SEQ
SAMPLE_KEYS = 1152
SAMPLE_ROWS = DEC_SEQ * SEQ_GROUP


def _t5_bucket_np(dist):
    n = np.maximum(dist, 0)
    max_exact = N_BUCKETS // 2
    nf = np.maximum(n, 1).astype(np.float32)
    large = max_exact + (np.log(nf / np.float32(max_exact)) / np.float32(math.log(MAX_DISTANCE / max_exact))
                         * np.float32(N_BUCKETS - max_exact)).astype(np.int32)
    large = np.minimum(large, N_BUCKETS - 1)
    return np.where(n < max_exact, n, large).astype(np.int32)


def _prompt_bucket_table():
    qi = np.arange(WINDOW)[:, None] + WINDOW
    kj = np.arange(2 * WINDOW)[None, :]
    dist = qi - kj
    valid = (dist >= 0) & (dist < WINDOW)
    return np.where(valid, _t5_bucket_np(dist), -1).astype(np.int32)


def _sample_bucket_table():
    t = (np.arange(SAMPLE_ROWS) // SEQ_GROUP)[:, None]
    b = (np.arange(SAMPLE_ROWS) % SEQ_GROUP)[:, None]
    col = np.arange(SAMPLE_KEYS)[None, :]
    is_state = col < STATE_KEYS
    is_new = (col >= STATE_KEYS) & (col < STATE_KEYS + NEW_KEYS)
    kb = np.where(is_state, col // WINDOW, (col - STATE_KEYS) % SEQ_GROUP)
    kpos = np.where(is_state, col % WINDOW, WINDOW + (col - STATE_KEYS) // SEQ_GROUP)
    dist = (WINDOW + t) - kpos
    valid = (is_state | is_new) & (kb == b) & (dist >= 0) & (dist < WINDOW)
    return np.where(valid, _t5_bucket_np(dist), -1).astype(np.int32)


def _rms(x, g):
    y = x * lax.rsqrt(jnp.mean(x * x, axis=-1, keepdims=True) + EPS)
    return y * g


def _bias_from_buckets(bucket, rb_ref, h):
    t = jnp.zeros(bucket.shape, F32)
    for bkt in range(N_BUCKETS):
        t = jnp.where(bucket == bkt, rb_ref[bkt, h], t)
    return t


def _softmax_pv(s, valid, bias, sink, v):
    s = jnp.where(valid, s + bias, NEG)
    m = jnp.maximum(jnp.max(s, axis=-1, keepdims=True), sink)
    p = jnp.exp(s - m)
    denom = jnp.sum(p, axis=-1, keepdims=True) + jnp.exp(sink - m)
    o = jnp.dot(p.astype(BF16), v, preferred_element_type=F32)
    return o / denom


def _ffn_kernel(*refs, layer, half, tail_in, final_split):
    refs = list(refs)
    x_ref = refs.pop(0)
    xt_ref = refs.pop(0) if tail_in else None
    g_ref, wg_hbm, wu_hbm, wd_hbm = refs[:4]
    del refs[:4]
    fg_ref = refs.pop(0) if final_split else None
    o_ref = refs.pop(0)
    ot_ref = refs.pop(0) if final_split else None
    wg, wu, wd, stage_in, stage_dn, sem = refs
    i = pl.program_id(0)
    x = x_ref[...]
    if tail_in:
        x = jnp.where(i < PROMPT_BLOCKS, x, xt_ref[...])
    h = _rms(x, g_ref[...]).astype(BF16)

    def finish(y):
        out = x + 0.5 * y
        if not final_split:
            o_ref[...] = out
            return
        out = _rms(out, fg_ref[...])

        @pl.when(i < PROMPT_BLOCKS)
        def _():
            o_ref[...] = out

        @pl.when(i == PROMPT_BLOCKS)
        def _():
            ot_ref[...] = out

    def chunk_copies(c, slot):
        cols = pl.ds(c * FF_CHUNK, FF_CHUNK)
        return (
            pltpu.make_async_copy(wg_hbm.at[layer, half, :, cols], stage_in.at[0, slot], sem.at[0, slot]),
            pltpu.make_async_copy(wu_hbm.at[layer, half, :, cols], stage_in.at[1, slot], sem.at[1, slot]),
            pltpu.make_async_copy(wd_hbm.at[layer, half, cols, :], stage_dn.at[slot], sem.at[2, slot]),
        )

    @pl.when(i == 0)
    def _():
        for c in range(min(2, N_FF_CHUNKS)):
            for cp in chunk_copies(c, c % 2):
                cp.start()
        acc = jnp.zeros((TM, D_MODEL), F32)
        for c in range(N_FF_CHUNKS):
            slot = c % 2
            cols = slice(c * FF_CHUNK, (c + 1) * FF_CHUNK)
            for cp in chunk_copies(c, slot):
                cp.wait()
            wg[:, cols] = stage_in[0, slot].astype(BF16)
            wu[:, cols] = stage_in[1, slot].astype(BF16)
            wd[cols, :] = stage_dn[slot].astype(BF16)
            if c + 2 < N_FF_CHUNKS:
                for cp in chunk_copies(c + 2, slot):
                    cp.start()
            gate = jnp.dot(h, wg[:, cols], preferred_element_type=F32)
            up = jnp.dot(h, wu[:, cols], preferred_element_type=F32)
            act = (jax.nn.silu(gate) * up).astype(BF16)
            acc = acc + jnp.dot(act, wd[cols, :], preferred_element_type=F32)
        finish(acc)

    @pl.when(i > 0)
    def _():
        gate = jnp.dot(h, wg[...], preferred_element_type=F32)
        up = jnp.dot(h, wu[...], preferred_element_type=F32)
        act = (jax.nn.silu(gate) * up).astype(BF16)
        finish(jnp.dot(act, wd[...], preferred_element_type=F32))


def _resident(shape):
    return pl.BlockSpec(shape, lambda *_: (0,) * len(shape), pipeline_mode=pl.Buffered(1))


def _ffn(x_in, gain, wg, wu, wd, layer, half, *, tail_in=None, final_gain=None):
    hbm = pl.BlockSpec(memory_space=pl.ANY)
    last_prompt = PROMPT_BLOCKS - 1
    row_block = pl.BlockSpec((TM, D_MODEL), lambda i: (i, 0))
    prompt_block = pl.BlockSpec((TM, D_MODEL), lambda i: (jnp.minimum(i, last_prompt), 0))
    sample_block = pl.BlockSpec((TM, D_MODEL), lambda i: (0, 0))
    if tail_in is None:
        args, in_specs = [x_in], [row_block]
    else:
        args, in_specs = [x_in, tail_in], [prompt_block, sample_block]
    args += [gain.reshape(1, D_MODEL), wg, wu, wd]
    in_specs += [_resident((1, D_MODEL)), hbm, hbm, hbm]
    if final_gain is None:
        out_shape = jax.ShapeDtypeStruct((N_TOK, D_MODEL), F32)
        out_specs = row_block
    else:
        args.append(final_gain.reshape(1, D_MODEL))
        in_specs.append(_resident((1, D_MODEL)))
        out_shape = (jax.ShapeDtypeStruct((PROMPT_TOK, D_MODEL), F32),
                     jax.ShapeDtypeStruct((SAMPLE_TOK, D_MODEL), F32))
        out_specs = (prompt_block, sample_block)
    aliases = {0: 0} if (tail_in is None and final_gain is None) else {}
    return pl.pallas_call(
        functools.partial(_ffn_kernel, layer=layer, half=half, tail_in=tail_in is not None,
                          final_split=final_gain is not None),
        out_shape=out_shape,
        grid=(PROMPT_BLOCKS + 1,),
        in_specs=in_specs,
        out_specs=out_specs,
        scratch_shapes=[
            pltpu.VMEM((D_MODEL, D_FF), BF16),
            pltpu.VMEM((D_MODEL, D_FF), BF16),
            pltpu.VMEM((D_FF, D_MODEL), BF16),
            pltpu.VMEM((2, 2, D_MODEL, FF_CHUNK), F32),
            pltpu.VMEM((2, FF_CHUNK, D_MODEL), F32),
            pltpu.SemaphoreType.DMA((3, 2)),
        ],
        input_output_aliases=aliases,
        compiler_params=pltpu.CompilerParams(
            dimension_semantics=("arbitrary",), vmem_limit_bytes=VMEM_LIMIT_BYTES),
        name="macaron_half",
    )(*args)


def _even_prompt_kernel(x_ref, g_ref, win_ref, wout_ref, wpool_ref, pscale_ref, bucket_ref, rb_ref, sink_ref,
                        o_ref, klast_ref, vlast_ref, plast_ref,
                        kbuf, kbuf_r, vbuf, vbuf_r, zb, s2b, s4b, s8b, bias_ref, cat_ref):
    b = pl.program_id(0)
    s = pl.program_id(1)
    last = pl.num_programs(1) - 1

    @pl.when((b == 0) & (s == 0))
    def _():
        bucket = bucket_ref[...]
        for h in range(N_HEADS_A):
            bias_ref[h] = _bias_from_buckets(bucket, rb_ref, h)

    @pl.when(s == 0)
    def _():
        for buf in (kbuf, kbuf_r, vbuf, vbuf_r):
            buf[0:WINDOW, :] = jnp.zeros((WINDOW, KV_WIDTH), BF16)
        zb[0:POOL_CTX, :] = jnp.zeros((POOL_CTX, POOL_WIDTH), F32)

    @pl.when(s > 0)
    def _():
        for buf in (kbuf, kbuf_r, vbuf, vbuf_r):
            buf[0:WINDOW, :] = buf[TM:TM + WINDOW, :]
        zb[0:POOL_CTX, :] = zb[TM:TM + POOL_CTX, :]

    x = x_ref[...]
    h = _rms(x, g_ref[...]).astype(BF16)
    proj = jnp.dot(h, win_ref[...], preferred_element_type=F32)
    q = proj[:, 0:ATTN_WIDTH] * (HEAD_DIM ** -0.5)
    k = proj[:, ATTN_WIDTH:ATTN_WIDTH + KV_WIDTH]
    v = proj[:, ATTN_WIDTH + KV_WIDTH:ATTN_WIDTH + 2 * KV_WIDTH]
    up = proj[:, ATTN_WIDTH + 2 * KV_WIDTH:IN_EVEN]

    qlane = lax.broadcasted_iota(jnp.int32, (TM, ATTN_WIDTH), 1) % LANES
    q_half = (jnp.where(qlane < HEAD_DIM, q, 0.0).astype(BF16),
              jnp.where(qlane >= HEAD_DIM, q, 0.0).astype(BF16))
    kbuf[WINDOW:, :] = k.astype(BF16)
    vbuf[WINDOW:, :] = v.astype(BF16)
    kbuf_r[WINDOW:, :] = pltpu.roll(k, HEAD_DIM, 1).astype(BF16)
    vbuf_r[WINDOW:, :] = pltpu.roll(v, HEAD_DIM, 1).astype(BF16)

    bucket = bucket_ref[...]
    col = lax.broadcasted_iota(jnp.int32, (WINDOW, 2 * WINDOW), 1)
    olane = lax.broadcasted_iota(jnp.int32, (WINDOW, LANES), 1)
    for i in range(QBLK_PER_TILE):
        valid = bucket >= 0
        if i == 0:
            valid = valid & (col >= jnp.where(s > 0, 0, WINDOW))
        rows = slice(i * WINDOW, (i + 1) * WINDOW)
        keys = slice(i * WINDOW, (i + 2) * WINDOW)
        for j in range(N_HEADS_A // 2):
            halves = []
            for half in range(2):
                hd = 2 * j + half
                kvh = hd // GQA_GROUP
                kk = (kbuf if kvh == half else kbuf_r)[keys, :]
                vv = (vbuf if kvh == half else vbuf_r)[keys, :]
                qh = q_half[half][rows, j * LANES:(j + 1) * LANES]
                sc = lax.dot_general(qh, kk, (((1,), (1,)), ((), ())), preferred_element_type=F32)
                halves.append(_softmax_pv(sc, valid, bias_ref[hd], sink_ref[hd], vv))
            cat_ref[rows, j * LANES:(j + 1) * LANES] = jnp.where(
                olane < HEAD_DIM, halves[0], halves[1]).astype(BF16)

    zb[POOL_CTX:, :] = up
    n = TM + POOL_CTX
    s2b[8:, :] = zb[8:n, :] + zb[7:n - 1, :]
    s4b[16:, :] = s2b[16:n, POOL_GROUP:] + s2b[14:n - 2, POOL_GROUP:]
    s8b[24:, :] = s4b[24:n, POOL_GROUP:] + s4b[20:n - 4, POOL_GROUP:]
    s16 = s8b[32:n, POOL_GROUP:] + s8b[24:n - 8, POOL_GROUP:]
    sums = (s2b[POOL_CTX:, 0:POOL_GROUP], s4b[POOL_CTX:, 0:POOL_GROUP], s8b[POOL_CTX:, 0:POOL_GROUP], s16)
    pos1 = lax.broadcasted_iota(jnp.int32, (TM, POOL_GROUP), 0) + s * TM + 1
    for g, w in enumerate(POOL_WINDOWS):
        lanes = slice(g * POOL_GROUP, (g + 1) * POOL_GROUP)
        cnt = jnp.minimum(pos1, w).astype(F32)
        d = (sums[g] / cnt - up[:, lanes]).astype(BF16)
        y = jnp.dot(d, wpool_ref[g], preferred_element_type=F32) * pscale_ref[:, lanes]
        cat_ref[:, ATTN_WIDTH + g * POOL_GROUP:ATTN_WIDTH + (g + 1) * POOL_GROUP] = y.astype(BF16)

    o_ref[...] = x + jnp.dot(cat_ref[...], wout_ref[...], preferred_element_type=F32)

    @pl.when(s == last)
    def _():
        klast_ref[0] = k[TM - WINDOW:, :]
        vlast_ref[0] = v[TM - WINDOW:, :]
        plast_ref[0] = up[TM - 2 * SUBLANES:, :]


def _even_prompt(x, gain, w_in, w_out, w_pool, pool_scale, rel_bias, sinks):
    bucket = jnp.asarray(_prompt_bucket_table())
    smem = pl.BlockSpec(memory_space=pltpu.SMEM)
    return pl.pallas_call(
        _even_prompt_kernel,
        out_shape=(
            jax.ShapeDtypeStruct((N_TOK, D_MODEL), F32),
            jax.ShapeDtypeStruct((BATCH, WINDOW, KV_WIDTH), F32),
            jax.ShapeDtypeStruct((BATCH, WINDOW, KV_WIDTH), F32),
            jax.ShapeDtypeStruct((BATCH, 2 * SUBLANES, POOL_WIDTH), F32),
        ),
        grid=(BATCH, TILES_PER_SEQ),
        in_specs=[
            pl.BlockSpec((TM, D_MODEL), lambda b, s: (b * TILES_PER_SEQ + s, 0)),
            _resident((1, D_MODEL)),
            _resident((D_MODEL, IN_EVEN)),
            _resident((ATTN_WIDTH + POOL_WIDTH, D_MODEL)),
            _resident((N_POOL_GROUPS, POOL_GROUP, POOL_GROUP)),
            _resident((1, POOL_WIDTH)),
            _resident((WINDOW, 2 * WINDOW)),
            smem,
            smem,
        ],
        out_specs=(
            pl.BlockSpec((TM, D_MODEL), lambda b, s: (b * TILES_PER_SEQ + s, 0)),
            pl.BlockSpec((1, WINDOW, KV_WIDTH), lambda b, s: (b, 0, 0)),
            pl.BlockSpec((1, WINDOW, KV_WIDTH), lambda b, s: (b, 0, 0)),
            pl.BlockSpec((1, 2 * SUBLANES, POOL_WIDTH), lambda b, s: (b, 0, 0)),
        ),
        scratch_shapes=[
            pltpu.VMEM((WINDOW + TM, KV_WIDTH), BF16),
            pltpu.VMEM((WINDOW + TM, KV_WIDTH), BF16),
            pltpu.VMEM((WINDOW + TM, KV_WIDTH), BF16),
            pltpu.VMEM((WINDOW + TM, KV_WIDTH), BF16),
            pltpu.VMEM((POOL_CTX + TM, POOL_WIDTH), F32),
            pltpu.VMEM((POOL_CTX + TM, POOL_WIDTH), F32),
            pltpu.VMEM((POOL_CTX + TM, POOL_WIDTH - POOL_GROUP), F32),
            pltpu.VMEM((POOL_CTX + TM, POOL_WIDTH - 2 * POOL_GROUP), F32),
            pltpu.VMEM((N_HEADS_A, WINDOW, 2 * WINDOW), F32),
            pltpu.VMEM((TM, ATTN_WIDTH + POOL_WIDTH), BF16),
        ],
        input_output_aliases={0: 0},
        compiler_params=pltpu.CompilerParams(
            dimension_semantics=("arbitrary", "arbitrary"), vmem_limit_bytes=VMEM_LIMIT_BYTES),
        name="even_mixer_prompt",
    )(x, gain.reshape(1, D_MODEL), w_in, w_out, w_pool, pool_scale.reshape(1, POOL_WIDTH), bucket,
      rel_bias, sinks)


def _even_sample_kernel(x_ref, g_ref, win_ref, wout_ref, wpool_ref, pscale_ref, bucket_ref, rb_ref, sink_ref,
                        sk_ref, sv_ref, spool_ref,
                        o_ref, knew_ref, vnew_ref, upnew_ref,
                        qsel, kall, vall, bias_ref, attn_f32, cat_ref):
    g = pl.program_id(0)
    last = pl.num_programs(0) - 1
    olane = lax.broadcasted_iota(jnp.int32, (SAMPLE_ROWS, LANES), 1)

    @pl.when(g == 0)
    def _():
        bucket = bucket_ref[...]
        for h in range(N_HEADS_A):
            bias_ref[h] = _bias_from_buckets(bucket, rb_ref, h)
        x = x_ref[...]
        h = _rms(x, g_ref[...]).astype(BF16)
        proj = jnp.dot(h, win_ref[...], preferred_element_type=F32)
        q = proj[:, 0:ATTN_WIDTH] * (HEAD_DIM ** -0.5)
        knew_ref[...] = proj[:, ATTN_WIDTH:ATTN_WIDTH + KV_WIDTH]
        vnew_ref[...] = proj[:, ATTN_WIDTH + KV_WIDTH:ATTN_WIDTH + 2 * KV_WIDTH]
        upnew_ref[...] = proj[:, ATTN_WIDTH + 2 * KV_WIDTH:IN_EVEN]
        qlane = lax.broadcasted_iota(jnp.int32, (SAMPLE_TOK, LANES), 1)
        for hd in range(N_HEADS_A):
            j, half, kvh = hd // 2, hd % 2, hd // GQA_GROUP
            blk = q[:, j * LANES:(j + 1) * LANES]
            if half != kvh:
                blk = pltpu.roll(blk, HEAD_DIM, 1)
            keep = (qlane < HEAD_DIM) if kvh == 0 else (qlane >= HEAD_DIM)
            qsel[hd] = jnp.where(keep, blk, 0.0)
        kall[STATE_KEYS:, :] = jnp.zeros((SAMPLE_KEYS - STATE_KEYS, KV_WIDTH), BF16)
        vall[STATE_KEYS:, :] = jnp.zeros((SAMPLE_KEYS - STATE_KEYS, KV_WIDTH), BF16)

    kall[0:STATE_KEYS, :] = sk_ref[...].reshape(STATE_KEYS, KV_WIDTH).astype(BF16)
    vall[0:STATE_KEYS, :] = sv_ref[...].reshape(STATE_KEYS, KV_WIDTH).astype(BF16)
    row0 = pl.multiple_of(g * SEQ_GROUP, SEQ_GROUP)
    knew = [knew_ref[pl.ds(t * DEC_BATCH + row0, SEQ_GROUP), :] for t in range(DEC_SEQ)]
    vnew = [vnew_ref[pl.ds(t * DEC_BATCH + row0, SEQ_GROUP), :] for t in range(DEC_SEQ)]
    kall[STATE_KEYS:STATE_KEYS + NEW_KEYS, :] = jnp.concatenate(knew, axis=0).astype(BF16)
    vall[STATE_KEYS:STATE_KEYS + NEW_KEYS, :] = jnp.concatenate(vnew, axis=0).astype(BF16)

    qg = jnp.concatenate(
        [qsel[hd, pl.ds(t * DEC_BATCH + row0, SEQ_GROUP), :] for hd in range(N_HEADS_A) for t in range(DEC_SEQ)],
        axis=0).astype(BF16)
    sc = lax.dot_general(qg, kall[...], (((1,), (1,)), ((), ())), preferred_element_type=F32)
    valid = jnp.concatenate([bucket_ref[...]] * N_HEADS_A, axis=0) >= 0
    bias = bias_ref[...].reshape(N_HEADS_A * SAMPLE_ROWS, SAMPLE_KEYS)
    sink = jnp.concatenate([jnp.full((SAMPLE_ROWS, 1), sink_ref[hd], F32) for hd in range(N_HEADS_A)], axis=0)
    o_all = _softmax_pv(sc, valid, bias, sink, vall[...])
    for j in range(N_HEADS_A // 2):
        halves = []
        for half in range(2):
            hd = 2 * j + half
            o = o_all[hd * SAMPLE_ROWS:(hd + 1) * SAMPLE_ROWS, :]
            if hd // GQA_GROUP != half:
                o = pltpu.roll(o, HEAD_DIM, 1)
            halves.append(o)
        blk = jnp.where(olane < HEAD_DIM, halves[0], halves[1])
        for t in range(DEC_SEQ):
            attn_f32[pl.ds(t * DEC_BATCH + row0, SEQ_GROUP), j * LANES:(j + 1) * LANES] = (
                blk[t * SEQ_GROUP:(t + 1) * SEQ_GROUP, :])

    @pl.when(g == last)
    def _():
        cat_ref[:, 0:ATTN_WIDTH] = attn_f32[...].astype(BF16)
        for t in range(DEC_SEQ):
            rows = slice(t * DEC_BATCH, (t + 1) * DEC_BATCH)
            for gi, w in enumerate(POOL_WINDOWS):
                lanes = slice(gi * POOL_GROUP, (gi + 1) * POOL_GROUP)
                acc = upnew_ref[rows, lanes]
                for back in range(1, w):
                    tt = t - back
                    if tt >= 0:
                        acc = acc + upnew_ref[tt * DEC_BATCH:(tt + 1) * DEC_BATCH, lanes]
                    else:
                        acc = acc + spool_ref[POOL_BUF + tt, :, lanes]
                cnt = float(min(PAST_LEN + t + 1, w))
                d = (acc / cnt - upnew_ref[rows, lanes]).astype(BF16)
                y = jnp.dot(d, wpool_ref[gi], preferred_element_type=F32) * pscale_ref[:, lanes]
                cat_ref[rows, ATTN_WIDTH + gi * POOL_GROUP:ATTN_WIDTH + (gi + 1) * POOL_GROUP] = y.astype(BF16)
        o_ref[...] = x_ref[...] + jnp.dot(cat_ref[...], wout_ref[...], preferred_element_type=F32)


def _even_sample(x, gain, w_in, w_out, w_pool, pool_scale, rel_bias, sinks, state_k, state_v, state_pool_t):
    bucket = jnp.asarray(_sample_bucket_table())
    smem = pl.BlockSpec(memory_space=pltpu.SMEM)
    sample_block = PROMPT_TOK // SAMPLE_TOK
    return pl.pallas_call(
        _even_sample_kernel,
        out_shape=(
            jax.ShapeDtypeStruct((N_TOK, D_MODEL), F32),
            jax.ShapeDtypeStruct((SAMPLE_TOK, KV_WIDTH), F32),
            jax.ShapeDtypeStruct((SAMPLE_TOK, KV_WIDTH), F32),
            jax.ShapeDtypeStruct((SAMPLE_TOK, POOL_WIDTH), F32),
        ),
        grid=(N_SEQ_GROUPS,),
        in_specs=[
            pl.BlockSpec((SAMPLE_TOK, D_MODEL), lambda g: (sample_block, 0)),
            _resident((1, D_MODEL)),
            _resident((D_MODEL, IN_EVEN)),
            _resident((ATTN_WIDTH + POOL_WIDTH, D_MODEL)),
            _resident((N_POOL_GROUPS, POOL_GROUP, POOL_GROUP)),
            _resident((1, POOL_WIDTH)),
            _resident((SAMPLE_ROWS, SAMPLE_KEYS)),
            smem,
            smem,
            pl.BlockSpec((SEQ_GROUP, WINDOW, KV_WIDTH), lambda g: (g, 0, 0)),
            pl.BlockSpec((SEQ_GROUP, WINDOW, KV_WIDTH), lambda g: (g, 0, 0)),
            _resident((POOL_BUF, DEC_BATCH, POOL_WIDTH)),
        ],
        out_specs=(
            pl.BlockSpec((SAMPLE_TOK, D_MODEL), lambda g: (sample_block, 0)),
            pl.BlockSpec((SAMPLE_TOK, KV_WIDTH), lambda g: (0, 0)),
            pl.BlockSpec((SAMPLE_TOK, KV_WIDTH), lambda g: (0, 0)),
            pl.BlockSpec((SAMPLE_TOK, POOL_WIDTH), lambda g: (0, 0)),
        ),
        scratch_shapes=[
            pltpu.VMEM((N_HEADS_A, SAMPLE_TOK, LANES), F32),
            pltpu.VMEM((SAMPLE_KEYS, KV_WIDTH), BF16),
            pltpu.VMEM((SAMPLE_KEYS, KV_WIDTH), BF16),
            pltpu.VMEM((N_HEADS_A, SAMPLE_ROWS, SAMPLE_KEYS), F32),
            pltpu.VMEM((SAMPLE_TOK, ATTN_WIDTH), F32),
            pltpu.VMEM((SAMPLE_TOK, ATTN_WIDTH + POOL_WIDTH), BF16),
        ],
        input_output_aliases={0: 0},
        compiler_params=pltpu.CompilerParams(
            dimension_semantics=("arbitrary",), vmem_limit_bytes=VMEM_LIMIT_BYTES),
        name="even_mixer_sample",
    )(x, gain.reshape(1, D_MODEL), w_in, w_out, w_pool, pool_scale.reshape(1, POOL_WIDTH), bucket,
      rel_bias, sinks, state_k, state_v, state_pool_t)


def _sgu_front(x, g_ref, win_ref, gv_ref):
    h = _rms(x, g_ref[...]).astype(BF16)
    uv = jax.nn.gelu(jnp.dot(h, win_ref[...], preferred_element_type=F32))
    return uv[:, 0:SGU_WIDTH], _rms(uv[:, SGU_WIDTH:], gv_ref[...])


def _odd_prompt_kernel(x_ref, g_ref, win_ref, gv_ref, ws_ref, bs_ref, wout_ref, o_ref, gated_ref):
    x = x_ref[...]
    u, v = _sgu_front(x, g_ref, win_ref, gv_ref)
    vb = v.astype(BF16)
    r = lax.broadcasted_iota(jnp.int32, (CHUNK, CHUNK), 0)
    c = lax.broadcasted_iota(jnp.int32, (CHUNK, CHUNK), 1)
    for gi in range(SGU_GROUPS):
        w = jnp.where(r >= c, ws_ref[gi], 0.0).astype(BF16)
        bias = jnp.concatenate([bs_ref[gi]] * (SGU_GROUP_W // LANES), axis=1)
        lanes = slice(gi * SGU_GROUP_W, (gi + 1) * SGU_GROUP_W)
        for ci in range(TM // CHUNK):
            rows = slice(ci * CHUNK, (ci + 1) * CHUNK)
            mixed = jnp.dot(w, vb[rows, lanes], preferred_element_type=F32) + bias
            gated_ref[rows, lanes] = (u[rows, lanes] * mixed).astype(BF16)
    o_ref[...] = x + jnp.dot(gated_ref[...], wout_ref[...], preferred_element_type=F32)


def _odd_prompt(x, gain, w_in, g_v, w_s, b_s, w_out):
    bias = jnp.broadcast_to(b_s[:, :, None], (SGU_GROUPS, CHUNK, LANES))
    return pl.pallas_call(
        _odd_prompt_kernel,
        out_shape=jax.ShapeDtypeStruct((N_TOK, D_MODEL), F32),
        grid=(PROMPT_BLOCKS,),
        in_specs=[
            pl.BlockSpec((TM, D_MODEL), lambda i: (i, 0)),
            _resident((1, D_MODEL)),
            _resident((D_MODEL, 2 * SGU_WIDTH)),
            _resident((1, SGU_WIDTH)),
            _resident((SGU_GROUPS, CHUNK, CHUNK)),
            _resident((SGU_GROUPS, CHUNK, LANES)),
            _resident((SGU_WIDTH, D_MODEL)),
        ],
        out_specs=pl.BlockSpec((TM, D_MODEL), lambda i: (i, 0)),
        scratch_shapes=[pltpu.VMEM((TM, SGU_WIDTH), BF16)],
        input_output_aliases={0: 0},
        compiler_params=pltpu.CompilerParams(
            dimension_semantics=("arbitrary",), vmem_limit_bytes=VMEM_LIMIT_BYTES),
        name="odd_mixer_prompt",
    )(x, gain.reshape(1, D_MODEL), w_in, g_v.reshape(1, SGU_WIDTH), w_s, bias, w_out)


def _odd_sample_kernel(x_ref, g_ref, win_ref, gv_ref, coef_ref, bias_ref, wout_ref, o_ref, v_ref):
    x = x_ref[...]
    u, v = _sgu_front(x, g_ref, win_ref, gv_ref)
    v_ref[...] = v
    gated = []
    for t in range(DEC_SEQ):
        mixed = bias_ref[t:t + 1, :]
        for s in range(t + 1):
            mixed = mixed + coef_ref[t, s:s + 1, :] * v[s * DEC_BATCH:(s + 1) * DEC_BATCH, :]
        gated.append((u[t * DEC_BATCH:(t + 1) * DEC_BATCH, :] * mixed).astype(BF16))
    o_ref[...] = x + jnp.dot(jnp.concatenate(gated, axis=0), wout_ref[...], preferred_element_type=F32)


def _odd_sample(x, gain, w_in, g_v, w_s, b_s, w_out):
    coef = jnp.repeat(jnp.transpose(w_s[:, :DEC_SEQ, :DEC_SEQ], (1, 2, 0)), SGU_GROUP_W, axis=-1)
    bias = jnp.repeat(b_s[:, :DEC_SEQ].T, SGU_GROUP_W, axis=-1)
    sample_block = PROMPT_TOK // SAMPLE_TOK
    return pl.pallas_call(
        _odd_sample_kernel,
        out_shape=(
            jax.ShapeDtypeStruct((N_TOK, D_MODEL), F32),
            jax.ShapeDtypeStruct((SAMPLE_TOK, SGU_WIDTH), F32),
        ),
        grid=(1,),
        in_specs=[
            pl.BlockSpec((SAMPLE_TOK, D_MODEL), lambda i: (sample_block, 0)),
            _resident((1, D_MODEL)),
            _resident((D_MODEL, 2 * SGU_WIDTH)),
            _resident((1, SGU_WIDTH)),
            _resident((DEC_SEQ, DEC_SEQ, SGU_WIDTH)),
            _resident((DEC_SEQ, SGU_WIDTH)),
            _resident((SGU_WIDTH, D_MODEL)),
        ],
        out_specs=(
            pl.BlockSpec((SAMPLE_TOK, D_MODEL), lambda i: (sample_block, 0)),
            pl.BlockSpec((SAMPLE_TOK, SGU_WIDTH), lambda i: (0, 0)),
        ),
        input_output_aliases={0: 0},
        compiler_params=pltpu.CompilerParams(
            dimension_semantics=("arbitrary",), vmem_limit_bytes=VMEM_LIMIT_BYTES),
        name="odd_mixer_sample",
    )(x, gain.reshape(1, D_MODEL), w_in, g_v.reshape(1, SGU_WIDTH), coef, bias, w_out)


def _to_step_major(a):
    return jnp.swapaxes(a, 0, 1).reshape(SAMPLE_TOK, a.shape[-1])


def _to_batch_major(a):
    return jnp.swapaxes(a.reshape(DEC_SEQ, DEC_BATCH, a.shape[-1]), 0, 1)


def kernel(x_prompt, x_sample, state_win_k, state_win_v, state_pool, rel_bias, norm_gains, final_gain,
           ffn_gate, ffn_up, ffn_down, w_in_even, w_out_even, attn_sinks, w_pool, pool_scale,
           w_in_odd, sgu_norm, w_spatial, b_spatial, w_out_odd):
    xp = x_prompt.reshape(PROMPT_TOK, D_MODEL)
    xs = _to_step_major(x_sample)

    kp_l, vp_l, pp_l, ks_l, vs_l, ps_l, sv_l = [], [], [], [], [], [], []
    x = None
    y_prompt = y_sample = None
    for l in range(DEPTH):
        fa = (norm_gains[l, 0], ffn_gate, ffn_up, ffn_down, l, 0)
        fb = (norm_gains[l, 2], ffn_gate, ffn_up, ffn_down, l, 1)
        if l == 0:
            x = _ffn(xp, *fa, tail_in=xs)
        else:
            x = _ffn(x, *fa)
        g1 = norm_gains[l, 1]
        if l % 2 == 0:
            e = l // 2
            w_in, w_out = w_in_even[e].astype(BF16), w_out_even[e].astype(BF16)
            wp = w_pool[e].astype(BF16)
            x, kp, vp, pp = _even_prompt(x, g1, w_in, w_out, wp, pool_scale[e], rel_bias, attn_sinks[e])
            sk = state_win_k[e].reshape(DEC_BATCH, WINDOW, KV_WIDTH)
            sv = state_win_v[e].reshape(DEC_BATCH, WINDOW, KV_WIDTH)
            x, kn, vn, un = _even_sample(x, g1, w_in, w_out, wp, pool_scale[e], rel_bias, attn_sinks[e],
                                         sk, sv, jnp.swapaxes(state_pool[e], 0, 1))
            kp_l.append(kp.reshape(BATCH, WINDOW, N_KV_A, HEAD_DIM))
            vp_l.append(vp.reshape(BATCH, WINDOW, N_KV_A, HEAD_DIM))
            pp_l.append(pp[:, 2 * SUBLANES - POOL_BUF:, :])
            ks_l.append(jnp.concatenate(
                [state_win_k[e][:, DEC_SEQ:], _to_batch_major(kn).reshape(DEC_BATCH, DEC_SEQ, N_KV_A, HEAD_DIM)],
                axis=1))
            vs_l.append(jnp.concatenate(
                [state_win_v[e][:, DEC_SEQ:], _to_batch_major(vn).reshape(DEC_BATCH, DEC_SEQ, N_KV_A, HEAD_DIM)],
                axis=1))
            ps_l.append(jnp.concatenate([state_pool[e][:, DEC_SEQ:], _to_batch_major(un)], axis=1))
        else:
            o = l // 2
            w_in, w_out = w_in_odd[o].astype(BF16), w_out_odd[o].astype(BF16)
            x = _odd_prompt(x, g1, w_in, sgu_norm[o], w_spatial[o], b_spatial[o], w_out)
            x, sv_new = _odd_sample(x, g1, w_in, sgu_norm[o], w_spatial[o], b_spatial[o], w_out)
            sv_l.append(_to_batch_major(sv_new))
        if l < DEPTH - 1:
            x = _ffn(x, *fb)
        else:
            y_prompt, y_sample = _ffn(x, *fb, final_gain=final_gain)

    return (y_prompt.reshape(BATCH, SEQ, D_MODEL), _to_batch_major(y_sample),
            jnp.stack(kp_l), jnp.stack(vp_l), jnp.stack(pp_l),
            jnp.stack(ks_l), jnp.stack(vs_l), jnp.stack(ps_l), jnp.stack(sv_l))
```

```python
import functools
import math

import numpy as np
import jax
import jax.numpy as jnp
from jax import lax
from jax.experimental import pallas as pl
from jax.experimental.pallas import tpu as pltpu

F32 = jnp.float32
BF16 = jnp.bfloat16

D_MODEL = 1024
BATCH = 8
SEQ = 2048
DEPTH = 2
DEC_BATCH = 128
DEC_SEQ = 4
PAST_LEN = 16384
N_HEADS_A = 8
N_KV_A = 2
HEAD_DIM = 64
GQA_GROUP = N_HEADS_A // N_KV_A
WINDOW = 128
ATTN_WIDTH = N_HEADS_A * HEAD_DIM
KV_WIDTH = N_KV_A * HEAD_DIM
N_BUCKETS = 32
MAX_DISTANCE = 128
POOL_WINDOWS = (2, 4, 8, 16)
N_POOL_GROUPS = len(POOL_WINDOWS)
POOL_GROUP = 128
POOL_WIDTH = N_POOL_GROUPS * POOL_GROUP
POOL_BUF = max(POOL_WINDOWS) - 1
IN_EVEN = ATTN_WIDTH + 2 * KV_WIDTH + POOL_WIDTH
CHUNK = 128
SGU_WIDTH = 1024
SGU_GROUPS = 4
SGU_GROUP_W = SGU_WIDTH // SGU_GROUPS
D_FF = 2816
EPS = 1e-6
NEG = -1e30

LANES = 128
SUBLANES = 8
VMEM_LIMIT_BYTES = 56 * 1024 * 1024

PROMPT_TOK = BATCH * SEQ
SAMPLE_TOK = DEC_BATCH * DEC_SEQ
N_TOK = PROMPT_TOK + SAMPLE_TOK

TM = SAMPLE_TOK
assert PROMPT_TOK % TM == 0
PROMPT_BLOCKS = PROMPT_TOK // TM
TILES_PER_SEQ = SEQ // TM
QBLK_PER_TILE = TM // WINDOW
ODD_TM = 1024
assert PROMPT_TOK % ODD_TM == 0 and ODD_TM % CHUNK == 0
FF_CHUNK = 256
N_FF_CHUNKS = D_FF // FF_CHUNK
POOL_HEAD = 2 * SUBLANES
assert all(w & (w - 1) == 0 and w <= POOL_HEAD for w in POOL_WINDOWS)
POOL_CTX = 32
SEQ_GROUP = 8
N_SEQ_GROUPS = DEC_BATCH // SEQ_GROUP
STATE_KEYS = SEQ_GROUP * WINDOW
NEW_KEYS = SEQ_GROUP * DEC_SEQ
SAMPLE_KEYS = 1152
SAMPLE_ROWS = DEC_SEQ * SEQ_GROUP


def _t5_bucket_np(dist):
    n = np.maximum(dist, 0)
    max_exact = N_BUCKETS // 2
    nf = np.maximum(n, 1).astype(np.float32)
    large = max_exact + (np.log(nf / np.float32(max_exact)) / np.float32(math.log(MAX_DISTANCE / max_exact))
                         * np.float32(N_BUCKETS - max_exact)).astype(np.int32)
    large = np.minimum(large, N_BUCKETS - 1)
    return np.where(n < max_exact, n, large).astype(np.int32)


def _prompt_bucket_table():
    qi = np.arange(WINDOW)[:, None]
    c = np.arange(WINDOW)[None, :]
    dist = np.where(c > qi, qi + WINDOW - c, qi - c)
    return _t5_bucket_np(dist)


def _sample_bucket_table():
    t = (np.arange(SAMPLE_ROWS) // SEQ_GROUP)[:, None]
    b = (np.arange(SAMPLE_ROWS) % SEQ_GROUP)[:, None]
    col = np.arange(SAMPLE_KEYS)[None, :]
    is_state = col < STATE_KEYS
    is_new = (col >= STATE_KEYS) & (col < STATE_KEYS + NEW_KEYS)
    kb = np.where(is_state, col // WINDOW, (col - STATE_KEYS) % SEQ_GROUP)
    kpos = np.where(is_state, col % WINDOW, WINDOW + (col - STATE_KEYS) // SEQ_GROUP)
    dist = (WINDOW + t) - kpos
    valid = (is_state | is_new) & (kb == b) & (dist >= 0) & (dist < WINDOW)
    return np.where(valid, _t5_bucket_np(dist), -1).astype(np.int32)


def _rms(x, g):
    y = x * lax.rsqrt(jnp.mean(x * x, axis=-1, keepdims=True) + EPS)
    return y * g


LOG2E = math.log2(math.e)
Q_SCALE = HEAD_DIM ** -0.5 * LOG2E


def _bias_from_buckets(bucket, rb_ref, h):
    t = jnp.zeros(bucket.shape, F32)
    for bkt in range(N_BUCKETS):
        t = jnp.where(bucket == bkt, rb_ref[bkt, h] * LOG2E, t)
    return t


def _sink_softmax(s, sink):
    m = jnp.maximum(jnp.max(s, axis=-1, keepdims=True), sink)
    p = jnp.exp2(s - m)
    return p, jnp.sum(p, axis=-1, keepdims=True) + jnp.exp2(sink - m)


def _softmax_pv(s, valid, bias, sink, v):
    p, denom = _sink_softmax(jnp.where(valid, s + bias, NEG), sink)
    return jnp.dot(p.astype(BF16), v, preferred_element_type=F32) / denom


def _ffn_kernel(*refs, layer, half, tail_in, final_split):
    refs = list(refs)
    x_ref = refs.pop(0)
    xt_ref = refs.pop(0) if tail_in else None
    g_ref, wg_hbm, wu_hbm, wd_hbm = refs[:4]
    del refs[:4]
    fg_ref = refs.pop(0) if final_split else None
    o_ref = refs.pop(0)
    ot_ref = refs.pop(0) if final_split else None
    wg, wu, wd, stage_in, stage_dn, sem = refs
    i = pl.program_id(0)
    x = x_ref[...]
    if tail_in:
        x = jnp.where(i < PROMPT_BLOCKS, x, xt_ref[...])
    h = _rms(x, g_ref[...]).astype(BF16)

    def finish(y):
        out = x + 0.5 * y
        if not final_split:
            o_ref[...] = out
            return
        out = _rms(out, fg_ref[...])

        @pl.when(i < PROMPT_BLOCKS)
        def _():
            o_ref[...] = out

        @pl.when(i == PROMPT_BLOCKS)
        def _():
            ot_ref[...] = out

    def chunk_copies(c, slot):
        cols = pl.ds(c * FF_CHUNK, FF_CHUNK)
        return (
            pltpu.make_async_copy(wg_hbm.at[layer, half, :, cols], stage_in.at[0, slot], sem.at[0, slot]),
            pltpu.make_async_copy(wu_hbm.at[layer, half, :, cols], stage_in.at[1, slot], sem.at[1, slot]),
            pltpu.make_async_copy(wd_hbm.at[layer, half, cols, :], stage_dn.at[slot], sem.at[2, slot]),
        )

    @pl.when(i == 0)
    def _():
        for c in range(min(2, N_FF_CHUNKS)):
            for cp in chunk_copies(c, c % 2):
                cp.start()
        acc = jnp.zeros((TM, D_MODEL), F32)
        for c in range(N_FF_CHUNKS):
            slot = c % 2
            cols = slice(c * FF_CHUNK, (c + 1) * FF_CHUNK)
            for cp in chunk_copies(c, slot):
                cp.wait()
            wg[:, cols] = stage_in[0, slot].astype(BF16)
            wu[:, cols] = stage_in[1, slot].astype(BF16)
            wd[cols, :] = stage_dn[slot].astype(BF16)
            if c + 2 < N_FF_CHUNKS:
                for cp in chunk_copies(c + 2, slot):
                    cp.start()
            gate = jnp.dot(h, wg[:, cols], preferred_element_type=F32)
            up = jnp.dot(h, wu[:, cols], preferred_element_type=F32)
            act = (jax.nn.silu(gate) * up).astype(BF16)
            acc = acc + jnp.dot(act, wd[cols, :], preferred_element_type=F32)
        finish(acc)

    @pl.when(i > 0)
    def _():
        gate = jnp.dot(h, wg[...], preferred_element_type=F32)
        up = jnp.dot(h, wu[...], preferred_element_type=F32)
        act = (jax.nn.silu(gate) * up).astype(BF16)
        finish(jnp.dot(act, wd[...], preferred_element_type=F32))


def _resident(shape):
    return pl.BlockSpec(shape, lambda *_: (0,) * len(shape), pipeline_mode=pl.Buffered(1))


def _ffn(x_in, gain, wg, wu, wd, layer, half, *, tail_in=None, final_gain=None):
    hbm = pl.BlockSpec(memory_space=pl.ANY)
    last_prompt = PROMPT_BLOCKS - 1
    row_block = pl.BlockSpec((TM, D_MODEL), lambda i: (i, 0))
    prompt_block = pl.BlockSpec((TM, D_MODEL), lambda i: (jnp.minimum(i, last_prompt), 0))
    sample_block = pl.BlockSpec((TM, D_MODEL), lambda i: (0, 0))
    if tail_in is None:
        args, in_specs = [x_in], [row_block]
    else:
        args, in_specs = [x_in, tail_in], [prompt_block, sample_block]
    args += [gain.reshape(1, D_MODEL), wg, wu, wd]
    in_specs += [_resident((1, D_MODEL)), hbm, hbm, hbm]
    if final_gain is None:
        out_shape = jax.ShapeDtypeStruct((N_TOK, D_MODEL), F32)
        out_specs = row_block
    else:
        args.append(final_gain.reshape(1, D_MODEL))
        in_specs.append(_resident((1, D_MODEL)))
        out_shape = (jax.ShapeDtypeStruct((PROMPT_TOK, D_MODEL), F32),
                     jax.ShapeDtypeStruct((SAMPLE_TOK, D_MODEL), F32))
        out_specs = (prompt_block, sample_block)
    aliases = {0: 0} if (tail_in is None and final_gain is None) else {}
    return pl.pallas_call(
        functools.partial(_ffn_kernel, layer=layer, half=half, tail_in=tail_in is not None,
                          final_split=final_gain is not None),
        out_shape=out_shape,
        grid=(PROMPT_BLOCKS + 1,),
        in_specs=in_specs,
        out_specs=out_specs,
        scratch_shapes=[
            pltpu.VMEM((D_MODEL, D_FF), BF16),
            pltpu.VMEM((D_MODEL, D_FF), BF16),
            pltpu.VMEM((D_FF, D_MODEL), BF16),
            pltpu.VMEM((2, 2, D_MODEL, FF_CHUNK), F32),
            pltpu.VMEM((2, FF_CHUNK, D_MODEL), F32),
            pltpu.SemaphoreType.DMA((3, 2)),
        ],
        input_output_aliases=aliases,
        compiler_params=pltpu.CompilerParams(
            dimension_semantics=("arbitrary",), vmem_limit_bytes=VMEM_LIMIT_BYTES),
        name="macaron_half",
    )(*args)


def _even_prompt_kernel(x_ref, g_ref, win_ref, wout_ref, wpool_ref, pscale_ref, bucket_ref, rb_ref, sink_ref,
                        o_ref, klast_ref, vlast_ref, plast_ref,
                        kbuf, kbuf_r, vbuf, vbuf_r, zb, s2b, s4b, s8b, bias_ref, cat_ref):
    b = pl.program_id(0)
    s = pl.program_id(1)
    last = pl.num_programs(1) - 1

    @pl.when((b == 0) & (s == 0))
    def _():
        bucket = bucket_ref[...]
        for h in range(N_HEADS_A):
            bias_ref[h] = _bias_from_buckets(bucket, rb_ref, h)

    @pl.when(s == 0)
    def _():
        for buf in (kbuf, kbuf_r, vbuf, vbuf_r):
            buf[0:WINDOW, :] = jnp.zeros((WINDOW, KV_WIDTH), BF16)
        zb[0:POOL_CTX, :] = jnp.zeros((POOL_CTX, POOL_WIDTH), F32)

    @pl.when(s > 0)
    def _():
        for buf in (kbuf, kbuf_r, vbuf, vbuf_r):
            buf[0:WINDOW, :] = buf[TM:TM + WINDOW, :]
        zb[0:POOL_CTX, :] = zb[TM:TM + POOL_CTX, :]

    x = x_ref[...]
    h = _rms(x, g_ref[...]).astype(BF16)
    proj = jnp.dot(h, win_ref[...], preferred_element_type=F32)
    q = proj[:, 0:ATTN_WIDTH] * Q_SCALE
    k = proj[:, ATTN_WIDTH:ATTN_WIDTH + KV_WIDTH]
    v = proj[:, ATTN_WIDTH + KV_WIDTH:ATTN_WIDTH + 2 * KV_WIDTH]
    up = proj[:, ATTN_WIDTH + 2 * KV_WIDTH:IN_EVEN]

    qlane = lax.broadcasted_iota(jnp.int32, (TM, ATTN_WIDTH), 1) % LANES
    q_half = (jnp.where(qlane < HEAD_DIM, q, 0.0).astype(BF16),
              jnp.where(qlane >= HEAD_DIM, q, 0.0).astype(BF16))
    kbuf[WINDOW:, :] = k.astype(BF16)
    vbuf[WINDOW:, :] = v.astype(BF16)
    kbuf_r[WINDOW:, :] = pltpu.roll(k, HEAD_DIM, 1).astype(BF16)
    vbuf_r[WINDOW:, :] = pltpu.roll(v, HEAD_DIM, 1).astype(BF16)

    qrow = lax.broadcasted_iota(jnp.int32, (WINDOW, WINDOW), 0)
    kcol = lax.broadcasted_iota(jnp.int32, (WINDOW, WINDOW), 1)
    prev = kcol > qrow
    olane = lax.broadcasted_iota(jnp.int32, (WINDOW, LANES), 1)
    for i in range(QBLK_PER_TILE):
        rows = slice(i * WINDOW, (i + 1) * WINDOW)
        keys = slice(i * WINDOW, (i + 2) * WINDOW)
        for j in range(N_HEADS_A // 2):
            halves = []
            for half in range(2):
                hd = 2 * j + half
                kvh = hd // GQA_GROUP
                kk = (kbuf if kvh == half else kbuf_r)[keys, :]
                vv = (vbuf if kvh == half else vbuf_r)[keys, :]
                qh = q_half[half][rows, j * LANES:(j + 1) * LANES]
                sc = lax.dot_general(qh, kk, (((1,), (1,)), ((), ())), preferred_element_type=F32)
                sc = jnp.where(prev, sc[:, 0:WINDOW], sc[:, WINDOW:]) + bias_ref[hd]
                if i == 0:
                    sc = jnp.where(prev & (qrow >= jnp.where(s > 0, WINDOW, 0)), NEG, sc)
                p, denom = _sink_softmax(sc, sink_ref[hd] * LOG2E)
                pcat = jnp.concatenate([jnp.where(prev, p, 0.0), jnp.where(prev, 0.0, p)], axis=1)
                halves.append(jnp.dot(pcat.astype(BF16), vv, preferred_element_type=F32) / denom)
            cat_ref[rows, j * LANES:(j + 1) * LANES] = jnp.where(
                olane < HEAD_DIM, halves[0], halves[1]).astype(BF16)

    zb[POOL_CTX:, :] = up
    n = TM + POOL_CTX
    s2b[8:, :] = zb[8:n, :] + zb[7:n - 1, :]
    s4b[16:, :] = s2b[16:n, POOL_GROUP:] + s2b[14:n - 2, POOL_GROUP:]
    s8b[24:, :] = s4b[24:n, POOL_GROUP:] + s4b[20:n - 4, POOL_GROUP:]
    s16 = s8b[32:n, POOL_GROUP:] + s8b[24:n - 8, POOL_GROUP:]
    sums = (s2b[POOL_CTX:, 0:POOL_GROUP], s4b[POOL_CTX:, 0:POOL_GROUP], s8b[POOL_CTX:, 0:POOL_GROUP], s16)
    pos1 = lax.broadcasted_iota(jnp.int32, (POOL_HEAD, POOL_GROUP), 0) + s * TM + 1
    for g, w in enumerate(POOL_WINDOWS):
        lanes = slice(g * POOL_GROUP, (g + 1) * POOL_GROUP)
        cnt = jnp.minimum(pos1, w).astype(F32)
        mean = jnp.concatenate([sums[g][0:POOL_HEAD] / cnt, sums[g][POOL_HEAD:] * (1.0 / w)], axis=0)
        d = (mean - up[:, lanes]).astype(BF16)
        y = jnp.dot(d, wpool_ref[g], preferred_element_type=F32) * pscale_ref[:, lanes]
        cat_ref[:, ATTN_WIDTH + g * POOL_GROUP:ATTN_WIDTH + (g + 1) * POOL_GROUP] = y.astype(BF16)

    o_ref[...] = x + jnp.dot(cat_ref[...], wout_ref[...], preferred_element_type=F32)

    @pl.when(s == last)
    def _():
        klast_ref[0] = k[TM - WINDOW:, :]
        vlast_ref[0] = v[TM - WINDOW:, :]
        plast_ref[0] = up[TM - 2 * SUBLANES:, :]


def _even_prompt(x, gain, w_in, w_out, w_pool, pool_scale, rel_bias, sinks):
    bucket = jnp.asarray(_prompt_bucket_table())
    smem = pl.BlockSpec(memory_space=pltpu.SMEM)
    return pl.pallas_call(
        _even_prompt_kernel,
        out_shape=(
            jax.ShapeDtypeStruct((N_TOK, D_MODEL), F32),
            jax.ShapeDtypeStruct((BATCH, WINDOW, KV_WIDTH), F32),
            jax.ShapeDtypeStruct((BATCH, WINDOW, KV_WIDTH), F32),
            jax.ShapeDtypeStruct((BATCH, 2 * SUBLANES, POOL_WIDTH), F32),
        ),
        grid=(BATCH, TILES_PER_SEQ),
        in_specs=[
            pl.BlockSpec((TM, D_MODEL), lambda b, s: (b * TILES_PER_SEQ + s, 0)),
            _resident((1, D_MODEL)),
            _resident((D_MODEL, IN_EVEN)),
            _resident((ATTN_WIDTH + POOL_WIDTH, D_MODEL)),
            _resident((N_POOL_GROUPS, POOL_GROUP, POOL_GROUP)),
            _resident((1, POOL_WIDTH)),
            _resident((WINDOW, WINDOW)),
            smem,
            smem,
        ],
        out_specs=(
            pl.BlockSpec((TM, D_MODEL), lambda b, s: (b * TILES_PER_SEQ + s, 0)),
            pl.BlockSpec((1, WINDOW, KV_WIDTH), lambda b, s: (b, 0, 0)),
            pl.BlockSpec((1, WINDOW, KV_WIDTH), lambda b, s: (b, 0, 0)),
            pl.BlockSpec((1, 2 * SUBLANES, POOL_WIDTH), lambda b, s: (b, 0, 0)),
        ),
        scratch_shapes=[
            pltpu.VMEM((WINDOW + TM, KV_WIDTH), BF16),
            pltpu.VMEM((WINDOW + TM, KV_WIDTH), BF16),
            pltpu.VMEM((WINDOW + TM, KV_WIDTH), BF16),
            pltpu.VMEM((WINDOW + TM, KV_WIDTH), BF16),
            pltpu.VMEM((POOL_CTX + TM, POOL_WIDTH), F32),
            pltpu.VMEM((POOL_CTX + TM, POOL_WIDTH), F32),
            pltpu.VMEM((POOL_CTX + TM, POOL_WIDTH - POOL_GROUP), F32),
            pltpu.VMEM((POOL_CTX + TM, POOL_WIDTH - 2 * POOL_GROUP), F32),
            pltpu.VMEM((N_HEADS_A, WINDOW, WINDOW), F32),
            pltpu.VMEM((TM, ATTN_WIDTH + POOL_WIDTH), BF16),
        ],
        input_output_aliases={0: 0},
        compiler_params=pltpu.CompilerParams(
            dimension_semantics=("arbitrary", "arbitrary"), vmem_limit_bytes=VMEM_LIMIT_BYTES),
        name="even_mixer_prompt",
    )(x, gain.reshape(1, D_MODEL), w_in, w_out, w_pool, pool_scale.reshape(1, POOL_WIDTH), bucket,
      rel_bias, sinks)


def _even_sample_kernel(x_ref, g_ref, win_ref, wout_ref, wpool_ref, pscale_ref, bucket_ref, rb_ref, sink_ref,
                        sk_ref, sv_ref, spool_ref,
                        o_ref, knew_ref, vnew_ref, upnew_ref,
                        qsel, kall, vall, bias_ref, attn_f32, cat_ref):
    g = pl.program_id(0)
    last = pl.num_programs(0) - 1
    olane = lax.broadcasted_iota(jnp.int32, (SAMPLE_ROWS, LANES), 1)

    @pl.when(g == 0)
    def _():
        bucket = bucket_ref[...]
        for h in range(N_HEADS_A):
            bias_ref[h] = _bias_from_buckets(bucket, rb_ref, h)
        x = x_ref[...]
        h = _rms(x, g_ref[...]).astype(BF16)
        proj = jnp.dot(h, win_ref[...], preferred_element_type=F32)
        q = proj[:, 0:ATTN_WIDTH] * Q_SCALE
        knew_ref[...] = proj[:, ATTN_WIDTH:ATTN_WIDTH + KV_WIDTH]
        vnew_ref[...] = proj[:, ATTN_WIDTH + KV_WIDTH:ATTN_WIDTH + 2 * KV_WIDTH]
        upnew_ref[...] = proj[:, ATTN_WIDTH + 2 * KV_WIDTH:IN_EVEN]
        qlane = lax.broadcasted_iota(jnp.int32, (SAMPLE_TOK, LANES), 1)
        for hd in range(N_HEADS_A):
            j, half, kvh = hd // 2, hd % 2, hd // GQA_GROUP
            blk = q[:, j * LANES:(j + 1) * LANES]
            if half != kvh:
                blk = pltpu.roll(blk, HEAD_DIM, 1)
            keep = (qlane < HEAD_DIM) if kvh == 0 else (qlane >= HEAD_DIM)
            qsel[hd] = jnp.where(keep, blk, 0.0)
        kall[STATE_KEYS:, :] = jnp.zeros((SAMPLE_KEYS - STATE_KEYS, KV_WIDTH), BF16)
        vall[STATE_KEYS:, :] = jnp.zeros((SAMPLE_KEYS - STATE_KEYS, KV_WIDTH), BF16)

    kall[0:STATE_KEYS, :] = sk_ref[...].reshape(STATE_KEYS, KV_WIDTH).astype(BF16)
    vall[0:STATE_KEYS, :] = sv_ref[...].reshape(STATE_KEYS, KV_WIDTH).astype(BF16)
    row0 = pl.multiple_of(g * SEQ_GROUP, SEQ_GROUP)
    knew = [knew_ref[pl.ds(t * DEC_BATCH + row0, SEQ_GROUP), :] for t in range(DEC_SEQ)]
    vnew = [vnew_ref[pl.ds(t * DEC_BATCH + row0, SEQ_GROUP), :] for t in range(DEC_SEQ)]
    kall[STATE_KEYS:STATE_KEYS + NEW_KEYS, :] = jnp.concatenate(knew, axis=0).astype(BF16)
    vall[STATE_KEYS:STATE_KEYS + NEW_KEYS, :] = jnp.concatenate(vnew, axis=0).astype(BF16)

    qg = jnp.concatenate(
        [qsel[hd, pl.ds(t * DEC_BATCH + row0, SEQ_GROUP), :] for hd in range(N_HEADS_A) for t in range(DEC_SEQ)],
        axis=0).astype(BF16)
    sc = lax.dot_general(qg, kall[...], (((1,), (1,)), ((), ())), preferred_element_type=F32)
    valid = jnp.concatenate([bucket_ref[...]] * N_HEADS_A, axis=0) >= 0
    bias = bias_ref[...].reshape(N_HEADS_A * SAMPLE_ROWS, SAMPLE_KEYS)
    sink = jnp.concatenate(
        [jnp.full((SAMPLE_ROWS, 1), sink_ref[hd] * LOG2E, F32) for hd in range(N_HEADS_A)], axis=0)
    o_all = _softmax_pv(sc, valid, bias, sink, vall[...])
    for j in range(N_HEADS_A // 2):
        halves = []
        for half in range(2):
            hd = 2 * j + half
            o = o_all[hd * SAMPLE_ROWS:(hd + 1) * SAMPLE_ROWS, :]
            if hd // GQA_GROUP != half:
                o = pltpu.roll(o, HEAD_DIM, 1)
            halves.append(o)
        blk = jnp.where(olane < HEAD_DIM, halves[0], halves[1])
        for t in range(DEC_SEQ):
            attn_f32[pl.ds(t * DEC_BATCH + row0, SEQ_GROUP), j * LANES:(j + 1) * LANES] = (
                blk[t * SEQ_GROUP:(t + 1) * SEQ_GROUP, :])

    @pl.when(g == last)
    def _():
        cat_ref[:, 0:ATTN_WIDTH] = attn_f32[...].astype(BF16)
        for t in range(DEC_SEQ):
            rows = slice(t * DEC_BATCH, (t + 1) * DEC_BATCH)
            for gi, w in enumerate(POOL_WINDOWS):
                lanes = slice(gi * POOL_GROUP, (gi + 1) * POOL_GROUP)
                acc = upnew_ref[rows, lanes]
                for back in range(1, w):
                    tt = t - back
                    if tt >= 0:
                        acc = acc + upnew_ref[tt * DEC_BATCH:(tt + 1) * DEC_BATCH, lanes]
                    else:
                        acc = acc + spool_ref[POOL_BUF + tt, :, lanes]
                cnt = float(min(PAST_LEN + t + 1, w))
                d = (acc / cnt - upnew_ref[rows, lanes]).astype(BF16)
                y = jnp.dot(d, wpool_ref[gi], preferred_element_type=F32) * pscale_ref[:, lanes]
                cat_ref[rows, ATTN_WIDTH + gi * POOL_GROUP:ATTN_WIDTH + (gi + 1) * POOL_GROUP] = y.astype(BF16)
        o_ref[...] = x_ref[...] + jnp.dot(cat_ref[...], wout_ref[...], preferred_element_type=F32)


def _even_sample(x, gain, w_in, w_out, w_pool, pool_scale, rel_bias, sinks, state_k, state_v, state_pool_t):
    bucket = jnp.asarray(_sample_bucket_table())
    smem = pl.BlockSpec(memory_space=pltpu.SMEM)
    sample_block = PROMPT_TOK // SAMPLE_TOK
    return pl.pallas_call(
        _even_sample_kernel,
        out_shape=(
            jax.ShapeDtypeStruct((N_TOK, D_MODEL), F32),
            jax.ShapeDtypeStruct((SAMPLE_TOK, KV_WIDTH), F32),
            jax.ShapeDtypeStruct((SAMPLE_TOK, KV_WIDTH), F32),
            jax.ShapeDtypeStruct((SAMPLE_TOK, POOL_WIDTH), F32),
        ),
        grid=(N_SEQ_GROUPS,),
        in_specs=[
            pl.BlockSpec((SAMPLE_TOK, D_MODEL), lambda g: (sample_block, 0)),
            _resident((1, D_MODEL)),
            _resident((D_MODEL, IN_EVEN)),
            _resident((ATTN_WIDTH + POOL_WIDTH, D_MODEL)),
            _resident((N_POOL_GROUPS, POOL_GROUP, POOL_GROUP)),
            _resident((1, POOL_WIDTH)),
            _resident((SAMPLE_ROWS, SAMPLE_KEYS)),
            smem,
            smem,
            pl.BlockSpec((SEQ_GROUP, WINDOW, KV_WIDTH), lambda g: (g, 0, 0)),
            pl.BlockSpec((SEQ_GROUP, WINDOW, KV_WIDTH), lambda g: (g, 0, 0)),
            _resident((POOL_BUF, DEC_BATCH, POOL_WIDTH)),
        ],
        out_specs=(
            pl.BlockSpec((SAMPLE_TOK, D_MODEL), lambda g: (sample_block, 0)),
            pl.BlockSpec((SAMPLE_TOK, KV_WIDTH), lambda g: (0, 0)),
            pl.BlockSpec((SAMPLE_TOK, KV_WIDTH), lambda g: (0, 0)),
            pl.BlockSpec((SAMPLE_TOK, POOL_WIDTH), lambda g: (0, 0)),
        ),
        scratch_shapes=[
            pltpu.VMEM((N_HEADS_A, SAMPLE_TOK, LANES), F32),
            pltpu.VMEM((SAMPLE_KEYS, KV_WIDTH), BF16),
            pltpu.VMEM((SAMPLE_KEYS, KV_WIDTH), BF16),
            pltpu.VMEM((N_HEADS_A, SAMPLE_ROWS, SAMPLE_KEYS), F32),
            pltpu.VMEM((SAMPLE_TOK, ATTN_WIDTH), F32),
            pltpu.VMEM((SAMPLE_TOK, ATTN_WIDTH + POOL_WIDTH), BF16),
        ],
        input_output_aliases={0: 0},
        compiler_params=pltpu.CompilerParams(
            dimension_semantics=("arbitrary",), vmem_limit_bytes=VMEM_LIMIT_BYTES),
        name="even_mixer_sample",
    )(x, gain.reshape(1, D_MODEL), w_in, w_out, w_pool, pool_scale.reshape(1, POOL_WIDTH), bucket,
      rel_bias, sinks, state_k, state_v, state_pool_t)


def _sgu_front(x, g_ref, win_ref, gv_ref):
    h = _rms(x, g_ref[...]).astype(BF16)
    uv = jax.nn.gelu(jnp.dot(h, win_ref[...], preferred_element_type=F32))
    return uv[:, 0:SGU_WIDTH], _rms(uv[:, SGU_WIDTH:], gv_ref[...])


def _odd_prompt_kernel(x_ref, g_ref, win_ref, gv_ref, ws_ref, bs_ref, wout_ref, o_ref, gated_ref):
    x = x_ref[...]
    u, v = _sgu_front(x, g_ref, win_ref, gv_ref)
    vb = v.astype(BF16)
    r = lax.broadcasted_iota(jnp.int32, (CHUNK, CHUNK), 0)
    c = lax.broadcasted_iota(jnp.int32, (CHUNK, CHUNK), 1)
    for gi in range(SGU_GROUPS):
        w = jnp.where(r >= c, ws_ref[gi], 0.0).astype(BF16)
        bias = jnp.concatenate([bs_ref[gi]] * (SGU_GROUP_W // LANES), axis=1)
        lanes = slice(gi * SGU_GROUP_W, (gi + 1) * SGU_GROUP_W)
        for ci in range(ODD_TM // CHUNK):
            rows = slice(ci * CHUNK, (ci + 1) * CHUNK)
            mixed = jnp.dot(w, vb[rows, lanes], preferred_element_type=F32) + bias
            gated_ref[rows, lanes] = (u[rows, lanes] * mixed).astype(BF16)
    o_ref[...] = x + jnp.dot(gated_ref[...], wout_ref[...], preferred_element_type=F32)


def _odd_prompt(x, gain, w_in, g_v, w_s, b_s, w_out):
    bias = jnp.broadcast_to(b_s[:, :, None], (SGU_GROUPS, CHUNK, LANES))
    return pl.pallas_call(
        _odd_prompt_kernel,
        out_shape=jax.ShapeDtypeStruct((N_TOK, D_MODEL), F32),
        grid=(PROMPT_TOK // ODD_TM,),
        in_specs=[
            pl.BlockSpec((ODD_TM, D_MODEL), lambda i: (i, 0)),
            _resident((1, D_MODEL)),
            _resident((D_MODEL, 2 * SGU_WIDTH)),
            _resident((1, SGU_WIDTH)),
            _resident((SGU_GROUPS, CHUNK, CHUNK)),
            _resident((SGU_GROUPS, CHUNK, LANES)),
            _resident((SGU_WIDTH, D_MODEL)),
        ],
        out_specs=pl.BlockSpec((ODD_TM, D_MODEL), lambda i: (i, 0)),
        scratch_shapes=[pltpu.VMEM((ODD_TM, SGU_WIDTH), BF16)],
        input_output_aliases={0: 0},
        compiler_params=pltpu.CompilerParams(
            dimension_semantics=("arbitrary",), vmem_limit_bytes=VMEM_LIMIT_BYTES),
        name="odd_mixer_prompt",
    )(x, gain.reshape(1, D_MODEL), w_in, g_v.reshape(1, SGU_WIDTH), w_s, bias, w_out)


def _odd_sample_kernel(x_ref, g_ref, win_ref, gv_ref, coef_ref, bias_ref, wout_ref, o_ref, v_ref):
    x = x_ref[...]
    u, v = _sgu_front(x, g_ref, win_ref, gv_ref)
    v_ref[...] = v
    gated = []
    for t in range(DEC_SEQ):
        mixed = bias_ref[t:t + 1, :]
        for s in range(t + 1):
            mixed = mixed + coef_ref[t, s:s + 1, :] * v[s * DEC_BATCH:(s + 1) * DEC_BATCH, :]
        gated.append((u[t * DEC_BATCH:(t + 1) * DEC_BATCH, :] * mixed).astype(BF16))
    o_ref[...] = x + jnp.dot(jnp.concatenate(gated, axis=0), wout_ref[...], preferred_element_type=F32)


def _odd_sample(x, gain, w_in, g_v, w_s, b_s, w_out):
    coef = jnp.repeat(jnp.transpose(w_s[:, :DEC_SEQ, :DEC_SEQ], (1, 2, 0)), SGU_GROUP_W, axis=-1)
    bias = jnp.repeat(b_s[:, :DEC_SEQ].T, SGU_GROUP_W, axis=-1)
    sample_block = PROMPT_TOK // SAMPLE_TOK
    return pl.pallas_call(
        _odd_sample_kernel,
        out_shape=(
            jax.ShapeDtypeStruct((N_TOK, D_MODEL), F32),
            jax.ShapeDtypeStruct((SAMPLE_TOK, SGU_WIDTH), F32),
        ),
        grid=(1,),
        in_specs=[
            pl.BlockSpec((SAMPLE_TOK, D_MODEL), lambda i: (sample_block, 0)),
            _resident((1, D_MODEL)),
            _resident((D_MODEL, 2 * SGU_WIDTH)),
            _resident((1, SGU_WIDTH)),
            _resident((DEC_SEQ, DEC_SEQ, SGU_WIDTH)),
            _resident((DEC_SEQ, SGU_WIDTH)),
            _resident((SGU_WIDTH, D_MODEL)),
        ],
        out_specs=(
            pl.BlockSpec((SAMPLE_TOK, D_MODEL), lambda i: (sample_block, 0)),
            pl.BlockSpec((SAMPLE_TOK, SGU_WIDTH), lambda i: (0, 0)),
        ),
        input_output_aliases={0: 0},
        compiler_params=pltpu.CompilerParams(
            dimension_semantics=("arbitrary",), vmem_limit_bytes=VMEM_LIMIT_BYTES),
        name="odd_mixer_sample",
    )(x, gain.reshape(1, D_MODEL), w_in, g_v.reshape(1, SGU_WIDTH), coef, bias, w_out)


def _to_step_major(a):
    return jnp.swapaxes(a, 0, 1).reshape(SAMPLE_TOK, a.shape[-1])


def _to_batch_major(a):
    return jnp.swapaxes(a.reshape(DEC_SEQ, DEC_BATCH, a.shape[-1]), 0, 1)


def kernel(x_prompt, x_sample, state_win_k, state_win_v, state_pool, rel_bias, norm_gains, final_gain,
           ffn_gate, ffn_up, ffn_down, w_in_even, w_out_even, attn_sinks, w_pool, pool_scale,
           w_in_odd, sgu_norm, w_spatial, b_spatial, w_out_odd):
    xp = x_prompt.reshape(PROMPT_TOK, D_MODEL)
    xs = _to_step_major(x_sample)

    kp_l, vp_l, pp_l, ks_l, vs_l, ps_l, sv_l = [], [], [], [], [], [], []
    x = None
    y_prompt = y_sample = None
    for l in range(DEPTH):
        fa = (norm_gains[l, 0], ffn_gate, ffn_up, ffn_down, l, 0)
        fb = (norm_gains[l, 2], ffn_gate, ffn_up, ffn_down, l, 1)
        if l == 0:
            x = _ffn(xp, *fa, tail_in=xs)
        else:
            x = _ffn(x, *fa)
        g1 = norm_gains[l, 1]
        if l % 2 == 0:
            e = l // 2
            w_in, w_out = w_in_even[e].astype(BF16), w_out_even[e].astype(BF16)
            wp = w_pool[e].astype(BF16)
            x, kp, vp, pp = _even_prompt(x, g1, w_in, w_out, wp, pool_scale[e], rel_bias, attn_sinks[e])
            sk = state_win_k[e].reshape(DEC_BATCH, WINDOW, KV_WIDTH)
            sv = state_win_v[e].reshape(DEC_BATCH, WINDOW, KV_WIDTH)
            x, kn, vn, un = _even_sample(x, g1, w_in, w_out, wp, pool_scale[e], rel_bias, attn_sinks[e],
                                         sk, sv, jnp.swapaxes(state_pool[e], 0, 1))
            kp_l.append(kp.reshape(BATCH, WINDOW, N_KV_A, HEAD_DIM))
            vp_l.append(vp.reshape(BATCH, WINDOW, N_KV_A, HEAD_DIM))
            pp_l.append(pp[:, 2 * SUBLANES - POOL_BUF:, :])
            ks_l.append(jnp.concatenate(
                [state_win_k[e][:, DEC_SEQ:], _to_batch_major(kn).reshape(DEC_BATCH, DEC_SEQ, N_KV_A, HEAD_DIM)],
                axis=1))
            vs_l.append(jnp.concatenate(
                [state_win_v[e][:, DEC_SEQ:], _to_batch_major(vn).reshape(DEC_BATCH, DEC_SEQ, N_KV_A, HEAD_DIM)],
                axis=1))
            ps_l.append(jnp.concatenate([state_pool[e][:, DEC_SEQ:], _to_batch_major(un)], axis=1))
        else:
            o = l // 2
            w_in, w_out = w_in_odd[o].astype(BF16), w_out_odd[o].astype(BF16)
            x = _odd_prompt(x, g1, w_in, sgu_norm[o], w_spatial[o], b_spatial[o], w_out)
            x, sv_new = _odd_sample(x, g1, w_in, sgu_norm[o], w_spatial[o], b_spatial[o], w_out)
            sv_l.append(_to_batch_major(sv_new))
        if l < DEPTH - 1:
            x = _ffn(x, *fb)
        else:
            y_prompt, y_sample = _ffn(x, *fb, final_gain=final_gain)

    return (y_prompt.reshape(BATCH, SEQ, D_MODEL), _to_batch_major(y_sample),
            jnp.stack(kp_l), jnp.stack(vp_l), jnp.stack(pp_l),
            jnp.stack(ks_l), jnp.stack(vs_l), jnp.stack(ps_l), jnp.stack(sv_l))
```

```python
import functools
import math

import numpy as np
import jax
import jax.numpy as jnp
from jax import lax
from jax.experimental import pallas as pl
from jax.experimental.pallas import tpu as pltpu

F32 = jnp.float32
BF16 = jnp.bfloat16

D_MODEL = 1024
BATCH = 8
SEQ = 2048
DEPTH = 2
DEC_BATCH = 128
DEC_SEQ = 4
PAST_LEN = 16384
N_HEADS_A = 8
N_KV_A = 2
HEAD_DIM = 64
GQA_GROUP = N_HEADS_A // N_KV_A
WINDOW = 128
ATTN_WIDTH = N_HEADS_A * HEAD_DIM
KV_WIDTH = N_KV_A * HEAD_DIM
N_BUCKETS = 32
MAX_DISTANCE = 128
POOL_WINDOWS = (2, 4, 8, 16)
N_POOL_GROUPS = len(POOL_WINDOWS)
POOL_GROUP = 128
POOL_WIDTH = N_POOL_GROUPS * POOL_GROUP
POOL_BUF = max(POOL_WINDOWS) - 1
IN_EVEN = ATTN_WIDTH + 2 * KV_WIDTH + POOL_WIDTH
CHUNK = 128
SGU_WIDTH = 1024
SGU_GROUPS = 4
SGU_GROUP_W = SGU_WIDTH // SGU_GROUPS
D_FF = 2816
EPS = 1e-6
NEG = -1e30

LANES = 128
SUBLANES = 8
VMEM_LIMIT_BYTES = 56 * 1024 * 1024

PROMPT_TOK = BATCH * SEQ
SAMPLE_TOK = DEC_BATCH * DEC_SEQ
N_TOK = PROMPT_TOK + SAMPLE_TOK

TM = SAMPLE_TOK
assert PROMPT_TOK % TM == 0
PROMPT_BLOCKS = PROMPT_TOK // TM
TILES_PER_SEQ = SEQ // TM
QBLK_PER_TILE = TM // WINDOW
FFN_TM = 2 * TM
assert PROMPT_TOK % FFN_TM == 0
FFN_PROMPT_STEPS = PROMPT_TOK // FFN_TM
ODD_TM = 1024
assert PROMPT_TOK % ODD_TM == 0 and ODD_TM % CHUNK == 0
FF_CHUNK = 256
N_FF_CHUNKS = D_FF // FF_CHUNK
POOL_HEAD = 2 * SUBLANES
assert all(w & (w - 1) == 0 and w <= POOL_HEAD for w in POOL_WINDOWS)
POOL_CTX = 32
SEQ_GROUP = 8
N_SEQ_GROUPS = DEC_BATCH // SEQ_GROUP
STATE_KEYS = SEQ_GROUP * WINDOW
NEW_KEYS = SEQ_GROUP * DEC_SEQ
SAMPLE_KEYS = 1152
SAMPLE_ROWS = DEC_SEQ * SEQ_GROUP


def _t5_bucket_np(dist):
    n = np.maximum(dist, 0)
    max_exact = N_BUCKETS // 2
    nf = np.maximum(n, 1).astype(np.float32)
    large = max_exact + (np.log(nf / np.float32(max_exact)) / np.float32(math.log(MAX_DISTANCE / max_exact))
                         * np.float32(N_BUCKETS - max_exact)).astype(np.int32)
    large = np.minimum(large, N_BUCKETS - 1)
    return np.where(n < max_exact, n, large).astype(np.int32)


def _prompt_bucket_table():
    qi = np.arange(WINDOW)[:, None]
    c = np.arange(WINDOW)[None, :]
    dist = np.where(c > qi, qi + WINDOW - c, qi - c)
    return _t5_bucket_np(dist)


def _sample_bucket_table():
    t = (np.arange(SAMPLE_ROWS) // SEQ_GROUP)[:, None]
    b = (np.arange(SAMPLE_ROWS) % SEQ_GROUP)[:, None]
    col = np.arange(SAMPLE_KEYS)[None, :]
    is_state = col < STATE_KEYS
    is_new = (col >= STATE_KEYS) & (col < STATE_KEYS + NEW_KEYS)
    kb = np.where(is_state, col // WINDOW, (col - STATE_KEYS) % SEQ_GROUP)
    kpos = np.where(is_state, col % WINDOW, WINDOW + (col - STATE_KEYS) // SEQ_GROUP)
    dist = (WINDOW + t) - kpos
    valid = (is_state | is_new) & (kb == b) & (dist >= 0) & (dist < WINDOW)
    return np.where(valid, _t5_bucket_np(dist), -1).astype(np.int32)


def _rms(x, g):
    y = x * lax.rsqrt(jnp.mean(x * x, axis=-1, keepdims=True) + EPS)
    return y * g


LOG2E = math.log2(math.e)
Q_SCALE = HEAD_DIM ** -0.5 * LOG2E


def _bias_from_buckets(bucket, rb_ref, h):
    t = jnp.zeros(bucket.shape, F32)
    for bkt in range(N_BUCKETS):
        t = jnp.where(bucket == bkt, rb_ref[bkt, h] * LOG2E, t)
    return t


def _sink_softmax(s, sink):
    m = jnp.maximum(jnp.max(s, axis=-1, keepdims=True), sink)
    p = jnp.exp2(s - m)
    return p, jnp.sum(p, axis=-1, keepdims=True) + jnp.exp2(sink - m)


def _softmax_pv(s, valid, bias, sink, v):
    p, denom = _sink_softmax(jnp.where(valid, s + bias, NEG), sink)
    return jnp.dot(p.astype(BF16), v, preferred_element_type=F32) / denom


def _ffn_kernel(*refs, layer, half, tail_in, final_split):
    refs = list(refs)
    x_ref = refs.pop(0)
    xt_ref = refs.pop(0) if tail_in else None
    g_ref, wg_hbm, wu_hbm, wd_hbm = refs[:4]
    del refs[:4]
    fg_ref = refs.pop(0) if final_split else None
    o_ref = refs.pop(0)
    ot_ref = refs.pop(0) if final_split else None
    wg, wu, wd, stage_in, stage_dn, sem = refs
    i = pl.program_id(0)

    def load(rows):
        x = x_ref[rows, :]
        if tail_in:
            x = jnp.where(i < FFN_PROMPT_STEPS, x, xt_ref[...])
        return x, _rms(x, g_ref[...]).astype(BF16)

    def finish(rows, x, y):
        out = x + 0.5 * y
        if not final_split:
            o_ref[rows, :] = out
            return
        out = _rms(out, fg_ref[...])

        @pl.when(i < FFN_PROMPT_STEPS)
        def _():
            o_ref[rows, :] = out

        @pl.when(i == FFN_PROMPT_STEPS)
        def _():
            ot_ref[...] = out

    def chunk_copies(c, slot):
        cols = pl.ds(c * FF_CHUNK, FF_CHUNK)
        return (
            pltpu.make_async_copy(wg_hbm.at[layer, half, :, cols], stage_in.at[0, slot], sem.at[0, slot]),
            pltpu.make_async_copy(wu_hbm.at[layer, half, :, cols], stage_in.at[1, slot], sem.at[1, slot]),
            pltpu.make_async_copy(wd_hbm.at[layer, half, cols, :], stage_dn.at[slot], sem.at[2, slot]),
        )

    @pl.when(i == 0)
    def _():
        for c in range(min(2, N_FF_CHUNKS)):
            for cp in chunk_copies(c, c % 2):
                cp.start()
        rows = slice(0, TM)
        x, h = load(rows)
        acc = jnp.zeros((TM, D_MODEL), F32)
        for c in range(N_FF_CHUNKS):
            slot = c % 2
            cols = slice(c * FF_CHUNK, (c + 1) * FF_CHUNK)
            for cp in chunk_copies(c, slot):
                cp.wait()
            wg[:, cols] = stage_in[0, slot].astype(BF16)
            wu[:, cols] = stage_in[1, slot].astype(BF16)
            wd[cols, :] = stage_dn[slot].astype(BF16)
            if c + 2 < N_FF_CHUNKS:
                for cp in chunk_copies(c + 2, slot):
                    cp.start()
            gate = jnp.dot(h, wg[:, cols], preferred_element_type=F32)
            up = jnp.dot(h, wu[:, cols], preferred_element_type=F32)
            act = (jax.nn.silu(gate) * up).astype(BF16)
            acc = acc + jnp.dot(act, wd[cols, :], preferred_element_type=F32)
        finish(rows, x, acc)

    def half_block(hf, carry):
        rows = pl.ds(pl.multiple_of(hf * TM, TM), TM)
        x, h = load(rows)
        gate = jnp.dot(h, wg[...], preferred_element_type=F32)
        up = jnp.dot(h, wu[...], preferred_element_type=F32)
        act = (jax.nn.silu(gate) * up).astype(BF16)
        finish(rows, x, jnp.dot(act, wd[...], preferred_element_type=F32))
        return carry

    first = jnp.where(i == 0, 1, 0)
    stop = jnp.where(i < FFN_PROMPT_STEPS, FFN_TM // TM, 1)
    lax.fori_loop(first, stop, half_block, 0)


def _resident(shape):
    return pl.BlockSpec(shape, lambda *_: (0,) * len(shape), pipeline_mode=pl.Buffered(1))


def _ffn(x_in, gain, wg, wu, wd, layer, half, *, tail_in=None, final_gain=None):
    hbm = pl.BlockSpec(memory_space=pl.ANY)
    last_prompt = FFN_PROMPT_STEPS - 1
    row_block = pl.BlockSpec((FFN_TM, D_MODEL), lambda i: (i, 0))
    prompt_block = pl.BlockSpec((FFN_TM, D_MODEL), lambda i: (jnp.minimum(i, last_prompt), 0))
    sample_block = pl.BlockSpec((TM, D_MODEL), lambda i: (0, 0))
    if tail_in is None:
        args, in_specs = [x_in], [row_block]
    else:
        args, in_specs = [x_in, tail_in], [prompt_block, sample_block]
    args += [gain.reshape(1, D_MODEL), wg, wu, wd]
    in_specs += [_resident((1, D_MODEL)), hbm, hbm, hbm]
    if final_gain is None:
        out_shape = jax.ShapeDtypeStruct((N_TOK, D_MODEL), F32)
        out_specs = row_block
    else:
        args.append(final_gain.reshape(1, D_MODEL))
        in_specs.append(_resident((1, D_MODEL)))
        out_shape = (jax.ShapeDtypeStruct((PROMPT_TOK, D_MODEL), F32),
                     jax.ShapeDtypeStruct((SAMPLE_TOK, D_MODEL), F32))
        out_specs = (prompt_block, sample_block)
    aliases = {0: 0} if (tail_in is None and final_gain is None) else {}
    return pl.pallas_call(
        functools.partial(_ffn_kernel, layer=layer, half=half, tail_in=tail_in is not None,
                          final_split=final_gain is not None),
        out_shape=out_shape,
        grid=(FFN_PROMPT_STEPS + 1,),
        in_specs=in_specs,
        out_specs=out_specs,
        scratch_shapes=[
            pltpu.VMEM((D_MODEL, D_FF), BF16),
            pltpu.VMEM((D_MODEL, D_FF), BF16),
            pltpu.VMEM((D_FF, D_MODEL), BF16),
            pltpu.VMEM((2, 2, D_MODEL, FF_CHUNK), F32),
            pltpu.VMEM((2, FF_CHUNK, D_MODEL), F32),
            pltpu.SemaphoreType.DMA((3, 2)),
        ],
        input_output_aliases=aliases,
        compiler_params=pltpu.CompilerParams(
            dimension_semantics=("arbitrary",), vmem_limit_bytes=VMEM_LIMIT_BYTES),
        name="macaron_half",
    )(*args)


def _even_prompt_kernel(x_ref, g_ref, win_ref, wout_ref, wpool_ref, pscale_ref, bucket_ref, rb_ref, sink_ref,
                        o_ref, klast_ref, vlast_ref, plast_ref,
                        kbuf, kbuf_r, vbuf, vbuf_r, zb, s2b, s4b, s8b, bias_ref, cat_ref):
    b = pl.program_id(0)
    s = pl.program_id(1)
    last = pl.num_programs(1) - 1

    @pl.when((b == 0) & (s == 0))
    def _():
        bucket = bucket_ref[...]
        for h in range(N_HEADS_A):
            bias_ref[h] = _bias_from_buckets(bucket, rb_ref, h)

    @pl.when(s == 0)
    def _():
        for buf in (kbuf, kbuf_r, vbuf, vbuf_r):
            buf[0:WINDOW, :] = jnp.zeros((WINDOW, KV_WIDTH), BF16)
        zb[0:POOL_CTX, :] = jnp.zeros((POOL_CTX, POOL_WIDTH), F32)

    @pl.when(s > 0)
    def _():
        for buf in (kbuf, kbuf_r, vbuf, vbuf_r):
            buf[0:WINDOW, :] = buf[TM:TM + WINDOW, :]
        zb[0:POOL_CTX, :] = zb[TM:TM + POOL_CTX, :]

    x = x_ref[...]
    h = _rms(x, g_ref[...]).astype(BF16)
    proj = jnp.dot(h, win_ref[...], preferred_element_type=F32)
    q = proj[:, 0:ATTN_WIDTH] * Q_SCALE
    k = proj[:, ATTN_WIDTH:ATTN_WIDTH + KV_WIDTH]
    v = proj[:, ATTN_WIDTH + KV_WIDTH:ATTN_WIDTH + 2 * KV_WIDTH]
    up = proj[:, ATTN_WIDTH + 2 * KV_WIDTH:IN_EVEN]

    qlane = lax.broadcasted_iota(jnp.int32, (TM, ATTN_WIDTH), 1) % LANES
    q_half = (jnp.where(qlane < HEAD_DIM, q, 0.0).astype(BF16),
              jnp.where(qlane >= HEAD_DIM, q, 0.0).astype(BF16))
    kbuf[WINDOW:, :] = k.astype(BF16)
    vbuf[WINDOW:, :] = v.astype(BF16)
    kbuf_r[WINDOW:, :] = pltpu.roll(k, HEAD_DIM, 1).astype(BF16)
    vbuf_r[WINDOW:, :] = pltpu.roll(v, HEAD_DIM, 1).astype(BF16)

    qrow = lax.broadcasted_iota(jnp.int32, (WINDOW, WINDOW), 0)
    kcol = lax.broadcasted_iota(jnp.int32, (WINDOW, WINDOW), 1)
    prev = kcol > qrow
    olane = lax.broadcasted_iota(jnp.int32, (WINDOW, LANES), 1)
    for i in range(QBLK_PER_TILE):
        rows = slice(i * WINDOW, (i + 1) * WINDOW)
        keys = slice(i * WINDOW, (i + 2) * WINDOW)
        for j in range(N_HEADS_A // 2):
            halves = []
            for half in range(2):
                hd = 2 * j + half
                kvh = hd // GQA_GROUP
                kk = (kbuf if kvh == half else kbuf_r)[keys, :]
                vv = (vbuf if kvh == half else vbuf_r)[keys, :]
                qh = q_half[half][rows, j * LANES:(j + 1) * LANES]
                sc = lax.dot_general(qh, kk, (((1,), (1,)), ((), ())), preferred_element_type=F32)
                sc = jnp.where(prev, sc[:, 0:WINDOW], sc[:, WINDOW:]) + bias_ref[hd]
                if i == 0:
                    sc = jnp.where(prev & (qrow >= jnp.where(s > 0, WINDOW, 0)), NEG, sc)
                p, denom = _sink_softmax(sc, sink_ref[hd] * LOG2E)
                pcat = jnp.concatenate([jnp.where(prev, p, 0.0), jnp.where(prev, 0.0, p)], axis=1)
                halves.append(jnp.dot(pcat.astype(BF16), vv, preferred_element_type=F32) / denom)
            cat_ref[rows, j * LANES:(j + 1) * LANES] = jnp.where(
                olane < HEAD_DIM, halves[0], halves[1]).astype(BF16)

    zb[POOL_CTX:, :] = up
    n = TM + POOL_CTX
    s2b[8:, :] = zb[8:n, :] + zb[7:n - 1, :]
    s4b[16:, :] = s2b[16:n, POOL_GROUP:] + s2b[14:n - 2, POOL_GROUP:]
    s8b[24:, :] = s4b[24:n, POOL_GROUP:] + s4b[20:n - 4, POOL_GROUP:]
    s16 = s8b[32:n, POOL_GROUP:] + s8b[24:n - 8, POOL_GROUP:]
    sums = (s2b[POOL_CTX:, 0:POOL_GROUP], s4b[POOL_CTX:, 0:POOL_GROUP], s8b[POOL_CTX:, 0:POOL_GROUP], s16)
    pos1 = lax.broadcasted_iota(jnp.int32, (POOL_HEAD, POOL_GROUP), 0) + s * TM + 1
    for g, w in enumerate(POOL_WINDOWS):
        lanes = slice(g * POOL_GROUP, (g + 1) * POOL_GROUP)
        cnt = jnp.minimum(pos1, w).astype(F32)
        mean = jnp.concatenate([sums[g][0:POOL_HEAD] / cnt, sums[g][POOL_HEAD:] * (1.0 / w)], axis=0)
        d = (mean - up[:, lanes]).astype(BF16)
        y = jnp.dot(d, wpool_ref[g], preferred_element_type=F32) * pscale_ref[:, lanes]
        cat_ref[:, ATTN_WIDTH + g * POOL_GROUP:ATTN_WIDTH + (g + 1) * POOL_GROUP] = y.astype(BF16)

    o_ref[...] = x + jnp.dot(cat_ref[...], wout_ref[...], preferred_element_type=F32)

    @pl.when(s == last)
    def _():
        klast_ref[0] = k[TM - WINDOW:, :]
        vlast_ref[0] = v[TM - WINDOW:, :]
        plast_ref[0] = up[TM - 2 * SUBLANES:, :]


def _even_prompt(x, gain, w_in, w_out, w_pool, pool_scale, rel_bias, sinks):
    bucket = jnp.asarray(_prompt_bucket_table())
    smem = pl.BlockSpec(memory_space=pltpu.SMEM)
    return pl.pallas_call(
        _even_prompt_kernel,
        out_shape=(
            jax.ShapeDtypeStruct((N_TOK, D_MODEL), F32),
            jax.ShapeDtypeStruct((BATCH, WINDOW, KV_WIDTH), F32),
            jax.ShapeDtypeStruct((BATCH, WINDOW, KV_WIDTH), F32),
            jax.ShapeDtypeStruct((BATCH, 2 * SUBLANES, POOL_WIDTH), F32),
        ),
        grid=(BATCH, TILES_PER_SEQ),
        in_specs=[
            pl.BlockSpec((TM, D_MODEL), lambda b, s: (b * TILES_PER_SEQ + s, 0)),
            _resident((1, D_MODEL)),
            _resident((D_MODEL, IN_EVEN)),
            _resident((ATTN_WIDTH + POOL_WIDTH, D_MODEL)),
            _resident((N_POOL_GROUPS, POOL_GROUP, POOL_GROUP)),
            _resident((1, POOL_WIDTH)),
            _resident((WINDOW, WINDOW)),
            smem,
            smem,
        ],
        out_specs=(
            pl.BlockSpec((TM, D_MODEL), lambda b, s: (b * TILES_PER_SEQ + s, 0)),
            pl.BlockSpec((1, WINDOW, KV_WIDTH), lambda b, s: (b, 0, 0)),
            pl.BlockSpec((1, WINDOW, KV_WIDTH), lambda b, s: (b, 0, 0)),
            pl.BlockSpec((1, 2 * SUBLANES, POOL_WIDTH), lambda b, s: (b, 0, 0)),
        ),
        scratch_shapes=[
            pltpu.VMEM((WINDOW + TM, KV_WIDTH), BF16),
            pltpu.VMEM((WINDOW + TM, KV_WIDTH), BF16),
            pltpu.VMEM((WINDOW + TM, KV_WIDTH), BF16),
            pltpu.VMEM((WINDOW + TM, KV_WIDTH), BF16),
            pltpu.VMEM((POOL_CTX + TM, POOL_WIDTH), F32),
            pltpu.VMEM((POOL_CTX + TM, POOL_WIDTH), F32),
            pltpu.VMEM((POOL_CTX + TM, POOL_WIDTH - POOL_GROUP), F32),
            pltpu.VMEM((POOL_CTX + TM, POOL_WIDTH - 2 * POOL_GROUP), F32),
            pltpu.VMEM((N_HEADS_A, WINDOW, WINDOW), F32),
            pltpu.VMEM((TM, ATTN_WIDTH + POOL_WIDTH), BF16),
        ],
        input_output_aliases={0: 0},
        compiler_params=pltpu.CompilerParams(
            dimension_semantics=("arbitrary", "arbitrary"), vmem_limit_bytes=VMEM_LIMIT_BYTES),
        name="even_mixer_prompt",
    )(x, gain.reshape(1, D_MODEL), w_in, w_out, w_pool, pool_scale.reshape(1, POOL_WIDTH), bucket,
      rel_bias, sinks)


def _even_sample_kernel(x_ref, g_ref, win_ref, wout_ref, wpool_ref, pscale_ref, bucket_ref, rb_ref, sink_ref,
                        sk_ref, sv_ref, spool_ref,
                        o_ref, knew_ref, vnew_ref, upnew_ref,
                        qsel, kall, vall, bias_ref, attn_f32, cat_ref):
    g = pl.program_id(0)
    last = pl.num_programs(0) - 1
    olane = lax.broadcasted_iota(jnp.int32, (SAMPLE_ROWS, LANES), 1)

    @pl.when(g == 0)
    def _():
        bucket = bucket_ref[...]
        for h in range(N_HEADS_A):
            bias_ref[h] = _bias_from_buckets(bucket, rb_ref, h)
        x = x_ref[...]
        h = _rms(x, g_ref[...]).astype(BF16)
        proj = jnp.dot(h, win_ref[...], preferred_element_type=F32)
        q = proj[:, 0:ATTN_WIDTH] * Q_SCALE
        knew_ref[...] = proj[:, ATTN_WIDTH:ATTN_WIDTH + KV_WIDTH]
        vnew_ref[...] = proj[:, ATTN_WIDTH + KV_WIDTH:ATTN_WIDTH + 2 * KV_WIDTH]
        upnew_ref[...] = proj[:, ATTN_WIDTH + 2 * KV_WIDTH:IN_EVEN]
        qlane = lax.broadcasted_iota(jnp.int32, (SAMPLE_TOK, LANES), 1)
        for hd in range(N_HEADS_A):
            j, half, kvh = hd // 2, hd % 2, hd // GQA_GROUP
            blk = q[:, j * LANES:(j + 1) * LANES]
            if half != kvh:
                blk = pltpu.roll(blk, HEAD_DIM, 1)
            keep = (qlane < HEAD_DIM) if kvh == 0 else (qlane >= HEAD_DIM)
            qsel[hd] = jnp.where(keep, blk, 0.0)
        kall[STATE_KEYS:, :] = jnp.zeros((SAMPLE_KEYS - STATE_KEYS, KV_WIDTH), BF16)
        vall[STATE_KEYS:, :] = jnp.zeros((SAMPLE_KEYS - STATE_KEYS, KV_WIDTH), BF16)

    kall[0:STATE_KEYS, :] = sk_ref[...].reshape(STATE_KEYS, KV_WIDTH).astype(BF16)
    vall[0:STATE_KEYS, :] = sv_ref[...].reshape(STATE_KEYS, KV_WIDTH).astype(BF16)
    row0 = pl.multiple_of(g * SEQ_GROUP, SEQ_GROUP)
    knew = [knew_ref[pl.ds(t * DEC_BATCH + row0, SEQ_GROUP), :] for t in range(DEC_SEQ)]
    vnew = [vnew_ref[pl.ds(t * DEC_BATCH + row0, SEQ_GROUP), :] for t in range(DEC_SEQ)]
    kall[STATE_KEYS:STATE_KEYS + NEW_KEYS, :] = jnp.concatenate(knew, axis=0).astype(BF16)
    vall[STATE_KEYS:STATE_KEYS + NEW_KEYS, :] = jnp.concatenate(vnew, axis=0).astype(BF16)

    qg = jnp.concatenate(
        [qsel[hd, pl.ds(t * DEC_BATCH + row0, SEQ_GROUP), :] for hd in range(N_HEADS_A) for t in range(DEC_SEQ)],
        axis=0).astype(BF16)
    sc = lax.dot_general(qg, kall[...], (((1,), (1,)), ((), ())), preferred_element_type=F32)
    valid = jnp.concatenate([bucket_ref[...]] * N_HEADS_A, axis=0) >= 0
    bias = bias_ref[...].reshape(N_HEADS_A * SAMPLE_ROWS, SAMPLE_KEYS)
    sink = jnp.concatenate(
        [jnp.full((SAMPLE_ROWS, 1), sink_ref[hd] * LOG2E, F32) for hd in range(N_HEADS_A)], axis=0)
    o_all = _softmax_pv(sc, valid, bias, sink, vall[...])
    for j in range(N_HEADS_A // 2):
        halves = []
        for half in range(2):
            hd = 2 * j + half
            o = o_all[hd * SAMPLE_ROWS:(hd + 1) * SAMPLE_ROWS, :]
            if hd // GQA_GROUP != half:
                o = pltpu.roll(o, HEAD_DIM, 1)
            halves.append(o)
        blk = jnp.where(olane < HEAD_DIM, halves[0], halves[1])
        for t in range(DEC_SEQ):
            attn_f32[pl.ds(t * DEC_BATCH + row0, SEQ_GROUP), j * LANES:(j + 1) * LANES] = (
                blk[t * SEQ_GROUP:(t + 1) * SEQ_GROUP, :])

    @pl.when(g == last)
    def _():
        cat_ref[:, 0:ATTN_WIDTH] = attn_f32[...].astype(BF16)
        for t in range(DEC_SEQ):
            rows = slice(t * DEC_BATCH, (t + 1) * DEC_BATCH)
            for gi, w in enumerate(POOL_WINDOWS):
                lanes = slice(gi * POOL_GROUP, (gi + 1) * POOL_GROUP)
                acc = upnew_ref[rows, lanes]
                for back in range(1, w):
                    tt = t - back
                    if tt >= 0:
                        acc = acc + upnew_ref[tt * DEC_BATCH:(tt + 1) * DEC_BATCH, lanes]
                    else:
                        acc = acc + spool_ref[POOL_BUF + tt, :, lanes]
                cnt = float(min(PAST_LEN + t + 1, w))
                d = (acc / cnt - upnew_ref[rows, lanes]).astype(BF16)
                y = jnp.dot(d, wpool_ref[gi], preferred_element_type=F32) * pscale_ref[:, lanes]
                cat_ref[rows, ATTN_WIDTH + gi * POOL_GROUP:ATTN_WIDTH + (gi + 1) * POOL_GROUP] = y.astype(BF16)
        o_ref[...] = x_ref[...] + jnp.dot(cat_ref[...], wout_ref[...], preferred_element_type=F32)


def _even_sample(x, gain, w_in, w_out, w_pool, pool_scale, rel_bias, sinks, state_k, state_v, state_pool_t):
    bucket = jnp.asarray(_sample_bucket_table())
    smem = pl.BlockSpec(memory_space=pltpu.SMEM)
    sample_block = PROMPT_TOK // SAMPLE_TOK
    return pl.pallas_call(
        _even_sample_kernel,
        out_shape=(
            jax.ShapeDtypeStruct((N_TOK, D_MODEL), F32),
            jax.ShapeDtypeStruct((SAMPLE_TOK, KV_WIDTH), F32),
            jax.ShapeDtypeStruct((SAMPLE_TOK, KV_WIDTH), F32),
            jax.ShapeDtypeStruct((SAMPLE_TOK, POOL_WIDTH), F32),
        ),
        grid=(N_SEQ_GROUPS,),
        in_specs=[
            pl.BlockSpec((SAMPLE_TOK, D_MODEL), lambda g: (sample_block, 0)),
            _resident((1, D_MODEL)),
            _resident((D_MODEL, IN_EVEN)),
            _resident((ATTN_WIDTH + POOL_WIDTH, D_MODEL)),
            _resident((N_POOL_GROUPS, POOL_GROUP, POOL_GROUP)),
            _resident((1, POOL_WIDTH)),
            _resident((SAMPLE_ROWS, SAMPLE_KEYS)),
            smem,
            smem,
            pl.BlockSpec((SEQ_GROUP, WINDOW, KV_WIDTH), lambda g: (g, 0, 0)),
            pl.BlockSpec((SEQ_GROUP, WINDOW, KV_WIDTH), lambda g: (g, 0, 0)),
            _resident((POOL_BUF, DEC_BATCH, POOL_WIDTH)),
        ],
        out_specs=(
            pl.BlockSpec((SAMPLE_TOK, D_MODEL), lambda g: (sample_block, 0)),
            pl.BlockSpec((SAMPLE_TOK, KV_WIDTH), lambda g: (0, 0)),
            pl.BlockSpec((SAMPLE_TOK, KV_WIDTH), lambda g: (0, 0)),
            pl.BlockSpec((SAMPLE_TOK, POOL_WIDTH), lambda g: (0, 0)),
        ),
        scratch_shapes=[
            pltpu.VMEM((N_HEADS_A, SAMPLE_TOK, LANES), F32),
            pltpu.VMEM((SAMPLE_KEYS, KV_WIDTH), BF16),
            pltpu.VMEM((SAMPLE_KEYS, KV_WIDTH), BF16),
            pltpu.VMEM((N_HEADS_A, SAMPLE_ROWS, SAMPLE_KEYS), F32),
            pltpu.VMEM((SAMPLE_TOK, ATTN_WIDTH), F32),
            pltpu.VMEM((SAMPLE_TOK, ATTN_WIDTH + POOL_WIDTH), BF16),
        ],
        input_output_aliases={0: 0},
        compiler_params=pltpu.CompilerParams(
            dimension_semantics=("arbitrary",), vmem_limit_bytes=VMEM_LIMIT_BYTES),
        name="even_mixer_sample",
    )(x, gain.reshape(1, D_MODEL), w_in, w_out, w_pool, pool_scale.reshape(1, POOL_WIDTH), bucket,
      rel_bias, sinks, state_k, state_v, state_pool_t)


def _sgu_front(x, g_ref, win_ref, gv_ref):
    h = _rms(x, g_ref[...]).astype(BF16)
    uv = jax.nn.gelu(jnp.dot(h, win_ref[...], preferred_element_type=F32))
    return uv[:, 0:SGU_WIDTH], _rms(uv[:, SGU_WIDTH:], gv_ref[...])


def _odd_prompt_kernel(x_ref, g_ref, win_ref, gv_ref, ws_ref, bs_ref, wout_ref, o_ref, gated_ref):
    x = x_ref[...]
    u, v = _sgu_front(x, g_ref, win_ref, gv_ref)
    vb = v.astype(BF16)
    r = lax.broadcasted_iota(jnp.int32, (CHUNK, CHUNK), 0)
    c = lax.broadcasted_iota(jnp.int32, (CHUNK, CHUNK), 1)
    for gi in range(SGU_GROUPS):
        w = jnp.where(r >= c, ws_ref[gi], 0.0).astype(BF16)
        bias = jnp.concatenate([bs_ref[gi]] * (SGU_GROUP_W // LANES), axis=1)
        lanes = slice(gi * SGU_GROUP_W, (gi + 1) * SGU_GROUP_W)
        for ci in range(ODD_TM // CHUNK):
            rows = slice(ci * CHUNK, (ci + 1) * CHUNK)
            mixed = jnp.dot(w, vb[rows, lanes], preferred_element_type=F32) + bias
            gated_ref[rows, lanes] = (u[rows, lanes] * mixed).astype(BF16)
    o_ref[...] = x + jnp.dot(gated_ref[...], wout_ref[...], preferred_element_type=F32)


def _odd_prompt(x, gain, w_in, g_v, w_s, b_s, w_out):
    bias = jnp.broadcast_to(b_s[:, :, None], (SGU_GROUPS, CHUNK, LANES))
    return pl.pallas_call(
        _odd_prompt_kernel,
        out_shape=jax.ShapeDtypeStruct((N_TOK, D_MODEL), F32),
        grid=(PROMPT_TOK // ODD_TM,),
        in_specs=[
            pl.BlockSpec((ODD_TM, D_MODEL), lambda i: (i, 0)),
            _resident((1, D_MODEL)),
            _resident((D_MODEL, 2 * SGU_WIDTH)),
            _resident((1, SGU_WIDTH)),
            _resident((SGU_GROUPS, CHUNK, CHUNK)),
            _resident((SGU_GROUPS, CHUNK, LANES)),
            _resident((SGU_WIDTH, D_MODEL)),
        ],
        out_specs=pl.BlockSpec((ODD_TM, D_MODEL), lambda i: (i, 0)),
        scratch_shapes=[pltpu.VMEM((ODD_TM, SGU_WIDTH), BF16)],
        input_output_aliases={0: 0},
        compiler_params=pltpu.CompilerParams(
            dimension_semantics=("arbitrary",), vmem_limit_bytes=VMEM_LIMIT_BYTES),
        name="odd_mixer_prompt",
    )(x, gain.reshape(1, D_MODEL), w_in, g_v.reshape(1, SGU_WIDTH), w_s, bias, w_out)


def _odd_sample_kernel(x_ref, g_ref, win_ref, gv_ref, coef_ref, bias_ref, wout_ref, o_ref, v_ref):
    x = x_ref[...]
    u, v = _sgu_front(x, g_ref, win_ref, gv_ref)
    v_ref[...] = v
    gated = []
    for t in range(DEC_SEQ):
        mixed = bias_ref[t:t + 1, :]
        for s in range(t + 1):
            mixed = mixed + coef_ref[t, s:s + 1, :] * v[s * DEC_BATCH:(s + 1) * DEC_BATCH, :]
        gated.append((u[t * DEC_BATCH:(t + 1) * DEC_BATCH, :] * mixed).astype(BF16))
    o_ref[...] = x + jnp.dot(jnp.concatenate(gated, axis=0), wout_ref[...], preferred_element_type=F32)


def _odd_sample(x, gain, w_in, g_v, w_s, b_s, w_out):
    coef = jnp.repeat(jnp.transpose(w_s[:, :DEC_SEQ, :DEC_SEQ], (1, 2, 0)), SGU_GROUP_W, axis=-1)
    bias = jnp.repeat(b_s[:, :DEC_SEQ].T, SGU_GROUP_W, axis=-1)
    sample_block = PROMPT_TOK // SAMPLE_TOK
    return pl.pallas_call(
        _odd_sample_kernel,
        out_shape=(
            jax.ShapeDtypeStruct((N_TOK, D_MODEL), F32),
            jax.ShapeDtypeStruct((SAMPLE_TOK, SGU_WIDTH), F32),
        ),
        grid=(1,),
        in_specs=[
            pl.BlockSpec((SAMPLE_TOK, D_MODEL), lambda i: (sample_block, 0)),
            _resident((1, D_MODEL)),
            _resident((D_MODEL, 2 * SGU_WIDTH)),
            _resident((1, SGU_WIDTH)),
            _resident((DEC_SEQ, DEC_SEQ, SGU_WIDTH)),
            _resident((DEC_SEQ, SGU_WIDTH)),
            _resident((SGU_WIDTH, D_MODEL)),
        ],
        out_specs=(
            pl.BlockSpec((SAMPLE_TOK, D_MODEL), lambda i: (sample_block, 0)),
            pl.BlockSpec((SAMPLE_TOK, SGU_WIDTH), lambda i: (0, 0)),
        ),
        input_output_aliases={0: 0},
        compiler_params=pltpu.CompilerParams(
            dimension_semantics=("arbitrary",), vmem_limit_bytes=VMEM_LIMIT_BYTES),
        name="odd_mixer_sample",
    )(x, gain.reshape(1, D_MODEL), w_in, g_v.reshape(1, SGU_WIDTH), coef, bias, w_out)


def _to_step_major(a):
    return jnp.swapaxes(a, 0, 1).reshape(SAMPLE_TOK, a.shape[-1])


def _to_batch_major(a):
    return jnp.swapaxes(a.reshape(DEC_SEQ, DEC_BATCH, a.shape[-1]), 0, 1)


def kernel(x_prompt, x_sample, state_win_k, state_win_v, state_pool, rel_bias, norm_gains, final_gain,
           ffn_gate, ffn_up, ffn_down, w_in_even, w_out_even, attn_sinks, w_pool, pool_scale,
           w_in_odd, sgu_norm, w_spatial, b_spatial, w_out_odd):
    xp = x_prompt.reshape(PROMPT_TOK, D_MODEL)
    xs = _to_step_major(x_sample)

    kp_l, vp_l, pp_l, ks_l, vs_l, ps_l, sv_l = [], [], [], [], [], [], []
    x = None
    y_prompt = y_sample = None
    for l in range(DEPTH):
        fa = (norm_gains[l, 0], ffn_gate, ffn_up, ffn_down, l, 0)
        fb = (norm_gains[l, 2], ffn_gate, ffn_up, ffn_down, l, 1)
        if l == 0:
            x = _ffn(xp, *fa, tail_in=xs)
        else:
            x = _ffn(x, *fa)
        g1 = norm_gains[l, 1]
        if l % 2 == 0:
            e = l // 2
            w_in, w_out = w_in_even[e].astype(BF16), w_out_even[e].astype(BF16)
            wp = w_pool[e].astype(BF16)
            x, kp, vp, pp = _even_prompt(x, g1, w_in, w_out, wp, pool_scale[e], rel_bias, attn_sinks[e])
            sk = state_win_k[e].reshape(DEC_BATCH, WINDOW, KV_WIDTH)
            sv = state_win_v[e].reshape(DEC_BATCH, WINDOW, KV_WIDTH)
            x, kn, vn, un = _even_sample(x, g1, w_in, w_out, wp, pool_scale[e], rel_bias, attn_sinks[e],
                                         sk, sv, jnp.swapaxes(state_pool[e], 0, 1))
            kp_l.append(kp.reshape(BATCH, WINDOW, N_KV_A, HEAD_DIM))
            vp_l.append(vp.reshape(BATCH, WINDOW, N_KV_A, HEAD_DIM))
            pp_l.append(pp[:, 2 * SUBLANES - POOL_BUF:, :])
            ks_l.append(jnp.concatenate(
                [state_win_k[e][:, DEC_SEQ:], _to_batch_major(kn).reshape(DEC_BATCH, DEC_SEQ, N_KV_A, HEAD_DIM)],
                axis=1))
            vs_l.append(jnp.concatenate(
                [state_win_v[e][:, DEC_SEQ:], _to_batch_major(vn).reshape(DEC_BATCH, DEC_SEQ, N_KV_A, HEAD_DIM)],
                axis=1))
            ps_l.append(jnp.concatenate([state_pool[e][:, DEC_SEQ:], _to_batch_major(un)], axis=1))
        else:
            o = l // 2
            w_in, w_out = w_in_odd[o].astype(BF16), w_out_odd[o].astype(BF16)
            x = _odd_prompt(x, g1, w_in, sgu_norm[o], w_spatial[o], b_spatial[o], w_out)
            x, sv_new = _odd_sample(x, g1, w_in, sgu_norm[o], w_spatial[o], b_spatial[o], w_out)
            sv_l.append(_to_batch_major(sv_new))
        if l < DEPTH - 1:
            x = _ffn(x, *fb)
        else:
            y_prompt, y_sample = _ffn(x, *fb, final_gain=final_gain)

    return (y_prompt.reshape(BATCH, SEQ, D_MODEL), _to_batch_major(y_sample),
            jnp.stack(kp_l), jnp.stack(vp_l), jnp.stack(pp_l),
            jnp.stack(ks_l), jnp.stack(vs_l), jnp.stack(ps_l), jnp.stack(sv_l))
```

```python
import functools
import math

import numpy as np
import jax
import jax.numpy as jnp
from jax import lax
from jax.experimental import pallas as pl
from jax.experimental.pallas import tpu as pltpu

F32 = jnp.float32
BF16 = jnp.bfloat16

D_MODEL = 1024
BATCH = 8
SEQ = 2048
DEPTH = 2
DEC_BATCH = 128
DEC_SEQ = 4
PAST_LEN = 16384
N_HEADS_A = 8
N_KV_A = 2
HEAD_DIM = 64
GQA_GROUP = N_HEADS_A // N_KV_A
WINDOW = 128
ATTN_WIDTH = N_HEADS_A * HEAD_DIM
KV_WIDTH = N_KV_A * HEAD_DIM
N_BUCKETS = 32
MAX_DISTANCE = 128
POOL_WINDOWS = (2, 4, 8, 16)
N_POOL_GROUPS = len(POOL_WINDOWS)
POOL_GROUP = 128
POOL_WIDTH = N_POOL_GROUPS * POOL_GROUP
POOL_BUF = max(POOL_WINDOWS) - 1
IN_EVEN = ATTN_WIDTH + 2 * KV_WIDTH + POOL_WIDTH
CHUNK = 128
SGU_WIDTH = 1024
SGU_GROUPS = 4
SGU_GROUP_W = SGU_WIDTH // SGU_GROUPS
D_FF = 2816
EPS = 1e-6
NEG = -1e30

LANES = 128
SUBLANES = 8
VMEM_LIMIT_BYTES = 56 * 1024 * 1024

PROMPT_TOK = BATCH * SEQ
SAMPLE_TOK = DEC_BATCH * DEC_SEQ
N_TOK = PROMPT_TOK + SAMPLE_TOK

TM = SAMPLE_TOK
assert PROMPT_TOK % TM == 0
PROMPT_BLOCKS = PROMPT_TOK // TM
TILES_PER_SEQ = SEQ // TM
QBLK_PER_TILE = TM // WINDOW
FFN_TM = 2 * TM
assert PROMPT_TOK % FFN_TM == 0
FFN_PROMPT_STEPS = PROMPT_TOK // FFN_TM
ODD_TM = 1024
assert PROMPT_TOK % ODD_TM == 0 and ODD_TM % CHUNK == 0
FF_CHUNK = 256
N_FF_CHUNKS = D_FF // FF_CHUNK
POOL_HEAD = 2 * SUBLANES
assert all(w & (w - 1) == 0 and w <= POOL_HEAD for w in POOL_WINDOWS)
POOL_CTX = 32
SEQ_GROUP = 8
N_SEQ_GROUPS = DEC_BATCH // SEQ_GROUP
STATE_KEYS = SEQ_GROUP * WINDOW
NEW_KEYS = SEQ_GROUP * DEC_SEQ
SAMPLE_KEYS = 1152
SAMPLE_ROWS = DEC_SEQ * SEQ_GROUP


def _t5_bucket_np(dist):
    n = np.maximum(dist, 0)
    max_exact = N_BUCKETS // 2
    nf = np.maximum(n, 1).astype(np.float32)
    large = max_exact + (np.log(nf / np.float32(max_exact)) / np.float32(math.log(MAX_DISTANCE / max_exact))
                         * np.float32(N_BUCKETS - max_exact)).astype(np.int32)
    large = np.minimum(large, N_BUCKETS - 1)
    return np.where(n < max_exact, n, large).astype(np.int32)


def _prompt_bucket_table():
    qi = np.arange(WINDOW)[:, None]
    c = np.arange(WINDOW)[None, :]
    dist = np.where(c > qi, qi + WINDOW - c, qi - c)
    return _t5_bucket_np(dist)


def _sample_bucket_table():
    t = (np.arange(SAMPLE_ROWS) // SEQ_GROUP)[:, None]
    b = (np.arange(SAMPLE_ROWS) % SEQ_GROUP)[:, None]
    col = np.arange(SAMPLE_KEYS)[None, :]
    is_state = col < STATE_KEYS
    is_new = (col >= STATE_KEYS) & (col < STATE_KEYS + NEW_KEYS)
    kb = np.where(is_state, col // WINDOW, (col - STATE_KEYS) % SEQ_GROUP)
    kpos = np.where(is_state, col % WINDOW, WINDOW + (col - STATE_KEYS) // SEQ_GROUP)
    dist = (WINDOW + t) - kpos
    valid = (is_state | is_new) & (kb == b) & (dist >= 0) & (dist < WINDOW)
    return np.where(valid, _t5_bucket_np(dist), -1).astype(np.int32)


def _rms(x, g):
    y = x * lax.rsqrt(jnp.mean(x * x, axis=-1, keepdims=True) + EPS)
    return y * g


LOG2E = math.log2(math.e)
Q_SCALE = HEAD_DIM ** -0.5 * LOG2E


def _bias_from_buckets(bucket, rb_ref, h):
    t = jnp.zeros(bucket.shape, F32)
    for bkt in range(N_BUCKETS):
        t = jnp.where(bucket == bkt, rb_ref[bkt, h] * LOG2E, t)
    return t


def _sink_softmax(s, sink):
    m = jnp.maximum(jnp.max(s, axis=-1, keepdims=True), sink)
    p = jnp.exp2(s - m)
    return p, jnp.sum(p, axis=-1, keepdims=True) + jnp.exp2(sink - m)


def _softmax_pv(s, valid, bias, sink, v):
    p, denom = _sink_softmax(jnp.where(valid, s + bias, NEG), sink)
    return jnp.dot(p.astype(BF16), v, preferred_element_type=F32) / denom


def _ffn_kernel(*refs, layer, half, tail_in, final_split):
    refs = list(refs)
    x_ref = refs.pop(0)
    xt_ref = refs.pop(0) if tail_in else None
    g_ref, wg_hbm, wu_hbm, wd_hbm = refs[:4]
    del refs[:4]
    fg_ref = refs.pop(0) if final_split else None
    o_ref = refs.pop(0)
    ot_ref = refs.pop(0) if final_split else None
    wg, wu, wd, stage_in, stage_dn, sem = refs
    i = pl.program_id(0)

    def load(rows):
        x = x_ref[rows, :]
        if tail_in:
            x = jnp.where(i < FFN_PROMPT_STEPS, x, xt_ref[...])
        return x, _rms(x, g_ref[...]).astype(BF16)

    def finish(rows, x, y):
        out = x + 0.5 * y
        if not final_split:
            o_ref[rows, :] = out
            return
        out = _rms(out, fg_ref[...])

        @pl.when(i < FFN_PROMPT_STEPS)
        def _():
            o_ref[rows, :] = out

        @pl.when(i == FFN_PROMPT_STEPS)
        def _():
            ot_ref[...] = out

    def chunk_copies(c, slot):
        cols = pl.ds(c * FF_CHUNK, FF_CHUNK)
        return (
            pltpu.make_async_copy(wg_hbm.at[layer, half, :, cols], stage_in.at[0, slot], sem.at[0, slot]),
            pltpu.make_async_copy(wu_hbm.at[layer, half, :, cols], stage_in.at[1, slot], sem.at[1, slot]),
            pltpu.make_async_copy(wd_hbm.at[layer, half, cols, :], stage_dn.at[slot], sem.at[2, slot]),
        )

    @pl.when(i == 0)
    def _():
        for c in range(min(2, N_FF_CHUNKS)):
            for cp in chunk_copies(c, c % 2):
                cp.start()
        rows = slice(0, TM)
        x, h = load(rows)
        acc = jnp.zeros((TM, D_MODEL), F32)
        for c in range(N_FF_CHUNKS):
            slot = c % 2
            cols = slice(c * FF_CHUNK, (c + 1) * FF_CHUNK)
            for cp in chunk_copies(c, slot):
                cp.wait()
            wg[:, cols] = stage_in[0, slot].astype(BF16)
            wu[:, cols] = stage_in[1, slot].astype(BF16)
            wd[cols, :] = stage_dn[slot].astype(BF16)
            if c + 2 < N_FF_CHUNKS:
                for cp in chunk_copies(c + 2, slot):
                    cp.start()
            gate = jnp.dot(h, wg[:, cols], preferred_element_type=F32)
            up = jnp.dot(h, wu[:, cols], preferred_element_type=F32)
            act = (jax.nn.silu(gate) * up).astype(BF16)
            acc = acc + jnp.dot(act, wd[cols, :], preferred_element_type=F32)
        finish(rows, x, acc)

    def half_block(hf, carry):
        rows = pl.ds(pl.multiple_of(hf * TM, TM), TM)
        x, h = load(rows)
        gate = jnp.dot(h, wg[...], preferred_element_type=F32)
        up = jnp.dot(h, wu[...], preferred_element_type=F32)
        act = (jax.nn.silu(gate) * up).astype(BF16)
        finish(rows, x, jnp.dot(act, wd[...], preferred_element_type=F32))
        return carry

    first = jnp.where(i == 0, 1, 0)
    stop = jnp.where(i < FFN_PROMPT_STEPS, FFN_TM // TM, 1)
    lax.fori_loop(first, stop, half_block, 0)


def _resident(shape):
    return pl.BlockSpec(shape, lambda *_: (0,) * len(shape), pipeline_mode=pl.Buffered(1))


def _ffn(x_in, gain, wg, wu, wd, layer, half, *, tail_in=None, final_gain=None):
    hbm = pl.BlockSpec(memory_space=pl.ANY)
    last_prompt = FFN_PROMPT_STEPS - 1
    row_block = pl.BlockSpec((FFN_TM, D_MODEL), lambda i: (i, 0))
    prompt_block = pl.BlockSpec((FFN_TM, D_MODEL), lambda i: (jnp.minimum(i, last_prompt), 0))
    sample_block = pl.BlockSpec((TM, D_MODEL), lambda i: (0, 0))
    if tail_in is None:
        args, in_specs = [x_in], [row_block]
    else:
        args, in_specs = [x_in, tail_in], [prompt_block, sample_block]
    args += [gain.reshape(1, D_MODEL), wg, wu, wd]
    in_specs += [_resident((1, D_MODEL)), hbm, hbm, hbm]
    if final_gain is None:
        out_shape = jax.ShapeDtypeStruct((N_TOK, D_MODEL), F32)
        out_specs = row_block
    else:
        args.append(final_gain.reshape(1, D_MODEL))
        in_specs.append(_resident((1, D_MODEL)))
        out_shape = (jax.ShapeDtypeStruct((PROMPT_TOK, D_MODEL), F32),
                     jax.ShapeDtypeStruct((SAMPLE_TOK, D_MODEL), F32))
        out_specs = (prompt_block, sample_block)
    aliases = {0: 0} if (tail_in is None and final_gain is None) else {}
    return pl.pallas_call(
        functools.partial(_ffn_kernel, layer=layer, half=half, tail_in=tail_in is not None,
                          final_split=final_gain is not None),
        out_shape=out_shape,
        grid=(FFN_PROMPT_STEPS + 1,),
        in_specs=in_specs,
        out_specs=out_specs,
        scratch_shapes=[
            pltpu.VMEM((D_MODEL, D_FF), BF16),
            pltpu.VMEM((D_MODEL, D_FF), BF16),
            pltpu.VMEM((D_FF, D_MODEL), BF16),
            pltpu.VMEM((2, 2, D_MODEL, FF_CHUNK), F32),
            pltpu.VMEM((2, FF_CHUNK, D_MODEL), F32),
            pltpu.SemaphoreType.DMA((3, 2)),
        ],
        input_output_aliases=aliases,
        compiler_params=pltpu.CompilerParams(
            dimension_semantics=("arbitrary",), vmem_limit_bytes=VMEM_LIMIT_BYTES),
        name="macaron_half",
    )(*args)


def _even_prompt_kernel(x_ref, g_ref, win_ref, wout_ref, wpool_ref, pscale_ref, bucket_ref, rb_ref, sink_ref,
                        o_ref, klast_ref, vlast_ref, plast_ref,
                        kbuf, kbuf_r, vbuf, vbuf_r, zb, s2b, s4b, s8b, bias_ref, cat_ref):
    b = pl.program_id(0)
    s = pl.program_id(1)
    last = pl.num_programs(1) - 1

    @pl.when((b == 0) & (s == 0))
    def _():
        bucket = bucket_ref[...]
        for h in range(N_HEADS_A):
            bias_ref[h] = _bias_from_buckets(bucket, rb_ref, h)

    @pl.when(s == 0)
    def _():
        for buf in (kbuf, kbuf_r, vbuf, vbuf_r):
            buf[0:WINDOW, :] = jnp.zeros((WINDOW, KV_WIDTH), BF16)
        zb[0:POOL_CTX, :] = jnp.zeros((POOL_CTX, POOL_WIDTH), F32)

    @pl.when(s > 0)
    def _():
        for buf in (kbuf, kbuf_r, vbuf, vbuf_r):
            buf[0:WINDOW, :] = buf[TM:TM + WINDOW, :]
        zb[0:POOL_CTX, :] = zb[TM:TM + POOL_CTX, :]

    x = x_ref[...]
    h = _rms(x, g_ref[...]).astype(BF16)
    proj = jnp.dot(h, win_ref[...], preferred_element_type=F32)
    q = proj[:, 0:ATTN_WIDTH] * Q_SCALE
    k = proj[:, ATTN_WIDTH:ATTN_WIDTH + KV_WIDTH]
    v = proj[:, ATTN_WIDTH + KV_WIDTH:ATTN_WIDTH + 2 * KV_WIDTH]
    up = proj[:, ATTN_WIDTH + 2 * KV_WIDTH:IN_EVEN]

    qlane = lax.broadcasted_iota(jnp.int32, (TM, ATTN_WIDTH), 1) % LANES
    q_half = (jnp.where(qlane < HEAD_DIM, q, 0.0).astype(BF16),
              jnp.where(qlane >= HEAD_DIM, q, 0.0).astype(BF16))
    kbuf[WINDOW:, :] = k.astype(BF16)
    vbuf[WINDOW:, :] = v.astype(BF16)
    kbuf_r[WINDOW:, :] = pltpu.roll(k, HEAD_DIM, 1).astype(BF16)
    vbuf_r[WINDOW:, :] = pltpu.roll(v, HEAD_DIM, 1).astype(BF16)

    qrow = lax.broadcasted_iota(jnp.int32, (WINDOW, WINDOW), 0)
    kcol = lax.broadcasted_iota(jnp.int32, (WINDOW, WINDOW), 1)
    prev = kcol > qrow
    olane = lax.broadcasted_iota(jnp.int32, (WINDOW, LANES), 1)
    for i in range(QBLK_PER_TILE):
        rows = slice(i * WINDOW, (i + 1) * WINDOW)
        keys = slice(i * WINDOW, (i + 2) * WINDOW)
        for j in range(N_HEADS_A // 2):
            halves = []
            for half in range(2):
                hd = 2 * j + half
                kvh = hd // GQA_GROUP
                kk = (kbuf if kvh == half else kbuf_r)[keys, :]
                vv = (vbuf if kvh == half else vbuf_r)[keys, :]
                qh = q_half[half][rows, j * LANES:(j + 1) * LANES]
                sc = lax.dot_general(qh, kk, (((1,), (1,)), ((), ())), preferred_element_type=F32)
                sc = jnp.where(prev, sc[:, 0:WINDOW], sc[:, WINDOW:]) + bias_ref[hd]
                if i == 0:
                    sc = jnp.where(prev & (qrow >= jnp.where(s > 0, WINDOW, 0)), NEG, sc)
                p, denom = _sink_softmax(sc, sink_ref[hd] * LOG2E)
                pcat = jnp.concatenate([jnp.where(prev, p, 0.0), jnp.where(prev, 0.0, p)], axis=1)
                halves.append(jnp.dot(pcat.astype(BF16), vv, preferred_element_type=F32) / denom)
            cat_ref[rows, j * LANES:(j + 1) * LANES] = jnp.where(
                olane < HEAD_DIM, halves[0], halves[1]).astype(BF16)

    zb[POOL_CTX:, :] = up
    n = TM + POOL_CTX
    s2b[8:, :] = zb[8:n, :] + zb[7:n - 1, :]
    s4b[16:, :] = s2b[16:n, POOL_GROUP:] + s2b[14:n - 2, POOL_GROUP:]
    s8b[24:, :] = s4b[24:n, POOL_GROUP:] + s4b[20:n - 4, POOL_GROUP:]
    s16 = s8b[32:n, POOL_GROUP:] + s8b[24:n - 8, POOL_GROUP:]
    sums = (s2b[POOL_CTX:, 0:POOL_GROUP], s4b[POOL_CTX:, 0:POOL_GROUP], s8b[POOL_CTX:, 0:POOL_GROUP], s16)
    pos1 = lax.broadcasted_iota(jnp.int32, (POOL_HEAD, POOL_GROUP), 0) + s * TM + 1
    for g, w in enumerate(POOL_WINDOWS):
        lanes = slice(g * POOL_GROUP, (g + 1) * POOL_GROUP)
        cnt = jnp.minimum(pos1, w).astype(F32)
        mean = jnp.concatenate([sums[g][0:POOL_HEAD] / cnt, sums[g][POOL_HEAD:] * (1.0 / w)], axis=0)
        d = (mean - up[:, lanes]).astype(BF16)
        y = jnp.dot(d, wpool_ref[g], preferred_element_type=F32) * pscale_ref[:, lanes]
        cat_ref[:, ATTN_WIDTH + g * POOL_GROUP:ATTN_WIDTH + (g + 1) * POOL_GROUP] = y.astype(BF16)

    o_ref[...] = x + jnp.dot(cat_ref[...], wout_ref[...], preferred_element_type=F32)

    @pl.when(s == last)
    def _():
        klast_ref[0] = k[TM - WINDOW:, :]
        vlast_ref[0] = v[TM - WINDOW:, :]
        plast_ref[0] = up[TM - 2 * SUBLANES:, :]


def _even_prompt(x, gain, w_in, w_out, w_pool, pool_scale, rel_bias, sinks):
    bucket = jnp.asarray(_prompt_bucket_table())
    smem = pl.BlockSpec(memory_space=pltpu.SMEM)
    return pl.pallas_call(
        _even_prompt_kernel,
        out_shape=(
            jax.ShapeDtypeStruct((N_TOK, D_MODEL), F32),
            jax.ShapeDtypeStruct((BATCH, WINDOW, KV_WIDTH), F32),
            jax.ShapeDtypeStruct((BATCH, WINDOW, KV_WIDTH), F32),
            jax.ShapeDtypeStruct((BATCH, 2 * SUBLANES, POOL_WIDTH), F32),
        ),
        grid=(BATCH, TILES_PER_SEQ),
        in_specs=[
            pl.BlockSpec((TM, D_MODEL), lambda b, s: (b * TILES_PER_SEQ + s, 0)),
            _resident((1, D_MODEL)),
            _resident((D_MODEL, IN_EVEN)),
            _resident((ATTN_WIDTH + POOL_WIDTH, D_MODEL)),
            _resident((N_POOL_GROUPS, POOL_GROUP, POOL_GROUP)),
            _resident((1, POOL_WIDTH)),
            _resident((WINDOW, WINDOW)),
            smem,
            smem,
        ],
        out_specs=(
            pl.BlockSpec((TM, D_MODEL), lambda b, s: (b * TILES_PER_SEQ + s, 0)),
            pl.BlockSpec((1, WINDOW, KV_WIDTH), lambda b, s: (b, 0, 0)),
            pl.BlockSpec((1, WINDOW, KV_WIDTH), lambda b, s: (b, 0, 0)),
            pl.BlockSpec((1, 2 * SUBLANES, POOL_WIDTH), lambda b, s: (b, 0, 0)),
        ),
        scratch_shapes=[
            pltpu.VMEM((WINDOW + TM, KV_WIDTH), BF16),
            pltpu.VMEM((WINDOW + TM, KV_WIDTH), BF16),
            pltpu.VMEM((WINDOW + TM, KV_WIDTH), BF16),
            pltpu.VMEM((WINDOW + TM, KV_WIDTH), BF16),
            pltpu.VMEM((POOL_CTX + TM, POOL_WIDTH), F32),
            pltpu.VMEM((POOL_CTX + TM, POOL_WIDTH), F32),
            pltpu.VMEM((POOL_CTX + TM, POOL_WIDTH - POOL_GROUP), F32),
            pltpu.VMEM((POOL_CTX + TM, POOL_WIDTH - 2 * POOL_GROUP), F32),
            pltpu.VMEM((N_HEADS_A, WINDOW, WINDOW), F32),
            pltpu.VMEM((TM, ATTN_WIDTH + POOL_WIDTH), BF16),
        ],
        input_output_aliases={0: 0},
        compiler_params=pltpu.CompilerParams(
            dimension_semantics=("arbitrary", "arbitrary"), vmem_limit_bytes=VMEM_LIMIT_BYTES),
        name="even_mixer_prompt",
    )(x, gain.reshape(1, D_MODEL), w_in, w_out, w_pool, pool_scale.reshape(1, POOL_WIDTH), bucket,
      rel_bias, sinks)


def _even_sample_kernel(x_ref, g_ref, win_ref, wout_ref, wpool_ref, pscale_ref, bucket_ref, rb_ref, sink_ref,
                        sk_ref, sv_ref, spool_ref,
                        o_ref, nk_ref, nv_ref, npool_ref,
                        knew_ref, vnew_ref, upnew_ref, qsel, bias_ref, attn_f32, cat_ref):
    g = pl.program_id(0)
    last = pl.num_programs(0) - 1
    olane = lax.broadcasted_iota(jnp.int32, (SAMPLE_ROWS, LANES), 1)

    @pl.when(g == 0)
    def _():
        bucket = bucket_ref[...]
        for h in range(N_HEADS_A):
            bias_ref[h] = _bias_from_buckets(bucket, rb_ref, h)
        x = x_ref[...]
        h = _rms(x, g_ref[...]).astype(BF16)
        proj = jnp.dot(h, win_ref[...], preferred_element_type=F32)
        q = proj[:, 0:ATTN_WIDTH] * Q_SCALE
        knew_ref[...] = proj[:, ATTN_WIDTH:ATTN_WIDTH + KV_WIDTH]
        vnew_ref[...] = proj[:, ATTN_WIDTH + KV_WIDTH:ATTN_WIDTH + 2 * KV_WIDTH]
        upnew_ref[...] = proj[:, ATTN_WIDTH + 2 * KV_WIDTH:IN_EVEN]
        qlane = lax.broadcasted_iota(jnp.int32, (SAMPLE_TOK, LANES), 1)
        for hd in range(N_HEADS_A):
            j, half, kvh = hd // 2, hd % 2, hd // GQA_GROUP
            blk = q[:, j * LANES:(j + 1) * LANES]
            if half != kvh:
                blk = pltpu.roll(blk, HEAD_DIM, 1)
            keep = (qlane < HEAD_DIM) if kvh == 0 else (qlane >= HEAD_DIM)
            qsel[hd] = jnp.where(keep, blk, 0.0)

    row0 = pl.multiple_of(g * SEQ_GROUP, SEQ_GROUP)
    pad = [jnp.zeros((WINDOW - NEW_KEYS, KV_WIDTH), F32)]
    knew = jnp.concatenate(
        [knew_ref[pl.ds(t * DEC_BATCH + row0, SEQ_GROUP), :] for t in range(DEC_SEQ)] + pad, axis=0)
    vnew = jnp.concatenate(
        [vnew_ref[pl.ds(t * DEC_BATCH + row0, SEQ_GROUP), :] for t in range(DEC_SEQ)] + pad, axis=0)
    kt_all = jnp.concatenate([sk_ref[b] for b in range(SEQ_GROUP)], axis=1).astype(BF16)
    vt_all = jnp.concatenate([sv_ref[b] for b in range(SEQ_GROUP)], axis=1).astype(BF16)

    qg = jnp.concatenate(
        [qsel[hd, pl.ds(t * DEC_BATCH + row0, SEQ_GROUP), :] for hd in range(N_HEADS_A) for t in range(DEC_SEQ)],
        axis=0).astype(BF16)
    nt = (((1,), (1,)), ((), ()))
    sc = jnp.concatenate(
        [jnp.dot(qg, kt_all, preferred_element_type=F32),
         lax.dot_general(qg, knew.astype(BF16), nt, preferred_element_type=F32)], axis=1)
    valid = jnp.concatenate([bucket_ref[...]] * N_HEADS_A, axis=0) >= 0
    bias = bias_ref[...].reshape(N_HEADS_A * SAMPLE_ROWS, SAMPLE_KEYS)
    sink = jnp.concatenate(
        [jnp.full((SAMPLE_ROWS, 1), sink_ref[hd] * LOG2E, F32) for hd in range(N_HEADS_A)], axis=0)
    p, denom = _sink_softmax(jnp.where(valid, sc + bias, NEG), sink)
    p = p.astype(BF16)
    o_all = (lax.dot_general(p[:, 0:STATE_KEYS], vt_all, nt, preferred_element_type=F32)
             + jnp.dot(p[:, STATE_KEYS:], vnew.astype(BF16), preferred_element_type=F32)) / denom
    for j in range(N_HEADS_A // 2):
        halves = []
        for half in range(2):
            hd = 2 * j + half
            o = o_all[hd * SAMPLE_ROWS:(hd + 1) * SAMPLE_ROWS, :]
            if hd // GQA_GROUP != half:
                o = pltpu.roll(o, HEAD_DIM, 1)
            halves.append(o)
        blk = jnp.where(olane < HEAD_DIM, halves[0], halves[1])
        for t in range(DEC_SEQ):
            attn_f32[pl.ds(t * DEC_BATCH + row0, SEQ_GROUP), j * LANES:(j + 1) * LANES] = (
                blk[t * SEQ_GROUP:(t + 1) * SEQ_GROUP, :])

    lane = lax.broadcasted_iota(jnp.int32, (KV_WIDTH, WINDOW), 1)
    for new, old_ref, out_ref in ((knew, sk_ref, nk_ref), (vnew, sv_ref, nv_ref)):
        new_t = new.T
        for b in range(SEQ_GROUP):
            win = pltpu.roll(old_ref[b], WINDOW - DEC_SEQ, 1)
            for t in range(DEC_SEQ):
                dst = WINDOW - DEC_SEQ + t
                win = jnp.where(lane == dst, pltpu.roll(new_t, (dst - (t * SEQ_GROUP + b)) % WINDOW, 1), win)
            out_ref[b] = win

    @pl.when(g == last)
    def _():
        cat_ref[:, 0:ATTN_WIDTH] = attn_f32[...].astype(BF16)
        for r in range(POOL_BUF):
            if r < POOL_BUF - DEC_SEQ:
                npool_ref[r] = spool_ref[r + DEC_SEQ]
            else:
                t = r - (POOL_BUF - DEC_SEQ)
                npool_ref[r] = upnew_ref[t * DEC_BATCH:(t + 1) * DEC_BATCH, :]
        for t in range(DEC_SEQ):
            rows = slice(t * DEC_BATCH, (t + 1) * DEC_BATCH)
            for gi, w in enumerate(POOL_WINDOWS):
                lanes = slice(gi * POOL_GROUP, (gi + 1) * POOL_GROUP)
                acc = upnew_ref[rows, lanes]
                for back in range(1, w):
                    tt = t - back
                    if tt >= 0:
                        acc = acc + upnew_ref[tt * DEC_BATCH:(tt + 1) * DEC_BATCH, lanes]
                    else:
                        acc = acc + spool_ref[POOL_BUF + tt, :, lanes]
                cnt = float(min(PAST_LEN + t + 1, w))
                d = (acc / cnt - upnew_ref[rows, lanes]).astype(BF16)
                y = jnp.dot(d, wpool_ref[gi], preferred_element_type=F32) * pscale_ref[:, lanes]
                cat_ref[rows, ATTN_WIDTH + gi * POOL_GROUP:ATTN_WIDTH + (gi + 1) * POOL_GROUP] = y.astype(BF16)
        o_ref[...] = x_ref[...] + jnp.dot(cat_ref[...], wout_ref[...], preferred_element_type=F32)


def _even_sample(x, gain, w_in, w_out, w_pool, pool_scale, rel_bias, sinks, state_k, state_v, state_pool_t):
    bucket = jnp.asarray(_sample_bucket_table())
    smem = pl.BlockSpec(memory_space=pltpu.SMEM)
    sample_block = PROMPT_TOK // SAMPLE_TOK
    return pl.pallas_call(
        _even_sample_kernel,
        out_shape=(
            jax.ShapeDtypeStruct((N_TOK, D_MODEL), F32),
            jax.ShapeDtypeStruct((DEC_BATCH, KV_WIDTH, WINDOW), F32),
            jax.ShapeDtypeStruct((DEC_BATCH, KV_WIDTH, WINDOW), F32),
            jax.ShapeDtypeStruct((POOL_BUF, DEC_BATCH, POOL_WIDTH), F32),
        ),
        grid=(N_SEQ_GROUPS,),
        in_specs=[
            pl.BlockSpec((SAMPLE_TOK, D_MODEL), lambda g: (sample_block, 0)),
            _resident((1, D_MODEL)),
            _resident((D_MODEL, IN_EVEN)),
            _resident((ATTN_WIDTH + POOL_WIDTH, D_MODEL)),
            _resident((N_POOL_GROUPS, POOL_GROUP, POOL_GROUP)),
            _resident((1, POOL_WIDTH)),
            _resident((SAMPLE_ROWS, SAMPLE_KEYS)),
            smem,
            smem,
            pl.BlockSpec((SEQ_GROUP, KV_WIDTH, WINDOW), lambda g: (g, 0, 0)),
            pl.BlockSpec((SEQ_GROUP, KV_WIDTH, WINDOW), lambda g: (g, 0, 0)),
            _resident((POOL_BUF, DEC_BATCH, POOL_WIDTH)),
        ],
        out_specs=(
            pl.BlockSpec((SAMPLE_TOK, D_MODEL), lambda g: (sample_block, 0)),
            pl.BlockSpec((SEQ_GROUP, KV_WIDTH, WINDOW), lambda g: (g, 0, 0)),
            pl.BlockSpec((SEQ_GROUP, KV_WIDTH, WINDOW), lambda g: (g, 0, 0)),
            pl.BlockSpec((POOL_BUF, DEC_BATCH, POOL_WIDTH), lambda g: (0, 0, 0)),
        ),
        scratch_shapes=[
            pltpu.VMEM((SAMPLE_TOK, KV_WIDTH), F32),
            pltpu.VMEM((SAMPLE_TOK, KV_WIDTH), F32),
            pltpu.VMEM((SAMPLE_TOK, POOL_WIDTH), F32),
            pltpu.VMEM((N_HEADS_A, SAMPLE_TOK, LANES), F32),
            pltpu.VMEM((N_HEADS_A, SAMPLE_ROWS, SAMPLE_KEYS), F32),
            pltpu.VMEM((SAMPLE_TOK, ATTN_WIDTH), F32),
            pltpu.VMEM((SAMPLE_TOK, ATTN_WIDTH + POOL_WIDTH), BF16),
        ],
        input_output_aliases={0: 0},
        compiler_params=pltpu.CompilerParams(
            dimension_semantics=("arbitrary",), vmem_limit_bytes=VMEM_LIMIT_BYTES),
        name="even_mixer_sample",
    )(x, gain.reshape(1, D_MODEL), w_in, w_out, w_pool, pool_scale.reshape(1, POOL_WIDTH), bucket,
      rel_bias, sinks, state_k, state_v, state_pool_t)


def _sgu_front(x, g_ref, win_ref, gv_ref):
    h = _rms(x, g_ref[...]).astype(BF16)
    uv = jax.nn.gelu(jnp.dot(h, win_ref[...], preferred_element_type=F32))
    return uv[:, 0:SGU_WIDTH], _rms(uv[:, SGU_WIDTH:], gv_ref[...])


def _odd_prompt_kernel(x_ref, g_ref, win_ref, gv_ref, ws_ref, bs_ref, wout_ref, o_ref, gated_ref):
    x = x_ref[...]
    u, v = _sgu_front(x, g_ref, win_ref, gv_ref)
    vb = v.astype(BF16)
    r = lax.broadcasted_iota(jnp.int32, (CHUNK, CHUNK), 0)
    c = lax.broadcasted_iota(jnp.int32, (CHUNK, CHUNK), 1)
    for gi in range(SGU_GROUPS):
        w = jnp.where(r >= c, ws_ref[gi], 0.0).astype(BF16)
        bias = jnp.concatenate([bs_ref[gi]] * (SGU_GROUP_W // LANES), axis=1)
        lanes = slice(gi * SGU_GROUP_W, (gi + 1) * SGU_GROUP_W)
        for ci in range(ODD_TM // CHUNK):
            rows = slice(ci * CHUNK, (ci + 1) * CHUNK)
            mixed = jnp.dot(w, vb[rows, lanes], preferred_element_type=F32) + bias
            gated_ref[rows, lanes] = (u[rows, lanes] * mixed).astype(BF16)
    o_ref[...] = x + jnp.dot(gated_ref[...], wout_ref[...], preferred_element_type=F32)


def _odd_prompt(x, gain, w_in, g_v, w_s, b_s, w_out):
    bias = jnp.broadcast_to(b_s[:, :, None], (SGU_GROUPS, CHUNK, LANES))
    return pl.pallas_call(
        _odd_prompt_kernel,
        out_shape=jax.ShapeDtypeStruct((N_TOK, D_MODEL), F32),
        grid=(PROMPT_TOK // ODD_TM,),
        in_specs=[
            pl.BlockSpec((ODD_TM, D_MODEL), lambda i: (i, 0)),
            _resident((1, D_MODEL)),
            _resident((D_MODEL, 2 * SGU_WIDTH)),
            _resident((1, SGU_WIDTH)),
            _resident((SGU_GROUPS, CHUNK, CHUNK)),
            _resident((SGU_GROUPS, CHUNK, LANES)),
            _resident((SGU_WIDTH, D_MODEL)),
        ],
        out_specs=pl.BlockSpec((ODD_TM, D_MODEL), lambda i: (i, 0)),
        scratch_shapes=[pltpu.VMEM((ODD_TM, SGU_WIDTH), BF16)],
        input_output_aliases={0: 0},
        compiler_params=pltpu.CompilerParams(
            dimension_semantics=("arbitrary",), vmem_limit_bytes=VMEM_LIMIT_BYTES),
        name="odd_mixer_prompt",
    )(x, gain.reshape(1, D_MODEL), w_in, g_v.reshape(1, SGU_WIDTH), w_s, bias, w_out)


def _odd_sample_kernel(x_ref, g_ref, win_ref, gv_ref, coef_ref, bias_ref, wout_ref, o_ref, v_ref):
    x = x_ref[...]
    u, v = _sgu_front(x, g_ref, win_ref, gv_ref)
    v_ref[...] = v
    gated = []
    for t in range(DEC_SEQ):
        mixed = bias_ref[t:t + 1, :]
        for s in range(t + 1):
            mixed = mixed + coef_ref[t, s:s + 1, :] * v[s * DEC_BATCH:(s + 1) * DEC_BATCH, :]
        gated.append((u[t * DEC_BATCH:(t + 1) * DEC_BATCH, :] * mixed).astype(BF16))
    o_ref[...] = x + jnp.dot(jnp.concatenate(gated, axis=0), wout_ref[...], preferred_element_type=F32)


def _odd_sample(x, gain, w_in, g_v, w_s, b_s, w_out):
    coef = jnp.repeat(jnp.transpose(w_s[:, :DEC_SEQ, :DEC_SEQ], (1, 2, 0)), SGU_GROUP_W, axis=-1)
    bias = jnp.repeat(b_s[:, :DEC_SEQ].T, SGU_GROUP_W, axis=-1)
    sample_block = PROMPT_TOK // SAMPLE_TOK
    return pl.pallas_call(
        _odd_sample_kernel,
        out_shape=(
            jax.ShapeDtypeStruct((N_TOK, D_MODEL), F32),
            jax.ShapeDtypeStruct((SAMPLE_TOK, SGU_WIDTH), F32),
        ),
        grid=(1,),
        in_specs=[
            pl.BlockSpec((SAMPLE_TOK, D_MODEL), lambda i: (sample_block, 0)),
            _resident((1, D_MODEL)),
            _resident((D_MODEL, 2 * SGU_WIDTH)),
            _resident((1, SGU_WIDTH)),
            _resident((DEC_SEQ, DEC_SEQ, SGU_WIDTH)),
            _resident((DEC_SEQ, SGU_WIDTH)),
            _resident((SGU_WIDTH, D_MODEL)),
        ],
        out_specs=(
            pl.BlockSpec((SAMPLE_TOK, D_MODEL), lambda i: (sample_block, 0)),
            pl.BlockSpec((SAMPLE_TOK, SGU_WIDTH), lambda i: (0, 0)),
        ),
        input_output_aliases={0: 0},
        compiler_params=pltpu.CompilerParams(
            dimension_semantics=("arbitrary",), vmem_limit_bytes=VMEM_LIMIT_BYTES),
        name="odd_mixer_sample",
    )(x, gain.reshape(1, D_MODEL), w_in, g_v.reshape(1, SGU_WIDTH), coef, bias, w_out)


def _to_step_major(a):
    return jnp.swapaxes(a, 0, 1).reshape(SAMPLE_TOK, a.shape[-1])


def _to_batch_major(a):
    return jnp.swapaxes(a.reshape(DEC_SEQ, DEC_BATCH, a.shape[-1]), 0, 1)


def _window_to_lanes(w):
    return jnp.transpose(w, (0, 2, 3, 1)).reshape(DEC_BATCH, KV_WIDTH, WINDOW)


def _window_from_lanes(w):
    return jnp.transpose(w.reshape(DEC_BATCH, N_KV_A, HEAD_DIM, WINDOW), (0, 3, 1, 2))


def kernel(x_prompt, x_sample, state_win_k, state_win_v, state_pool, rel_bias, norm_gains, final_gain,
           ffn_gate, ffn_up, ffn_down, w_in_even, w_out_even, attn_sinks, w_pool, pool_scale,
           w_in_odd, sgu_norm, w_spatial, b_spatial, w_out_odd):
    xp = x_prompt.reshape(PROMPT_TOK, D_MODEL)
    xs = _to_step_major(x_sample)

    kp_l, vp_l, pp_l, ks_l, vs_l, ps_l, sv_l = [], [], [], [], [], [], []
    x = None
    y_prompt = y_sample = None
    for l in range(DEPTH):
        fa = (norm_gains[l, 0], ffn_gate, ffn_up, ffn_down, l, 0)
        fb = (norm_gains[l, 2], ffn_gate, ffn_up, ffn_down, l, 1)
        if l == 0:
            x = _ffn(xp, *fa, tail_in=xs)
        else:
            x = _ffn(x, *fa)
        g1 = norm_gains[l, 1]
        if l % 2 == 0:
            e = l // 2
            w_in, w_out = w_in_even[e].astype(BF16), w_out_even[e].astype(BF16)
            wp = w_pool[e].astype(BF16)
            x, kp, vp, pp = _even_prompt(x, g1, w_in, w_out, wp, pool_scale[e], rel_bias, attn_sinks[e])
            x, nk, nv, npool = _even_sample(
                x, g1, w_in, w_out, wp, pool_scale[e], rel_bias, attn_sinks[e],
                _window_to_lanes(state_win_k[e]), _window_to_lanes(state_win_v[e]),
                jnp.swapaxes(state_pool[e], 0, 1))
            kp_l.append(kp.reshape(BATCH, WINDOW, N_KV_A, HEAD_DIM))
            vp_l.append(vp.reshape(BATCH, WINDOW, N_KV_A, HEAD_DIM))
            pp_l.append(pp[:, 2 * SUBLANES - POOL_BUF:, :])
            ks_l.append(_window_from_lanes(nk))
            vs_l.append(_window_from_lanes(nv))
            ps_l.append(jnp.swapaxes(npool, 0, 1))
        else:
            o = l // 2
            w_in, w_out = w_in_odd[o].astype(BF16), w_out_odd[o].astype(BF16)
            x = _odd_prompt(x, g1, w_in, sgu_norm[o], w_spatial[o], b_spatial[o], w_out)
            x, sv_new = _odd_sample(x, g1, w_in, sgu_norm[o], w_spatial[o], b_spatial[o], w_out)
            sv_l.append(_to_batch_major(sv_new))
        if l < DEPTH - 1:
            x = _ffn(x, *fb)
        else:
            y_prompt, y_sample = _ffn(x, *fb, final_gain=final_gain)

    return (y_prompt.reshape(BATCH, SEQ, D_MODEL), _to_batch_major(y_sample),
            jnp.stack(kp_l), jnp.stack(vp_l), jnp.stack(pp_l),
            jnp.stack(ks_l), jnp.stack(vs_l), jnp.stack(ps_l), jnp.stack(sv_l))
```

```python
import functools
import math

import numpy as np
import jax
import jax.numpy as jnp
from jax import lax
from jax.experimental import pallas as pl
from jax.experimental.pallas import tpu as pltpu

F32 = jnp.float32
BF16 = jnp.bfloat16

D_MODEL = 1024
BATCH = 8
SEQ = 2048
DEPTH = 2
DEC_BATCH = 128
DEC_SEQ = 4
PAST_LEN = 16384
N_HEADS_A = 8
N_KV_A = 2
HEAD_DIM = 64
GQA_GROUP = N_HEADS_A // N_KV_A
WINDOW = 128
ATTN_WIDTH = N_HEADS_A * HEAD_DIM
KV_WIDTH = N_KV_A * HEAD_DIM
N_BUCKETS = 32
MAX_DISTANCE = 128
POOL_WINDOWS = (2, 4, 8, 16)
N_POOL_GROUPS = len(POOL_WINDOWS)
POOL_GROUP = 128
POOL_WIDTH = N_POOL_GROUPS * POOL_GROUP
POOL_BUF = max(POOL_WINDOWS) - 1
IN_EVEN = ATTN_WIDTH + 2 * KV_WIDTH + POOL_WIDTH
CHUNK = 128
SGU_WIDTH = 1024
SGU_GROUPS = 4
SGU_GROUP_W = SGU_WIDTH // SGU_GROUPS
D_FF = 2816
EPS = 1e-6
NEG = -1e30

LANES = 128
SUBLANES = 8
VMEM_LIMIT_BYTES = 56 * 1024 * 1024

PROMPT_TOK = BATCH * SEQ
SAMPLE_TOK = DEC_BATCH * DEC_SEQ
N_TOK = PROMPT_TOK + SAMPLE_TOK

TM = SAMPLE_TOK
assert PROMPT_TOK % TM == 0
PROMPT_BLOCKS = PROMPT_TOK // TM
EVEN_TM = 1024
assert SEQ % EVEN_TM == 0 and EVEN_TM % WINDOW == 0
TILES_PER_SEQ = SEQ // EVEN_TM
QBLK_PER_TILE = EVEN_TM // WINDOW
FFN_TM = 2 * TM
assert PROMPT_TOK % FFN_TM == 0
FFN_PROMPT_STEPS = PROMPT_TOK // FFN_TM
ODD_TM = 1024
assert PROMPT_TOK % ODD_TM == 0 and ODD_TM % CHUNK == 0
FF_CHUNK = 256
N_FF_CHUNKS = D_FF // FF_CHUNK
POOL_HEAD = 2 * SUBLANES
assert all(w & (w - 1) == 0 and w <= POOL_HEAD for w in POOL_WINDOWS)
POOL_CTX = 32
SEQ_GROUP = 8
N_SEQ_GROUPS = DEC_BATCH // SEQ_GROUP
STATE_KEYS = SEQ_GROUP * WINDOW
NEW_KEYS = SEQ_GROUP * DEC_SEQ
SAMPLE_KEYS = 1152
SAMPLE_ROWS = DEC_SEQ * SEQ_GROUP


def _t5_bucket_np(dist):
    n = np.maximum(dist, 0)
    max_exact = N_BUCKETS // 2
    nf = np.maximum(n, 1).astype(np.float32)
    large = max_exact + (np.log(nf / np.float32(max_exact)) / np.float32(math.log(MAX_DISTANCE / max_exact))
                         * np.float32(N_BUCKETS - max_exact)).astype(np.int32)
    large = np.minimum(large, N_BUCKETS - 1)
    return np.where(n < max_exact, n, large).astype(np.int32)


def _prompt_bucket_table():
    qi = np.arange(WINDOW)[:, None]
    c = np.arange(WINDOW)[None, :]
    dist = np.where(c > qi, qi + WINDOW - c, qi - c)
    return _t5_bucket_np(dist)


def _sample_bucket_table():
    t = (np.arange(SAMPLE_ROWS) // SEQ_GROUP)[:, None]
    b = (np.arange(SAMPLE_ROWS) % SEQ_GROUP)[:, None]
    col = np.arange(SAMPLE_KEYS)[None, :]
    is_state = col < STATE_KEYS
    is_new = (col >= STATE_KEYS) & (col < STATE_KEYS + NEW_KEYS)
    kb = np.where(is_state, col // WINDOW, (col - STATE_KEYS) % SEQ_GROUP)
    kpos = np.where(is_state, col % WINDOW, WINDOW + (col - STATE_KEYS) // SEQ_GROUP)
    dist = (WINDOW + t) - kpos
    valid = (is_state | is_new) & (kb == b) & (dist >= 0) & (dist < WINDOW)
    return np.where(valid, _t5_bucket_np(dist), -1).astype(np.int32)


def _rms(x, g):
    y = x * lax.rsqrt(jnp.mean(x * x, axis=-1, keepdims=True) + EPS)
    return y * g


LOG2E = math.log2(math.e)
Q_SCALE = HEAD_DIM ** -0.5 * LOG2E


def _bias_from_buckets(bucket, rb_ref, h):
    t = jnp.zeros(bucket.shape, F32)
    for bkt in range(N_BUCKETS):
        t = jnp.where(bucket == bkt, rb_ref[bkt, h] * LOG2E, t)
    return t


def _sink_softmax(s, sink):
    m = jnp.maximum(jnp.max(s, axis=-1, keepdims=True), sink)
    p = jnp.exp2(s - m)
    return p, jnp.sum(p, axis=-1, keepdims=True) + jnp.exp2(sink - m)


def _softmax_pv(s, valid, bias, sink, v):
    p, denom = _sink_softmax(jnp.where(valid, s + bias, NEG), sink)
    return jnp.dot(p.astype(BF16), v, preferred_element_type=F32) / denom


def _ffn_kernel(*refs, layer, half, tail_in, final_split):
    refs = list(refs)
    x_ref = refs.pop(0)
    xt_ref = refs.pop(0) if tail_in else None
    g_ref, wg_hbm, wu_hbm, wd_hbm = refs[:4]
    del refs[:4]
    fg_ref = refs.pop(0) if final_split else None
    o_ref = refs.pop(0)
    ot_ref = refs.pop(0) if final_split else None
    wg, wu, wd, stage_in, stage_dn, sem = refs
    i = pl.program_id(0)

    def load(rows):
        x = x_ref[rows, :]
        if tail_in:
            x = jnp.where(i < FFN_PROMPT_STEPS, x, xt_ref[...])
        return x, _rms(x, g_ref[...]).astype(BF16)

    def finish(rows, x, y):
        out = x + 0.5 * y
        if not final_split:
            o_ref[rows, :] = out
            return
        out = _rms(out, fg_ref[...])

        @pl.when(i < FFN_PROMPT_STEPS)
        def _():
            o_ref[rows, :] = out

        @pl.when(i == FFN_PROMPT_STEPS)
        def _():
            ot_ref[...] = out

    def chunk_copies(c, slot):
        cols = pl.ds(c * FF_CHUNK, FF_CHUNK)
        return (
            pltpu.make_async_copy(wg_hbm.at[layer, half, :, cols], stage_in.at[0, slot], sem.at[0, slot]),
            pltpu.make_async_copy(wu_hbm.at[layer, half, :, cols], stage_in.at[1, slot], sem.at[1, slot]),
            pltpu.make_async_copy(wd_hbm.at[layer, half, cols, :], stage_dn.at[slot], sem.at[2, slot]),
        )

    @pl.when(i == 0)
    def _():
        for c in range(min(2, N_FF_CHUNKS)):
            for cp in chunk_copies(c, c % 2):
                cp.start()
        rows = slice(0, TM)
        x, h = load(rows)
        acc = jnp.zeros((TM, D_MODEL), F32)
        for c in range(N_FF_CHUNKS):
            slot = c % 2
            cols = slice(c * FF_CHUNK, (c + 1) * FF_CHUNK)
            for cp in chunk_copies(c, slot):
                cp.wait()
            wg[:, cols] = stage_in[0, slot].astype(BF16)
            wu[:, cols] = stage_in[1, slot].astype(BF16)
            wd[cols, :] = stage_dn[slot].astype(BF16)
            if c + 2 < N_FF_CHUNKS:
                for cp in chunk_copies(c + 2, slot):
                    cp.start()
            gate = jnp.dot(h, wg[:, cols], preferred_element_type=F32)
            up = jnp.dot(h, wu[:, cols], preferred_element_type=F32)
            act = (jax.nn.silu(gate) * up).astype(BF16)
            acc = acc + jnp.dot(act, wd[cols, :], preferred_element_type=F32)
        finish(rows, x, acc)

    def half_block(hf, carry):
        rows = pl.ds(pl.multiple_of(hf * TM, TM), TM)
        x, h = load(rows)
        gate = jnp.dot(h, wg[...], preferred_element_type=F32)
        up = jnp.dot(h, wu[...], preferred_element_type=F32)
        act = (jax.nn.silu(gate) * up).astype(BF16)
        finish(rows, x, jnp.dot(act, wd[...], preferred_element_type=F32))
        return carry

    first = jnp.where(i == 0, 1, 0)
    stop = jnp.where(i < FFN_PROMPT_STEPS, FFN_TM // TM, 1)
    lax.fori_loop(first, stop, half_block, 0)


def _resident(shape):
    return pl.BlockSpec(shape, lambda *_: (0,) * len(shape), pipeline_mode=pl.Buffered(1))


def _ffn(x_in, gain, wg, wu, wd, layer, half, *, tail_in=None, final_gain=None):
    hbm = pl.BlockSpec(memory_space=pl.ANY)
    last_prompt = FFN_PROMPT_STEPS - 1
    row_block = pl.BlockSpec((FFN_TM, D_MODEL), lambda i: (i, 0))
    prompt_block = pl.BlockSpec((FFN_TM, D_MODEL), lambda i: (jnp.minimum(i, last_prompt), 0))
    sample_block = pl.BlockSpec((TM, D_MODEL), lambda i: (0, 0))
    if tail_in is None:
        args, in_specs = [x_in], [row_block]
    else:
        args, in_specs = [x_in, tail_in], [prompt_block, sample_block]
    args += [gain.reshape(1, D_MODEL), wg, wu, wd]
    in_specs += [_resident((1, D_MODEL)), hbm, hbm, hbm]
    if final_gain is None:
        out_shape = jax.ShapeDtypeStruct((N_TOK, D_MODEL), F32)
        out_specs = row_block
    else:
        args.append(final_gain.reshape(1, D_MODEL))
        in_specs.append(_resident((1, D_MODEL)))
        out_shape = (jax.ShapeDtypeStruct((PROMPT_TOK, D_MODEL), F32),
                     jax.ShapeDtypeStruct((SAMPLE_TOK, D_MODEL), F32))
        out_specs = (prompt_block, sample_block)
    aliases = {0: 0} if (tail_in is None and final_gain is None) else {}
    return pl.pallas_call(
        functools.partial(_ffn_kernel, layer=layer, half=half, tail_in=tail_in is not None,
                          final_split=final_gain is not None),
        out_shape=out_shape,
        grid=(FFN_PROMPT_STEPS + 1,),
        in_specs=in_specs,
        out_specs=out_specs,
        scratch_shapes=[
            pltpu.VMEM((D_MODEL, D_FF), BF16),
            pltpu.VMEM((D_MODEL, D_FF), BF16),
            pltpu.VMEM((D_FF, D_MODEL), BF16),
            pltpu.VMEM((2, 2, D_MODEL, FF_CHUNK), F32),
            pltpu.VMEM((2, FF_CHUNK, D_MODEL), F32),
            pltpu.SemaphoreType.DMA((3, 2)),
        ],
        input_output_aliases=aliases,
        compiler_params=pltpu.CompilerParams(
            dimension_semantics=("arbitrary",), vmem_limit_bytes=VMEM_LIMIT_BYTES),
        name="macaron_half",
    )(*args)


def _even_prompt_kernel(x_ref, g_ref, win_ref, wout_ref, wpool_ref, pscale_ref, bucket_ref, rb_ref, sink_ref,
                        o_ref, klast_ref, vlast_ref, plast_ref,
                        kbuf, kbuf_r, vbuf, vbuf_r, zb, s2b, s4b, s8b, bias_ref, cat_ref):
    b = pl.program_id(0)
    s = pl.program_id(1)
    last = pl.num_programs(1) - 1

    @pl.when((b == 0) & (s == 0))
    def _():
        bucket = bucket_ref[...]
        for h in range(N_HEADS_A):
            bias_ref[h] = _bias_from_buckets(bucket, rb_ref, h)

    @pl.when(s == 0)
    def _():
        for buf in (kbuf, kbuf_r, vbuf, vbuf_r):
            buf[0:WINDOW, :] = jnp.zeros((WINDOW, KV_WIDTH), BF16)
        zb[0:POOL_CTX, :] = jnp.zeros((POOL_CTX, POOL_WIDTH), F32)

    @pl.when(s > 0)
    def _():
        for buf in (kbuf, kbuf_r, vbuf, vbuf_r):
            buf[0:WINDOW, :] = buf[EVEN_TM:EVEN_TM + WINDOW, :]
        zb[0:POOL_CTX, :] = zb[EVEN_TM:EVEN_TM + POOL_CTX, :]

    x = x_ref[...]
    h = _rms(x, g_ref[...]).astype(BF16)
    proj = jnp.dot(h, win_ref[...], preferred_element_type=F32)
    q = proj[:, 0:ATTN_WIDTH] * Q_SCALE
    k = proj[:, ATTN_WIDTH:ATTN_WIDTH + KV_WIDTH]
    v = proj[:, ATTN_WIDTH + KV_WIDTH:ATTN_WIDTH + 2 * KV_WIDTH]
    up = proj[:, ATTN_WIDTH + 2 * KV_WIDTH:IN_EVEN]

    qlane = lax.broadcasted_iota(jnp.int32, (EVEN_TM, ATTN_WIDTH), 1) % LANES
    q_half = (jnp.where(qlane < HEAD_DIM, q, 0.0).astype(BF16),
              jnp.where(qlane >= HEAD_DIM, q, 0.0).astype(BF16))
    kbuf[WINDOW:, :] = k.astype(BF16)
    vbuf[WINDOW:, :] = v.astype(BF16)
    kbuf_r[WINDOW:, :] = pltpu.roll(k, HEAD_DIM, 1).astype(BF16)
    vbuf_r[WINDOW:, :] = pltpu.roll(v, HEAD_DIM, 1).astype(BF16)

    qrow = lax.broadcasted_iota(jnp.int32, (WINDOW, WINDOW), 0)
    kcol = lax.broadcasted_iota(jnp.int32, (WINDOW, WINDOW), 1)
    prev = kcol > qrow
    olane = lax.broadcasted_iota(jnp.int32, (WINDOW, LANES), 1)
    head_sets = [[hd for hd in range(N_HEADS_A) if (hd // GQA_GROUP == hd % 2) == plain] for plain in (True, False)]
    for i in range(QBLK_PER_TILE):
        rows = slice(i * WINDOW, (i + 1) * WINDOW)
        keys = slice(i * WINDOW, (i + 2) * WINDOW)
        out = {}
        for heads, kb, vb in zip(head_sets, (kbuf, kbuf_r), (vbuf, vbuf_r)):
            qs = jnp.concatenate(
                [q_half[hd % 2][rows, (hd // 2) * LANES:(hd // 2 + 1) * LANES] for hd in heads], axis=0)
            sc_all = lax.dot_general(qs, kb[keys, :], (((1,), (1,)), ((), ())), preferred_element_type=F32)
            ps, denoms = [], []
            for r, hd in enumerate(heads):
                sc = sc_all[r * WINDOW:(r + 1) * WINDOW, :]
                sc = jnp.where(prev, sc[:, 0:WINDOW], sc[:, WINDOW:]) + bias_ref[hd]
                if i == 0:
                    sc = jnp.where(prev & (qrow >= jnp.where(s > 0, WINDOW, 0)), NEG, sc)
                p, denom = _sink_softmax(sc, sink_ref[hd] * LOG2E)
                ps.append(jnp.concatenate([jnp.where(prev, p, 0.0), jnp.where(prev, 0.0, p)], axis=1).astype(BF16))
                denoms.append(denom)
            o_all = jnp.dot(jnp.concatenate(ps, axis=0), vb[keys, :], preferred_element_type=F32)
            for r, hd in enumerate(heads):
                out[hd] = o_all[r * WINDOW:(r + 1) * WINDOW, :] / denoms[r]
        for j in range(N_HEADS_A // 2):
            cat_ref[rows, j * LANES:(j + 1) * LANES] = jnp.where(
                olane < HEAD_DIM, out[2 * j], out[2 * j + 1]).astype(BF16)

    zb[POOL_CTX:, :] = up
    n = EVEN_TM + POOL_CTX
    s2b[8:, :] = zb[8:n, :] + zb[7:n - 1, :]
    s4b[16:, :] = s2b[16:n, POOL_GROUP:] + s2b[14:n - 2, POOL_GROUP:]
    s8b[24:, :] = s4b[24:n, POOL_GROUP:] + s4b[20:n - 4, POOL_GROUP:]
    s16 = s8b[32:n, POOL_GROUP:] + s8b[24:n - 8, POOL_GROUP:]
    sums = (s2b[POOL_CTX:, 0:POOL_GROUP], s4b[POOL_CTX:, 0:POOL_GROUP], s8b[POOL_CTX:, 0:POOL_GROUP], s16)
    pos1 = lax.broadcasted_iota(jnp.int32, (POOL_HEAD, POOL_GROUP), 0) + s * EVEN_TM + 1
    for g, w in enumerate(POOL_WINDOWS):
        lanes = slice(g * POOL_GROUP, (g + 1) * POOL_GROUP)
        cnt = jnp.minimum(pos1, w).astype(F32)
        mean = jnp.concatenate([sums[g][0:POOL_HEAD] / cnt, sums[g][POOL_HEAD:] * (1.0 / w)], axis=0)
        d = (mean - up[:, lanes]).astype(BF16)
        y = jnp.dot(d, wpool_ref[g], preferred_element_type=F32) * pscale_ref[:, lanes]
        cat_ref[:, ATTN_WIDTH + g * POOL_GROUP:ATTN_WIDTH + (g + 1) * POOL_GROUP] = y.astype(BF16)

    o_ref[...] = x + jnp.dot(cat_ref[...], wout_ref[...], preferred_element_type=F32)

    @pl.when(s == last)
    def _():
        klast_ref[0] = k[EVEN_TM - WINDOW:, :]
        vlast_ref[0] = v[EVEN_TM - WINDOW:, :]
        plast_ref[0] = up[EVEN_TM - 2 * SUBLANES:, :]


def _even_prompt(x, gain, w_in, w_out, w_pool, pool_scale, rel_bias, sinks):
    bucket = jnp.asarray(_prompt_bucket_table())
    smem = pl.BlockSpec(memory_space=pltpu.SMEM)
    return pl.pallas_call(
        _even_prompt_kernel,
        out_shape=(
            jax.ShapeDtypeStruct((N_TOK, D_MODEL), F32),
            jax.ShapeDtypeStruct((BATCH, WINDOW, KV_WIDTH), F32),
            jax.ShapeDtypeStruct((BATCH, WINDOW, KV_WIDTH), F32),
            jax.ShapeDtypeStruct((BATCH, 2 * SUBLANES, POOL_WIDTH), F32),
        ),
        grid=(BATCH, TILES_PER_SEQ),
        in_specs=[
            pl.BlockSpec((EVEN_TM, D_MODEL), lambda b, s: (b * TILES_PER_SEQ + s, 0)),
            _resident((1, D_MODEL)),
            _resident((D_MODEL, IN_EVEN)),
            _resident((ATTN_WIDTH + POOL_WIDTH, D_MODEL)),
            _resident((N_POOL_GROUPS, POOL_GROUP, POOL_GROUP)),
            _resident((1, POOL_WIDTH)),
            _resident((WINDOW, WINDOW)),
            smem,
            smem,
        ],
        out_specs=(
            pl.BlockSpec((EVEN_TM, D_MODEL), lambda b, s: (b * TILES_PER_SEQ + s, 0)),
            pl.BlockSpec((1, WINDOW, KV_WIDTH), lambda b, s: (b, 0, 0)),
            pl.BlockSpec((1, WINDOW, KV_WIDTH), lambda b, s: (b, 0, 0)),
            pl.BlockSpec((1, 2 * SUBLANES, POOL_WIDTH), lambda b, s: (b, 0, 0)),
        ),
        scratch_shapes=[
            pltpu.VMEM((WINDOW + EVEN_TM, KV_WIDTH), BF16),
            pltpu.VMEM((WINDOW + EVEN_TM, KV_WIDTH), BF16),
            pltpu.VMEM((WINDOW + EVEN_TM, KV_WIDTH), BF16),
            pltpu.VMEM((WINDOW + EVEN_TM, KV_WIDTH), BF16),
            pltpu.VMEM((POOL_CTX + EVEN_TM, POOL_WIDTH), F32),
            pltpu.VMEM((POOL_CTX + EVEN_TM, POOL_WIDTH), F32),
            pltpu.VMEM((POOL_CTX + EVEN_TM, POOL_WIDTH - POOL_GROUP), F32),
            pltpu.VMEM((POOL_CTX + EVEN_TM, POOL_WIDTH - 2 * POOL_GROUP), F32),
            pltpu.VMEM((N_HEADS_A, WINDOW, WINDOW), F32),
            pltpu.VMEM((EVEN_TM, ATTN_WIDTH + POOL_WIDTH), BF16),
        ],
        input_output_aliases={0: 0},
        compiler_params=pltpu.CompilerParams(
            dimension_semantics=("arbitrary", "arbitrary"), vmem_limit_bytes=VMEM_LIMIT_BYTES),
        name="even_mixer_prompt",
    )(x, gain.reshape(1, D_MODEL), w_in, w_out, w_pool, pool_scale.reshape(1, POOL_WIDTH), bucket,
      rel_bias, sinks)


def _even_sample_kernel(x_ref, g_ref, win_ref, wout_ref, wpool_ref, pscale_ref, bucket_ref, rb_ref, sink_ref,
                        sk_ref, sv_ref, spool_ref,
                        o_ref, nk_ref, nv_ref, npool_ref,
                        knew_ref, vnew_ref, upnew_ref, qsel, bias_ref, attn_f32, cat_ref):
    g = pl.program_id(0)
    last = pl.num_programs(0) - 1
    olane = lax.broadcasted_iota(jnp.int32, (SAMPLE_ROWS, LANES), 1)

    @pl.when(g == 0)
    def _():
        bucket = bucket_ref[...]
        for h in range(N_HEADS_A):
            bias_ref[h] = _bias_from_buckets(bucket, rb_ref, h)
        x = x_ref[...]
        h = _rms(x, g_ref[...]).astype(BF16)
        proj = jnp.dot(h, win_ref[...], preferred_element_type=F32)
        q = proj[:, 0:ATTN_WIDTH] * Q_SCALE
        knew_ref[...] = proj[:, ATTN_WIDTH:ATTN_WIDTH + KV_WIDTH]
        vnew_ref[...] = proj[:, ATTN_WIDTH + KV_WIDTH:ATTN_WIDTH + 2 * KV_WIDTH]
        upnew_ref[...] = proj[:, ATTN_WIDTH + 2 * KV_WIDTH:IN_EVEN]
        qlane = lax.broadcasted_iota(jnp.int32, (SAMPLE_TOK, LANES), 1)
        for hd in range(N_HEADS_A):
            j, half, kvh = hd // 2, hd % 2, hd // GQA_GROUP
            blk = q[:, j * LANES:(j + 1) * LANES]
            if half != kvh:
                blk = pltpu.roll(blk, HEAD_DIM, 1)
            keep = (qlane < HEAD_DIM) if kvh == 0 else (qlane >= HEAD_DIM)
            qsel[hd] = jnp.where(keep, blk, 0.0)

    row0 = pl.multiple_of(g * SEQ_GROUP, SEQ_GROUP)
    pad = [jnp.zeros((WINDOW - NEW_KEYS, KV_WIDTH), F32)]
    knew = jnp.concatenate(
        [knew_ref[pl.ds(t * DEC_BATCH + row0, SEQ_GROUP), :] for t in range(DEC_SEQ)] + pad, axis=0)
    vnew = jnp.concatenate(
        [vnew_ref[pl.ds(t * DEC_BATCH + row0, SEQ_GROUP), :] for t in range(DEC_SEQ)] + pad, axis=0)
    kt_all = jnp.concatenate([sk_ref[b] for b in range(SEQ_GROUP)], axis=1).astype(BF16)
    vt_all = jnp.concatenate([sv_ref[b] for b in range(SEQ_GROUP)], axis=1).astype(BF16)

    qg = jnp.concatenate(
        [qsel[hd, pl.ds(t * DEC_BATCH + row0, SEQ_GROUP), :] for hd in range(N_HEADS_A) for t in range(DEC_SEQ)],
        axis=0).astype(BF16)
    nt = (((1,), (1,)), ((), ()))
    sc = jnp.concatenate(
        [jnp.dot(qg, kt_all, preferred_element_type=F32),
         lax.dot_general(qg, knew.astype(BF16), nt, preferred_element_type=F32)], axis=1)
    valid = jnp.concatenate([bucket_ref[...]] * N_HEADS_A, axis=0) >= 0
    bias = bias_ref[...].reshape(N_HEADS_A * SAMPLE_ROWS, SAMPLE_KEYS)
    sink = jnp.concatenate(
        [jnp.full((SAMPLE_ROWS, 1), sink_ref[hd] * LOG2E, F32) for hd in range(N_HEADS_A)], axis=0)
    p, denom = _sink_softmax(jnp.where(valid, sc + bias, NEG), sink)
    p = p.astype(BF16)
    o_all = (lax.dot_general(p[:, 0:STATE_KEYS], vt_all, nt, preferred_element_type=F32)
             + jnp.dot(p[:, STATE_KEYS:], vnew.astype(BF16), preferred_element_type=F32)) / denom
    for j in range(N_HEADS_A // 2):
        halves = []
        for half in range(2):
            hd = 2 * j + half
            o = o_all[hd * SAMPLE_ROWS:(hd + 1) * SAMPLE_ROWS, :]
            if hd // GQA_GROUP != half:
                o = pltpu.roll(o, HEAD_DIM, 1)
            halves.append(o)
        blk = jnp.where(olane < HEAD_DIM, halves[0], halves[1])
        for t in range(DEC_SEQ):
            attn_f32[pl.ds(t * DEC_BATCH + row0, SEQ_GROUP), j * LANES:(j + 1) * LANES] = (
                blk[t * SEQ_GROUP:(t + 1) * SEQ_GROUP, :])

    lane = lax.broadcasted_iota(jnp.int32, (KV_WIDTH, WINDOW), 1)
    for new, old_ref, out_ref in ((knew, sk_ref, nk_ref), (vnew, sv_ref, nv_ref)):
        new_t = new.T
        for b in range(SEQ_GROUP):
            win = pltpu.roll(old_ref[b], WINDOW - DEC_SEQ, 1)
            for t in range(DEC_SEQ):
                dst = WINDOW - DEC_SEQ + t
                win = jnp.where(lane == dst, pltpu.roll(new_t, (dst - (t * SEQ_GROUP + b)) % WINDOW, 1), win)
            out_ref[b] = win

    @pl.when(g == last)
    def _():
        cat_ref[:, 0:ATTN_WIDTH] = attn_f32[...].astype(BF16)
        for r in range(POOL_BUF):
            if r < POOL_BUF - DEC_SEQ:
                npool_ref[r] = spool_ref[r + DEC_SEQ]
            else:
                t = r - (POOL_BUF - DEC_SEQ)
                npool_ref[r] = upnew_ref[t * DEC_BATCH:(t + 1) * DEC_BATCH, :]
        for t in range(DEC_SEQ):
            rows = slice(t * DEC_BATCH, (t + 1) * DEC_BATCH)
            for gi, w in enumerate(POOL_WINDOWS):
                lanes = slice(gi * POOL_GROUP, (gi + 1) * POOL_GROUP)
                acc = upnew_ref[rows, lanes]
                for back in range(1, w):
                    tt = t - back
                    if tt >= 0:
                        acc = acc + upnew_ref[tt * DEC_BATCH:(tt + 1) * DEC_BATCH, lanes]
                    else:
                        acc = acc + spool_ref[POOL_BUF + tt, :, lanes]
                cnt = float(min(PAST_LEN + t + 1, w))
                d = (acc / cnt - upnew_ref[rows, lanes]).astype(BF16)
                y = jnp.dot(d, wpool_ref[gi], preferred_element_type=F32) * pscale_ref[:, lanes]
                cat_ref[rows, ATTN_WIDTH + gi * POOL_GROUP:ATTN_WIDTH + (gi + 1) * POOL_GROUP] = y.astype(BF16)
        o_ref[...] = x_ref[...] + jnp.dot(cat_ref[...], wout_ref[...], preferred_element_type=F32)


def _even_sample(x, gain, w_in, w_out, w_pool, pool_scale, rel_bias, sinks, state_k, state_v, state_pool_t):
    bucket = jnp.asarray(_sample_bucket_table())
    smem = pl.BlockSpec(memory_space=pltpu.SMEM)
    sample_block = PROMPT_TOK // SAMPLE_TOK
    return pl.pallas_call(
        _even_sample_kernel,
        out_shape=(
            jax.ShapeDtypeStruct((N_TOK, D_MODEL), F32),
            jax.ShapeDtypeStruct((DEC_BATCH, KV_WIDTH, WINDOW), F32),
            jax.ShapeDtypeStruct((DEC_BATCH, KV_WIDTH, WINDOW), F32),
            jax.ShapeDtypeStruct((POOL_BUF, DEC_BATCH, POOL_WIDTH), F32),
        ),
        grid=(N_SEQ_GROUPS,),
        in_specs=[
            pl.BlockSpec((SAMPLE_TOK, D_MODEL), lambda g: (sample_block, 0)),
            _resident((1, D_MODEL)),
            _resident((D_MODEL, IN_EVEN)),
            _resident((ATTN_WIDTH + POOL_WIDTH, D_MODEL)),
            _resident((N_POOL_GROUPS, POOL_GROUP, POOL_GROUP)),
            _resident((1, POOL_WIDTH)),
            _resident((SAMPLE_ROWS, SAMPLE_KEYS)),
            smem,
            smem,
            pl.BlockSpec((SEQ_GROUP, KV_WIDTH, WINDOW), lambda g: (g, 0, 0)),
            pl.BlockSpec((SEQ_GROUP, KV_WIDTH, WINDOW), lambda g: (g, 0, 0)),
            _resident((POOL_BUF, DEC_BATCH, POOL_WIDTH)),
        ],
        out_specs=(
            pl.BlockSpec((SAMPLE_TOK, D_MODEL), lambda g: (sample_block, 0)),
            pl.BlockSpec((SEQ_GROUP, KV_WIDTH, WINDOW), lambda g: (g, 0, 0)),
            pl.BlockSpec((SEQ_GROUP, KV_WIDTH, WINDOW), lambda g: (g, 0, 0)),
            pl.BlockSpec((POOL_BUF, DEC_BATCH, POOL_WIDTH), lambda g: (0, 0, 0)),
        ),
        scratch_shapes=[
            pltpu.VMEM((SAMPLE_TOK, KV_WIDTH), F32),
            pltpu.VMEM((SAMPLE_TOK, KV_WIDTH), F32),
            pltpu.VMEM((SAMPLE_TOK, POOL_WIDTH), F32),
            pltpu.VMEM((N_HEADS_A, SAMPLE_TOK, LANES), F32),
            pltpu.VMEM((N_HEADS_A, SAMPLE_ROWS, SAMPLE_KEYS), F32),
            pltpu.VMEM((SAMPLE_TOK, ATTN_WIDTH), F32),
            pltpu.VMEM((SAMPLE_TOK, ATTN_WIDTH + POOL_WIDTH), BF16),
        ],
        input_output_aliases={0: 0},
        compiler_params=pltpu.CompilerParams(
            dimension_semantics=("arbitrary",), vmem_limit_bytes=VMEM_LIMIT_BYTES),
        name="even_mixer_sample",
    )(x, gain.reshape(1, D_MODEL), w_in, w_out, w_pool, pool_scale.reshape(1, POOL_WIDTH), bucket,
      rel_bias, sinks, state_k, state_v, state_pool_t)


def _sgu_front(x, g_ref, win_ref, gv_ref):
    h = _rms(x, g_ref[...]).astype(BF16)
    uv = jax.nn.gelu(jnp.dot(h, win_ref[...], preferred_element_type=F32))
    return uv[:, 0:SGU_WIDTH], _rms(uv[:, SGU_WIDTH:], gv_ref[...])


def _odd_prompt_kernel(x_ref, g_ref, win_ref, gv_ref, ws_ref, bs_ref, wout_ref, o_ref, gated_ref):
    x = x_ref[...]
    u, v = _sgu_front(x, g_ref, win_ref, gv_ref)
    vb = v.astype(BF16)
    r = lax.broadcasted_iota(jnp.int32, (CHUNK, CHUNK), 0)
    c = lax.broadcasted_iota(jnp.int32, (CHUNK, CHUNK), 1)
    for gi in range(SGU_GROUPS):
        w = jnp.where(r >= c, ws_ref[gi], 0.0).astype(BF16)
        bias = jnp.concatenate([bs_ref[gi]] * (SGU_GROUP_W // LANES), axis=1)
        lanes = slice(gi * SGU_GROUP_W, (gi + 1) * SGU_GROUP_W)
        for ci in range(ODD_TM // CHUNK):
            rows = slice(ci * CHUNK, (ci + 1) * CHUNK)
            mixed = jnp.dot(w, vb[rows, lanes], preferred_element_type=F32) + bias
            gated_ref[rows, lanes] = (u[rows, lanes] * mixed).astype(BF16)
    o_ref[...] = x + jnp.dot(gated_ref[...], wout_ref[...], preferred_element_type=F32)


def _odd_prompt(x, gain, w_in, g_v, w_s, b_s, w_out):
    bias = jnp.broadcast_to(b_s[:, :, None], (SGU_GROUPS, CHUNK, LANES))
    return pl.pallas_call(
        _odd_prompt_kernel,
        out_shape=jax.ShapeDtypeStruct((N_TOK, D_MODEL), F32),
        grid=(PROMPT_TOK // ODD_TM,),
        in_specs=[
            pl.BlockSpec((ODD_TM, D_MODEL), lambda i: (i, 0)),
            _resident((1, D_MODEL)),
            _resident((D_MODEL, 2 * SGU_WIDTH)),
            _resident((1, SGU_WIDTH)),
            _resident((SGU_GROUPS, CHUNK, CHUNK)),
            _resident((SGU_GROUPS, CHUNK, LANES)),
            _resident((SGU_WIDTH, D_MODEL)),
        ],
        out_specs=pl.BlockSpec((ODD_TM, D_MODEL), lambda i: (i, 0)),
        scratch_shapes=[pltpu.VMEM((ODD_TM, SGU_WIDTH), BF16)],
        input_output_aliases={0: 0},
        compiler_params=pltpu.CompilerParams(
            dimension_semantics=("arbitrary",), vmem_limit_bytes=VMEM_LIMIT_BYTES),
        name="odd_mixer_prompt",
    )(x, gain.reshape(1, D_MODEL), w_in, g_v.reshape(1, SGU_WIDTH), w_s, bias, w_out)


def _odd_sample_kernel(x_ref, g_ref, win_ref, gv_ref, coef_ref, bias_ref, wout_ref, o_ref, v_ref):
    x = x_ref[...]
    u, v = _sgu_front(x, g_ref, win_ref, gv_ref)
    v_ref[...] = v
    gated = []
    for t in range(DEC_SEQ):
        mixed = bias_ref[t:t + 1, :]
        for s in range(t + 1):
            mixed = mixed + coef_ref[t, s:s + 1, :] * v[s * DEC_BATCH:(s + 1) * DEC_BATCH, :]
        gated.append((u[t * DEC_BATCH:(t + 1) * DEC_BATCH, :] * mixed).astype(BF16))
    o_ref[...] = x + jnp.dot(jnp.concatenate(gated, axis=0), wout_ref[...], preferred_element_type=F32)


def _odd_sample(x, gain, w_in, g_v, w_s, b_s, w_out):
    coef = jnp.repeat(jnp.transpose(w_s[:, :DEC_SEQ, :DEC_SEQ], (1, 2, 0)), SGU_GROUP_W, axis=-1)
    bias = jnp.repeat(b_s[:, :DEC_SEQ].T, SGU_GROUP_W, axis=-1)
    sample_block = PROMPT_TOK // SAMPLE_TOK
    return pl.pallas_call(
        _odd_sample_kernel,
        out_shape=(
            jax.ShapeDtypeStruct((N_TOK, D_MODEL), F32),
            jax.ShapeDtypeStruct((SAMPLE_TOK, SGU_WIDTH), F32),
        ),
        grid=(1,),
        in_specs=[
            pl.BlockSpec((SAMPLE_TOK, D_MODEL), lambda i: (sample_block, 0)),
            _resident((1, D_MODEL)),
            _resident((D_MODEL, 2 * SGU_WIDTH)),
            _resident((1, SGU_WIDTH)),
            _resident((DEC_SEQ, DEC_SEQ, SGU_WIDTH)),
            _resident((DEC_SEQ, SGU_WIDTH)),
            _resident((SGU_WIDTH, D_MODEL)),
        ],
        out_specs=(
            pl.BlockSpec((SAMPLE_TOK, D_MODEL), lambda i: (sample_block, 0)),
            pl.BlockSpec((SAMPLE_TOK, SGU_WIDTH), lambda i: (0, 0)),
        ),
        input_output_aliases={0: 0},
        compiler_params=pltpu.CompilerParams(
            dimension_semantics=("arbitrary",), vmem_limit_bytes=VMEM_LIMIT_BYTES),
        name="odd_mixer_sample",
    )(x, gain.reshape(1, D_MODEL), w_in, g_v.reshape(1, SGU_WIDTH), coef, bias, w_out)


def _to_step_major(a):
    return jnp.swapaxes(a, 0, 1).reshape(SAMPLE_TOK, a.shape[-1])


def _to_batch_major(a):
    return jnp.swapaxes(a.reshape(DEC_SEQ, DEC_BATCH, a.shape[-1]), 0, 1)


def _window_to_lanes(w):
    return jnp.transpose(w, (0, 2, 3, 1)).reshape(DEC_BATCH, KV_WIDTH, WINDOW)


def _window_from_lanes(w):
    return jnp.transpose(w.reshape(DEC_BATCH, N_KV_A, HEAD_DIM, WINDOW), (0, 3, 1, 2))


def kernel(x_prompt, x_sample, state_win_k, state_win_v, state_pool, rel_bias, norm_gains, final_gain,
           ffn_gate, ffn_up, ffn_down, w_in_even, w_out_even, attn_sinks, w_pool, pool_scale,
           w_in_odd, sgu_norm, w_spatial, b_spatial, w_out_odd):
    xp = x_prompt.reshape(PROMPT_TOK, D_MODEL)
    xs = _to_step_major(x_sample)

    kp_l, vp_l, pp_l, ks_l, vs_l, ps_l, sv_l = [], [], [], [], [], [], []
    x = None
    y_prompt = y_sample = None
    for l in range(DEPTH):
        fa = (norm_gains[l, 0], ffn_gate, ffn_up, ffn_down, l, 0)
        fb = (norm_gains[l, 2], ffn_gate, ffn_up, ffn_down, l, 1)
        if l == 0:
            x = _ffn(xp, *fa, tail_in=xs)
        else:
            x = _ffn(x, *fa)
        g1 = norm_gains[l, 1]
        if l % 2 == 0:
            e = l // 2
            w_in, w_out = w_in_even[e].astype(BF16), w_out_even[e].astype(BF16)
            wp = w_pool[e].astype(BF16)
            x, kp, vp, pp = _even_prompt(x, g1, w_in, w_out, wp, pool_scale[e], rel_bias, attn_sinks[e])
            x, nk, nv, npool = _even_sample(
                x, g1, w_in, w_out, wp, pool_scale[e], rel_bias, attn_sinks[e],
                _window_to_lanes(state_win_k[e]), _window_to_lanes(state_win_v[e]),
                jnp.swapaxes(state_pool[e], 0, 1))
            kp_l.append(kp.reshape(BATCH, WINDOW, N_KV_A, HEAD_DIM))
            vp_l.append(vp.reshape(BATCH, WINDOW, N_KV_A, HEAD_DIM))
            pp_l.append(pp[:, 2 * SUBLANES - POOL_BUF:, :])
            ks_l.append(_window_from_lanes(nk))
            vs_l.append(_window_from_lanes(nv))
            ps_l.append(jnp.swapaxes(npool, 0, 1))
        else:
            o = l // 2
            w_in, w_out = w_in_odd[o].astype(BF16), w_out_odd[o].astype(BF16)
            x = _odd_prompt(x, g1, w_in, sgu_norm[o], w_spatial[o], b_spatial[o], w_out)
            x, sv_new = _odd_sample(x, g1, w_in, sgu_norm[o], w_spatial[o], b_spatial[o], w_out)
            sv_l.append(_to_batch_major(sv_new))
        if l < DEPTH - 1:
            x = _ffn(x, *fb)
        else:
            y_prompt, y_sample = _ffn(x, *fb, final_gain=final_gain)

    return (y_prompt.reshape(BATCH, SEQ, D_MODEL), _to_batch_major(y_sample),
            jnp.stack(kp_l), jnp.stack(vp_l), jnp.stack(pp_l),
            jnp.stack(ks_l), jnp.stack(vs_l), jnp.stack(ps_l), jnp.stack(sv_l))
```

```python
import functools
import math

import numpy as np
import jax
import jax.numpy as jnp
from jax import lax
from jax.experimental import pallas as pl
from jax.experimental.pallas import tpu as pltpu

F32 = jnp.float32
BF16 = jnp.bfloat16

D_MODEL = 1024
BATCH = 8
SEQ = 2048
DEPTH = 2
DEC_BATCH = 128
DEC_SEQ = 4
PAST_LEN = 16384
N_HEADS_A = 8
N_KV_A = 2
HEAD_DIM = 64
GQA_GROUP = N_HEADS_A // N_KV_A
WINDOW = 128
ATTN_WIDTH = N_HEADS_A * HEAD_DIM
KV_WIDTH = N_KV_A * HEAD_DIM
N_BUCKETS = 32
MAX_DISTANCE = 128
POOL_WINDOWS = (2, 4, 8, 16)
N_POOL_GROUPS = len(POOL_WINDOWS)
POOL_GROUP = 128
POOL_WIDTH = N_POOL_GROUPS * POOL_GROUP
POOL_BUF = max(POOL_WINDOWS) - 1
IN_EVEN = ATTN_WIDTH + 2 * KV_WIDTH + POOL_WIDTH
CHUNK = 128
SGU_WIDTH = 1024
SGU_GROUPS = 4
SGU_GROUP_W = SGU_WIDTH // SGU_GROUPS
D_FF = 2816
EPS = 1e-6
NEG = -1e30

LANES = 128
SUBLANES = 8
VMEM_LIMIT_BYTES = 56 * 1024 * 1024

PROMPT_TOK = BATCH * SEQ
SAMPLE_TOK = DEC_BATCH * DEC_SEQ
N_TOK = PROMPT_TOK + SAMPLE_TOK

TM = SAMPLE_TOK
assert PROMPT_TOK % TM == 0
PROMPT_BLOCKS = PROMPT_TOK // TM
EVEN_TM = 1024
assert SEQ % EVEN_TM == 0 and EVEN_TM % WINDOW == 0
TILES_PER_SEQ = SEQ // EVEN_TM
QBLK_PER_TILE = EVEN_TM // WINDOW
FFN_TM = 2 * TM
assert PROMPT_TOK % FFN_TM == 0
FFN_PROMPT_STEPS = PROMPT_TOK // FFN_TM
ODD_TM = 1024
assert PROMPT_TOK % ODD_TM == 0 and ODD_TM % CHUNK == 0
FF_CHUNK = 256
N_FF_CHUNKS = D_FF // FF_CHUNK
POOL_HEAD = 2 * SUBLANES
assert all(w & (w - 1) == 0 and w <= POOL_HEAD for w in POOL_WINDOWS)
POOL_CTX = 32
SEQ_GROUP = 8
N_SEQ_GROUPS = DEC_BATCH // SEQ_GROUP
STATE_KEYS = SEQ_GROUP * WINDOW
NEW_SLOTS = SUBLANES
assert DEC_SEQ <= NEW_SLOTS and SEQ_GROUP * NEW_SLOTS <= WINDOW
SAMPLE_KEYS = STATE_KEYS + WINDOW
SAMPLE_ROWS = DEC_SEQ * SEQ_GROUP


def _t5_bucket_np(dist):
    n = np.maximum(dist, 0)
    max_exact = N_BUCKETS // 2
    nf = np.maximum(n, 1).astype(np.float32)
    large = max_exact + (np.log(nf / np.float32(max_exact)) / np.float32(math.log(MAX_DISTANCE / max_exact))
                         * np.float32(N_BUCKETS - max_exact)).astype(np.int32)
    large = np.minimum(large, N_BUCKETS - 1)
    return np.where(n < max_exact, n, large).astype(np.int32)


def _prompt_bucket_table():
    qi = np.arange(WINDOW)[:, None]
    c = np.arange(WINDOW)[None, :]
    dist = np.where(c > qi, qi + WINDOW - c, qi - c)
    return _t5_bucket_np(dist)


def _sample_bucket_table():
    t = (np.arange(SAMPLE_ROWS) // SEQ_GROUP)[:, None]
    b = (np.arange(SAMPLE_ROWS) % SEQ_GROUP)[:, None]
    col = np.arange(SAMPLE_KEYS)[None, :]
    is_state = col < STATE_KEYS
    slot = (col - STATE_KEYS) % NEW_SLOTS
    is_new = (col >= STATE_KEYS) & (col < STATE_KEYS + SEQ_GROUP * NEW_SLOTS) & (slot < DEC_SEQ)
    kb = np.where(is_state, col // WINDOW, (col - STATE_KEYS) // NEW_SLOTS)
    kpos = np.where(is_state, col % WINDOW, WINDOW + slot)
    dist = (WINDOW + t) - kpos
    valid = (is_state | is_new) & (kb == b) & (dist >= 0) & (dist < WINDOW)
    return np.where(valid, _t5_bucket_np(dist), -1).astype(np.int32)


def _rms(x, g):
    y = x * lax.rsqrt(jnp.mean(x * x, axis=-1, keepdims=True) + EPS)
    return y * g


LOG2E = math.log2(math.e)
Q_SCALE = HEAD_DIM ** -0.5 * LOG2E


def _bias_from_buckets(bucket, rb_ref, h):
    t = jnp.zeros(bucket.shape, F32)
    for bkt in range(N_BUCKETS):
        t = jnp.where(bucket == bkt, rb_ref[bkt, h] * LOG2E, t)
    return t


def _sink_softmax(s, sink):
    m = jnp.maximum(jnp.max(s, axis=-1, keepdims=True), sink)
    p = jnp.exp2(s - m)
    return p, jnp.sum(p, axis=-1, keepdims=True) + jnp.exp2(sink - m)


def _softmax_pv(s, valid, bias, sink, v):
    p, denom = _sink_softmax(jnp.where(valid, s + bias, NEG), sink)
    return jnp.dot(p.astype(BF16), v, preferred_element_type=F32) / denom


def _ffn_kernel(*refs, layer, half, tail_in, final_split):
    refs = list(refs)
    x_ref = refs.pop(0)
    xt_ref = refs.pop(0) if tail_in else None
    g_ref, wg_hbm, wu_hbm, wd_hbm = refs[:4]
    del refs[:4]
    fg_ref = refs.pop(0) if final_split else None
    o_ref = refs.pop(0)
    ot_ref = refs.pop(0) if final_split else None
    wg, wu, wd, stage_in, stage_dn, sem = refs
    i = pl.program_id(0)

    def load(rows):
        x = x_ref[rows, :]
        if tail_in:
            x = jnp.where(i < FFN_PROMPT_STEPS, x, xt_ref[...])
        return x, _rms(x, g_ref[...]).astype(BF16)

    def finish(rows, x, y):
        out = x + 0.5 * y
        if not final_split:
            o_ref[rows, :] = out
            return
        out = _rms(out, fg_ref[...])

        @pl.when(i < FFN_PROMPT_STEPS)
        def _():
            o_ref[rows, :] = out

        @pl.when(i == FFN_PROMPT_STEPS)
        def _():
            ot_ref[...] = out

    def chunk_copies(c, slot):
        cols = pl.ds(c * FF_CHUNK, FF_CHUNK)
        return (
            pltpu.make_async_copy(wg_hbm.at[layer, half, :, cols], stage_in.at[0, slot], sem.at[0, slot]),
            pltpu.make_async_copy(wu_hbm.at[layer, half, :, cols], stage_in.at[1, slot], sem.at[1, slot]),
            pltpu.make_async_copy(wd_hbm.at[layer, half, cols, :], stage_dn.at[slot], sem.at[2, slot]),
        )

    @pl.when(i == 0)
    def _():
        for c in range(min(2, N_FF_CHUNKS)):
            for cp in chunk_copies(c, c % 2):
                cp.start()
        rows = slice(0, TM)
        x, h = load(rows)
        acc = jnp.zeros((TM, D_MODEL), F32)
        for c in range(N_FF_CHUNKS):
            slot = c % 2
            cols = slice(c * FF_CHUNK, (c + 1) * FF_CHUNK)
            for cp in chunk_copies(c, slot):
                cp.wait()
            wg[:, cols] = stage_in[0, slot].astype(BF16)
            wu[:, cols] = stage_in[1, slot].astype(BF16)
            wd[cols, :] = stage_dn[slot].astype(BF16)
            if c + 2 < N_FF_CHUNKS:
                for cp in chunk_copies(c + 2, slot):
                    cp.start()
            gate = jnp.dot(h, wg[:, cols], preferred_element_type=F32)
            up = jnp.dot(h, wu[:, cols], preferred_element_type=F32)
            act = (jax.nn.silu(gate) * up).astype(BF16)
            acc = acc + jnp.dot(act, wd[cols, :], preferred_element_type=F32)
        finish(rows, x, acc)

    def half_block(hf, carry):
        rows = pl.ds(pl.multiple_of(hf * TM, TM), TM)
        x, h = load(rows)
        gate = jnp.dot(h, wg[...], preferred_element_type=F32)
        up = jnp.dot(h, wu[...], preferred_element_type=F32)
        act = (jax.nn.silu(gate) * up).astype(BF16)
        finish(rows, x, jnp.dot(act, wd[...], preferred_element_type=F32))
        return carry

    first = jnp.where(i == 0, 1, 0)
    stop = jnp.where(i < FFN_PROMPT_STEPS, FFN_TM // TM, 1)
    lax.fori_loop(first, stop, half_block, 0)


def _resident(shape):
    return pl.BlockSpec(shape, lambda *_: (0,) * len(shape), pipeline_mode=pl.Buffered(1))


def _ffn(x_in, gain, wg, wu, wd, layer, half, *, tail_in=None, final_gain=None):
    hbm = pl.BlockSpec(memory_space=pl.ANY)
    last_prompt = FFN_PROMPT_STEPS - 1
    row_block = pl.BlockSpec((FFN_TM, D_MODEL), lambda i: (i, 0))
    prompt_block = pl.BlockSpec((FFN_TM, D_MODEL), lambda i: (jnp.minimum(i, last_prompt), 0))
    sample_block = pl.BlockSpec((TM, D_MODEL), lambda i: (0, 0))
    if tail_in is None:
        args, in_specs = [x_in], [row_block]
    else:
        args, in_specs = [x_in, tail_in], [prompt_block, sample_block]
    args += [gain.reshape(1, D_MODEL), wg, wu, wd]
    in_specs += [_resident((1, D_MODEL)), hbm, hbm, hbm]
    if final_gain is None:
        out_shape = jax.ShapeDtypeStruct((N_TOK, D_MODEL), F32)
        out_specs = row_block
    else:
        args.append(final_gain.reshape(1, D_MODEL))
        in_specs.append(_resident((1, D_MODEL)))
        out_shape = (jax.ShapeDtypeStruct((PROMPT_TOK, D_MODEL), F32),
                     jax.ShapeDtypeStruct((SAMPLE_TOK, D_MODEL), F32))
        out_specs = (prompt_block, sample_block)
    aliases = {0: 0} if (tail_in is None and final_gain is None) else {}
    return pl.pallas_call(
        functools.partial(_ffn_kernel, layer=layer, half=half, tail_in=tail_in is not None,
                          final_split=final_gain is not None),
        out_shape=out_shape,
        grid=(FFN_PROMPT_STEPS + 1,),
        in_specs=in_specs,
        out_specs=out_specs,
        scratch_shapes=[
            pltpu.VMEM((D_MODEL, D_FF), BF16),
            pltpu.VMEM((D_MODEL, D_FF), BF16),
            pltpu.VMEM((D_FF, D_MODEL), BF16),
            pltpu.VMEM((2, 2, D_MODEL, FF_CHUNK), F32),
            pltpu.VMEM((2, FF_CHUNK, D_MODEL), F32),
            pltpu.SemaphoreType.DMA((3, 2)),
        ],
        input_output_aliases=aliases,
        compiler_params=pltpu.CompilerParams(
            dimension_semantics=("arbitrary",), vmem_limit_bytes=VMEM_LIMIT_BYTES),
        name="macaron_half",
    )(*args)


def _even_prompt_kernel(x_ref, g_ref, win_ref, wout_ref, wpool_ref, pscale_ref, bucket_ref, rb_ref, sink_ref,
                        o_ref, klast_ref, vlast_ref, plast_ref,
                        kbuf, kbuf_r, vbuf, vbuf_r, zb, s2b, s4b, s8b, bias_ref, cat_ref):
    b = pl.program_id(0)
    s = pl.program_id(1)
    last = pl.num_programs(1) - 1

    @pl.when((b == 0) & (s == 0))
    def _():
        bucket = bucket_ref[...]
        for h in range(N_HEADS_A):
            bias_ref[h] = _bias_from_buckets(bucket, rb_ref, h)

    @pl.when(s == 0)
    def _():
        for buf in (kbuf, kbuf_r, vbuf, vbuf_r):
            buf[0:WINDOW, :] = jnp.zeros((WINDOW, KV_WIDTH), BF16)
        zb[0:POOL_CTX, :] = jnp.zeros((POOL_CTX, POOL_WIDTH), F32)

    @pl.when(s > 0)
    def _():
        for buf in (kbuf, kbuf_r, vbuf, vbuf_r):
            buf[0:WINDOW, :] = buf[EVEN_TM:EVEN_TM + WINDOW, :]
        zb[0:POOL_CTX, :] = zb[EVEN_TM:EVEN_TM + POOL_CTX, :]

    x = x_ref[...]
    h = _rms(x, g_ref[...]).astype(BF16)
    proj = jnp.dot(h, win_ref[...], preferred_element_type=F32)
    q = proj[:, 0:ATTN_WIDTH] * Q_SCALE
    k = proj[:, ATTN_WIDTH:ATTN_WIDTH + KV_WIDTH]
    v = proj[:, ATTN_WIDTH + KV_WIDTH:ATTN_WIDTH + 2 * KV_WIDTH]
    up = proj[:, ATTN_WIDTH + 2 * KV_WIDTH:IN_EVEN]

    qlane = lax.broadcasted_iota(jnp.int32, (EVEN_TM, ATTN_WIDTH), 1) % LANES
    q_half = (jnp.where(qlane < HEAD_DIM, q, 0.0).astype(BF16),
              jnp.where(qlane >= HEAD_DIM, q, 0.0).astype(BF16))
    kbuf[WINDOW:, :] = k.astype(BF16)
    vbuf[WINDOW:, :] = v.astype(BF16)
    kbuf_r[WINDOW:, :] = pltpu.roll(k, HEAD_DIM, 1).astype(BF16)
    vbuf_r[WINDOW:, :] = pltpu.roll(v, HEAD_DIM, 1).astype(BF16)

    qrow = lax.broadcasted_iota(jnp.int32, (WINDOW, WINDOW), 0)
    kcol = lax.broadcasted_iota(jnp.int32, (WINDOW, WINDOW), 1)
    prev = kcol > qrow
    olane = lax.broadcasted_iota(jnp.int32, (WINDOW, LANES), 1)
    head_sets = [[hd for hd in range(N_HEADS_A) if (hd // GQA_GROUP == hd % 2) == plain] for plain in (True, False)]
    for i in range(QBLK_PER_TILE):
        rows = slice(i * WINDOW, (i + 1) * WINDOW)
        keys = slice(i * WINDOW, (i + 2) * WINDOW)
        out = {}
        for heads, kb, vb in zip(head_sets, (kbuf, kbuf_r), (vbuf, vbuf_r)):
            qs = jnp.concatenate(
                [q_half[hd % 2][rows, (hd // 2) * LANES:(hd // 2 + 1) * LANES] for hd in heads], axis=0)
            sc_all = lax.dot_general(qs, kb[keys, :], (((1,), (1,)), ((), ())), preferred_element_type=F32)
            ps, denoms = [], []
            for r, hd in enumerate(heads):
                sc = sc_all[r * WINDOW:(r + 1) * WINDOW, :]
                sc = jnp.where(prev, sc[:, 0:WINDOW], sc[:, WINDOW:]) + bias_ref[hd]
                if i == 0:
                    sc = jnp.where(prev & (qrow >= jnp.where(s > 0, WINDOW, 0)), NEG, sc)
                p, denom = _sink_softmax(sc, sink_ref[hd] * LOG2E)
                ps.append(jnp.concatenate([jnp.where(prev, p, 0.0), jnp.where(prev, 0.0, p)], axis=1).astype(BF16))
                denoms.append(denom)
            o_all = jnp.dot(jnp.concatenate(ps, axis=0), vb[keys, :], preferred_element_type=F32)
            for r, hd in enumerate(heads):
                out[hd] = o_all[r * WINDOW:(r + 1) * WINDOW, :] / denoms[r]
        for j in range(N_HEADS_A // 2):
            cat_ref[rows, j * LANES:(j + 1) * LANES] = jnp.where(
                olane < HEAD_DIM, out[2 * j], out[2 * j + 1]).astype(BF16)

    zb[POOL_CTX:, :] = up
    n = EVEN_TM + POOL_CTX
    s2b[8:, :] = zb[8:n, :] + zb[7:n - 1, :]
    s4b[16:, :] = s2b[16:n, POOL_GROUP:] + s2b[14:n - 2, POOL_GROUP:]
    s8b[24:, :] = s4b[24:n, POOL_GROUP:] + s4b[20:n - 4, POOL_GROUP:]
    s16 = s8b[32:n, POOL_GROUP:] + s8b[24:n - 8, POOL_GROUP:]
    sums = (s2b[POOL_CTX:, 0:POOL_GROUP], s4b[POOL_CTX:, 0:POOL_GROUP], s8b[POOL_CTX:, 0:POOL_GROUP], s16)
    pos1 = lax.broadcasted_iota(jnp.int32, (POOL_HEAD, POOL_GROUP), 0) + s * EVEN_TM + 1
    for g, w in enumerate(POOL_WINDOWS):
        lanes = slice(g * POOL_GROUP, (g + 1) * POOL_GROUP)
        cnt = jnp.minimum(pos1, w).astype(F32)
        mean = jnp.concatenate([sums[g][0:POOL_HEAD] / cnt, sums[g][POOL_HEAD:] * (1.0 / w)], axis=0)
        d = (mean - up[:, lanes]).astype(BF16)
        y = jnp.dot(d, wpool_ref[g], preferred_element_type=F32) * pscale_ref[:, lanes]
        cat_ref[:, ATTN_WIDTH + g * POOL_GROUP:ATTN_WIDTH + (g + 1) * POOL_GROUP] = y.astype(BF16)

    o_ref[...] = x + jnp.dot(cat_ref[...], wout_ref[...], preferred_element_type=F32)

    @pl.when(s == last)
    def _():
        klast_ref[0] = k[EVEN_TM - WINDOW:, :]
        vlast_ref[0] = v[EVEN_TM - WINDOW:, :]
        plast_ref[0] = up[EVEN_TM - 2 * SUBLANES:, :]


def _even_prompt(x, gain, w_in, w_out, w_pool, pool_scale, rel_bias, sinks):
    bucket = jnp.asarray(_prompt_bucket_table())
    smem = pl.BlockSpec(memory_space=pltpu.SMEM)
    return pl.pallas_call(
        _even_prompt_kernel,
        out_shape=(
            jax.ShapeDtypeStruct((N_TOK, D_MODEL), F32),
            jax.ShapeDtypeStruct((BATCH, WINDOW, KV_WIDTH), F32),
            jax.ShapeDtypeStruct((BATCH, WINDOW, KV_WIDTH), F32),
            jax.ShapeDtypeStruct((BATCH, 2 * SUBLANES, POOL_WIDTH), F32),
        ),
        grid=(BATCH, TILES_PER_SEQ),
        in_specs=[
            pl.BlockSpec((EVEN_TM, D_MODEL), lambda b, s: (b * TILES_PER_SEQ + s, 0)),
            _resident((1, D_MODEL)),
            _resident((D_MODEL, IN_EVEN)),
            _resident((ATTN_WIDTH + POOL_WIDTH, D_MODEL)),
            _resident((N_POOL_GROUPS, POOL_GROUP, POOL_GROUP)),
            _resident((1, POOL_WIDTH)),
            _resident((WINDOW, WINDOW)),
            smem,
            smem,
        ],
        out_specs=(
            pl.BlockSpec((EVEN_TM, D_MODEL), lambda b, s: (b * TILES_PER_SEQ + s, 0)),
            pl.BlockSpec((1, WINDOW, KV_WIDTH), lambda b, s: (b, 0, 0)),
            pl.BlockSpec((1, WINDOW, KV_WIDTH), lambda b, s: (b, 0, 0)),
            pl.BlockSpec((1, 2 * SUBLANES, POOL_WIDTH), lambda b, s: (b, 0, 0)),
        ),
        scratch_shapes=[
            pltpu.VMEM((WINDOW + EVEN_TM, KV_WIDTH), BF16),
            pltpu.VMEM((WINDOW + EVEN_TM, KV_WIDTH), BF16),
            pltpu.VMEM((WINDOW + EVEN_TM, KV_WIDTH), BF16),
            pltpu.VMEM((WINDOW + EVEN_TM, KV_WIDTH), BF16),
            pltpu.VMEM((POOL_CTX + EVEN_TM, POOL_WIDTH), F32),
            pltpu.VMEM((POOL_CTX + EVEN_TM, POOL_WIDTH), F32),
            pltpu.VMEM((POOL_CTX + EVEN_TM, POOL_WIDTH - POOL_GROUP), F32),
            pltpu.VMEM((POOL_CTX + EVEN_TM, POOL_WIDTH - 2 * POOL_GROUP), F32),
            pltpu.VMEM((N_HEADS_A, WINDOW, WINDOW), F32),
            pltpu.VMEM((EVEN_TM, ATTN_WIDTH + POOL_WIDTH), BF16),
        ],
        input_output_aliases={0: 0},
        compiler_params=pltpu.CompilerParams(
            dimension_semantics=("arbitrary", "arbitrary"), vmem_limit_bytes=VMEM_LIMIT_BYTES),
        name="even_mixer_prompt",
    )(x, gain.reshape(1, D_MODEL), w_in, w_out, w_pool, pool_scale.reshape(1, POOL_WIDTH), bucket,
      rel_bias, sinks)


def _even_sample_kernel(x_ref, g_ref, win_ref, wout_ref, wpool_ref, pscale_ref, bucket_ref, rb_ref, sink_ref,
                        sk_ref, sv_ref, spool_ref,
                        o_ref, nk_ref, nv_ref, npool_ref,
                        knew_ref, vnew_ref, upnew_ref, qsel, bias_ref, attn_f32, cat_ref):
    g = pl.program_id(0)
    last = pl.num_programs(0) - 1
    olane = lax.broadcasted_iota(jnp.int32, (SAMPLE_ROWS, LANES), 1)

    @pl.when(g == 0)
    def _():
        bucket = bucket_ref[...]
        for h in range(N_HEADS_A):
            bias_ref[h] = _bias_from_buckets(bucket, rb_ref, h)
        x = x_ref[...]
        h = _rms(x, g_ref[...]).astype(BF16)
        proj = jnp.dot(h, win_ref[...], preferred_element_type=F32)
        q = proj[:, 0:ATTN_WIDTH] * Q_SCALE
        zeros = jnp.zeros(((NEW_SLOTS - DEC_SEQ) * DEC_BATCH, KV_WIDTH), F32)
        knew_ref[0:SAMPLE_TOK, :] = proj[:, ATTN_WIDTH:ATTN_WIDTH + KV_WIDTH]
        vnew_ref[0:SAMPLE_TOK, :] = proj[:, ATTN_WIDTH + KV_WIDTH:ATTN_WIDTH + 2 * KV_WIDTH]
        knew_ref[SAMPLE_TOK:, :] = zeros
        vnew_ref[SAMPLE_TOK:, :] = zeros
        upnew_ref[...] = proj[:, ATTN_WIDTH + 2 * KV_WIDTH:IN_EVEN]
        qlane = lax.broadcasted_iota(jnp.int32, (SAMPLE_TOK, LANES), 1)
        for hd in range(N_HEADS_A):
            j, half, kvh = hd // 2, hd % 2, hd // GQA_GROUP
            blk = q[:, j * LANES:(j + 1) * LANES]
            if half != kvh:
                blk = pltpu.roll(blk, HEAD_DIM, 1)
            keep = (qlane < HEAD_DIM) if kvh == 0 else (qlane >= HEAD_DIM)
            qsel[hd] = jnp.where(keep, blk, 0.0)

    row0 = pl.multiple_of(g * SEQ_GROUP, SEQ_GROUP)
    pad = [jnp.zeros((WINDOW - SEQ_GROUP * NEW_SLOTS, KV_WIDTH), F32)]
    knew = jnp.concatenate(
        [knew_ref[pl.ds(row0 + b, NEW_SLOTS, stride=DEC_BATCH), :] for b in range(SEQ_GROUP)] + pad, axis=0)
    vnew = jnp.concatenate(
        [vnew_ref[pl.ds(row0 + b, NEW_SLOTS, stride=DEC_BATCH), :] for b in range(SEQ_GROUP)] + pad, axis=0)
    kt_all = jnp.concatenate([sk_ref[b] for b in range(SEQ_GROUP)], axis=1).astype(BF16)
    vt_all = jnp.concatenate([sv_ref[b] for b in range(SEQ_GROUP)], axis=1).astype(BF16)

    qg = jnp.concatenate(
        [qsel[hd, pl.ds(t * DEC_BATCH + row0, SEQ_GROUP), :] for hd in range(N_HEADS_A) for t in range(DEC_SEQ)],
        axis=0).astype(BF16)
    nt = (((1,), (1,)), ((), ()))
    sc = jnp.concatenate(
        [jnp.dot(qg, kt_all, preferred_element_type=F32),
         lax.dot_general(qg, knew.astype(BF16), nt, preferred_element_type=F32)], axis=1)
    valid = jnp.concatenate([bucket_ref[...]] * N_HEADS_A, axis=0) >= 0
    bias = bias_ref[...].reshape(N_HEADS_A * SAMPLE_ROWS, SAMPLE_KEYS)
    sink = jnp.concatenate(
        [jnp.full((SAMPLE_ROWS, 1), sink_ref[hd] * LOG2E, F32) for hd in range(N_HEADS_A)], axis=0)
    p, denom = _sink_softmax(jnp.where(valid, sc + bias, NEG), sink)
    p = p.astype(BF16)
    o_all = (lax.dot_general(p[:, 0:STATE_KEYS], vt_all, nt, preferred_element_type=F32)
             + jnp.dot(p[:, STATE_KEYS:], vnew.astype(BF16), preferred_element_type=F32)) / denom
    for j in range(N_HEADS_A // 2):
        halves = []
        for half in range(2):
            hd = 2 * j + half
            o = o_all[hd * SAMPLE_ROWS:(hd + 1) * SAMPLE_ROWS, :]
            if hd // GQA_GROUP != half:
                o = pltpu.roll(o, HEAD_DIM, 1)
            halves.append(o)
        blk = jnp.where(olane < HEAD_DIM, halves[0], halves[1])
        for t in range(DEC_SEQ):
            attn_f32[pl.ds(t * DEC_BATCH + row0, SEQ_GROUP), j * LANES:(j + 1) * LANES] = (
                blk[t * SEQ_GROUP:(t + 1) * SEQ_GROUP, :])

    lane = lax.broadcasted_iota(jnp.int32, (KV_WIDTH, WINDOW), 1)
    for new, old_ref, out_ref in ((knew, sk_ref, nk_ref), (vnew, sv_ref, nv_ref)):
        new_t = new.T
        for b in range(SEQ_GROUP):
            tail = pltpu.roll(new_t, (WINDOW - DEC_SEQ - b * NEW_SLOTS) % WINDOW, 1)
            out_ref[b] = jnp.where(lane >= WINDOW - DEC_SEQ, tail, pltpu.roll(old_ref[b], WINDOW - DEC_SEQ, 1))

    @pl.when(g == last)
    def _():
        cat_ref[:, 0:ATTN_WIDTH] = attn_f32[...].astype(BF16)
        for r in range(POOL_BUF):
            if r < POOL_BUF - DEC_SEQ:
                npool_ref[r] = spool_ref[r + DEC_SEQ]
            else:
                t = r - (POOL_BUF - DEC_SEQ)
                npool_ref[r] = upnew_ref[t * DEC_BATCH:(t + 1) * DEC_BATCH, :]
        for t in range(DEC_SEQ):
            rows = slice(t * DEC_BATCH, (t + 1) * DEC_BATCH)
            for gi, w in enumerate(POOL_WINDOWS):
                lanes = slice(gi * POOL_GROUP, (gi + 1) * POOL_GROUP)
                acc = upnew_ref[rows, lanes]
                for back in range(1, w):
                    tt = t - back
                    if tt >= 0:
                        acc = acc + upnew_ref[tt * DEC_BATCH:(tt + 1) * DEC_BATCH, lanes]
                    else:
                        acc = acc + spool_ref[POOL_BUF + tt, :, lanes]
                cnt = float(min(PAST_LEN + t + 1, w))
                d = (acc / cnt - upnew_ref[rows, lanes]).astype(BF16)
                y = jnp.dot(d, wpool_ref[gi], preferred_element_type=F32) * pscale_ref[:, lanes]
                cat_ref[rows, ATTN_WIDTH + gi * POOL_GROUP:ATTN_WIDTH + (gi + 1) * POOL_GROUP] = y.astype(BF16)
        o_ref[...] = x_ref[...] + jnp.dot(cat_ref[...], wout_ref[...], preferred_element_type=F32)


def _even_sample(x, gain, w_in, w_out, w_pool, pool_scale, rel_bias, sinks, state_k, state_v, state_pool_t):
    bucket = jnp.asarray(_sample_bucket_table())
    smem = pl.BlockSpec(memory_space=pltpu.SMEM)
    sample_block = PROMPT_TOK // SAMPLE_TOK
    return pl.pallas_call(
        _even_sample_kernel,
        out_shape=(
            jax.ShapeDtypeStruct((N_TOK, D_MODEL), F32),
            jax.ShapeDtypeStruct((DEC_BATCH, KV_WIDTH, WINDOW), F32),
            jax.ShapeDtypeStruct((DEC_BATCH, KV_WIDTH, WINDOW), F32),
            jax.ShapeDtypeStruct((POOL_BUF, DEC_BATCH, POOL_WIDTH), F32),
        ),
        grid=(N_SEQ_GROUPS,),
        in_specs=[
            pl.BlockSpec((SAMPLE_TOK, D_MODEL), lambda g: (sample_block, 0)),
            _resident((1, D_MODEL)),
            _resident((D_MODEL, IN_EVEN)),
            _resident((ATTN_WIDTH + POOL_WIDTH, D_MODEL)),
            _resident((N_POOL_GROUPS, POOL_GROUP, POOL_GROUP)),
            _resident((1, POOL_WIDTH)),
            _resident((SAMPLE_ROWS, SAMPLE_KEYS)),
            smem,
            smem,
            pl.BlockSpec((SEQ_GROUP, KV_WIDTH, WINDOW), lambda g: (g, 0, 0)),
            pl.BlockSpec((SEQ_GROUP, KV_WIDTH, WINDOW), lambda g: (g, 0, 0)),
            _resident((POOL_BUF, DEC_BATCH, POOL_WIDTH)),
        ],
        out_specs=(
            pl.BlockSpec((SAMPLE_TOK, D_MODEL), lambda g: (sample_block, 0)),
            pl.BlockSpec((SEQ_GROUP, KV_WIDTH, WINDOW), lambda g: (g, 0, 0)),
            pl.BlockSpec((SEQ_GROUP, KV_WIDTH, WINDOW), lambda g: (g, 0, 0)),
            pl.BlockSpec((POOL_BUF, DEC_BATCH, POOL_WIDTH), lambda g: (0, 0, 0)),
        ),
        scratch_shapes=[
            pltpu.VMEM((NEW_SLOTS * DEC_BATCH, KV_WIDTH), F32),
            pltpu.VMEM((NEW_SLOTS * DEC_BATCH, KV_WIDTH), F32),
            pltpu.VMEM((SAMPLE_TOK, POOL_WIDTH), F32),
            pltpu.VMEM((N_HEADS_A, SAMPLE_TOK, LANES), F32),
            pltpu.VMEM((N_HEADS_A, SAMPLE_ROWS, SAMPLE_KEYS), F32),
            pltpu.VMEM((SAMPLE_TOK, ATTN_WIDTH), F32),
            pltpu.VMEM((SAMPLE_TOK, ATTN_WIDTH + POOL_WIDTH), BF16),
        ],
        input_output_aliases={0: 0},
        compiler_params=pltpu.CompilerParams(
            dimension_semantics=("arbitrary",), vmem_limit_bytes=VMEM_LIMIT_BYTES),
        name="even_mixer_sample",
    )(x, gain.reshape(1, D_MODEL), w_in, w_out, w_pool, pool_scale.reshape(1, POOL_WIDTH), bucket,
      rel_bias, sinks, state_k, state_v, state_pool_t)


def _sgu_front(x, g_ref, win_ref, gv_ref):
    h = _rms(x, g_ref[...]).astype(BF16)
    uv = jax.nn.gelu(jnp.dot(h, win_ref[...], preferred_element_type=F32))
    return uv[:, 0:SGU_WIDTH], _rms(uv[:, SGU_WIDTH:], gv_ref[...])


def _odd_prompt_kernel(x_ref, g_ref, win_ref, gv_ref, ws_ref, bs_ref, wout_ref, o_ref, gated_ref):
    x = x_ref[...]
    u, v = _sgu_front(x, g_ref, win_ref, gv_ref)
    vb = v.astype(BF16)
    r = lax.broadcasted_iota(jnp.int32, (CHUNK, CHUNK), 0)
    c = lax.broadcasted_iota(jnp.int32, (CHUNK, CHUNK), 1)
    for gi in range(SGU_GROUPS):
        w = jnp.where(r >= c, ws_ref[gi], 0.0).astype(BF16)
        bias = jnp.concatenate([bs_ref[gi]] * (SGU_GROUP_W // LANES), axis=1)
        lanes = slice(gi * SGU_GROUP_W, (gi + 1) * SGU_GROUP_W)
        for ci in range(ODD_TM // CHUNK):
            rows = slice(ci * CHUNK, (ci + 1) * CHUNK)
            mixed = jnp.dot(w, vb[rows, lanes], preferred_element_type=F32) + bias
            gated_ref[rows, lanes] = (u[rows, lanes] * mixed).astype(BF16)
    o_ref[...] = x + jnp.dot(gated_ref[...], wout_ref[...], preferred_element_type=F32)


def _odd_prompt(x, gain, w_in, g_v, w_s, b_s, w_out):
    bias = jnp.broadcast_to(b_s[:, :, None], (SGU_GROUPS, CHUNK, LANES))
    return pl.pallas_call(
        _odd_prompt_kernel,
        out_shape=jax.ShapeDtypeStruct((N_TOK, D_MODEL), F32),
        grid=(PROMPT_TOK // ODD_TM,),
        in_specs=[
            pl.BlockSpec((ODD_TM, D_MODEL), lambda i: (i, 0)),
            _resident((1, D_MODEL)),
            _resident((D_MODEL, 2 * SGU_WIDTH)),
            _resident((1, SGU_WIDTH)),
            _resident((SGU_GROUPS, CHUNK, CHUNK)),
            _resident((SGU_GROUPS, CHUNK, LANES)),
            _resident((SGU_WIDTH, D_MODEL)),
        ],
        out_specs=pl.BlockSpec((ODD_TM, D_MODEL), lambda i: (i, 0)),
        scratch_shapes=[pltpu.VMEM((ODD_TM, SGU_WIDTH), BF16)],
        input_output_aliases={0: 0},
        compiler_params=pltpu.CompilerParams(
            dimension_semantics=("arbitrary",), vmem_limit_bytes=VMEM_LIMIT_BYTES),
        name="odd_mixer_prompt",
    )(x, gain.reshape(1, D_MODEL), w_in, g_v.reshape(1, SGU_WIDTH), w_s, bias, w_out)


def _odd_sample_kernel(x_ref, g_ref, win_ref, gv_ref, coef_ref, bias_ref, wout_ref, o_ref, v_ref):
    x = x_ref[...]
    u, v = _sgu_front(x, g_ref, win_ref, gv_ref)
    v_ref[...] = v
    gated = []
    for t in range(DEC_SEQ):
        mixed = bias_ref[t:t + 1, :]
        for s in range(t + 1):
            mixed = mixed + coef_ref[t, s:s + 1, :] * v[s * DEC_BATCH:(s + 1) * DEC_BATCH, :]
        gated.append((u[t * DEC_BATCH:(t + 1) * DEC_BATCH, :] * mixed).astype(BF16))
    o_ref[...] = x + jnp.dot(jnp.concatenate(gated, axis=0), wout_ref[...], preferred_element_type=F32)


def _odd_sample(x, gain, w_in, g_v, w_s, b_s, w_out):
    coef = jnp.repeat(jnp.transpose(w_s[:, :DEC_SEQ, :DEC_SEQ], (1, 2, 0)), SGU_GROUP_W, axis=-1)
    bias = jnp.repeat(b_s[:, :DEC_SEQ].T, SGU_GROUP_W, axis=-1)
    sample_block = PROMPT_TOK // SAMPLE_TOK
    return pl.pallas_call(
        _odd_sample_kernel,
        out_shape=(
            jax.ShapeDtypeStruct((N_TOK, D_MODEL), F32),
            jax.ShapeDtypeStruct((SAMPLE_TOK, SGU_WIDTH), F32),
        ),
        grid=(1,),
        in_specs=[
            pl.BlockSpec((SAMPLE_TOK, D_MODEL), lambda i: (sample_block, 0)),
            _resident((1, D_MODEL)),
            _resident((D_MODEL, 2 * SGU_WIDTH)),
            _resident((1, SGU_WIDTH)),
            _resident((DEC_SEQ, DEC_SEQ, SGU_WIDTH)),
            _resident((DEC_SEQ, SGU_WIDTH)),
            _resident((SGU_WIDTH, D_MODEL)),
        ],
        out_specs=(
            pl.BlockSpec((SAMPLE_TOK, D_MODEL), lambda i: (sample_block, 0)),
            pl.BlockSpec((SAMPLE_TOK, SGU_WIDTH), lambda i: (0, 0)),
        ),
        input_output_aliases={0: 0},
        compiler_params=pltpu.CompilerParams(
            dimension_semantics=("arbitrary",), vmem_limit_bytes=VMEM_LIMIT_BYTES),
        name="odd_mixer_sample",
    )(x, gain.reshape(1, D_MODEL), w_in, g_v.reshape(1, SGU_WIDTH), coef, bias, w_out)


def _to_step_major(a):
    return jnp.swapaxes(a, 0, 1).reshape(SAMPLE_TOK, a.shape[-1])


def _to_batch_major(a):
    return jnp.swapaxes(a.reshape(DEC_SEQ, DEC_BATCH, a.shape[-1]), 0, 1)


def _window_to_lanes(w):
    return jnp.transpose(w, (0, 2, 3, 1)).reshape(DEC_BATCH, KV_WIDTH, WINDOW)


def _window_from_lanes(w):
    return jnp.transpose(w.reshape(DEC_BATCH, N_KV_A, HEAD_DIM, WINDOW), (0, 3, 1, 2))


def kernel(x_prompt, x_sample, state_win_k, state_win_v, state_pool, rel_bias, norm_gains, final_gain,
           ffn_gate, ffn_up, ffn_down, w_in_even, w_out_even, attn_sinks, w_pool, pool_scale,
           w_in_odd, sgu_norm, w_spatial, b_spatial, w_out_odd):
    xp = x_prompt.reshape(PROMPT_TOK, D_MODEL)
    xs = _to_step_major(x_sample)

    kp_l, vp_l, pp_l, ks_l, vs_l, ps_l, sv_l = [], [], [], [], [], [], []
    x = None
    y_prompt = y_sample = None
    for l in range(DEPTH):
        fa = (norm_gains[l, 0], ffn_gate, ffn_up, ffn_down, l, 0)
        fb = (norm_gains[l, 2], ffn_gate, ffn_up, ffn_down, l, 1)
        if l == 0:
            x = _ffn(xp, *fa, tail_in=xs)
        else:
            x = _ffn(x, *fa)
        g1 = norm_gains[l, 1]
        if l % 2 == 0:
            e = l // 2
            w_in, w_out = w_in_even[e].astype(BF16), w_out_even[e].astype(BF16)
            wp = w_pool[e].astype(BF16)
            x, kp, vp, pp = _even_prompt(x, g1, w_in, w_out, wp, pool_scale[e], rel_bias, attn_sinks[e])
            x, nk, nv, npool = _even_sample(
                x, g1, w_in, w_out, wp, pool_scale[e], rel_bias, attn_sinks[e],
                _window_to_lanes(state_win_k[e]), _window_to_lanes(state_win_v[e]),
                jnp.swapaxes(state_pool[e], 0, 1))
            kp_l.append(kp.reshape(BATCH, WINDOW, N_KV_A, HEAD_DIM))
            vp_l.append(vp.reshape(BATCH, WINDOW, N_KV_A, HEAD_DIM))
            pp_l.append(pp[:, 2 * SUBLANES - POOL_BUF:, :])
            ks_l.append(_window_from_lanes(nk))
            vs_l.append(_window_from_lanes(nv))
            ps_l.append(jnp.swapaxes(npool, 0, 1))
        else:
            o = l // 2
            w_in, w_out = w_in_odd[o].astype(BF16), w_out_odd[o].astype(BF16)
            x = _odd_prompt(x, g1, w_in, sgu_norm[o], w_spatial[o], b_spatial[o], w_out)
            x, sv_new = _odd_sample(x, g1, w_in, sgu_norm[o], w_spatial[o], b_spatial[o], w_out)
            sv_l.append(_to_batch_major(sv_new))
        if l < DEPTH - 1:
            x = _ffn(x, *fb)
        else:
            y_prompt, y_sample = _ffn(x, *fb, final_gain=final_gain)

    return (y_prompt.reshape(BATCH, SEQ, D_MODEL), _to_batch_major(y_sample),
            jnp.stack(kp_l), jnp.stack(vp_l), jnp.stack(pp_l),
            jnp.stack(ks_l), jnp.stack(vs_l), jnp.stack(ps_l), jnp.stack(sv_l))
```

```python
import functools
import math

import numpy as np
import jax
import jax.numpy as jnp
from jax import lax
from jax.experimental import pallas as pl
from jax.experimental.pallas import tpu as pltpu

F32 = jnp.float32
BF16 = jnp.bfloat16

D_MODEL = 1024
BATCH = 8
SEQ = 2048
DEPTH = 2
DEC_BATCH = 128
DEC_SEQ = 4
PAST_LEN = 16384
N_HEADS_A = 8
N_KV_A = 2
HEAD_DIM = 64
GQA_GROUP = N_HEADS_A // N_KV_A
WINDOW = 128
ATTN_WIDTH = N_HEADS_A * HEAD_DIM
KV_WIDTH = N_KV_A * HEAD_DIM
N_BUCKETS = 32
MAX_DISTANCE = 128
POOL_WINDOWS = (2, 4, 8, 16)
N_POOL_GROUPS = len(POOL_WINDOWS)
POOL_GROUP = 128
POOL_WIDTH = N_POOL_GROUPS * POOL_GROUP
POOL_BUF = max(POOL_WINDOWS) - 1
IN_EVEN = ATTN_WIDTH + 2 * KV_WIDTH + POOL_WIDTH
CHUNK = 128
SGU_WIDTH = 1024
SGU_GROUPS = 4
SGU_GROUP_W = SGU_WIDTH // SGU_GROUPS
D_FF = 2816
EPS = 1e-6
NEG = -1e30

LANES = 128
SUBLANES = 8
VMEM_LIMIT_BYTES = 56 * 1024 * 1024

PROMPT_TOK = BATCH * SEQ
SAMPLE_TOK = DEC_BATCH * DEC_SEQ
N_TOK = PROMPT_TOK + SAMPLE_TOK

TM = SAMPLE_TOK
assert PROMPT_TOK % TM == 0
PROMPT_BLOCKS = PROMPT_TOK // TM
EVEN_TM = 1024
assert SEQ % EVEN_TM == 0 and EVEN_TM % WINDOW == 0
TILES_PER_SEQ = SEQ // EVEN_TM
QBLK_PER_TILE = EVEN_TM // WINDOW
FFN_TM = 2 * TM
assert PROMPT_TOK % FFN_TM == 0
FFN_PROMPT_STEPS = PROMPT_TOK // FFN_TM
ODD_TM = 1024
assert PROMPT_TOK % ODD_TM == 0 and ODD_TM % CHUNK == 0
FF_CHUNK = 256
N_FF_CHUNKS = D_FF // FF_CHUNK
POOL_HEAD = 2 * SUBLANES
assert all(w & (w - 1) == 0 and w <= POOL_HEAD for w in POOL_WINDOWS)
POOL_CTX = 32
SEQ_GROUP = 8
N_SEQ_GROUPS = DEC_BATCH // SEQ_GROUP
STATE_KEYS = SEQ_GROUP * WINDOW
NEW_SLOTS = SUBLANES
assert DEC_SEQ <= NEW_SLOTS and SEQ_GROUP * NEW_SLOTS <= WINDOW
SAMPLE_KEYS = STATE_KEYS + WINDOW
SAMPLE_ROWS = DEC_SEQ * SEQ_GROUP


def _t5_bucket_np(dist):
    n = np.maximum(dist, 0)
    max_exact = N_BUCKETS // 2
    nf = np.maximum(n, 1).astype(np.float32)
    large = max_exact + (np.log(nf / np.float32(max_exact)) / np.float32(math.log(MAX_DISTANCE / max_exact))
                         * np.float32(N_BUCKETS - max_exact)).astype(np.int32)
    large = np.minimum(large, N_BUCKETS - 1)
    return np.where(n < max_exact, n, large).astype(np.int32)


def _prompt_bucket_table():
    qi = np.arange(WINDOW)[:, None]
    c = np.arange(WINDOW)[None, :]
    dist = np.where(c > qi, qi + WINDOW - c, qi - c)
    return _t5_bucket_np(dist)


def _sample_bucket_table():
    t = (np.arange(SAMPLE_ROWS) // SEQ_GROUP)[:, None]
    b = (np.arange(SAMPLE_ROWS) % SEQ_GROUP)[:, None]
    col = np.arange(SAMPLE_KEYS)[None, :]
    is_state = col < STATE_KEYS
    slot = (col - STATE_KEYS) % NEW_SLOTS
    is_new = (col >= STATE_KEYS) & (col < STATE_KEYS + SEQ_GROUP * NEW_SLOTS) & (slot < DEC_SEQ)
    kb = np.where(is_state, col // WINDOW, (col - STATE_KEYS) // NEW_SLOTS)
    kpos = np.where(is_state, col % WINDOW, WINDOW + slot)
    dist = (WINDOW + t) - kpos
    valid = (is_state | is_new) & (kb == b) & (dist >= 0) & (dist < WINDOW)
    return np.where(valid, _t5_bucket_np(dist), -1).astype(np.int32)


def _rms(x, g):
    y = x * lax.rsqrt(jnp.mean(x * x, axis=-1, keepdims=True) + EPS)
    return y * g


LOG2E = math.log2(math.e)
Q_SCALE = HEAD_DIM ** -0.5 * LOG2E


def _bias_from_buckets(bucket, rb_ref, h):
    t = jnp.zeros(bucket.shape, F32)
    for bkt in range(N_BUCKETS):
        t = jnp.where(bucket == bkt, rb_ref[bkt, h] * LOG2E, t)
    return t


def _sink_softmax(s, sink):
    m = jnp.maximum(jnp.max(s, axis=-1, keepdims=True), sink)
    p = jnp.exp2(s - m)
    return p, jnp.sum(p, axis=-1, keepdims=True) + jnp.exp2(sink - m)


def _softmax_pv(s, valid, bias, sink, v):
    p, denom = _sink_softmax(jnp.where(valid, s + bias, NEG), sink)
    return jnp.dot(p.astype(BF16), v, preferred_element_type=F32) / denom


def _ffn_kernel(*refs, layer, half, tail_in, final_split):
    refs = list(refs)
    x_ref = refs.pop(0)
    xt_ref = refs.pop(0) if tail_in else None
    g_ref, wg_hbm, wu_hbm, wd_hbm = refs[:4]
    del refs[:4]
    fg_ref = refs.pop(0) if final_split else None
    o_ref = refs.pop(0)
    ot_ref = refs.pop(0) if final_split else None
    wg, wu, wd, stage_in, stage_dn, sem = refs
    i = pl.program_id(0)

    def normed(x):
        return _rms(x, g_ref[...]).astype(BF16)

    def result(x, y):
        out = x + 0.5 * y
        return _rms(out, fg_ref[...]) if final_split else out

    def chunk_copies(c, slot):
        cols = pl.ds(c * FF_CHUNK, FF_CHUNK)
        return (
            pltpu.make_async_copy(wg_hbm.at[layer, half, :, cols], stage_in.at[0, slot], sem.at[0, slot]),
            pltpu.make_async_copy(wu_hbm.at[layer, half, :, cols], stage_in.at[1, slot], sem.at[1, slot]),
            pltpu.make_async_copy(wd_hbm.at[layer, half, cols, :], stage_dn.at[slot], sem.at[2, slot]),
        )

    @pl.when(i == 0)
    def _():
        for c in range(min(2, N_FF_CHUNKS)):
            for cp in chunk_copies(c, c % 2):
                cp.start()
        if tail_in:
            x = jnp.concatenate([xt_ref[:, t, :] for t in range(DEC_SEQ)], axis=0)
        else:
            x = x_ref[0:TM, :]
        h = normed(x)
        acc = jnp.zeros((TM, D_MODEL), F32)
        for c in range(N_FF_CHUNKS):
            slot = c % 2
            cols = slice(c * FF_CHUNK, (c + 1) * FF_CHUNK)
            for cp in chunk_copies(c, slot):
                cp.wait()
            wg[:, cols] = stage_in[0, slot].astype(BF16)
            wu[:, cols] = stage_in[1, slot].astype(BF16)
            wd[cols, :] = stage_dn[slot].astype(BF16)
            if c + 2 < N_FF_CHUNKS:
                for cp in chunk_copies(c + 2, slot):
                    cp.start()
            gate = jnp.dot(h, wg[:, cols], preferred_element_type=F32)
            up = jnp.dot(h, wu[:, cols], preferred_element_type=F32)
            act = (jax.nn.silu(gate) * up).astype(BF16)
            acc = acc + jnp.dot(act, wd[cols, :], preferred_element_type=F32)
        out = result(x, acc)
        if final_split:
            for t in range(DEC_SEQ):
                ot_ref[:, t, :] = out[t * DEC_BATCH:(t + 1) * DEC_BATCH, :]
        else:
            o_ref[0:TM, :] = out

    def half_block(hf, carry):
        rows = pl.ds(pl.multiple_of(hf * TM, TM), TM)
        x = x_ref[rows, :]
        h = normed(x)
        gate = jnp.dot(h, wg[...], preferred_element_type=F32)
        up = jnp.dot(h, wu[...], preferred_element_type=F32)
        act = (jax.nn.silu(gate) * up).astype(BF16)
        o_ref[rows, :] = result(x, jnp.dot(act, wd[...], preferred_element_type=F32))
        return carry

    @pl.when(i > 0)
    def _():
        lax.fori_loop(0, FFN_TM // TM, half_block, 0)


def _resident(shape):
    return pl.BlockSpec(shape, lambda *_: (0,) * len(shape), pipeline_mode=pl.Buffered(1))


def _ffn(x_in, gain, wg, wu, wd, layer, half, *, tail_in=None, final_gain=None):
    hbm = pl.BlockSpec(memory_space=pl.ANY)
    row_block = pl.BlockSpec((FFN_TM, D_MODEL), lambda i: (jnp.where(i == 0, FFN_PROMPT_STEPS, i - 1), 0))
    prompt_block = pl.BlockSpec((FFN_TM, D_MODEL), lambda i: (jnp.maximum(i - 1, 0), 0))
    sample_block = pl.BlockSpec((DEC_BATCH, DEC_SEQ, D_MODEL), lambda i: (0, 0, 0))
    if tail_in is None:
        args, in_specs = [x_in], [row_block]
    else:
        args, in_specs = [x_in, tail_in], [prompt_block, sample_block]
    args += [gain.reshape(1, D_MODEL), wg, wu, wd]
    in_specs += [_resident((1, D_MODEL)), hbm, hbm, hbm]
    if final_gain is None:
        out_shape = jax.ShapeDtypeStruct((N_TOK, D_MODEL), F32)
        out_specs = row_block
    else:
        args.append(final_gain.reshape(1, D_MODEL))
        in_specs.append(_resident((1, D_MODEL)))
        out_shape = (jax.ShapeDtypeStruct((PROMPT_TOK, D_MODEL), F32),
                     jax.ShapeDtypeStruct((DEC_BATCH, DEC_SEQ, D_MODEL), F32))
        out_specs = (prompt_block, sample_block)
    aliases = {0: 0} if (tail_in is None and final_gain is None) else {}
    return pl.pallas_call(
        functools.partial(_ffn_kernel, layer=layer, half=half, tail_in=tail_in is not None,
                          final_split=final_gain is not None),
        out_shape=out_shape,
        grid=(FFN_PROMPT_STEPS + 1,),
        in_specs=in_specs,
        out_specs=out_specs,
        scratch_shapes=[
            pltpu.VMEM((D_MODEL, D_FF), BF16),
            pltpu.VMEM((D_MODEL, D_FF), BF16),
            pltpu.VMEM((D_FF, D_MODEL), BF16),
            pltpu.VMEM((2, 2, D_MODEL, FF_CHUNK), F32),
            pltpu.VMEM((2, FF_CHUNK, D_MODEL), F32),
            pltpu.SemaphoreType.DMA((3, 2)),
        ],
        input_output_aliases=aliases,
        compiler_params=pltpu.CompilerParams(
            dimension_semantics=("arbitrary",), vmem_limit_bytes=VMEM_LIMIT_BYTES),
        name="macaron_half",
    )(*args)


def _even_prompt_kernel(x_ref, g_ref, win_ref, wout_ref, wpool_ref, pscale_ref, bucket_ref, rb_ref, sink_ref,
                        o_ref, klast_ref, vlast_ref, plast_ref,
                        kbuf, kbuf_r, vbuf, vbuf_r, zb, s2b, s4b, s8b, bias_ref, cat_ref):
    b = pl.program_id(0)
    s = pl.program_id(1)
    last = pl.num_programs(1) - 1

    @pl.when((b == 0) & (s == 0))
    def _():
        bucket = bucket_ref[...]
        for h in range(N_HEADS_A):
            bias_ref[h] = _bias_from_buckets(bucket, rb_ref, h)

    @pl.when(s == 0)
    def _():
        for buf in (kbuf, kbuf_r, vbuf, vbuf_r):
            buf[0:WINDOW, :] = jnp.zeros((WINDOW, KV_WIDTH), BF16)
        zb[0:POOL_CTX, :] = jnp.zeros((POOL_CTX, POOL_WIDTH), F32)

    @pl.when(s > 0)
    def _():
        for buf in (kbuf, kbuf_r, vbuf, vbuf_r):
            buf[0:WINDOW, :] = buf[EVEN_TM:EVEN_TM + WINDOW, :]
        zb[0:POOL_CTX, :] = zb[EVEN_TM:EVEN_TM + POOL_CTX, :]

    x = x_ref[...]
    h = _rms(x, g_ref[...]).astype(BF16)
    proj = jnp.dot(h, win_ref[...], preferred_element_type=F32)
    q = proj[:, 0:ATTN_WIDTH] * Q_SCALE
    k = proj[:, ATTN_WIDTH:ATTN_WIDTH + KV_WIDTH]
    v = proj[:, ATTN_WIDTH + KV_WIDTH:ATTN_WIDTH + 2 * KV_WIDTH]
    up = proj[:, ATTN_WIDTH + 2 * KV_WIDTH:IN_EVEN]

    qlane = lax.broadcasted_iota(jnp.int32, (EVEN_TM, ATTN_WIDTH), 1) % LANES
    q_half = (jnp.where(qlane < HEAD_DIM, q, 0.0).astype(BF16),
              jnp.where(qlane >= HEAD_DIM, q, 0.0).astype(BF16))
    kbuf[WINDOW:, :] = k.astype(BF16)
    vbuf[WINDOW:, :] = v.astype(BF16)
    kbuf_r[WINDOW:, :] = pltpu.roll(k, HEAD_DIM, 1).astype(BF16)
    vbuf_r[WINDOW:, :] = pltpu.roll(v, HEAD_DIM, 1).astype(BF16)

    qrow = lax.broadcasted_iota(jnp.int32, (WINDOW, WINDOW), 0)
    kcol = lax.broadcasted_iota(jnp.int32, (WINDOW, WINDOW), 1)
    prev = kcol > qrow
    olane = lax.broadcasted_iota(jnp.int32, (WINDOW, LANES), 1)
    head_sets = [[hd for hd in range(N_HEADS_A) if (hd // GQA_GROUP == hd % 2) == plain] for plain in (True, False)]
    for i in range(QBLK_PER_TILE):
        rows = slice(i * WINDOW, (i + 1) * WINDOW)
        keys = slice(i * WINDOW, (i + 2) * WINDOW)
        out = {}
        for heads, kb, vb in zip(head_sets, (kbuf, kbuf_r), (vbuf, vbuf_r)):
            qs = jnp.concatenate(
                [q_half[hd % 2][rows, (hd // 2) * LANES:(hd // 2 + 1) * LANES] for hd in heads], axis=0)
            sc_all = lax.dot_general(qs, kb[keys, :], (((1,), (1,)), ((), ())), preferred_element_type=F32)
            ps, denoms = [], []
            for r, hd in enumerate(heads):
                sc = sc_all[r * WINDOW:(r + 1) * WINDOW, :]
                sc = jnp.where(prev, sc[:, 0:WINDOW], sc[:, WINDOW:]) + bias_ref[hd]
                if i == 0:
                    sc = jnp.where(prev & (qrow >= jnp.where(s > 0, WINDOW, 0)), NEG, sc)
                p, denom = _sink_softmax(sc, sink_ref[hd] * LOG2E)
                ps.append(jnp.concatenate([jnp.where(prev, p, 0.0), jnp.where(prev, 0.0, p)], axis=1).astype(BF16))
                denoms.append(denom)
            o_all = jnp.dot(jnp.concatenate(ps, axis=0), vb[keys, :], preferred_element_type=F32)
            for r, hd in enumerate(heads):
                out[hd] = o_all[r * WINDOW:(r + 1) * WINDOW, :] / denoms[r]
        for j in range(N_HEADS_A // 2):
            cat_ref[rows, j * LANES:(j + 1) * LANES] = jnp.where(
                olane < HEAD_DIM, out[2 * j], out[2 * j + 1]).astype(BF16)

    zb[POOL_CTX:, :] = up
    n = EVEN_TM + POOL_CTX
    s2b[8:, :] = zb[8:n, :] + zb[7:n - 1, :]
    s4b[16:, :] = s2b[16:n, POOL_GROUP:] + s2b[14:n - 2, POOL_GROUP:]
    s8b[24:, :] = s4b[24:n, POOL_GROUP:] + s4b[20:n - 4, POOL_GROUP:]
    s16 = s8b[32:n, POOL_GROUP:] + s8b[24:n - 8, POOL_GROUP:]
    sums = (s2b[POOL_CTX:, 0:POOL_GROUP], s4b[POOL_CTX:, 0:POOL_GROUP], s8b[POOL_CTX:, 0:POOL_GROUP], s16)
    pos1 = lax.broadcasted_iota(jnp.int32, (POOL_HEAD, POOL_GROUP), 0) + s * EVEN_TM + 1
    for g, w in enumerate(POOL_WINDOWS):
        lanes = slice(g * POOL_GROUP, (g + 1) * POOL_GROUP)
        cnt = jnp.minimum(pos1, w).astype(F32)
        mean = jnp.concatenate([sums[g][0:POOL_HEAD] / cnt, sums[g][POOL_HEAD:] * (1.0 / w)], axis=0)
        d = (mean - up[:, lanes]).astype(BF16)
        y = jnp.dot(d, wpool_ref[g], preferred_element_type=F32) * pscale_ref[:, lanes]
        cat_ref[:, ATTN_WIDTH + g * POOL_GROUP:ATTN_WIDTH + (g + 1) * POOL_GROUP] = y.astype(BF16)

    o_ref[...] = x + jnp.dot(cat_ref[...], wout_ref[...], preferred_element_type=F32)

    @pl.when(s == last)
    def _():
        klast_ref[0] = k[EVEN_TM - WINDOW:, :]
        vlast_ref[0] = v[EVEN_TM - WINDOW:, :]
        plast_ref[0] = up[EVEN_TM - 2 * SUBLANES:, :]


def _even_prompt(x, gain, w_in, w_out, w_pool, pool_scale, rel_bias, sinks):
    bucket = jnp.asarray(_prompt_bucket_table())
    smem = pl.BlockSpec(memory_space=pltpu.SMEM)
    return pl.pallas_call(
        _even_prompt_kernel,
        out_shape=(
            jax.ShapeDtypeStruct((N_TOK, D_MODEL), F32),
            jax.ShapeDtypeStruct((BATCH, WINDOW, KV_WIDTH), F32),
            jax.ShapeDtypeStruct((BATCH, WINDOW, KV_WIDTH), F32),
            jax.ShapeDtypeStruct((BATCH, 2 * SUBLANES, POOL_WIDTH), F32),
        ),
        grid=(BATCH, TILES_PER_SEQ),
        in_specs=[
            pl.BlockSpec((EVEN_TM, D_MODEL), lambda b, s: (b * TILES_PER_SEQ + s, 0)),
            _resident((1, D_MODEL)),
            _resident((D_MODEL, IN_EVEN)),
            _resident((ATTN_WIDTH + POOL_WIDTH, D_MODEL)),
            _resident((N_POOL_GROUPS, POOL_GROUP, POOL_GROUP)),
            _resident((1, POOL_WIDTH)),
            _resident((WINDOW, WINDOW)),
            smem,
            smem,
        ],
        out_specs=(
            pl.BlockSpec((EVEN_TM, D_MODEL), lambda b, s: (b * TILES_PER_SEQ + s, 0)),
            pl.BlockSpec((1, WINDOW, KV_WIDTH), lambda b, s: (b, 0, 0)),
            pl.BlockSpec((1, WINDOW, KV_WIDTH), lambda b, s: (b, 0, 0)),
            pl.BlockSpec((1, 2 * SUBLANES, POOL_WIDTH), lambda b, s: (b, 0, 0)),
        ),
        scratch_shapes=[
            pltpu.VMEM((WINDOW + EVEN_TM, KV_WIDTH), BF16),
            pltpu.VMEM((WINDOW + EVEN_TM, KV_WIDTH), BF16),
            pltpu.VMEM((WINDOW + EVEN_TM, KV_WIDTH), BF16),
            pltpu.VMEM((WINDOW + EVEN_TM, KV_WIDTH), BF16),
            pltpu.VMEM((POOL_CTX + EVEN_TM, POOL_WIDTH), F32),
            pltpu.VMEM((POOL_CTX + EVEN_TM, POOL_WIDTH), F32),
            pltpu.VMEM((POOL_CTX + EVEN_TM, POOL_WIDTH - POOL_GROUP), F32),
            pltpu.VMEM((POOL_CTX + EVEN_TM, POOL_WIDTH - 2 * POOL_GROUP), F32),
            pltpu.VMEM((N_HEADS_A, WINDOW, WINDOW), F32),
            pltpu.VMEM((EVEN_TM, ATTN_WIDTH + POOL_WIDTH), BF16),
        ],
        input_output_aliases={0: 0},
        compiler_params=pltpu.CompilerParams(
            dimension_semantics=("arbitrary", "arbitrary"), vmem_limit_bytes=VMEM_LIMIT_BYTES),
        name="even_mixer_prompt",
    )(x, gain.reshape(1, D_MODEL), w_in, w_out, w_pool, pool_scale.reshape(1, POOL_WIDTH), bucket,
      rel_bias, sinks)


def _even_sample_kernel(x_ref, g_ref, win_ref, wout_ref, wpool_ref, pscale_ref, bucket_ref, rb_ref, sink_ref,
                        sk_ref, sv_ref, spool_ref,
                        o_ref, nk_ref, nv_ref, npool_ref,
                        knew_ref, vnew_ref, upnew_ref, qsel, bias_ref, attn_f32, cat_ref):
    g = pl.program_id(0)
    last = pl.num_programs(0) - 1
    olane = lax.broadcasted_iota(jnp.int32, (SAMPLE_ROWS, LANES), 1)

    @pl.when(g == 0)
    def _():
        bucket = bucket_ref[...]
        for h in range(N_HEADS_A):
            bias_ref[h] = _bias_from_buckets(bucket, rb_ref, h)
        x = x_ref[...]
        h = _rms(x, g_ref[...]).astype(BF16)
        proj = jnp.dot(h, win_ref[...], preferred_element_type=F32)
        q = proj[:, 0:ATTN_WIDTH] * Q_SCALE
        zeros = jnp.zeros(((NEW_SLOTS - DEC_SEQ) * DEC_BATCH, KV_WIDTH), F32)
        knew_ref[0:SAMPLE_TOK, :] = proj[:, ATTN_WIDTH:ATTN_WIDTH + KV_WIDTH]
        vnew_ref[0:SAMPLE_TOK, :] = proj[:, ATTN_WIDTH + KV_WIDTH:ATTN_WIDTH + 2 * KV_WIDTH]
        knew_ref[SAMPLE_TOK:, :] = zeros
        vnew_ref[SAMPLE_TOK:, :] = zeros
        upnew_ref[...] = proj[:, ATTN_WIDTH + 2 * KV_WIDTH:IN_EVEN]
        qlane = lax.broadcasted_iota(jnp.int32, (SAMPLE_TOK, LANES), 1)
        for hd in range(N_HEADS_A):
            j, half, kvh = hd // 2, hd % 2, hd // GQA_GROUP
            blk = q[:, j * LANES:(j + 1) * LANES]
            if half != kvh:
                blk = pltpu.roll(blk, HEAD_DIM, 1)
            keep = (qlane < HEAD_DIM) if kvh == 0 else (qlane >= HEAD_DIM)
            qsel[hd] = jnp.where(keep, blk, 0.0)

    row0 = pl.multiple_of(g * SEQ_GROUP, SEQ_GROUP)
    pad = [jnp.zeros((WINDOW - SEQ_GROUP * NEW_SLOTS, KV_WIDTH), F32)]
    knew = jnp.concatenate(
        [knew_ref[pl.ds(row0 + b, NEW_SLOTS, stride=DEC_BATCH), :] for b in range(SEQ_GROUP)] + pad, axis=0)
    vnew = jnp.concatenate(
        [vnew_ref[pl.ds(row0 + b, NEW_SLOTS, stride=DEC_BATCH), :] for b in range(SEQ_GROUP)] + pad, axis=0)
    kt_all = jnp.concatenate([sk_ref[b] for b in range(SEQ_GROUP)], axis=1).astype(BF16)
    vt_all = jnp.concatenate([sv_ref[b] for b in range(SEQ_GROUP)], axis=1).astype(BF16)

    qg = jnp.concatenate(
        [qsel[hd, pl.ds(t * DEC_BATCH + row0, SEQ_GROUP), :] for hd in range(N_HEADS_A) for t in range(DEC_SEQ)],
        axis=0).astype(BF16)
    nt = (((1,), (1,)), ((), ()))
    sc = jnp.concatenate(
        [jnp.dot(qg, kt_all, preferred_element_type=F32),
         lax.dot_general(qg, knew.astype(BF16), nt, preferred_element_type=F32)], axis=1)
    valid = jnp.concatenate([bucket_ref[...]] * N_HEADS_A, axis=0) >= 0
    bias = bias_ref[...].reshape(N_HEADS_A * SAMPLE_ROWS, SAMPLE_KEYS)
    sink = jnp.concatenate(
        [jnp.full((SAMPLE_ROWS, 1), sink_ref[hd] * LOG2E, F32) for hd in range(N_HEADS_A)], axis=0)
    p, denom = _sink_softmax(jnp.where(valid, sc + bias, NEG), sink)
    p = p.astype(BF16)
    o_all = (lax.dot_general(p[:, 0:STATE_KEYS], vt_all, nt, preferred_element_type=F32)
             + jnp.dot(p[:, STATE_KEYS:], vnew.astype(BF16), preferred_element_type=F32)) / denom
    for j in range(N_HEADS_A // 2):
        halves = []
        for half in range(2):
            hd = 2 * j + half
            o = o_all[hd * SAMPLE_ROWS:(hd + 1) * SAMPLE_ROWS, :]
            if hd // GQA_GROUP != half:
                o = pltpu.roll(o, HEAD_DIM, 1)
            halves.append(o)
        blk = jnp.where(olane < HEAD_DIM, halves[0], halves[1])
        for t in range(DEC_SEQ):
            attn_f32[pl.ds(t * DEC_BATCH + row0, SEQ_GROUP), j * LANES:(j + 1) * LANES] = (
                blk[t * SEQ_GROUP:(t + 1) * SEQ_GROUP, :])

    lane = lax.broadcasted_iota(jnp.int32, (KV_WIDTH, WINDOW), 1)
    for new, old_ref, out_ref in ((knew, sk_ref, nk_ref), (vnew, sv_ref, nv_ref)):
        new_t = new.T
        for b in range(SEQ_GROUP):
            tail = pltpu.roll(new_t, (WINDOW - DEC_SEQ - b * NEW_SLOTS) % WINDOW, 1)
            out_ref[b] = jnp.where(lane >= WINDOW - DEC_SEQ, tail, pltpu.roll(old_ref[b], WINDOW - DEC_SEQ, 1))

    @pl.when(g == last)
    def _():
        cat_ref[:, 0:ATTN_WIDTH] = attn_f32[...].astype(BF16)
        for r in range(POOL_BUF):
            if r < POOL_BUF - DEC_SEQ:
                npool_ref[r] = spool_ref[r + DEC_SEQ]
            else:
                t = r - (POOL_BUF - DEC_SEQ)
                npool_ref[r] = upnew_ref[t * DEC_BATCH:(t + 1) * DEC_BATCH, :]
        for t in range(DEC_SEQ):
            rows = slice(t * DEC_BATCH, (t + 1) * DEC_BATCH)
            for gi, w in enumerate(POOL_WINDOWS):
                lanes = slice(gi * POOL_GROUP, (gi + 1) * POOL_GROUP)
                acc = upnew_ref[rows, lanes]
                for back in range(1, w):
                    tt = t - back
                    if tt >= 0:
                        acc = acc + upnew_ref[tt * DEC_BATCH:(tt + 1) * DEC_BATCH, lanes]
                    else:
                        acc = acc + spool_ref[POOL_BUF + tt, :, lanes]
                cnt = float(min(PAST_LEN + t + 1, w))
                d = (acc / cnt - upnew_ref[rows, lanes]).astype(BF16)
                y = jnp.dot(d, wpool_ref[gi], preferred_element_type=F32) * pscale_ref[:, lanes]
                cat_ref[rows, ATTN_WIDTH + gi * POOL_GROUP:ATTN_WIDTH + (gi + 1) * POOL_GROUP] = y.astype(BF16)
        o_ref[...] = x_ref[...] + jnp.dot(cat_ref[...], wout_ref[...], preferred_element_type=F32)


def _even_sample(x, gain, w_in, w_out, w_pool, pool_scale, rel_bias, sinks, state_k, state_v, state_pool_t):
    bucket = jnp.asarray(_sample_bucket_table())
    smem = pl.BlockSpec(memory_space=pltpu.SMEM)
    sample_block = PROMPT_TOK // SAMPLE_TOK
    return pl.pallas_call(
        _even_sample_kernel,
        out_shape=(
            jax.ShapeDtypeStruct((N_TOK, D_MODEL), F32),
            jax.ShapeDtypeStruct((DEC_BATCH, KV_WIDTH, WINDOW), F32),
            jax.ShapeDtypeStruct((DEC_BATCH, KV_WIDTH, WINDOW), F32),
            jax.ShapeDtypeStruct((POOL_BUF, DEC_BATCH, POOL_WIDTH), F32),
        ),
        grid=(N_SEQ_GROUPS,),
        in_specs=[
            pl.BlockSpec((SAMPLE_TOK, D_MODEL), lambda g: (sample_block, 0)),
            _resident((1, D_MODEL)),
            _resident((D_MODEL, IN_EVEN)),
            _resident((ATTN_WIDTH + POOL_WIDTH, D_MODEL)),
            _resident((N_POOL_GROUPS, POOL_GROUP, POOL_GROUP)),
            _resident((1, POOL_WIDTH)),
            _resident((SAMPLE_ROWS, SAMPLE_KEYS)),
            smem,
            smem,
            pl.BlockSpec((SEQ_GROUP, KV_WIDTH, WINDOW), lambda g: (g, 0, 0)),
            pl.BlockSpec((SEQ_GROUP, KV_WIDTH, WINDOW), lambda g: (g, 0, 0)),
            _resident((POOL_BUF, DEC_BATCH, POOL_WIDTH)),
        ],
        out_specs=(
            pl.BlockSpec((SAMPLE_TOK, D_MODEL), lambda g: (sample_block, 0)),
            pl.BlockSpec((SEQ_GROUP, KV_WIDTH, WINDOW), lambda g: (g, 0, 0)),
            pl.BlockSpec((SEQ_GROUP, KV_WIDTH, WINDOW), lambda g: (g, 0, 0)),
            pl.BlockSpec((POOL_BUF, DEC_BATCH, POOL_WIDTH), lambda g: (0, 0, 0)),
        ),
        scratch_shapes=[
            pltpu.VMEM((NEW_SLOTS * DEC_BATCH, KV_WIDTH), F32),
            pltpu.VMEM((NEW_SLOTS * DEC_BATCH, KV_WIDTH), F32),
            pltpu.VMEM((SAMPLE_TOK, POOL_WIDTH), F32),
            pltpu.VMEM((N_HEADS_A, SAMPLE_TOK, LANES), F32),
            pltpu.VMEM((N_HEADS_A, SAMPLE_ROWS, SAMPLE_KEYS), F32),
            pltpu.VMEM((SAMPLE_TOK, ATTN_WIDTH), F32),
            pltpu.VMEM((SAMPLE_TOK, ATTN_WIDTH + POOL_WIDTH), BF16),
        ],
        input_output_aliases={0: 0},
        compiler_params=pltpu.CompilerParams(
            dimension_semantics=("arbitrary",), vmem_limit_bytes=VMEM_LIMIT_BYTES),
        name="even_mixer_sample",
    )(x, gain.reshape(1, D_MODEL), w_in, w_out, w_pool, pool_scale.reshape(1, POOL_WIDTH), bucket,
      rel_bias, sinks, state_k, state_v, state_pool_t)


def _sgu_front(x, g_ref, win_ref, gv_ref):
    h = _rms(x, g_ref[...]).astype(BF16)
    uv = jax.nn.gelu(jnp.dot(h, win_ref[...], preferred_element_type=F32))
    return uv[:, 0:SGU_WIDTH], _rms(uv[:, SGU_WIDTH:], gv_ref[...])


def _odd_prompt_kernel(x_ref, g_ref, win_ref, gv_ref, ws_ref, bs_ref, wout_ref, o_ref, gated_ref):
    x = x_ref[...]
    u, v = _sgu_front(x, g_ref, win_ref, gv_ref)
    vb = v.astype(BF16)
    r = lax.broadcasted_iota(jnp.int32, (CHUNK, CHUNK), 0)
    c = lax.broadcasted_iota(jnp.int32, (CHUNK, CHUNK), 1)
    for gi in range(SGU_GROUPS):
        w = jnp.where(r >= c, ws_ref[gi], 0.0).astype(BF16)
        bias = jnp.concatenate([bs_ref[gi]] * (SGU_GROUP_W // LANES), axis=1)
        lanes = slice(gi * SGU_GROUP_W, (gi + 1) * SGU_GROUP_W)
        for ci in range(ODD_TM // CHUNK):
            rows = slice(ci * CHUNK, (ci + 1) * CHUNK)
            mixed = jnp.dot(w, vb[rows, lanes], preferred_element_type=F32) + bias
            gated_ref[rows, lanes] = (u[rows, lanes] * mixed).astype(BF16)
    o_ref[...] = x + jnp.dot(gated_ref[...], wout_ref[...], preferred_element_type=F32)


def _odd_prompt(x, gain, w_in, g_v, w_s, b_s, w_out):
    bias = jnp.broadcast_to(b_s[:, :, None], (SGU_GROUPS, CHUNK, LANES))
    return pl.pallas_call(
        _odd_prompt_kernel,
        out_shape=jax.ShapeDtypeStruct((N_TOK, D_MODEL), F32),
        grid=(PROMPT_TOK // ODD_TM,),
        in_specs=[
            pl.BlockSpec((ODD_TM, D_MODEL), lambda i: (i, 0)),
            _resident((1, D_MODEL)),
            _resident((D_MODEL, 2 * SGU_WIDTH)),
            _resident((1, SGU_WIDTH)),
            _resident((SGU_GROUPS, CHUNK, CHUNK)),
            _resident((SGU_GROUPS, CHUNK, LANES)),
            _resident((SGU_WIDTH, D_MODEL)),
        ],
        out_specs=pl.BlockSpec((ODD_TM, D_MODEL), lambda i: (i, 0)),
        scratch_shapes=[pltpu.VMEM((ODD_TM, SGU_WIDTH), BF16)],
        input_output_aliases={0: 0},
        compiler_params=pltpu.CompilerParams(
            dimension_semantics=("arbitrary",), vmem_limit_bytes=VMEM_LIMIT_BYTES),
        name="odd_mixer_prompt",
    )(x, gain.reshape(1, D_MODEL), w_in, g_v.reshape(1, SGU_WIDTH), w_s, bias, w_out)


def _odd_sample_kernel(x_ref, g_ref, win_ref, gv_ref, coef_ref, bias_ref, wout_ref, o_ref, v_ref):
    x = x_ref[...]
    u, v = _sgu_front(x, g_ref, win_ref, gv_ref)
    for t in range(DEC_SEQ):
        v_ref[:, t, :] = v[t * DEC_BATCH:(t + 1) * DEC_BATCH, :]
    gated = []
    for t in range(DEC_SEQ):
        mixed = bias_ref[t:t + 1, :]
        for s in range(t + 1):
            mixed = mixed + coef_ref[t, s:s + 1, :] * v[s * DEC_BATCH:(s + 1) * DEC_BATCH, :]
        gated.append((u[t * DEC_BATCH:(t + 1) * DEC_BATCH, :] * mixed).astype(BF16))
    o_ref[...] = x + jnp.dot(jnp.concatenate(gated, axis=0), wout_ref[...], preferred_element_type=F32)


def _odd_sample(x, gain, w_in, g_v, w_s, b_s, w_out):
    coef = jnp.repeat(jnp.transpose(w_s[:, :DEC_SEQ, :DEC_SEQ], (1, 2, 0)), SGU_GROUP_W, axis=-1)
    bias = jnp.repeat(b_s[:, :DEC_SEQ].T, SGU_GROUP_W, axis=-1)
    sample_block = PROMPT_TOK // SAMPLE_TOK
    return pl.pallas_call(
        _odd_sample_kernel,
        out_shape=(
            jax.ShapeDtypeStruct((N_TOK, D_MODEL), F32),
            jax.ShapeDtypeStruct((DEC_BATCH, DEC_SEQ, SGU_WIDTH), F32),
        ),
        grid=(1,),
        in_specs=[
            pl.BlockSpec((SAMPLE_TOK, D_MODEL), lambda i: (sample_block, 0)),
            _resident((1, D_MODEL)),
            _resident((D_MODEL, 2 * SGU_WIDTH)),
            _resident((1, SGU_WIDTH)),
            _resident((DEC_SEQ, DEC_SEQ, SGU_WIDTH)),
            _resident((DEC_SEQ, SGU_WIDTH)),
            _resident((SGU_WIDTH, D_MODEL)),
        ],
        out_specs=(
            pl.BlockSpec((SAMPLE_TOK, D_MODEL), lambda i: (sample_block, 0)),
            pl.BlockSpec((DEC_BATCH, DEC_SEQ, SGU_WIDTH), lambda i: (0, 0, 0)),
        ),
        input_output_aliases={0: 0},
        compiler_params=pltpu.CompilerParams(
            dimension_semantics=("arbitrary",), vmem_limit_bytes=VMEM_LIMIT_BYTES),
        name="odd_mixer_sample",
    )(x, gain.reshape(1, D_MODEL), w_in, g_v.reshape(1, SGU_WIDTH), coef, bias, w_out)


def _window_to_lanes(w):
    return jnp.transpose(w, (0, 2, 3, 1)).reshape(DEC_BATCH, KV_WIDTH, WINDOW)


def _window_from_lanes(w):
    return jnp.transpose(w.reshape(DEC_BATCH, N_KV_A, HEAD_DIM, WINDOW), (0, 3, 1, 2))


def kernel(x_prompt, x_sample, state_win_k, state_win_v, state_pool, rel_bias, norm_gains, final_gain,
           ffn_gate, ffn_up, ffn_down, w_in_even, w_out_even, attn_sinks, w_pool, pool_scale,
           w_in_odd, sgu_norm, w_spatial, b_spatial, w_out_odd):
    xp = x_prompt.reshape(PROMPT_TOK, D_MODEL)

    kp_l, vp_l, pp_l, ks_l, vs_l, ps_l, sv_l = [], [], [], [], [], [], []
    x = None
    y_prompt = y_sample = None
    for l in range(DEPTH):
        fa = (norm_gains[l, 0], ffn_gate, ffn_up, ffn_down, l, 0)
        fb = (norm_gains[l, 2], ffn_gate, ffn_up, ffn_down, l, 1)
        if l == 0:
            x = _ffn(xp, *fa, tail_in=x_sample)
        else:
            x = _ffn(x, *fa)
        g1 = norm_gains[l, 1]
        if l % 2 == 0:
            e = l // 2
            w_in, w_out = w_in_even[e].astype(BF16), w_out_even[e].astype(BF16)
            wp = w_pool[e].astype(BF16)
            x, kp, vp, pp = _even_prompt(x, g1, w_in, w_out, wp, pool_scale[e], rel_bias, attn_sinks[e])
            x, nk, nv, npool = _even_sample(
                x, g1, w_in, w_out, wp, pool_scale[e], rel_bias, attn_sinks[e],
                _window_to_lanes(state_win_k[e]), _window_to_lanes(state_win_v[e]),
                jnp.swapaxes(state_pool[e], 0, 1))
            kp_l.append(kp.reshape(BATCH, WINDOW, N_KV_A, HEAD_DIM))
            vp_l.append(vp.reshape(BATCH, WINDOW, N_KV_A, HEAD_DIM))
            pp_l.append(pp[:, 2 * SUBLANES - POOL_BUF:, :])
            ks_l.append(_window_from_lanes(nk))
            vs_l.append(_window_from_lanes(nv))
            ps_l.append(jnp.swapaxes(npool, 0, 1))
        else:
            o = l // 2
            w_in, w_out = w_in_odd[o].astype(BF16), w_out_odd[o].astype(BF16)
            x = _odd_prompt(x, g1, w_in, sgu_norm[o], w_spatial[o], b_spatial[o], w_out)
            x, sv_new = _odd_sample(x, g1, w_in, sgu_norm[o], w_spatial[o], b_spatial[o], w_out)
            sv_l.append(sv_new)
        if l < DEPTH - 1:
            x = _ffn(x, *fb)
        else:
            y_prompt, y_sample = _ffn(x, *fb, final_gain=final_gain)

    return (y_prompt.reshape(BATCH, SEQ, D_MODEL), y_sample,
            jnp.stack(kp_l), jnp.stack(vp_l), jnp.stack(pp_l),
            jnp.stack(ks_l), jnp.stack(vs_l), jnp.stack(ps_l), jnp.stack(sv_l))
```

```python
import functools
import math

import numpy as np
import jax
import jax.numpy as jnp
from jax import lax
from jax.experimental import pallas as pl
from jax.experimental.pallas import tpu as pltpu

F32 = jnp.float32
BF16 = jnp.bfloat16

D_MODEL = 1024
BATCH = 8
SEQ = 2048
DEPTH = 2
DEC_BATCH = 128
DEC_SEQ = 4
PAST_LEN = 16384
N_HEADS_A = 8
N_KV_A = 2
HEAD_DIM = 64
GQA_GROUP = N_HEADS_A // N_KV_A
WINDOW = 128
ATTN_WIDTH = N_HEADS_A * HEAD_DIM
KV_WIDTH = N_KV_A * HEAD_DIM
N_BUCKETS = 32
MAX_DISTANCE = 128
POOL_WINDOWS = (2, 4, 8, 16)
N_POOL_GROUPS = len(POOL_WINDOWS)
POOL_GROUP = 128
POOL_WIDTH = N_POOL_GROUPS * POOL_GROUP
POOL_BUF = max(POOL_WINDOWS) - 1
IN_EVEN = ATTN_WIDTH + 2 * KV_WIDTH + POOL_WIDTH
CHUNK = 128
SGU_WIDTH = 1024
SGU_GROUPS = 4
SGU_GROUP_W = SGU_WIDTH // SGU_GROUPS
D_FF = 2816
EPS = 1e-6
NEG = -1e30

LANES = 128
SUBLANES = 8
VMEM_LIMIT_BYTES = 56 * 1024 * 1024

PROMPT_TOK = BATCH * SEQ
SAMPLE_TOK = DEC_BATCH * DEC_SEQ
N_TOK = PROMPT_TOK + SAMPLE_TOK

TM = SAMPLE_TOK
assert PROMPT_TOK % TM == 0
PROMPT_BLOCKS = PROMPT_TOK // TM
EVEN_TM = 1024
assert SEQ % EVEN_TM == 0 and EVEN_TM % WINDOW == 0
TILES_PER_SEQ = SEQ // EVEN_TM
QBLK_PER_TILE = EVEN_TM // WINDOW
FFN_TM = 2 * TM
assert PROMPT_TOK % FFN_TM == 0
FFN_PROMPT_STEPS = PROMPT_TOK // FFN_TM
ODD_TM = 1024
assert PROMPT_TOK % ODD_TM == 0 and ODD_TM % CHUNK == 0
FF_CHUNK = 256
N_FF_CHUNKS = D_FF // FF_CHUNK
FF_SLOTS = 3
POOL_HEAD = 2 * SUBLANES
assert all(w & (w - 1) == 0 and w <= POOL_HEAD for w in POOL_WINDOWS)
POOL_CTX = 32
SEQ_GROUP = 8
N_SEQ_GROUPS = DEC_BATCH // SEQ_GROUP
STATE_KEYS = SEQ_GROUP * WINDOW
NEW_SLOTS = SUBLANES
assert DEC_SEQ <= NEW_SLOTS and SEQ_GROUP * NEW_SLOTS <= WINDOW
SAMPLE_KEYS = STATE_KEYS + WINDOW
SAMPLE_ROWS = DEC_SEQ * SEQ_GROUP


def _t5_bucket_np(dist):
    n = np.maximum(dist, 0)
    max_exact = N_BUCKETS // 2
    nf = np.maximum(n, 1).astype(np.float32)
    large = max_exact + (np.log(nf / np.float32(max_exact)) / np.float32(math.log(MAX_DISTANCE / max_exact))
                         * np.float32(N_BUCKETS - max_exact)).astype(np.int32)
    large = np.minimum(large, N_BUCKETS - 1)
    return np.where(n < max_exact, n, large).astype(np.int32)


def _prompt_bucket_table():
    qi = np.arange(WINDOW)[:, None]
    c = np.arange(WINDOW)[None, :]
    dist = np.where(c > qi, qi + WINDOW - c, qi - c)
    return _t5_bucket_np(dist)


def _sample_bucket_table():
    t = (np.arange(SAMPLE_ROWS) // SEQ_GROUP)[:, None]
    b = (np.arange(SAMPLE_ROWS) % SEQ_GROUP)[:, None]
    col = np.arange(SAMPLE_KEYS)[None, :]
    is_state = col < STATE_KEYS
    slot = (col - STATE_KEYS) % NEW_SLOTS
    is_new = (col >= STATE_KEYS) & (col < STATE_KEYS + SEQ_GROUP * NEW_SLOTS) & (slot < DEC_SEQ)
    kb = np.where(is_state, col // WINDOW, (col - STATE_KEYS) // NEW_SLOTS)
    kpos = np.where(is_state, col % WINDOW, WINDOW + slot)
    dist = (WINDOW + t) - kpos
    valid = (is_state | is_new) & (kb == b) & (dist >= 0) & (dist < WINDOW)
    return np.where(valid, _t5_bucket_np(dist), -1).astype(np.int32)


def _rms(x, g):
    y = x * lax.rsqrt(jnp.mean(x * x, axis=-1, keepdims=True) + EPS)
    return y * g


LOG2E = math.log2(math.e)
Q_SCALE = HEAD_DIM ** -0.5 * LOG2E


def _bias_from_buckets(bucket, rb_ref, h):
    t = jnp.zeros(bucket.shape, F32)
    for bkt in range(N_BUCKETS):
        t = jnp.where(bucket == bkt, rb_ref[bkt, h] * LOG2E, t)
    return t


def _sink_softmax(s, sink):
    m = jnp.maximum(jnp.max(s, axis=-1, keepdims=True), sink)
    p = jnp.exp2(s - m)
    return p, jnp.sum(p, axis=-1, keepdims=True) + jnp.exp2(sink - m)


def _softmax_pv(s, valid, bias, sink, v):
    p, denom = _sink_softmax(jnp.where(valid, s + bias, NEG), sink)
    return jnp.dot(p.astype(BF16), v, preferred_element_type=F32) / denom


def _ffn_kernel(*refs, layer, half, tail_in, final_split):
    refs = list(refs)
    x_ref = refs.pop(0)
    xt_ref = refs.pop(0) if tail_in else None
    g_ref, wg_hbm, wu_hbm, wd_hbm = refs[:4]
    del refs[:4]
    fg_ref = refs.pop(0) if final_split else None
    o_ref = refs.pop(0)
    ot_ref = refs.pop(0) if final_split else None
    wg, wu, wd, stage_in, stage_dn, sem = refs
    i = pl.program_id(0)

    def normed(x):
        return _rms(x, g_ref[...]).astype(BF16)

    def result(x, y):
        out = x + 0.5 * y
        return _rms(out, fg_ref[...]) if final_split else out

    def chunk_copies(c, slot):
        cols = pl.ds(c * FF_CHUNK, FF_CHUNK)
        return (
            pltpu.make_async_copy(wg_hbm.at[layer, half, :, cols], stage_in.at[0, slot], sem.at[0, slot]),
            pltpu.make_async_copy(wu_hbm.at[layer, half, :, cols], stage_in.at[1, slot], sem.at[1, slot]),
            pltpu.make_async_copy(wd_hbm.at[layer, half, cols, :], stage_dn.at[slot], sem.at[2, slot]),
        )

    @pl.when(i == 0)
    def _():
        for c in range(min(FF_SLOTS, N_FF_CHUNKS)):
            for cp in chunk_copies(c, c % FF_SLOTS):
                cp.start()
        if tail_in:
            x = jnp.concatenate([xt_ref[:, t, :] for t in range(DEC_SEQ)], axis=0)
        else:
            x = x_ref[0:TM, :]
        h = normed(x)
        acc = jnp.zeros((TM, D_MODEL), F32)
        for c in range(N_FF_CHUNKS):
            slot = c % FF_SLOTS
            cols = slice(c * FF_CHUNK, (c + 1) * FF_CHUNK)
            for cp in chunk_copies(c, slot):
                cp.wait()
            wg[:, cols] = stage_in[0, slot].astype(BF16)
            wu[:, cols] = stage_in[1, slot].astype(BF16)
            wd[cols, :] = stage_dn[slot].astype(BF16)
            if c + FF_SLOTS < N_FF_CHUNKS:
                for cp in chunk_copies(c + FF_SLOTS, slot):
                    cp.start()
            gate = jnp.dot(h, wg[:, cols], preferred_element_type=F32)
            up = jnp.dot(h, wu[:, cols], preferred_element_type=F32)
            act = (jax.nn.silu(gate) * up).astype(BF16)
            acc = acc + jnp.dot(act, wd[cols, :], preferred_element_type=F32)
        out = result(x, acc)
        if final_split:
            for t in range(DEC_SEQ):
                ot_ref[:, t, :] = out[t * DEC_BATCH:(t + 1) * DEC_BATCH, :]
        else:
            o_ref[0:TM, :] = out

    def half_block(hf, carry):
        rows = pl.ds(pl.multiple_of(hf * TM, TM), TM)
        x = x_ref[rows, :]
        h = normed(x)
        gate = jnp.dot(h, wg[...], preferred_element_type=F32)
        up = jnp.dot(h, wu[...], preferred_element_type=F32)
        act = (jax.nn.silu(gate) * up).astype(BF16)
        o_ref[rows, :] = result(x, jnp.dot(act, wd[...], preferred_element_type=F32))
        return carry

    @pl.when(i > 0)
    def _():
        lax.fori_loop(0, FFN_TM // TM, half_block, 0)


def _resident(shape):
    return pl.BlockSpec(shape, lambda *_: (0,) * len(shape), pipeline_mode=pl.Buffered(1))


def _ffn(x_in, gain, wg, wu, wd, layer, half, *, tail_in=None, final_gain=None):
    hbm = pl.BlockSpec(memory_space=pl.ANY)
    row_block = pl.BlockSpec((FFN_TM, D_MODEL), lambda i: (jnp.where(i == 0, FFN_PROMPT_STEPS, i - 1), 0))
    prompt_block = pl.BlockSpec((FFN_TM, D_MODEL), lambda i: (jnp.maximum(i - 1, 0), 0))
    sample_block = pl.BlockSpec((DEC_BATCH, DEC_SEQ, D_MODEL), lambda i: (0, 0, 0))
    if tail_in is None:
        args, in_specs = [x_in], [row_block]
    else:
        args, in_specs = [x_in, tail_in], [prompt_block, sample_block]
    args += [gain.reshape(1, D_MODEL), wg, wu, wd]
    in_specs += [_resident((1, D_MODEL)), hbm, hbm, hbm]
    if final_gain is None:
        out_shape = jax.ShapeDtypeStruct((N_TOK, D_MODEL), F32)
        out_specs = row_block
    else:
        args.append(final_gain.reshape(1, D_MODEL))
        in_specs.append(_resident((1, D_MODEL)))
        out_shape = (jax.ShapeDtypeStruct((PROMPT_TOK, D_MODEL), F32),
                     jax.ShapeDtypeStruct((DEC_BATCH, DEC_SEQ, D_MODEL), F32))
        out_specs = (prompt_block, sample_block)
    aliases = {0: 0} if (tail_in is None and final_gain is None) else {}
    return pl.pallas_call(
        functools.partial(_ffn_kernel, layer=layer, half=half, tail_in=tail_in is not None,
                          final_split=final_gain is not None),
        out_shape=out_shape,
        grid=(FFN_PROMPT_STEPS + 1,),
        in_specs=in_specs,
        out_specs=out_specs,
        scratch_shapes=[
            pltpu.VMEM((D_MODEL, D_FF), BF16),
            pltpu.VMEM((D_MODEL, D_FF), BF16),
            pltpu.VMEM((D_FF, D_MODEL), BF16),
            pltpu.VMEM((2, FF_SLOTS, D_MODEL, FF_CHUNK), F32),
            pltpu.VMEM((FF_SLOTS, FF_CHUNK, D_MODEL), F32),
            pltpu.SemaphoreType.DMA((3, FF_SLOTS)),
        ],
        input_output_aliases=aliases,
        compiler_params=pltpu.CompilerParams(
            dimension_semantics=("arbitrary",), vmem_limit_bytes=VMEM_LIMIT_BYTES),
        name="macaron_half",
    )(*args)


def _even_prompt_kernel(x_ref, g_ref, win_ref, wout_ref, wpool_ref, pscale_ref, bucket_ref, rb_ref, sink_ref,
                        o_ref, klast_ref, vlast_ref, plast_ref,
                        kbuf, kbuf_r, vbuf, vbuf_r, zb, s2b, s4b, s8b, bias_ref, cat_ref):
    b = pl.program_id(0)
    s = pl.program_id(1)
    last = pl.num_programs(1) - 1

    @pl.when((b == 0) & (s == 0))
    def _():
        bucket = bucket_ref[...]
        for h in range(N_HEADS_A):
            bias_ref[h] = _bias_from_buckets(bucket, rb_ref, h)

    @pl.when(s == 0)
    def _():
        for buf in (kbuf, kbuf_r, vbuf, vbuf_r):
            buf[0:WINDOW, :] = jnp.zeros((WINDOW, KV_WIDTH), BF16)
        zb[0:POOL_CTX, :] = jnp.zeros((POOL_CTX, POOL_WIDTH), F32)

    @pl.when(s > 0)
    def _():
        for buf in (kbuf, kbuf_r, vbuf, vbuf_r):
            buf[0:WINDOW, :] = buf[EVEN_TM:EVEN_TM + WINDOW, :]
        zb[0:POOL_CTX, :] = zb[EVEN_TM:EVEN_TM + POOL_CTX, :]

    x = x_ref[...]
    h = _rms(x, g_ref[...]).astype(BF16)
    proj = jnp.dot(h, win_ref[...], preferred_element_type=F32)
    q = proj[:, 0:ATTN_WIDTH] * Q_SCALE
    k = proj[:, ATTN_WIDTH:ATTN_WIDTH + KV_WIDTH]
    v = proj[:, ATTN_WIDTH + KV_WIDTH:ATTN_WIDTH + 2 * KV_WIDTH]
    up = proj[:, ATTN_WIDTH + 2 * KV_WIDTH:IN_EVEN]

    qlane = lax.broadcasted_iota(jnp.int32, (EVEN_TM, ATTN_WIDTH), 1) % LANES
    q_half = (jnp.where(qlane < HEAD_DIM, q, 0.0).astype(BF16),
              jnp.where(qlane >= HEAD_DIM, q, 0.0).astype(BF16))
    kbuf[WINDOW:, :] = k.astype(BF16)
    vbuf[WINDOW:, :] = v.astype(BF16)
    kbuf_r[WINDOW:, :] = pltpu.roll(k, HEAD_DIM, 1).astype(BF16)
    vbuf_r[WINDOW:, :] = pltpu.roll(v, HEAD_DIM, 1).astype(BF16)

    qrow = lax.broadcasted_iota(jnp.int32, (WINDOW, WINDOW), 0)
    kcol = lax.broadcasted_iota(jnp.int32, (WINDOW, WINDOW), 1)
    prev = kcol > qrow
    olane = lax.broadcasted_iota(jnp.int32, (WINDOW, LANES), 1)
    head_sets = [[hd for hd in range(N_HEADS_A) if (hd // GQA_GROUP == hd % 2) == plain] for plain in (True, False)]
    for i in range(QBLK_PER_TILE):
        rows = slice(i * WINDOW, (i + 1) * WINDOW)
        keys = slice(i * WINDOW, (i + 2) * WINDOW)
        out = {}
        for heads, kb, vb in zip(head_sets, (kbuf, kbuf_r), (vbuf, vbuf_r)):
            qs = jnp.concatenate(
                [q_half[hd % 2][rows, (hd // 2) * LANES:(hd // 2 + 1) * LANES] for hd in heads], axis=0)
            sc_all = lax.dot_general(qs, kb[keys, :], (((1,), (1,)), ((), ())), preferred_element_type=F32)
            ps, denoms = [], []
            for r, hd in enumerate(heads):
                sc = sc_all[r * WINDOW:(r + 1) * WINDOW, :]
                sc = jnp.where(prev, sc[:, 0:WINDOW], sc[:, WINDOW:]) + bias_ref[hd]
                if i == 0:
                    sc = jnp.where(prev & (qrow >= jnp.where(s > 0, WINDOW, 0)), NEG, sc)
                p, denom = _sink_softmax(sc, sink_ref[hd] * LOG2E)
                ps.append(jnp.concatenate([jnp.where(prev, p, 0.0), jnp.where(prev, 0.0, p)], axis=1).astype(BF16))
                denoms.append(denom)
            o_all = jnp.dot(jnp.concatenate(ps, axis=0), vb[keys, :], preferred_element_type=F32)
            for r, hd in enumerate(heads):
                out[hd] = o_all[r * WINDOW:(r + 1) * WINDOW, :] / denoms[r]
        for j in range(N_HEADS_A // 2):
            cat_ref[rows, j * LANES:(j + 1) * LANES] = jnp.where(
                olane < HEAD_DIM, out[2 * j], out[2 * j + 1]).astype(BF16)

    zb[POOL_CTX:, :] = up
    n = EVEN_TM + POOL_CTX
    s2b[8:, :] = zb[8:n, :] + zb[7:n - 1, :]
    s4b[16:, :] = s2b[16:n, POOL_GROUP:] + s2b[14:n - 2, POOL_GROUP:]
    s8b[24:, :] = s4b[24:n, POOL_GROUP:] + s4b[20:n - 4, POOL_GROUP:]
    s16 = s8b[32:n, POOL_GROUP:] + s8b[24:n - 8, POOL_GROUP:]
    sums = (s2b[POOL_CTX:, 0:POOL_GROUP], s4b[POOL_CTX:, 0:POOL_GROUP], s8b[POOL_CTX:, 0:POOL_GROUP], s16)
    pos1 = lax.broadcasted_iota(jnp.int32, (POOL_HEAD, POOL_GROUP), 0) + s * EVEN_TM + 1
    for g, w in enumerate(POOL_WINDOWS):
        lanes = slice(g * POOL_GROUP, (g + 1) * POOL_GROUP)
        cnt = jnp.minimum(pos1, w).astype(F32)
        mean = jnp.concatenate([sums[g][0:POOL_HEAD] / cnt, sums[g][POOL_HEAD:] * (1.0 / w)], axis=0)
        d = (mean - up[:, lanes]).astype(BF16)
        y = jnp.dot(d, wpool_ref[g], preferred_element_type=F32) * pscale_ref[:, lanes]
        cat_ref[:, ATTN_WIDTH + g * POOL_GROUP:ATTN_WIDTH + (g + 1) * POOL_GROUP] = y.astype(BF16)

    o_ref[...] = x + jnp.dot(cat_ref[...], wout_ref[...], preferred_element_type=F32)

    @pl.when(s == last)
    def _():
        klast_ref[0] = k[EVEN_TM - WINDOW:, :]
        vlast_ref[0] = v[EVEN_TM - WINDOW:, :]
        plast_ref[0] = up[EVEN_TM - 2 * SUBLANES:, :]


def _even_prompt(x, gain, w_in, w_out, w_pool, pool_scale, rel_bias, sinks):
    bucket = jnp.asarray(_prompt_bucket_table())
    smem = pl.BlockSpec(memory_space=pltpu.SMEM)
    return pl.pallas_call(
        _even_prompt_kernel,
        out_shape=(
            jax.ShapeDtypeStruct((N_TOK, D_MODEL), F32),
            jax.ShapeDtypeStruct((BATCH, WINDOW, KV_WIDTH), F32),
            jax.ShapeDtypeStruct((BATCH, WINDOW, KV_WIDTH), F32),
            jax.ShapeDtypeStruct((BATCH, 2 * SUBLANES, POOL_WIDTH), F32),
        ),
        grid=(BATCH, TILES_PER_SEQ),
        in_specs=[
            pl.BlockSpec((EVEN_TM, D_MODEL), lambda b, s: (b * TILES_PER_SEQ + s, 0)),
            _resident((1, D_MODEL)),
            _resident((D_MODEL, IN_EVEN)),
            _resident((ATTN_WIDTH + POOL_WIDTH, D_MODEL)),
            _resident((N_POOL_GROUPS, POOL_GROUP, POOL_GROUP)),
            _resident((1, POOL_WIDTH)),
            _resident((WINDOW, WINDOW)),
            smem,
            smem,
        ],
        out_specs=(
            pl.BlockSpec((EVEN_TM, D_MODEL), lambda b, s: (b * TILES_PER_SEQ + s, 0)),
            pl.BlockSpec((1, WINDOW, KV_WIDTH), lambda b, s: (b, 0, 0)),
            pl.BlockSpec((1, WINDOW, KV_WIDTH), lambda b, s: (b, 0, 0)),
            pl.BlockSpec((1, 2 * SUBLANES, POOL_WIDTH), lambda b, s: (b, 0, 0)),
        ),
        scratch_shapes=[
            pltpu.VMEM((WINDOW + EVEN_TM, KV_WIDTH), BF16),
            pltpu.VMEM((WINDOW + EVEN_TM, KV_WIDTH), BF16),
            pltpu.VMEM((WINDOW + EVEN_TM, KV_WIDTH), BF16),
            pltpu.VMEM((WINDOW + EVEN_TM, KV_WIDTH), BF16),
            pltpu.VMEM((POOL_CTX + EVEN_TM, POOL_WIDTH), F32),
            pltpu.VMEM((POOL_CTX + EVEN_TM, POOL_WIDTH), F32),
            pltpu.VMEM((POOL_CTX + EVEN_TM, POOL_WIDTH - POOL_GROUP), F32),
            pltpu.VMEM((POOL_CTX + EVEN_TM, POOL_WIDTH - 2 * POOL_GROUP), F32),
            pltpu.VMEM((N_HEADS_A, WINDOW, WINDOW), F32),
            pltpu.VMEM((EVEN_TM, ATTN_WIDTH + POOL_WIDTH), BF16),
        ],
        input_output_aliases={0: 0},
        compiler_params=pltpu.CompilerParams(
            dimension_semantics=("arbitrary", "arbitrary"), vmem_limit_bytes=VMEM_LIMIT_BYTES),
        name="even_mixer_prompt",
    )(x, gain.reshape(1, D_MODEL), w_in, w_out, w_pool, pool_scale.reshape(1, POOL_WIDTH), bucket,
      rel_bias, sinks)


def _even_sample_kernel(x_ref, g_ref, win_ref, wout_ref, wpool_ref, pscale_ref, bucket_ref, rb_ref, sink_ref,
                        sk_ref, sv_ref, spool_ref,
                        o_ref, nk_ref, nv_ref, npool_ref,
                        knew_ref, vnew_ref, upnew_ref, qsel, bias_ref, attn_f32, cat_ref):
    g = pl.program_id(0)
    last = pl.num_programs(0) - 1
    olane = lax.broadcasted_iota(jnp.int32, (SAMPLE_ROWS, LANES), 1)

    @pl.when(g == 0)
    def _():
        bucket = bucket_ref[...]
        for h in range(N_HEADS_A):
            bias_ref[h] = _bias_from_buckets(bucket, rb_ref, h)
        x = x_ref[...]
        h = _rms(x, g_ref[...]).astype(BF16)
        proj = jnp.dot(h, win_ref[...], preferred_element_type=F32)
        q = proj[:, 0:ATTN_WIDTH] * Q_SCALE
        zeros = jnp.zeros(((NEW_SLOTS - DEC_SEQ) * DEC_BATCH, KV_WIDTH), F32)
        knew_ref[0:SAMPLE_TOK, :] = proj[:, ATTN_WIDTH:ATTN_WIDTH + KV_WIDTH]
        vnew_ref[0:SAMPLE_TOK, :] = proj[:, ATTN_WIDTH + KV_WIDTH:ATTN_WIDTH + 2 * KV_WIDTH]
        knew_ref[SAMPLE_TOK:, :] = zeros
        vnew_ref[SAMPLE_TOK:, :] = zeros
        upnew_ref[...] = proj[:, ATTN_WIDTH + 2 * KV_WIDTH:IN_EVEN]
        qlane = lax.broadcasted_iota(jnp.int32, (SAMPLE_TOK, LANES), 1)
        for hd in range(N_HEADS_A):
            j, half, kvh = hd // 2, hd % 2, hd // GQA_GROUP
            blk = q[:, j * LANES:(j + 1) * LANES]
            if half != kvh:
                blk = pltpu.roll(blk, HEAD_DIM, 1)
            keep = (qlane < HEAD_DIM) if kvh == 0 else (qlane >= HEAD_DIM)
            qsel[hd] = jnp.where(keep, blk, 0.0)

    row0 = pl.multiple_of(g * SEQ_GROUP, SEQ_GROUP)
    pad = [jnp.zeros((WINDOW - SEQ_GROUP * NEW_SLOTS, KV_WIDTH), F32)]
    knew = jnp.concatenate(
        [knew_ref[pl.ds(row0 + b, NEW_SLOTS, stride=DEC_BATCH), :] for b in range(SEQ_GROUP)] + pad, axis=0)
    vnew = jnp.concatenate(
        [vnew_ref[pl.ds(row0 + b, NEW_SLOTS, stride=DEC_BATCH), :] for b in range(SEQ_GROUP)] + pad, axis=0)
    kt_all = jnp.concatenate([sk_ref[b] for b in range(SEQ_GROUP)], axis=1).astype(BF16)
    vt_all = jnp.concatenate([sv_ref[b] for b in range(SEQ_GROUP)], axis=1).astype(BF16)

    qg = jnp.concatenate(
        [qsel[hd, pl.ds(t * DEC_BATCH + row0, SEQ_GROUP), :] for hd in range(N_HEADS_A) for t in range(DEC_SEQ)],
        axis=0).astype(BF16)
    nt = (((1,), (1,)), ((), ()))
    sc = jnp.concatenate(
        [jnp.dot(qg, kt_all, preferred_element_type=F32),
         lax.dot_general(qg, knew.astype(BF16), nt, preferred_element_type=F32)], axis=1)
    valid = jnp.concatenate([bucket_ref[...]] * N_HEADS_A, axis=0) >= 0
    bias = bias_ref[...].reshape(N_HEADS_A * SAMPLE_ROWS, SAMPLE_KEYS)
    sink = jnp.concatenate(
        [jnp.full((SAMPLE_ROWS, 1), sink_ref[hd] * LOG2E, F32) for hd in range(N_HEADS_A)], axis=0)
    p, denom = _sink_softmax(jnp.where(valid, sc + bias, NEG), sink)
    p = p.astype(BF16)
    o_all = (lax.dot_general(p[:, 0:STATE_KEYS], vt_all, nt, preferred_element_type=F32)
             + jnp.dot(p[:, STATE_KEYS:], vnew.astype(BF16), preferred_element_type=F32)) / denom
    for j in range(N_HEADS_A // 2):
        halves = []
        for half in range(2):
            hd = 2 * j + half
            o = o_all[hd * SAMPLE_ROWS:(hd + 1) * SAMPLE_ROWS, :]
            if hd // GQA_GROUP != half:
                o = pltpu.roll(o, HEAD_DIM, 1)
            halves.append(o)
        blk = jnp.where(olane < HEAD_DIM, halves[0], halves[1])
        for t in range(DEC_SEQ):
            attn_f32[pl.ds(t * DEC_BATCH + row0, SEQ_GROUP), j * LANES:(j + 1) * LANES] = (
                blk[t * SEQ_GROUP:(t + 1) * SEQ_GROUP, :])

    lane = lax.broadcasted_iota(jnp.int32, (KV_WIDTH, WINDOW), 1)
    for new, old_ref, out_ref in ((knew, sk_ref, nk_ref), (vnew, sv_ref, nv_ref)):
        new_t = new.T
        for b in range(SEQ_GROUP):
            tail = pltpu.roll(new_t, (WINDOW - DEC_SEQ - b * NEW_SLOTS) % WINDOW, 1)
            out_ref[b] = jnp.where(lane >= WINDOW - DEC_SEQ, tail, pltpu.roll(old_ref[b], WINDOW - DEC_SEQ, 1))

    @pl.when(g == last)
    def _():
        cat_ref[:, 0:ATTN_WIDTH] = attn_f32[...].astype(BF16)
        for r in range(POOL_BUF):
            if r < POOL_BUF - DEC_SEQ:
                npool_ref[r] = spool_ref[r + DEC_SEQ]
            else:
                t = r - (POOL_BUF - DEC_SEQ)
                npool_ref[r] = upnew_ref[t * DEC_BATCH:(t + 1) * DEC_BATCH, :]
        for t in range(DEC_SEQ):
            rows = slice(t * DEC_BATCH, (t + 1) * DEC_BATCH)
            for gi, w in enumerate(POOL_WINDOWS):
                lanes = slice(gi * POOL_GROUP, (gi + 1) * POOL_GROUP)
                acc = upnew_ref[rows, lanes]
                for back in range(1, w):
                    tt = t - back
                    if tt >= 0:
                        acc = acc + upnew_ref[tt * DEC_BATCH:(tt + 1) * DEC_BATCH, lanes]
                    else:
                        acc = acc + spool_ref[POOL_BUF + tt, :, lanes]
                cnt = float(min(PAST_LEN + t + 1, w))
                d = (acc / cnt - upnew_ref[rows, lanes]).astype(BF16)
                y = jnp.dot(d, wpool_ref[gi], preferred_element_type=F32) * pscale_ref[:, lanes]
                cat_ref[rows, ATTN_WIDTH + gi * POOL_GROUP:ATTN_WIDTH + (gi + 1) * POOL_GROUP] = y.astype(BF16)
        o_ref[...] = x_ref[...] + jnp.dot(cat_ref[...], wout_ref[...], preferred_element_type=F32)


def _even_sample(x, gain, w_in, w_out, w_pool, pool_scale, rel_bias, sinks, state_k, state_v, state_pool_t):
    bucket = jnp.asarray(_sample_bucket_table())
    smem = pl.BlockSpec(memory_space=pltpu.SMEM)
    sample_block = PROMPT_TOK // SAMPLE_TOK
    return pl.pallas_call(
        _even_sample_kernel,
        out_shape=(
            jax.ShapeDtypeStruct((N_TOK, D_MODEL), F32),
            jax.ShapeDtypeStruct((DEC_BATCH, KV_WIDTH, WINDOW), F32),
            jax.ShapeDtypeStruct((DEC_BATCH, KV_WIDTH, WINDOW), F32),
            jax.ShapeDtypeStruct((POOL_BUF, DEC_BATCH, POOL_WIDTH), F32),
        ),
        grid=(N_SEQ_GROUPS,),
        in_specs=[
            pl.BlockSpec((SAMPLE_TOK, D_MODEL), lambda g: (sample_block, 0)),
            _resident((1, D_MODEL)),
            _resident((D_MODEL, IN_EVEN)),
            _resident((ATTN_WIDTH + POOL_WIDTH, D_MODEL)),
            _resident((N_POOL_GROUPS, POOL_GROUP, POOL_GROUP)),
            _resident((1, POOL_WIDTH)),
            _resident((SAMPLE_ROWS, SAMPLE_KEYS)),
            smem,
            smem,
            pl.BlockSpec((SEQ_GROUP, KV_WIDTH, WINDOW), lambda g: (g, 0, 0)),
            pl.BlockSpec((SEQ_GROUP, KV_WIDTH, WINDOW), lambda g: (g, 0, 0)),
            _resident((POOL_BUF, DEC_BATCH, POOL_WIDTH)),
        ],
        out_specs=(
            pl.BlockSpec((SAMPLE_TOK, D_MODEL), lambda g: (sample_block, 0)),
            pl.BlockSpec((SEQ_GROUP, KV_WIDTH, WINDOW), lambda g: (g, 0, 0)),
            pl.BlockSpec((SEQ_GROUP, KV_WIDTH, WINDOW), lambda g: (g, 0, 0)),
            pl.BlockSpec((POOL_BUF, DEC_BATCH, POOL_WIDTH), lambda g: (0, 0, 0)),
        ),
        scratch_shapes=[
            pltpu.VMEM((NEW_SLOTS * DEC_BATCH, KV_WIDTH), F32),
            pltpu.VMEM((NEW_SLOTS * DEC_BATCH, KV_WIDTH), F32),
            pltpu.VMEM((SAMPLE_TOK, POOL_WIDTH), F32),
            pltpu.VMEM((N_HEADS_A, SAMPLE_TOK, LANES), F32),
            pltpu.VMEM((N_HEADS_A, SAMPLE_ROWS, SAMPLE_KEYS), F32),
            pltpu.VMEM((SAMPLE_TOK, ATTN_WIDTH), F32),
            pltpu.VMEM((SAMPLE_TOK, ATTN_WIDTH + POOL_WIDTH), BF16),
        ],
        input_output_aliases={0: 0},
        compiler_params=pltpu.CompilerParams(
            dimension_semantics=("arbitrary",), vmem_limit_bytes=VMEM_LIMIT_BYTES),
        name="even_mixer_sample",
    )(x, gain.reshape(1, D_MODEL), w_in, w_out, w_pool, pool_scale.reshape(1, POOL_WIDTH), bucket,
      rel_bias, sinks, state_k, state_v, state_pool_t)


def _sgu_front(x, g_ref, win_ref, gv_ref):
    h = _rms(x, g_ref[...]).astype(BF16)
    uv = jax.nn.gelu(jnp.dot(h, win_ref[...], preferred_element_type=F32))
    return uv[:, 0:SGU_WIDTH], _rms(uv[:, SGU_WIDTH:], gv_ref[...])


def _odd_prompt_kernel(x_ref, g_ref, win_ref, gv_ref, ws_ref, bs_ref, wout_ref, o_ref, gated_ref):
    x = x_ref[...]
    u, v = _sgu_front(x, g_ref, win_ref, gv_ref)
    vb = v.astype(BF16)
    r = lax.broadcasted_iota(jnp.int32, (CHUNK, CHUNK), 0)
    c = lax.broadcasted_iota(jnp.int32, (CHUNK, CHUNK), 1)
    for gi in range(SGU_GROUPS):
        w = jnp.where(r >= c, ws_ref[gi], 0.0).astype(BF16)
        bias = jnp.concatenate([bs_ref[gi]] * (SGU_GROUP_W // LANES), axis=1)
        lanes = slice(gi * SGU_GROUP_W, (gi + 1) * SGU_GROUP_W)
        for ci in range(ODD_TM // CHUNK):
            rows = slice(ci * CHUNK, (ci + 1) * CHUNK)
            mixed = jnp.dot(w, vb[rows, lanes], preferred_element_type=F32) + bias
            gated_ref[rows, lanes] = (u[rows, lanes] * mixed).astype(BF16)
    o_ref[...] = x + jnp.dot(gated_ref[...], wout_ref[...], preferred_element_type=F32)


def _odd_prompt(x, gain, w_in, g_v, w_s, b_s, w_out):
    bias = jnp.broadcast_to(b_s[:, :, None], (SGU_GROUPS, CHUNK, LANES))
    return pl.pallas_call(
        _odd_prompt_kernel,
        out_shape=jax.ShapeDtypeStruct((N_TOK, D_MODEL), F32),
        grid=(PROMPT_TOK // ODD_TM,),
        in_specs=[
            pl.BlockSpec((ODD_TM, D_MODEL), lambda i: (i, 0)),
            _resident((1, D_MODEL)),
            _resident((D_MODEL, 2 * SGU_WIDTH)),
            _resident((1, SGU_WIDTH)),
            _resident((SGU_GROUPS, CHUNK, CHUNK)),
            _resident((SGU_GROUPS, CHUNK, LANES)),
            _resident((SGU_WIDTH, D_MODEL)),
        ],
        out_specs=pl.BlockSpec((ODD_TM, D_MODEL), lambda i: (i, 0)),
        scratch_shapes=[pltpu.VMEM((ODD_TM, SGU_WIDTH), BF16)],
        input_output_aliases={0: 0},
        compiler_params=pltpu.CompilerParams(
            dimension_semantics=("arbitrary",), vmem_limit_bytes=VMEM_LIMIT_BYTES),
        name="odd_mixer_prompt",
    )(x, gain.reshape(1, D_MODEL), w_in, g_v.reshape(1, SGU_WIDTH), w_s, bias, w_out)


def _odd_sample_kernel(x_ref, g_ref, win_ref, gv_ref, coef_ref, bias_ref, wout_ref, o_ref, v_ref):
    x = x_ref[...]
    u, v = _sgu_front(x, g_ref, win_ref, gv_ref)
    for t in range(DEC_SEQ):
        v_ref[:, t, :] = v[t * DEC_BATCH:(t + 1) * DEC_BATCH, :]
    gated = []
    for t in range(DEC_SEQ):
        mixed = bias_ref[t:t + 1, :]
        for s in range(t + 1):
            mixed = mixed + coef_ref[t, s:s + 1, :] * v[s * DEC_BATCH:(s + 1) * DEC_BATCH, :]
        gated.append((u[t * DEC_BATCH:(t + 1) * DEC_BATCH, :] * mixed).astype(BF16))
    o_ref[...] = x + jnp.dot(jnp.concatenate(gated, axis=0), wout_ref[...], preferred_element_type=F32)


def _odd_sample(x, gain, w_in, g_v, w_s, b_s, w_out):
    coef = jnp.repeat(jnp.transpose(w_s[:, :DEC_SEQ, :DEC_SEQ], (1, 2, 0)), SGU_GROUP_W, axis=-1)
    bias = jnp.repeat(b_s[:, :DEC_SEQ].T, SGU_GROUP_W, axis=-1)
    sample_block = PROMPT_TOK // SAMPLE_TOK
    return pl.pallas_call(
        _odd_sample_kernel,
        out_shape=(
            jax.ShapeDtypeStruct((N_TOK, D_MODEL), F32),
            jax.ShapeDtypeStruct((DEC_BATCH, DEC_SEQ, SGU_WIDTH), F32),
        ),
        grid=(1,),
        in_specs=[
            pl.BlockSpec((SAMPLE_TOK, D_MODEL), lambda i: (sample_block, 0)),
            _resident((1, D_MODEL)),
            _resident((D_MODEL, 2 * SGU_WIDTH)),
            _resident((1, SGU_WIDTH)),
            _resident((DEC_SEQ, DEC_SEQ, SGU_WIDTH)),
            _resident((DEC_SEQ, SGU_WIDTH)),
            _resident((SGU_WIDTH, D_MODEL)),
        ],
        out_specs=(
            pl.BlockSpec((SAMPLE_TOK, D_MODEL), lambda i: (sample_block, 0)),
            pl.BlockSpec((DEC_BATCH, DEC_SEQ, SGU_WIDTH), lambda i: (0, 0, 0)),
        ),
        input_output_aliases={0: 0},
        compiler_params=pltpu.CompilerParams(
            dimension_semantics=("arbitrary",), vmem_limit_bytes=VMEM_LIMIT_BYTES),
        name="odd_mixer_sample",
    )(x, gain.reshape(1, D_MODEL), w_in, g_v.reshape(1, SGU_WIDTH), coef, bias, w_out)


def _window_to_lanes(w):
    return jnp.transpose(w, (0, 2, 3, 1)).reshape(DEC_BATCH, KV_WIDTH, WINDOW)


def _window_from_lanes(w):
    return jnp.transpose(w.reshape(DEC_BATCH, N_KV_A, HEAD_DIM, WINDOW), (0, 3, 1, 2))


def kernel(x_prompt, x_sample, state_win_k, state_win_v, state_pool, rel_bias, norm_gains, final_gain,
           ffn_gate, ffn_up, ffn_down, w_in_even, w_out_even, attn_sinks, w_pool, pool_scale,
           w_in_odd, sgu_norm, w_spatial, b_spatial, w_out_odd):
    xp = x_prompt.reshape(PROMPT_TOK, D_MODEL)

    kp_l, vp_l, pp_l, ks_l, vs_l, ps_l, sv_l = [], [], [], [], [], [], []
    x = None
    y_prompt = y_sample = None
    for l in range(DEPTH):
        fa = (norm_gains[l, 0], ffn_gate, ffn_up, ffn_down, l, 0)
        fb = (norm_gains[l, 2], ffn_gate, ffn_up, ffn_down, l, 1)
        if l == 0:
            x = _ffn(xp, *fa, tail_in=x_sample)
        else:
            x = _ffn(x, *fa)
        g1 = norm_gains[l, 1]
        if l % 2 == 0:
            e = l // 2
            w_in, w_out = w_in_even[e].astype(BF16), w_out_even[e].astype(BF16)
            wp = w_pool[e].astype(BF16)
            x, kp, vp, pp = _even_prompt(x, g1, w_in, w_out, wp, pool_scale[e], rel_bias, attn_sinks[e])
            x, nk, nv, npool = _even_sample(
                x, g1, w_in, w_out, wp, pool_scale[e], rel_bias, attn_sinks[e],
                _window_to_lanes(state_win_k[e]), _window_to_lanes(state_win_v[e]),
                jnp.swapaxes(state_pool[e], 0, 1))
            kp_l.append(kp.reshape(BATCH, WINDOW, N_KV_A, HEAD_DIM))
            vp_l.append(vp.reshape(BATCH, WINDOW, N_KV_A, HEAD_DIM))
            pp_l.append(pp[:, 2 * SUBLANES - POOL_BUF:, :])
            ks_l.append(_window_from_lanes(nk))
            vs_l.append(_window_from_lanes(nv))
            ps_l.append(jnp.swapaxes(npool, 0, 1))
        else:
            o = l // 2
            w_in, w_out = w_in_odd[o].astype(BF16), w_out_odd[o].astype(BF16)
            x = _odd_prompt(x, g1, w_in, sgu_norm[o], w_spatial[o], b_spatial[o], w_out)
            x, sv_new = _odd_sample(x, g1, w_in, sgu_norm[o], w_spatial[o], b_spatial[o], w_out)
            sv_l.append(sv_new)
        if l < DEPTH - 1:
            x = _ffn(x, *fb)
        else:
            y_prompt, y_sample = _ffn(x, *fb, final_gain=final_gain)

    return (y_prompt.reshape(BATCH, SEQ, D_MODEL), y_sample,
            jnp.stack(kp_l), jnp.stack(vp_l), jnp.stack(pp_l),
            jnp.stack(ks_l), jnp.stack(vs_l), jnp.stack(ps_l), jnp.stack(sv_l))
```

```python
import functools
import math

import numpy as np
import jax
import jax.numpy as jnp
from jax import lax
from jax.experimental import pallas as pl
from jax.experimental.pallas import tpu as pltpu

F32 = jnp.float32
BF16 = jnp.bfloat16

D_MODEL = 1024
BATCH = 8
SEQ = 2048
DEPTH = 2
DEC_BATCH = 128
DEC_SEQ = 4
PAST_LEN = 16384
N_HEADS_A = 8
N_KV_A = 2
HEAD_DIM = 64
GQA_GROUP = N_HEADS_A // N_KV_A
WINDOW = 128
ATTN_WIDTH = N_HEADS_A * HEAD_DIM
KV_WIDTH = N_KV_A * HEAD_DIM
N_BUCKETS = 32
MAX_DISTANCE = 128
POOL_WINDOWS = (2, 4, 8, 16)
N_POOL_GROUPS = len(POOL_WINDOWS)
POOL_GROUP = 128
POOL_WIDTH = N_POOL_GROUPS * POOL_GROUP
POOL_BUF = max(POOL_WINDOWS) - 1
IN_EVEN = ATTN_WIDTH + 2 * KV_WIDTH + POOL_WIDTH
CHUNK = 128
SGU_WIDTH = 1024
SGU_GROUPS = 4
SGU_GROUP_W = SGU_WIDTH // SGU_GROUPS
D_FF = 2816
EPS = 1e-6
NEG = -1e30

LANES = 128
SUBLANES = 8
VMEM_LIMIT_BYTES = 56 * 1024 * 1024

PROMPT_TOK = BATCH * SEQ
SAMPLE_TOK = DEC_BATCH * DEC_SEQ
N_TOK = PROMPT_TOK + SAMPLE_TOK

TM = SAMPLE_TOK
assert PROMPT_TOK % TM == 0
PROMPT_BLOCKS = PROMPT_TOK // TM
EVEN_TM = 1024
assert SEQ % EVEN_TM == 0 and EVEN_TM % WINDOW == 0
TILES_PER_SEQ = SEQ // EVEN_TM
QBLK_PER_TILE = EVEN_TM // WINDOW
FFN_TM = 2 * TM
assert PROMPT_TOK % FFN_TM == 0
FFN_PROMPT_STEPS = PROMPT_TOK // FFN_TM
ODD_TM = 1024
assert PROMPT_TOK % ODD_TM == 0 and ODD_TM % CHUNK == 0
FF_CHUNK = 256
N_FF_CHUNKS = D_FF // FF_CHUNK
FF_SLOTS = 3
POOL_HEAD = 2 * SUBLANES
assert all(w & (w - 1) == 0 and w <= POOL_HEAD for w in POOL_WINDOWS)
POOL_CTX = 32
SEQ_GROUP = 8
N_SEQ_GROUPS = DEC_BATCH // SEQ_GROUP
GROUPS_PER_STEP = 4
assert N_SEQ_GROUPS % GROUPS_PER_STEP == 0
STATE_KEYS = SEQ_GROUP * WINDOW
NEW_SLOTS = SUBLANES
assert DEC_SEQ <= NEW_SLOTS and SEQ_GROUP * NEW_SLOTS <= WINDOW
SAMPLE_KEYS = STATE_KEYS + WINDOW
SAMPLE_ROWS = DEC_SEQ * SEQ_GROUP


def _t5_bucket_np(dist):
    n = np.maximum(dist, 0)
    max_exact = N_BUCKETS // 2
    nf = np.maximum(n, 1).astype(np.float32)
    large = max_exact + (np.log(nf / np.float32(max_exact)) / np.float32(math.log(MAX_DISTANCE / max_exact))
                         * np.float32(N_BUCKETS - max_exact)).astype(np.int32)
    large = np.minimum(large, N_BUCKETS - 1)
    return np.where(n < max_exact, n, large).astype(np.int32)


def _prompt_bucket_table():
    qi = np.arange(WINDOW)[:, None]
    c = np.arange(WINDOW)[None, :]
    dist = np.where(c > qi, qi + WINDOW - c, qi - c)
    return _t5_bucket_np(dist)


def _sample_bucket_table():
    t = (np.arange(SAMPLE_ROWS) // SEQ_GROUP)[:, None]
    b = (np.arange(SAMPLE_ROWS) % SEQ_GROUP)[:, None]
    col = np.arange(SAMPLE_KEYS)[None, :]
    is_state = col < STATE_KEYS
    slot = (col - STATE_KEYS) % NEW_SLOTS
    is_new = (col >= STATE_KEYS) & (col < STATE_KEYS + SEQ_GROUP * NEW_SLOTS) & (slot < DEC_SEQ)
    kb = np.where(is_state, col // WINDOW, (col - STATE_KEYS) // NEW_SLOTS)
    kpos = np.where(is_state, col % WINDOW, WINDOW + slot)
    dist = (WINDOW + t) - kpos
    valid = (is_state | is_new) & (kb == b) & (dist >= 0) & (dist < WINDOW)
    return np.where(valid, _t5_bucket_np(dist), -1).astype(np.int32)


def _rms(x, g):
    y = x * lax.rsqrt(jnp.mean(x * x, axis=-1, keepdims=True) + EPS)
    return y * g


LOG2E = math.log2(math.e)
Q_SCALE = HEAD_DIM ** -0.5 * LOG2E


def _bias_from_buckets(bucket, rb_ref, h):
    t = jnp.zeros(bucket.shape, F32)
    for bkt in range(N_BUCKETS):
        t = jnp.where(bucket == bkt, rb_ref[bkt, h] * LOG2E, t)
    return t


def _sink_softmax(s, sink):
    m = jnp.maximum(jnp.max(s, axis=-1, keepdims=True), sink)
    p = jnp.exp2(s - m)
    return p, jnp.sum(p, axis=-1, keepdims=True) + jnp.exp2(sink - m)


def _softmax_pv(s, valid, bias, sink, v):
    p, denom = _sink_softmax(jnp.where(valid, s + bias, NEG), sink)
    return jnp.dot(p.astype(BF16), v, preferred_element_type=F32) / denom


def _ffn_kernel(*refs, layer, half, tail_in, final_split):
    refs = list(refs)
    x_ref = refs.pop(0)
    xt_ref = refs.pop(0) if tail_in else None
    g_ref, wg_hbm, wu_hbm, wd_hbm = refs[:4]
    del refs[:4]
    fg_ref = refs.pop(0) if final_split else None
    o_ref = refs.pop(0)
    ot_ref = refs.pop(0) if final_split else None
    wg, wu, wd, stage_in, stage_dn, sem = refs
    i = pl.program_id(0)

    def normed(x):
        return _rms(x, g_ref[...]).astype(BF16)

    def result(x, y):
        out = x + 0.5 * y
        return _rms(out, fg_ref[...]) if final_split else out

    def chunk_copies(c, slot):
        cols = pl.ds(c * FF_CHUNK, FF_CHUNK)
        return (
            pltpu.make_async_copy(wg_hbm.at[layer, half, :, cols], stage_in.at[0, slot], sem.at[0, slot]),
            pltpu.make_async_copy(wu_hbm.at[layer, half, :, cols], stage_in.at[1, slot], sem.at[1, slot]),
            pltpu.make_async_copy(wd_hbm.at[layer, half, cols, :], stage_dn.at[slot], sem.at[2, slot]),
        )

    @pl.when(i == 0)
    def _():
        for c in range(min(FF_SLOTS, N_FF_CHUNKS)):
            for cp in chunk_copies(c, c % FF_SLOTS):
                cp.start()
        if tail_in:
            x = jnp.concatenate([xt_ref[:, t, :] for t in range(DEC_SEQ)], axis=0)
        else:
            x = x_ref[0:TM, :]
        h = normed(x)
        acc = jnp.zeros((TM, D_MODEL), F32)
        for c in range(N_FF_CHUNKS):
            slot = c % FF_SLOTS
            cols = slice(c * FF_CHUNK, (c + 1) * FF_CHUNK)
            for cp in chunk_copies(c, slot):
                cp.wait()
            wg[:, cols] = stage_in[0, slot].astype(BF16)
            wu[:, cols] = stage_in[1, slot].astype(BF16)
            wd[cols, :] = stage_dn[slot].astype(BF16)
            if c + FF_SLOTS < N_FF_CHUNKS:
                for cp in chunk_copies(c + FF_SLOTS, slot):
                    cp.start()
            gate = jnp.dot(h, wg[:, cols], preferred_element_type=F32)
            up = jnp.dot(h, wu[:, cols], preferred_element_type=F32)
            act = (jax.nn.silu(gate) * up).astype(BF16)
            acc = acc + jnp.dot(act, wd[cols, :], preferred_element_type=F32)
        out = result(x, acc)
        if final_split:
            for t in range(DEC_SEQ):
                ot_ref[:, t, :] = out[t * DEC_BATCH:(t + 1) * DEC_BATCH, :]
        else:
            o_ref[0:TM, :] = out

    def half_block(hf, carry):
        rows = pl.ds(pl.multiple_of(hf * TM, TM), TM)
        x = x_ref[rows, :]
        h = normed(x)
        gate = jnp.dot(h, wg[...], preferred_element_type=F32)
        up = jnp.dot(h, wu[...], preferred_element_type=F32)
        act = (jax.nn.silu(gate) * up).astype(BF16)
        o_ref[rows, :] = result(x, jnp.dot(act, wd[...], preferred_element_type=F32))
        return carry

    @pl.when(i > 0)
    def _():
        lax.fori_loop(0, FFN_TM // TM, half_block, 0)


def _resident(shape):
    return pl.BlockSpec(shape, lambda *_: (0,) * len(shape), pipeline_mode=pl.Buffered(1))


def _ffn(x_in, gain, wg, wu, wd, layer, half, *, tail_in=None, final_gain=None):
    hbm = pl.BlockSpec(memory_space=pl.ANY)
    row_block = pl.BlockSpec((FFN_TM, D_MODEL), lambda i: (jnp.where(i == 0, FFN_PROMPT_STEPS, i - 1), 0))
    prompt_block = pl.BlockSpec((FFN_TM, D_MODEL), lambda i: (jnp.maximum(i - 1, 0), 0))
    sample_block = pl.BlockSpec((DEC_BATCH, DEC_SEQ, D_MODEL), lambda i: (0, 0, 0))
    if tail_in is None:
        args, in_specs = [x_in], [row_block]
    else:
        args, in_specs = [x_in, tail_in], [prompt_block, sample_block]
    args += [gain.reshape(1, D_MODEL), wg, wu, wd]
    in_specs += [_resident((1, D_MODEL)), hbm, hbm, hbm]
    if final_gain is None:
        out_shape = jax.ShapeDtypeStruct((N_TOK, D_MODEL), F32)
        out_specs = row_block
    else:
        args.append(final_gain.reshape(1, D_MODEL))
        in_specs.append(_resident((1, D_MODEL)))
        out_shape = (jax.ShapeDtypeStruct((PROMPT_TOK, D_MODEL), F32),
                     jax.ShapeDtypeStruct((DEC_BATCH, DEC_SEQ, D_MODEL), F32))
        out_specs = (prompt_block, sample_block)
    aliases = {0: 0} if (tail_in is None and final_gain is None) else {}
    return pl.pallas_call(
        functools.partial(_ffn_kernel, layer=layer, half=half, tail_in=tail_in is not None,
                          final_split=final_gain is not None),
        out_shape=out_shape,
        grid=(FFN_PROMPT_STEPS + 1,),
        in_specs=in_specs,
        out_specs=out_specs,
        scratch_shapes=[
            pltpu.VMEM((D_MODEL, D_FF), BF16),
            pltpu.VMEM((D_MODEL, D_FF), BF16),
            pltpu.VMEM((D_FF, D_MODEL), BF16),
            pltpu.VMEM((2, FF_SLOTS, D_MODEL, FF_CHUNK), F32),
            pltpu.VMEM((FF_SLOTS, FF_CHUNK, D_MODEL), F32),
            pltpu.SemaphoreType.DMA((3, FF_SLOTS)),
        ],
        input_output_aliases=aliases,
        compiler_params=pltpu.CompilerParams(
            dimension_semantics=("arbitrary",), vmem_limit_bytes=VMEM_LIMIT_BYTES),
        name="macaron_half",
    )(*args)


def _even_prompt_kernel(x_ref, g_ref, win_ref, wout_ref, wpool_ref, pscale_ref, bucket_ref, rb_ref, sink_ref,
                        o_ref, klast_ref, vlast_ref, plast_ref,
                        kbuf, kbuf_r, vbuf, vbuf_r, zb, s2b, s4b, s8b, bias_ref, cat_ref):
    b = pl.program_id(0)
    s = pl.program_id(1)
    last = pl.num_programs(1) - 1

    @pl.when((b == 0) & (s == 0))
    def _():
        bucket = bucket_ref[...]
        for h in range(N_HEADS_A):
            bias_ref[h] = _bias_from_buckets(bucket, rb_ref, h)

    @pl.when(s == 0)
    def _():
        for buf in (kbuf, kbuf_r, vbuf, vbuf_r):
            buf[0:WINDOW, :] = jnp.zeros((WINDOW, KV_WIDTH), BF16)
        zb[0:POOL_CTX, :] = jnp.zeros((POOL_CTX, POOL_WIDTH), F32)

    @pl.when(s > 0)
    def _():
        for buf in (kbuf, kbuf_r, vbuf, vbuf_r):
            buf[0:WINDOW, :] = buf[EVEN_TM:EVEN_TM + WINDOW, :]
        zb[0:POOL_CTX, :] = zb[EVEN_TM:EVEN_TM + POOL_CTX, :]

    x = x_ref[...]
    h = _rms(x, g_ref[...]).astype(BF16)
    proj = jnp.dot(h, win_ref[...], preferred_element_type=F32)
    q = proj[:, 0:ATTN_WIDTH] * Q_SCALE
    k = proj[:, ATTN_WIDTH:ATTN_WIDTH + KV_WIDTH]
    v = proj[:, ATTN_WIDTH + KV_WIDTH:ATTN_WIDTH + 2 * KV_WIDTH]
    up = proj[:, ATTN_WIDTH + 2 * KV_WIDTH:IN_EVEN]

    qlane = lax.broadcasted_iota(jnp.int32, (EVEN_TM, ATTN_WIDTH), 1) % LANES
    q_half = (jnp.where(qlane < HEAD_DIM, q, 0.0).astype(BF16),
              jnp.where(qlane >= HEAD_DIM, q, 0.0).astype(BF16))
    kbuf[WINDOW:, :] = k.astype(BF16)
    vbuf[WINDOW:, :] = v.astype(BF16)
    kbuf_r[WINDOW:, :] = pltpu.roll(k, HEAD_DIM, 1).astype(BF16)
    vbuf_r[WINDOW:, :] = pltpu.roll(v, HEAD_DIM, 1).astype(BF16)

    qrow = lax.broadcasted_iota(jnp.int32, (WINDOW, WINDOW), 0)
    kcol = lax.broadcasted_iota(jnp.int32, (WINDOW, WINDOW), 1)
    prev = kcol > qrow
    olane = lax.broadcasted_iota(jnp.int32, (WINDOW, LANES), 1)
    head_sets = [[hd for hd in range(N_HEADS_A) if (hd // GQA_GROUP == hd % 2) == plain] for plain in (True, False)]
    for i in range(QBLK_PER_TILE):
        rows = slice(i * WINDOW, (i + 1) * WINDOW)
        keys = slice(i * WINDOW, (i + 2) * WINDOW)
        out = {}
        for heads, kb, vb in zip(head_sets, (kbuf, kbuf_r), (vbuf, vbuf_r)):
            qs = jnp.concatenate(
                [q_half[hd % 2][rows, (hd // 2) * LANES:(hd // 2 + 1) * LANES] for hd in heads], axis=0)
            sc_all = lax.dot_general(qs, kb[keys, :], (((1,), (1,)), ((), ())), preferred_element_type=F32)
            ps, denoms = [], []
            for r, hd in enumerate(heads):
                sc = sc_all[r * WINDOW:(r + 1) * WINDOW, :]
                sc = jnp.where(prev, sc[:, 0:WINDOW], sc[:, WINDOW:]) + bias_ref[hd]
                if i == 0:
                    sc = jnp.where(prev & (qrow >= jnp.where(s > 0, WINDOW, 0)), NEG, sc)
                p, denom = _sink_softmax(sc, sink_ref[hd] * LOG2E)
                ps.append(jnp.concatenate([jnp.where(prev, p, 0.0), jnp.where(prev, 0.0, p)], axis=1).astype(BF16))
                denoms.append(denom)
            o_all = jnp.dot(jnp.concatenate(ps, axis=0), vb[keys, :], preferred_element_type=F32)
            for r, hd in enumerate(heads):
                out[hd] = o_all[r * WINDOW:(r + 1) * WINDOW, :] / denoms[r]
        for j in range(N_HEADS_A // 2):
            cat_ref[rows, j * LANES:(j + 1) * LANES] = jnp.where(
                olane < HEAD_DIM, out[2 * j], out[2 * j + 1]).astype(BF16)

    zb[POOL_CTX:, :] = up
    n = EVEN_TM + POOL_CTX
    s2b[8:, :] = zb[8:n, :] + zb[7:n - 1, :]
    s4b[16:, :] = s2b[16:n, POOL_GROUP:] + s2b[14:n - 2, POOL_GROUP:]
    s8b[24:, :] = s4b[24:n, POOL_GROUP:] + s4b[20:n - 4, POOL_GROUP:]
    s16 = s8b[32:n, POOL_GROUP:] + s8b[24:n - 8, POOL_GROUP:]
    sums = (s2b[POOL_CTX:, 0:POOL_GROUP], s4b[POOL_CTX:, 0:POOL_GROUP], s8b[POOL_CTX:, 0:POOL_GROUP], s16)
    pos1 = lax.broadcasted_iota(jnp.int32, (POOL_HEAD, POOL_GROUP), 0) + s * EVEN_TM + 1
    for g, w in enumerate(POOL_WINDOWS):
        lanes = slice(g * POOL_GROUP, (g + 1) * POOL_GROUP)
        cnt = jnp.minimum(pos1, w).astype(F32)
        mean = jnp.concatenate([sums[g][0:POOL_HEAD] / cnt, sums[g][POOL_HEAD:] * (1.0 / w)], axis=0)
        d = (mean - up[:, lanes]).astype(BF16)
        y = jnp.dot(d, wpool_ref[g], preferred_element_type=F32) * pscale_ref[:, lanes]
        cat_ref[:, ATTN_WIDTH + g * POOL_GROUP:ATTN_WIDTH + (g + 1) * POOL_GROUP] = y.astype(BF16)

    o_ref[...] = x + jnp.dot(cat_ref[...], wout_ref[...], preferred_element_type=F32)

    @pl.when(s == last)
    def _():
        klast_ref[0] = k[EVEN_TM - WINDOW:, :]
        vlast_ref[0] = v[EVEN_TM - WINDOW:, :]
        plast_ref[0] = up[EVEN_TM - 2 * SUBLANES:, :]


def _even_prompt(x, gain, w_in, w_out, w_pool, pool_scale, rel_bias, sinks):
    bucket = jnp.asarray(_prompt_bucket_table())
    smem = pl.BlockSpec(memory_space=pltpu.SMEM)
    return pl.pallas_call(
        _even_prompt_kernel,
        out_shape=(
            jax.ShapeDtypeStruct((N_TOK, D_MODEL), F32),
            jax.ShapeDtypeStruct((BATCH, WINDOW, KV_WIDTH), F32),
            jax.ShapeDtypeStruct((BATCH, WINDOW, KV_WIDTH), F32),
            jax.ShapeDtypeStruct((BATCH, 2 * SUBLANES, POOL_WIDTH), F32),
        ),
        grid=(BATCH, TILES_PER_SEQ),
        in_specs=[
            pl.BlockSpec((EVEN_TM, D_MODEL), lambda b, s: (b * TILES_PER_SEQ + s, 0)),
            _resident((1, D_MODEL)),
            _resident((D_MODEL, IN_EVEN)),
            _resident((ATTN_WIDTH + POOL_WIDTH, D_MODEL)),
            _resident((N_POOL_GROUPS, POOL_GROUP, POOL_GROUP)),
            _resident((1, POOL_WIDTH)),
            _resident((WINDOW, WINDOW)),
            smem,
            smem,
        ],
        out_specs=(
            pl.BlockSpec((EVEN_TM, D_MODEL), lambda b, s: (b * TILES_PER_SEQ + s, 0)),
            pl.BlockSpec((1, WINDOW, KV_WIDTH), lambda b, s: (b, 0, 0)),
            pl.BlockSpec((1, WINDOW, KV_WIDTH), lambda b, s: (b, 0, 0)),
            pl.BlockSpec((1, 2 * SUBLANES, POOL_WIDTH), lambda b, s: (b, 0, 0)),
        ),
        scratch_shapes=[
            pltpu.VMEM((WINDOW + EVEN_TM, KV_WIDTH), BF16),
            pltpu.VMEM((WINDOW + EVEN_TM, KV_WIDTH), BF16),
            pltpu.VMEM((WINDOW + EVEN_TM, KV_WIDTH), BF16),
            pltpu.VMEM((WINDOW + EVEN_TM, KV_WIDTH), BF16),
            pltpu.VMEM((POOL_CTX + EVEN_TM, POOL_WIDTH), F32),
            pltpu.VMEM((POOL_CTX + EVEN_TM, POOL_WIDTH), F32),
            pltpu.VMEM((POOL_CTX + EVEN_TM, POOL_WIDTH - POOL_GROUP), F32),
            pltpu.VMEM((POOL_CTX + EVEN_TM, POOL_WIDTH - 2 * POOL_GROUP), F32),
            pltpu.VMEM((N_HEADS_A, WINDOW, WINDOW), F32),
            pltpu.VMEM((EVEN_TM, ATTN_WIDTH + POOL_WIDTH), BF16),
        ],
        input_output_aliases={0: 0},
        compiler_params=pltpu.CompilerParams(
            dimension_semantics=("arbitrary", "arbitrary"), vmem_limit_bytes=VMEM_LIMIT_BYTES),
        name="even_mixer_prompt",
    )(x, gain.reshape(1, D_MODEL), w_in, w_out, w_pool, pool_scale.reshape(1, POOL_WIDTH), bucket,
      rel_bias, sinks)


def _even_sample_kernel(x_ref, g_ref, win_ref, wout_ref, wpool_ref, pscale_ref, bucket_ref, rb_ref, sink_ref,
                        sk_ref, sv_ref, spool_ref,
                        o_ref, nk_ref, nv_ref, npool_ref,
                        knew_ref, vnew_ref, upnew_ref, qsel, bias_ref, attn_f32, cat_ref):
    g = pl.program_id(0)
    last = pl.num_programs(0) - 1

    @pl.when(g == 0)
    def _():
        bucket = bucket_ref[...]
        for h in range(N_HEADS_A):
            bias_ref[h] = _bias_from_buckets(bucket, rb_ref, h)
        x = x_ref[...]
        h = _rms(x, g_ref[...]).astype(BF16)
        proj = jnp.dot(h, win_ref[...], preferred_element_type=F32)
        q = proj[:, 0:ATTN_WIDTH] * Q_SCALE
        zeros = jnp.zeros(((NEW_SLOTS - DEC_SEQ) * DEC_BATCH, KV_WIDTH), F32)
        knew_ref[0:SAMPLE_TOK, :] = proj[:, ATTN_WIDTH:ATTN_WIDTH + KV_WIDTH]
        vnew_ref[0:SAMPLE_TOK, :] = proj[:, ATTN_WIDTH + KV_WIDTH:ATTN_WIDTH + 2 * KV_WIDTH]
        knew_ref[SAMPLE_TOK:, :] = zeros
        vnew_ref[SAMPLE_TOK:, :] = zeros
        upnew_ref[...] = proj[:, ATTN_WIDTH + 2 * KV_WIDTH:IN_EVEN]
        qlane = lax.broadcasted_iota(jnp.int32, (SAMPLE_TOK, LANES), 1)
        for hd in range(N_HEADS_A):
            j, half, kvh = hd // 2, hd % 2, hd // GQA_GROUP
            blk = q[:, j * LANES:(j + 1) * LANES]
            if half != kvh:
                blk = pltpu.roll(blk, HEAD_DIM, 1)
            keep = (qlane < HEAD_DIM) if kvh == 0 else (qlane >= HEAD_DIM)
            qsel[hd] = jnp.where(keep, blk, 0.0)

    lax.fori_loop(0, GROUPS_PER_STEP, functools.partial(
        _even_sample_group, g, bucket_ref, sink_ref, sk_ref, sv_ref, nk_ref, nv_ref,
        knew_ref, vnew_ref, qsel, bias_ref, attn_f32), 0)

    @pl.when(g == last)
    def _():
        cat_ref[:, 0:ATTN_WIDTH] = attn_f32[...].astype(BF16)
        for r in range(POOL_BUF):
            if r < POOL_BUF - DEC_SEQ:
                npool_ref[r] = spool_ref[r + DEC_SEQ]
            else:
                t = r - (POOL_BUF - DEC_SEQ)
                npool_ref[r] = upnew_ref[t * DEC_BATCH:(t + 1) * DEC_BATCH, :]
        for t in range(DEC_SEQ):
            rows = slice(t * DEC_BATCH, (t + 1) * DEC_BATCH)
            for gi, w in enumerate(POOL_WINDOWS):
                lanes = slice(gi * POOL_GROUP, (gi + 1) * POOL_GROUP)
                acc = upnew_ref[rows, lanes]
                for back in range(1, w):
                    tt = t - back
                    if tt >= 0:
                        acc = acc + upnew_ref[tt * DEC_BATCH:(tt + 1) * DEC_BATCH, lanes]
                    else:
                        acc = acc + spool_ref[POOL_BUF + tt, :, lanes]
                cnt = float(min(PAST_LEN + t + 1, w))
                d = (acc / cnt - upnew_ref[rows, lanes]).astype(BF16)
                y = jnp.dot(d, wpool_ref[gi], preferred_element_type=F32) * pscale_ref[:, lanes]
                cat_ref[rows, ATTN_WIDTH + gi * POOL_GROUP:ATTN_WIDTH + (gi + 1) * POOL_GROUP] = y.astype(BF16)
        o_ref[...] = x_ref[...] + jnp.dot(cat_ref[...], wout_ref[...], preferred_element_type=F32)


def _even_sample_group(g, bucket_ref, sink_ref, sk_ref, sv_ref, nk_ref, nv_ref,
                       knew_ref, vnew_ref, qsel, bias_ref, attn_f32, gg, carry):
    olane = lax.broadcasted_iota(jnp.int32, (SAMPLE_ROWS, LANES), 1)
    seq0 = gg * SEQ_GROUP
    row0 = pl.multiple_of((g * GROUPS_PER_STEP + gg) * SEQ_GROUP, SEQ_GROUP)
    pad = [jnp.zeros((WINDOW - SEQ_GROUP * NEW_SLOTS, KV_WIDTH), F32)]
    knew = jnp.concatenate(
        [knew_ref[pl.ds(row0 + b, NEW_SLOTS, stride=DEC_BATCH), :] for b in range(SEQ_GROUP)] + pad, axis=0)
    vnew = jnp.concatenate(
        [vnew_ref[pl.ds(row0 + b, NEW_SLOTS, stride=DEC_BATCH), :] for b in range(SEQ_GROUP)] + pad, axis=0)
    kt_all = jnp.concatenate([sk_ref[seq0 + b] for b in range(SEQ_GROUP)], axis=1).astype(BF16)
    vt_all = jnp.concatenate([sv_ref[seq0 + b] for b in range(SEQ_GROUP)], axis=1).astype(BF16)

    qg = jnp.concatenate(
        [qsel[hd, pl.ds(t * DEC_BATCH + row0, SEQ_GROUP), :] for hd in range(N_HEADS_A) for t in range(DEC_SEQ)],
        axis=0).astype(BF16)
    nt = (((1,), (1,)), ((), ()))
    sc = jnp.concatenate(
        [jnp.dot(qg, kt_all, preferred_element_type=F32),
         lax.dot_general(qg, knew.astype(BF16), nt, preferred_element_type=F32)], axis=1)
    valid = jnp.concatenate([bucket_ref[...]] * N_HEADS_A, axis=0) >= 0
    bias = bias_ref[...].reshape(N_HEADS_A * SAMPLE_ROWS, SAMPLE_KEYS)
    sink = jnp.concatenate(
        [jnp.full((SAMPLE_ROWS, 1), sink_ref[hd] * LOG2E, F32) for hd in range(N_HEADS_A)], axis=0)
    p, denom = _sink_softmax(jnp.where(valid, sc + bias, NEG), sink)
    p = p.astype(BF16)
    o_all = (lax.dot_general(p[:, 0:STATE_KEYS], vt_all, nt, preferred_element_type=F32)
             + jnp.dot(p[:, STATE_KEYS:], vnew.astype(BF16), preferred_element_type=F32)) / denom
    for j in range(N_HEADS_A // 2):
        halves = []
        for half in range(2):
            hd = 2 * j + half
            o = o_all[hd * SAMPLE_ROWS:(hd + 1) * SAMPLE_ROWS, :]
            if hd // GQA_GROUP != half:
                o = pltpu.roll(o, HEAD_DIM, 1)
            halves.append(o)
        blk = jnp.where(olane < HEAD_DIM, halves[0], halves[1])
        for t in range(DEC_SEQ):
            attn_f32[pl.ds(t * DEC_BATCH + row0, SEQ_GROUP), j * LANES:(j + 1) * LANES] = (
                blk[t * SEQ_GROUP:(t + 1) * SEQ_GROUP, :])

    lane = lax.broadcasted_iota(jnp.int32, (KV_WIDTH, WINDOW), 1)
    for new, old_ref, out_ref in ((knew, sk_ref, nk_ref), (vnew, sv_ref, nv_ref)):
        new_t = new.T
        for b in range(SEQ_GROUP):
            tail = pltpu.roll(new_t, (WINDOW - DEC_SEQ - b * NEW_SLOTS) % WINDOW, 1)
            out_ref[seq0 + b] = jnp.where(
                lane >= WINDOW - DEC_SEQ, tail, pltpu.roll(old_ref[seq0 + b], WINDOW - DEC_SEQ, 1))
    return carry


def _even_sample(x, gain, w_in, w_out, w_pool, pool_scale, rel_bias, sinks, state_k, state_v, state_pool_t):
    bucket = jnp.asarray(_sample_bucket_table())
    smem = pl.BlockSpec(memory_space=pltpu.SMEM)
    sample_block = PROMPT_TOK // SAMPLE_TOK
    return pl.pallas_call(
        _even_sample_kernel,
        out_shape=(
            jax.ShapeDtypeStruct((N_TOK, D_MODEL), F32),
            jax.ShapeDtypeStruct((DEC_BATCH, KV_WIDTH, WINDOW), F32),
            jax.ShapeDtypeStruct((DEC_BATCH, KV_WIDTH, WINDOW), F32),
            jax.ShapeDtypeStruct((POOL_BUF, DEC_BATCH, POOL_WIDTH), F32),
        ),
        grid=(N_SEQ_GROUPS // GROUPS_PER_STEP,),
        in_specs=[
            pl.BlockSpec((SAMPLE_TOK, D_MODEL), lambda g: (sample_block, 0)),
            _resident((1, D_MODEL)),
            _resident((D_MODEL, IN_EVEN)),
            _resident((ATTN_WIDTH + POOL_WIDTH, D_MODEL)),
            _resident((N_POOL_GROUPS, POOL_GROUP, POOL_GROUP)),
            _resident((1, POOL_WIDTH)),
            _resident((SAMPLE_ROWS, SAMPLE_KEYS)),
            smem,
            smem,
            pl.BlockSpec((GROUPS_PER_STEP * SEQ_GROUP, KV_WIDTH, WINDOW), lambda g: (g, 0, 0)),
            pl.BlockSpec((GROUPS_PER_STEP * SEQ_GROUP, KV_WIDTH, WINDOW), lambda g: (g, 0, 0)),
            _resident((POOL_BUF, DEC_BATCH, POOL_WIDTH)),
        ],
        out_specs=(
            pl.BlockSpec((SAMPLE_TOK, D_MODEL), lambda g: (sample_block, 0)),
            pl.BlockSpec((GROUPS_PER_STEP * SEQ_GROUP, KV_WIDTH, WINDOW), lambda g: (g, 0, 0)),
            pl.BlockSpec((GROUPS_PER_STEP * SEQ_GROUP, KV_WIDTH, WINDOW), lambda g: (g, 0, 0)),
            pl.BlockSpec((POOL_BUF, DEC_BATCH, POOL_WIDTH), lambda g: (0, 0, 0)),
        ),
        scratch_shapes=[
            pltpu.VMEM((NEW_SLOTS * DEC_BATCH, KV_WIDTH), F32),
            pltpu.VMEM((NEW_SLOTS * DEC_BATCH, KV_WIDTH), F32),
            pltpu.VMEM((SAMPLE_TOK, POOL_WIDTH), F32),
            pltpu.VMEM((N_HEADS_A, SAMPLE_TOK, LANES), F32),
            pltpu.VMEM((N_HEADS_A, SAMPLE_ROWS, SAMPLE_KEYS), F32),
            pltpu.VMEM((SAMPLE_TOK, ATTN_WIDTH), F32),
            pltpu.VMEM((SAMPLE_TOK, ATTN_WIDTH + POOL_WIDTH), BF16),
        ],
        input_output_aliases={0: 0},
        compiler_params=pltpu.CompilerParams(
            dimension_semantics=("arbitrary",), vmem_limit_bytes=VMEM_LIMIT_BYTES),
        name="even_mixer_sample",
    )(x, gain.reshape(1, D_MODEL), w_in, w_out, w_pool, pool_scale.reshape(1, POOL_WIDTH), bucket,
      rel_bias, sinks, state_k, state_v, state_pool_t)


def _sgu_front(x, g_ref, win_ref, gv_ref):
    h = _rms(x, g_ref[...]).astype(BF16)
    uv = jax.nn.gelu(jnp.dot(h, win_ref[...], preferred_element_type=F32))
    return uv[:, 0:SGU_WIDTH], _rms(uv[:, SGU_WIDTH:], gv_ref[...])


def _odd_prompt_kernel(x_ref, g_ref, win_ref, gv_ref, ws_ref, bs_ref, wout_ref, o_ref, gated_ref):
    x = x_ref[...]
    u, v = _sgu_front(x, g_ref, win_ref, gv_ref)
    vb = v.astype(BF16)
    r = lax.broadcasted_iota(jnp.int32, (CHUNK, CHUNK), 0)
    c = lax.broadcasted_iota(jnp.int32, (CHUNK, CHUNK), 1)
    for gi in range(SGU_GROUPS):
        w = jnp.where(r >= c, ws_ref[gi], 0.0).astype(BF16)
        bias = jnp.concatenate([bs_ref[gi]] * (SGU_GROUP_W // LANES), axis=1)
        lanes = slice(gi * SGU_GROUP_W, (gi + 1) * SGU_GROUP_W)
        for ci in range(ODD_TM // CHUNK):
            rows = slice(ci * CHUNK, (ci + 1) * CHUNK)
            mixed = jnp.dot(w, vb[rows, lanes], preferred_element_type=F32) + bias
            gated_ref[rows, lanes] = (u[rows, lanes] * mixed).astype(BF16)
    o_ref[...] = x + jnp.dot(gated_ref[...], wout_ref[...], preferred_element_type=F32)


def _odd_prompt(x, gain, w_in, g_v, w_s, b_s, w_out):
    bias = jnp.broadcast_to(b_s[:, :, None], (SGU_GROUPS, CHUNK, LANES))
    return pl.pallas_call(
        _odd_prompt_kernel,
        out_shape=jax.ShapeDtypeStruct((N_TOK, D_MODEL), F32),
        grid=(PROMPT_TOK // ODD_TM,),
        in_specs=[
            pl.BlockSpec((ODD_TM, D_MODEL), lambda i: (i, 0)),
            _resident((1, D_MODEL)),
            _resident((D_MODEL, 2 * SGU_WIDTH)),
            _resident((1, SGU_WIDTH)),
            _resident((SGU_GROUPS, CHUNK, CHUNK)),
            _resident((SGU_GROUPS, CHUNK, LANES)),
            _resident((SGU_WIDTH, D_MODEL)),
        ],
        out_specs=pl.BlockSpec((ODD_TM, D_MODEL), lambda i: (i, 0)),
        scratch_shapes=[pltpu.VMEM((ODD_TM, SGU_WIDTH), BF16)],
        input_output_aliases={0: 0},
        compiler_params=pltpu.CompilerParams(
            dimension_semantics=("arbitrary",), vmem_limit_bytes=VMEM_LIMIT_BYTES),
        name="odd_mixer_prompt",
    )(x, gain.reshape(1, D_MODEL), w_in, g_v.reshape(1, SGU_WIDTH), w_s, bias, w_out)


def _odd_sample_kernel(x_ref, g_ref, win_ref, gv_ref, coef_ref, bias_ref, wout_ref, o_ref, v_ref):
    x = x_ref[...]
    u, v = _sgu_front(x, g_ref, win_ref, gv_ref)
    for t in range(DEC_SEQ):
        v_ref[:, t, :] = v[t * DEC_BATCH:(t + 1) * DEC_BATCH, :]
    gated = []
    for t in range(DEC_SEQ):
        mixed = bias_ref[t:t + 1, :]
        for s in range(t + 1):
            mixed = mixed + coef_ref[t, s:s + 1, :] * v[s * DEC_BATCH:(s + 1) * DEC_BATCH, :]
        gated.append((u[t * DEC_BATCH:(t + 1) * DEC_BATCH, :] * mixed).astype(BF16))
    o_ref[...] = x + jnp.dot(jnp.concatenate(gated, axis=0), wout_ref[...], preferred_element_type=F32)


def _odd_sample(x, gain, w_in, g_v, w_s, b_s, w_out):
    coef = jnp.repeat(jnp.transpose(w_s[:, :DEC_SEQ, :DEC_SEQ], (1, 2, 0)), SGU_GROUP_W, axis=-1)
    bias = jnp.repeat(b_s[:, :DEC_SEQ].T, SGU_GROUP_W, axis=-1)
    sample_block = PROMPT_TOK // SAMPLE_TOK
    return pl.pallas_call(
        _odd_sample_kernel,
        out_shape=(
            jax.ShapeDtypeStruct((N_TOK, D_MODEL), F32),
            jax.ShapeDtypeStruct((DEC_BATCH, DEC_SEQ, SGU_WIDTH), F32),
        ),
        grid=(1,),
        in_specs=[
            pl.BlockSpec((SAMPLE_TOK, D_MODEL), lambda i: (sample_block, 0)),
            _resident((1, D_MODEL)),
            _resident((D_MODEL, 2 * SGU_WIDTH)),
            _resident((1, SGU_WIDTH)),
            _resident((DEC_SEQ, DEC_SEQ, SGU_WIDTH)),
            _resident((DEC_SEQ, SGU_WIDTH)),
            _resident((SGU_WIDTH, D_MODEL)),
        ],
        out_specs=(
            pl.BlockSpec((SAMPLE_TOK, D_MODEL), lambda i: (sample_block, 0)),
            pl.BlockSpec((DEC_BATCH, DEC_SEQ, SGU_WIDTH), lambda i: (0, 0, 0)),
        ),
        input_output_aliases={0: 0},
        compiler_params=pltpu.CompilerParams(
            dimension_semantics=("arbitrary",), vmem_limit_bytes=VMEM_LIMIT_BYTES),
        name="odd_mixer_sample",
    )(x, gain.reshape(1, D_MODEL), w_in, g_v.reshape(1, SGU_WIDTH), coef, bias, w_out)


def _window_to_lanes(w):
    return jnp.transpose(w, (0, 2, 3, 1)).reshape(DEC_BATCH, KV_WIDTH, WINDOW)


def _window_from_lanes(w):
    return jnp.transpose(w.reshape(DEC_BATCH, N_KV_A, HEAD_DIM, WINDOW), (0, 3, 1, 2))


def kernel(x_prompt, x_sample, state_win_k, state_win_v, state_pool, rel_bias, norm_gains, final_gain,
           ffn_gate, ffn_up, ffn_down, w_in_even, w_out_even, attn_sinks, w_pool, pool_scale,
           w_in_odd, sgu_norm, w_spatial, b_spatial, w_out_odd):
    xp = x_prompt.reshape(PROMPT_TOK, D_MODEL)

    kp_l, vp_l, pp_l, ks_l, vs_l, ps_l, sv_l = [], [], [], [], [], [], []
    x = None
    y_prompt = y_sample = None
    for l in range(DEPTH):
        fa = (norm_gains[l, 0], ffn_gate, ffn_up, ffn_down, l, 0)
        fb = (norm_gains[l, 2], ffn_gate, ffn_up, ffn_down, l, 1)
        if l == 0:
            x = _ffn(xp, *fa, tail_in=x_sample)
        else:
            x = _ffn(x, *fa)
        g1 = norm_gains[l, 1]
        if l % 2 == 0:
            e = l // 2
            w_in, w_out = w_in_even[e].astype(BF16), w_out_even[e].astype(BF16)
            wp = w_pool[e].astype(BF16)
            x, kp, vp, pp = _even_prompt(x, g1, w_in, w_out, wp, pool_scale[e], rel_bias, attn_sinks[e])
            x, nk, nv, npool = _even_sample(
                x, g1, w_in, w_out, wp, pool_scale[e], rel_bias, attn_sinks[e],
                _window_to_lanes(state_win_k[e]), _window_to_lanes(state_win_v[e]),
                jnp.swapaxes(state_pool[e], 0, 1))
            kp_l.append(kp.reshape(BATCH, WINDOW, N_KV_A, HEAD_DIM))
            vp_l.append(vp.reshape(BATCH, WINDOW, N_KV_A, HEAD_DIM))
            pp_l.append(pp[:, 2 * SUBLANES - POOL_BUF:, :])
            ks_l.append(_window_from_lanes(nk))
            vs_l.append(_window_from_lanes(nv))
            ps_l.append(jnp.swapaxes(npool, 0, 1))
        else:
            o = l // 2
            w_in, w_out = w_in_odd[o].astype(BF16), w_out_odd[o].astype(BF16)
            x = _odd_prompt(x, g1, w_in, sgu_norm[o], w_spatial[o], b_spatial[o], w_out)
            x, sv_new = _odd_sample(x, g1, w_in, sgu_norm[o], w_spatial[o], b_spatial[o], w_out)
            sv_l.append(sv_new)
        if l < DEPTH - 1:
            x = _ffn(x, *fb)
        else:
            y_prompt, y_sample = _ffn(x, *fb, final_gain=final_gain)

    return (y_prompt.reshape(BATCH, SEQ, D_MODEL), y_sample,
            jnp.stack(kp_l), jnp.stack(vp_l), jnp.stack(pp_l),
            jnp.stack(ks_l), jnp.stack(vs_l), jnp.stack(ps_l), jnp.stack(sv_l))
```

```python
import functools
import math

import numpy as np
import jax
import jax.numpy as jnp
from jax import lax
from jax.experimental import pallas as pl
from jax.experimental.pallas import tpu as pltpu

F32 = jnp.float32
BF16 = jnp.bfloat16

D_MODEL = 1024
BATCH = 8
SEQ = 2048
DEPTH = 2
DEC_BATCH = 128
DEC_SEQ = 4
PAST_LEN = 16384
N_HEADS_A = 8
N_KV_A = 2
HEAD_DIM = 64
GQA_GROUP = N_HEADS_A // N_KV_A
WINDOW = 128
ATTN_WIDTH = N_HEADS_A * HEAD_DIM
KV_WIDTH = N_KV_A * HEAD_DIM
N_BUCKETS = 32
MAX_DISTANCE = 128
POOL_WINDOWS = (2, 4, 8, 16)
N_POOL_GROUPS = len(POOL_WINDOWS)
POOL_GROUP = 128
POOL_WIDTH = N_POOL_GROUPS * POOL_GROUP
POOL_BUF = max(POOL_WINDOWS) - 1
IN_EVEN = ATTN_WIDTH + 2 * KV_WIDTH + POOL_WIDTH
CHUNK = 128
SGU_WIDTH = 1024
SGU_GROUPS = 4
SGU_GROUP_W = SGU_WIDTH // SGU_GROUPS
D_FF = 2816
EPS = 1e-6
NEG = -1e30

LANES = 128
SUBLANES = 8
VMEM_LIMIT_BYTES = 56 * 1024 * 1024

PROMPT_TOK = BATCH * SEQ
SAMPLE_TOK = DEC_BATCH * DEC_SEQ
N_TOK = PROMPT_TOK + SAMPLE_TOK

TM = SAMPLE_TOK
assert PROMPT_TOK % TM == 0
PROMPT_BLOCKS = PROMPT_TOK // TM
EVEN_TM = 1024
assert SEQ % EVEN_TM == 0 and EVEN_TM % WINDOW == 0
TILES_PER_SEQ = SEQ // EVEN_TM
QBLK_PER_TILE = EVEN_TM // WINDOW
FFN_TM = 2 * TM
assert PROMPT_TOK % FFN_TM == 0
FFN_PROMPT_STEPS = PROMPT_TOK // FFN_TM
ODD_TM = 1024
assert PROMPT_TOK % ODD_TM == 0 and ODD_TM % CHUNK == 0
FF_CHUNK = 256
N_FF_CHUNKS = D_FF // FF_CHUNK
FF_SLOTS = 3
POOL_HEAD = 2 * SUBLANES
assert all(w & (w - 1) == 0 and w <= POOL_HEAD for w in POOL_WINDOWS)
POOL_CTX = 32
SEQ_GROUP = 8
N_SEQ_GROUPS = DEC_BATCH // SEQ_GROUP
GROUPS_PER_STEP = 4
assert N_SEQ_GROUPS % GROUPS_PER_STEP == 0
STATE_KEYS = SEQ_GROUP * WINDOW
NEW_SLOTS = SUBLANES
assert DEC_SEQ <= NEW_SLOTS and SEQ_GROUP * NEW_SLOTS <= WINDOW
SAMPLE_KEYS = STATE_KEYS + WINDOW
SAMPLE_ROWS = DEC_SEQ * SEQ_GROUP


assert N_KV_A == 2 and KV_WIDTH == LANES
HEAD_ORDER = [h for j in range(GQA_GROUP) for h in (j, j + GQA_GROUP)]


def _t5_bucket_np(dist):
    n = np.maximum(dist, 0)
    max_exact = N_BUCKETS // 2
    nf = np.maximum(n, 1).astype(np.float32)
    large = max_exact + (np.log(nf / np.float32(max_exact)) / np.float32(math.log(MAX_DISTANCE / max_exact))
                         * np.float32(N_BUCKETS - max_exact)).astype(np.int32)
    large = np.minimum(large, N_BUCKETS - 1)
    return np.where(n < max_exact, n, large).astype(np.int32)


def _prompt_bucket_table():
    qi = np.arange(WINDOW)[:, None]
    c = np.arange(WINDOW)[None, :]
    dist = np.where(c > qi, qi + WINDOW - c, qi - c)
    return _t5_bucket_np(dist)


def _sample_bucket_table():
    t = (np.arange(SAMPLE_ROWS) // SEQ_GROUP)[:, None]
    b = (np.arange(SAMPLE_ROWS) % SEQ_GROUP)[:, None]
    col = np.arange(SAMPLE_KEYS)[None, :]
    is_state = col < STATE_KEYS
    slot = (col - STATE_KEYS) % NEW_SLOTS
    is_new = (col >= STATE_KEYS) & (col < STATE_KEYS + SEQ_GROUP * NEW_SLOTS) & (slot < DEC_SEQ)
    kb = np.where(is_state, col // WINDOW, (col - STATE_KEYS) // NEW_SLOTS)
    kpos = np.where(is_state, col % WINDOW, WINDOW + slot)
    dist = (WINDOW + t) - kpos
    valid = (is_state | is_new) & (kb == b) & (dist >= 0) & (dist < WINDOW)
    return np.where(valid, _t5_bucket_np(dist), -1).astype(np.int32)


def _rms(x, g):
    y = x * lax.rsqrt(jnp.mean(x * x, axis=-1, keepdims=True) + EPS)
    return y * g


LOG2E = math.log2(math.e)
Q_SCALE = HEAD_DIM ** -0.5 * LOG2E


def _bias_from_buckets(bucket, rb_ref, h):
    t = jnp.zeros(bucket.shape, F32)
    for bkt in range(N_BUCKETS):
        t = jnp.where(bucket == bkt, rb_ref[bkt, h] * LOG2E, t)
    return t


def _sink_softmax(s, sink):
    m = jnp.maximum(jnp.max(s, axis=-1, keepdims=True), sink)
    p = jnp.exp2(s - m)
    return p, jnp.sum(p, axis=-1, keepdims=True) + jnp.exp2(sink - m)


def _softmax_pv(s, valid, bias, sink, v):
    p, denom = _sink_softmax(jnp.where(valid, s + bias, NEG), sink)
    return jnp.dot(p.astype(BF16), v, preferred_element_type=F32) / denom


def _ffn_kernel(*refs, layer, half, tail_in, final_split):
    refs = list(refs)
    x_ref = refs.pop(0)
    xt_ref = refs.pop(0) if tail_in else None
    g_ref, wg_hbm, wu_hbm, wd_hbm = refs[:4]
    del refs[:4]
    fg_ref = refs.pop(0) if final_split else None
    o_ref = refs.pop(0)
    ot_ref = refs.pop(0) if final_split else None
    wg, wu, wd, stage_in, stage_dn, sem = refs
    i = pl.program_id(0)

    def normed(x):
        return _rms(x, g_ref[...]).astype(BF16)

    def result(x, y):
        out = x + 0.5 * y
        return _rms(out, fg_ref[...]) if final_split else out

    def chunk_copies(c, slot):
        cols = pl.ds(c * FF_CHUNK, FF_CHUNK)
        return (
            pltpu.make_async_copy(wg_hbm.at[layer, half, :, cols], stage_in.at[0, slot], sem.at[0, slot]),
            pltpu.make_async_copy(wu_hbm.at[layer, half, :, cols], stage_in.at[1, slot], sem.at[1, slot]),
            pltpu.make_async_copy(wd_hbm.at[layer, half, cols, :], stage_dn.at[slot], sem.at[2, slot]),
        )

    @pl.when(i == 0)
    def _():
        for c in range(min(FF_SLOTS, N_FF_CHUNKS)):
            for cp in chunk_copies(c, c % FF_SLOTS):
                cp.start()
        if tail_in:
            x = jnp.concatenate([xt_ref[:, t, :] for t in range(DEC_SEQ)], axis=0)
        else:
            x = x_ref[0:TM, :]
        h = normed(x)
        acc = jnp.zeros((TM, D_MODEL), F32)
        for c in range(N_FF_CHUNKS):
            slot = c % FF_SLOTS
            cols = slice(c * FF_CHUNK, (c + 1) * FF_CHUNK)
            for cp in chunk_copies(c, slot):
                cp.wait()
            wg[:, cols] = stage_in[0, slot].astype(BF16)
            wu[:, cols] = stage_in[1, slot].astype(BF16)
            wd[cols, :] = stage_dn[slot].astype(BF16)
            if c + FF_SLOTS < N_FF_CHUNKS:
                for cp in chunk_copies(c + FF_SLOTS, slot):
                    cp.start()
            gate = jnp.dot(h, wg[:, cols], preferred_element_type=F32)
            up = jnp.dot(h, wu[:, cols], preferred_element_type=F32)
            act = (jax.nn.silu(gate) * up).astype(BF16)
            acc = acc + jnp.dot(act, wd[cols, :], preferred_element_type=F32)
        out = result(x, acc)
        if final_split:
            for t in range(DEC_SEQ):
                ot_ref[:, t, :] = out[t * DEC_BATCH:(t + 1) * DEC_BATCH, :]
        else:
            o_ref[0:TM, :] = out

    @pl.when(i > 0)
    def _():
        for hf in range(FFN_TM // TM):
            rows = slice(hf * TM, (hf + 1) * TM)
            x = x_ref[rows, :]
            h = normed(x)
            gate = jnp.dot(h, wg[...], preferred_element_type=F32)
            up = jnp.dot(h, wu[...], preferred_element_type=F32)
            act = (jax.nn.silu(gate) * up).astype(BF16)
            o_ref[rows, :] = result(x, jnp.dot(act, wd[...], preferred_element_type=F32))


def _resident(shape):
    return pl.BlockSpec(shape, lambda *_: (0,) * len(shape), pipeline_mode=pl.Buffered(1))


def _ffn(x_in, gain, wg, wu, wd, layer, half, *, tail_in=None, final_gain=None):
    hbm = pl.BlockSpec(memory_space=pl.ANY)
    row_block = pl.BlockSpec((FFN_TM, D_MODEL), lambda i: (jnp.where(i == 0, FFN_PROMPT_STEPS, i - 1), 0))
    prompt_block = pl.BlockSpec((FFN_TM, D_MODEL), lambda i: (jnp.maximum(i - 1, 0), 0))
    sample_block = pl.BlockSpec((DEC_BATCH, DEC_SEQ, D_MODEL), lambda i: (0, 0, 0))
    if tail_in is None:
        args, in_specs = [x_in], [row_block]
    else:
        args, in_specs = [x_in, tail_in], [prompt_block, sample_block]
    args += [gain.reshape(1, D_MODEL), wg, wu, wd]
    in_specs += [_resident((1, D_MODEL)), hbm, hbm, hbm]
    if final_gain is None:
        out_shape = jax.ShapeDtypeStruct((N_TOK, D_MODEL), F32)
        out_specs = row_block
    else:
        args.append(final_gain.reshape(1, D_MODEL))
        in_specs.append(_resident((1, D_MODEL)))
        out_shape = (jax.ShapeDtypeStruct((PROMPT_TOK, D_MODEL), F32),
                     jax.ShapeDtypeStruct((DEC_BATCH, DEC_SEQ, D_MODEL), F32))
        out_specs = (prompt_block, sample_block)
    aliases = {0: 0} if (tail_in is None and final_gain is None) else {}
    return pl.pallas_call(
        functools.partial(_ffn_kernel, layer=layer, half=half, tail_in=tail_in is not None,
                          final_split=final_gain is not None),
        out_shape=out_shape,
        grid=(FFN_PROMPT_STEPS + 1,),
        in_specs=in_specs,
        out_specs=out_specs,
        scratch_shapes=[
            pltpu.VMEM((D_MODEL, D_FF), BF16),
            pltpu.VMEM((D_MODEL, D_FF), BF16),
            pltpu.VMEM((D_FF, D_MODEL), BF16),
            pltpu.VMEM((2, FF_SLOTS, D_MODEL, FF_CHUNK), F32),
            pltpu.VMEM((FF_SLOTS, FF_CHUNK, D_MODEL), F32),
            pltpu.SemaphoreType.DMA((3, FF_SLOTS)),
        ],
        input_output_aliases=aliases,
        compiler_params=pltpu.CompilerParams(
            dimension_semantics=("arbitrary",), vmem_limit_bytes=VMEM_LIMIT_BYTES),
        name="macaron_half",
    )(*args)


def _even_prompt_kernel(x_ref, g_ref, win_ref, wout_ref, wpool_ref, pscale_ref, bucket_ref, rb_ref, sink_ref,
                        o_ref, klast_ref, vlast_ref, plast_ref,
                        kbuf, vbuf, zb, s2b, s4b, s8b, bias_ref, cat_ref):
    b = pl.program_id(0)
    s = pl.program_id(1)
    last = pl.num_programs(1) - 1

    @pl.when((b == 0) & (s == 0))
    def _():
        bucket = bucket_ref[...]
        for h in range(N_HEADS_A):
            bias_ref[h] = _bias_from_buckets(bucket, rb_ref, h)

    @pl.when(s == 0)
    def _():
        for buf in (kbuf, vbuf):
            buf[0:WINDOW, :] = jnp.zeros((WINDOW, KV_WIDTH), BF16)
        zb[0:POOL_CTX, :] = jnp.zeros((POOL_CTX, POOL_WIDTH), F32)

    @pl.when(s > 0)
    def _():
        for buf in (kbuf, vbuf):
            buf[0:WINDOW, :] = buf[EVEN_TM:EVEN_TM + WINDOW, :]
        zb[0:POOL_CTX, :] = zb[EVEN_TM:EVEN_TM + POOL_CTX, :]

    x = x_ref[...]
    h = _rms(x, g_ref[...]).astype(BF16)
    proj = jnp.dot(h, win_ref[...], preferred_element_type=F32)
    q = proj[:, 0:ATTN_WIDTH] * Q_SCALE
    k = proj[:, ATTN_WIDTH:ATTN_WIDTH + KV_WIDTH]
    v = proj[:, ATTN_WIDTH + KV_WIDTH:ATTN_WIDTH + 2 * KV_WIDTH]
    up = proj[:, ATTN_WIDTH + 2 * KV_WIDTH:IN_EVEN]

    qlane = lax.broadcasted_iota(jnp.int32, (EVEN_TM, ATTN_WIDTH), 1) % LANES
    q_half = (jnp.where(qlane < HEAD_DIM, q, 0.0).astype(BF16),
              jnp.where(qlane >= HEAD_DIM, q, 0.0).astype(BF16))
    kbuf[WINDOW:, :] = k.astype(BF16)
    vbuf[WINDOW:, :] = v.astype(BF16)

    qrow = lax.broadcasted_iota(jnp.int32, (WINDOW, WINDOW), 0)
    kcol = lax.broadcasted_iota(jnp.int32, (WINDOW, WINDOW), 1)
    prev = kcol > qrow
    olane = lax.broadcasted_iota(jnp.int32, (WINDOW, LANES), 1)
    for i in range(QBLK_PER_TILE):
        rows = slice(i * WINDOW, (i + 1) * WINDOW)
        keys = slice(i * WINDOW, (i + 2) * WINDOW)
        qs = jnp.concatenate(
            [q_half[hd // GQA_GROUP][rows, (hd % GQA_GROUP) * LANES:(hd % GQA_GROUP + 1) * LANES]
             for hd in range(N_HEADS_A)], axis=0)
        sc_all = lax.dot_general(qs, kbuf[keys, :], (((1,), (1,)), ((), ())), preferred_element_type=F32)
        ps, denoms = [], []
        for hd in range(N_HEADS_A):
            sc = sc_all[hd * WINDOW:(hd + 1) * WINDOW, :]
            sc = jnp.where(prev, sc[:, 0:WINDOW], sc[:, WINDOW:]) + bias_ref[hd]
            if i == 0:
                sc = jnp.where(prev & (qrow >= jnp.where(s > 0, WINDOW, 0)), NEG, sc)
            p, denom = _sink_softmax(sc, sink_ref[hd] * LOG2E)
            ps.append(jnp.concatenate([jnp.where(prev, p, 0.0), jnp.where(prev, 0.0, p)], axis=1).astype(BF16))
            denoms.append(denom)
        o_all = jnp.dot(jnp.concatenate(ps, axis=0), vbuf[keys, :], preferred_element_type=F32)
        out = [o_all[hd * WINDOW:(hd + 1) * WINDOW, :] / denoms[hd] for hd in range(N_HEADS_A)]
        for j in range(GQA_GROUP):
            cat_ref[rows, j * LANES:(j + 1) * LANES] = jnp.where(
                olane < HEAD_DIM, out[j], out[j + GQA_GROUP]).astype(BF16)

    zb[POOL_CTX:, :] = up
    n = EVEN_TM + POOL_CTX
    s2b[8:, :] = zb[8:n, :] + zb[7:n - 1, :]
    s4b[16:, :] = s2b[16:n, POOL_GROUP:] + s2b[14:n - 2, POOL_GROUP:]
    s8b[24:, :] = s4b[24:n, POOL_GROUP:] + s4b[20:n - 4, POOL_GROUP:]
    s16 = s8b[32:n, POOL_GROUP:] + s8b[24:n - 8, POOL_GROUP:]
    sums = (s2b[POOL_CTX:, 0:POOL_GROUP], s4b[POOL_CTX:, 0:POOL_GROUP], s8b[POOL_CTX:, 0:POOL_GROUP], s16)
    pos1 = lax.broadcasted_iota(jnp.int32, (POOL_HEAD, POOL_GROUP), 0) + s * EVEN_TM + 1
    for g, w in enumerate(POOL_WINDOWS):
        lanes = slice(g * POOL_GROUP, (g + 1) * POOL_GROUP)
        cnt = jnp.minimum(pos1, w).astype(F32)
        mean = jnp.concatenate([sums[g][0:POOL_HEAD] / cnt, sums[g][POOL_HEAD:] * (1.0 / w)], axis=0)
        d = (mean - up[:, lanes]).astype(BF16)
        y = jnp.dot(d, wpool_ref[g], preferred_element_type=F32) * pscale_ref[:, lanes]
        cat_ref[:, ATTN_WIDTH + g * POOL_GROUP:ATTN_WIDTH + (g + 1) * POOL_GROUP] = y.astype(BF16)

    o_ref[...] = x + jnp.dot(cat_ref[...], wout_ref[...], preferred_element_type=F32)

    @pl.when(s == last)
    def _():
        klast_ref[0] = k[EVEN_TM - WINDOW:, :]
        vlast_ref[0] = v[EVEN_TM - WINDOW:, :]
        plast_ref[0] = up[EVEN_TM - 2 * SUBLANES:, :]


def _even_prompt(x, gain, w_in, w_out, w_pool, pool_scale, rel_bias, sinks):
    bucket = jnp.asarray(_prompt_bucket_table())
    smem = pl.BlockSpec(memory_space=pltpu.SMEM)
    return pl.pallas_call(
        _even_prompt_kernel,
        out_shape=(
            jax.ShapeDtypeStruct((N_TOK, D_MODEL), F32),
            jax.ShapeDtypeStruct((BATCH, WINDOW, KV_WIDTH), F32),
            jax.ShapeDtypeStruct((BATCH, WINDOW, KV_WIDTH), F32),
            jax.ShapeDtypeStruct((BATCH, 2 * SUBLANES, POOL_WIDTH), F32),
        ),
        grid=(BATCH, TILES_PER_SEQ),
        in_specs=[
            pl.BlockSpec((EVEN_TM, D_MODEL), lambda b, s: (b * TILES_PER_SEQ + s, 0)),
            _resident((1, D_MODEL)),
            _resident((D_MODEL, IN_EVEN)),
            _resident((ATTN_WIDTH + POOL_WIDTH, D_MODEL)),
            _resident((N_POOL_GROUPS, POOL_GROUP, POOL_GROUP)),
            _resident((1, POOL_WIDTH)),
            _resident((WINDOW, WINDOW)),
            smem,
            smem,
        ],
        out_specs=(
            pl.BlockSpec((EVEN_TM, D_MODEL), lambda b, s: (b * TILES_PER_SEQ + s, 0)),
            pl.BlockSpec((1, WINDOW, KV_WIDTH), lambda b, s: (b, 0, 0)),
            pl.BlockSpec((1, WINDOW, KV_WIDTH), lambda b, s: (b, 0, 0)),
            pl.BlockSpec((1, 2 * SUBLANES, POOL_WIDTH), lambda b, s: (b, 0, 0)),
        ),
        scratch_shapes=[
            pltpu.VMEM((WINDOW + EVEN_TM, KV_WIDTH), BF16),
            pltpu.VMEM((WINDOW + EVEN_TM, KV_WIDTH), BF16),
            pltpu.VMEM((POOL_CTX + EVEN_TM, POOL_WIDTH), F32),
            pltpu.VMEM((POOL_CTX + EVEN_TM, POOL_WIDTH), F32),
            pltpu.VMEM((POOL_CTX + EVEN_TM, POOL_WIDTH - POOL_GROUP), F32),
            pltpu.VMEM((POOL_CTX + EVEN_TM, POOL_WIDTH - 2 * POOL_GROUP), F32),
            pltpu.VMEM((N_HEADS_A, WINDOW, WINDOW), F32),
            pltpu.VMEM((EVEN_TM, ATTN_WIDTH + POOL_WIDTH), BF16),
        ],
        input_output_aliases={0: 0},
        compiler_params=pltpu.CompilerParams(
            dimension_semantics=("arbitrary", "arbitrary"), vmem_limit_bytes=VMEM_LIMIT_BYTES),
        name="even_mixer_prompt",
    )(x, gain.reshape(1, D_MODEL), w_in, w_out, w_pool, pool_scale.reshape(1, POOL_WIDTH), bucket,
      rel_bias, sinks)


def _even_sample_kernel(x_ref, g_ref, win_ref, wout_ref, wpool_ref, pscale_ref, bucket_ref, rb_ref, sink_ref,
                        sk_ref, sv_ref, spool_ref,
                        o_ref, nk_ref, nv_ref, npool_ref,
                        knew_ref, vnew_ref, upnew_ref, qsel, bias_ref, attn_f32, cat_ref):
    g = pl.program_id(0)
    last = pl.num_programs(0) - 1

    @pl.when(g == 0)
    def _():
        bucket = bucket_ref[...]
        for h in range(N_HEADS_A):
            bias_ref[h] = _bias_from_buckets(bucket, rb_ref, h)
        x = x_ref[...]
        h = _rms(x, g_ref[...]).astype(BF16)
        proj = jnp.dot(h, win_ref[...], preferred_element_type=F32)
        q = proj[:, 0:ATTN_WIDTH] * Q_SCALE
        zeros = jnp.zeros(((NEW_SLOTS - DEC_SEQ) * DEC_BATCH, KV_WIDTH), F32)
        knew_ref[0:SAMPLE_TOK, :] = proj[:, ATTN_WIDTH:ATTN_WIDTH + KV_WIDTH]
        vnew_ref[0:SAMPLE_TOK, :] = proj[:, ATTN_WIDTH + KV_WIDTH:ATTN_WIDTH + 2 * KV_WIDTH]
        knew_ref[SAMPLE_TOK:, :] = zeros
        vnew_ref[SAMPLE_TOK:, :] = zeros
        upnew_ref[...] = proj[:, ATTN_WIDTH + 2 * KV_WIDTH:IN_EVEN]
        qlane = lax.broadcasted_iota(jnp.int32, (SAMPLE_TOK, LANES), 1)
        for hd in range(N_HEADS_A):
            j, kvh = hd % GQA_GROUP, hd // GQA_GROUP
            keep = (qlane < HEAD_DIM) if kvh == 0 else (qlane >= HEAD_DIM)
            qsel[hd] = jnp.where(keep, q[:, j * LANES:(j + 1) * LANES], 0.0)

    lax.fori_loop(0, GROUPS_PER_STEP, functools.partial(
        _even_sample_group, g, bucket_ref, sink_ref, sk_ref, sv_ref, nk_ref, nv_ref,
        knew_ref, vnew_ref, qsel, bias_ref, attn_f32), 0)

    @pl.when(g == last)
    def _():
        cat_ref[:, 0:ATTN_WIDTH] = attn_f32[...].astype(BF16)
        for r in range(POOL_BUF):
            if r < POOL_BUF - DEC_SEQ:
                npool_ref[r] = spool_ref[r + DEC_SEQ]
            else:
                t = r - (POOL_BUF - DEC_SEQ)
                npool_ref[r] = upnew_ref[t * DEC_BATCH:(t + 1) * DEC_BATCH, :]
        for t in range(DEC_SEQ):
            rows = slice(t * DEC_BATCH, (t + 1) * DEC_BATCH)
            for gi, w in enumerate(POOL_WINDOWS):
                lanes = slice(gi * POOL_GROUP, (gi + 1) * POOL_GROUP)
                acc = upnew_ref[rows, lanes]
                for back in range(1, w):
                    tt = t - back
                    if tt >= 0:
                        acc = acc + upnew_ref[tt * DEC_BATCH:(tt + 1) * DEC_BATCH, lanes]
                    else:
                        acc = acc + spool_ref[POOL_BUF + tt, :, lanes]
                cnt = float(min(PAST_LEN + t + 1, w))
                d = (acc / cnt - upnew_ref[rows, lanes]).astype(BF16)
                y = jnp.dot(d, wpool_ref[gi], preferred_element_type=F32) * pscale_ref[:, lanes]
                cat_ref[rows, ATTN_WIDTH + gi * POOL_GROUP:ATTN_WIDTH + (gi + 1) * POOL_GROUP] = y.astype(BF16)
        o_ref[...] = x_ref[...] + jnp.dot(cat_ref[...], wout_ref[...], preferred_element_type=F32)


def _even_sample_group(g, bucket_ref, sink_ref, sk_ref, sv_ref, nk_ref, nv_ref,
                       knew_ref, vnew_ref, qsel, bias_ref, attn_f32, gg, carry):
    olane = lax.broadcasted_iota(jnp.int32, (SAMPLE_ROWS, LANES), 1)
    seq0 = gg * SEQ_GROUP
    row0 = pl.multiple_of((g * GROUPS_PER_STEP + gg) * SEQ_GROUP, SEQ_GROUP)
    pad = [jnp.zeros((WINDOW - SEQ_GROUP * NEW_SLOTS, KV_WIDTH), F32)]
    knew = jnp.concatenate(
        [knew_ref[pl.ds(row0 + b, NEW_SLOTS, stride=DEC_BATCH), :] for b in range(SEQ_GROUP)] + pad, axis=0)
    vnew = jnp.concatenate(
        [vnew_ref[pl.ds(row0 + b, NEW_SLOTS, stride=DEC_BATCH), :] for b in range(SEQ_GROUP)] + pad, axis=0)
    kt_all = jnp.concatenate([sk_ref[seq0 + b] for b in range(SEQ_GROUP)], axis=1).astype(BF16)
    vt_all = jnp.concatenate([sv_ref[seq0 + b] for b in range(SEQ_GROUP)], axis=1).astype(BF16)

    qg = jnp.concatenate(
        [qsel[hd, pl.ds(t * DEC_BATCH + row0, SEQ_GROUP), :] for hd in range(N_HEADS_A) for t in range(DEC_SEQ)],
        axis=0).astype(BF16)
    nt = (((1,), (1,)), ((), ()))
    sc = jnp.concatenate(
        [jnp.dot(qg, kt_all, preferred_element_type=F32),
         lax.dot_general(qg, knew.astype(BF16), nt, preferred_element_type=F32)], axis=1)
    valid = jnp.concatenate([bucket_ref[...]] * N_HEADS_A, axis=0) >= 0
    bias = bias_ref[...].reshape(N_HEADS_A * SAMPLE_ROWS, SAMPLE_KEYS)
    sink = jnp.concatenate(
        [jnp.full((SAMPLE_ROWS, 1), sink_ref[hd] * LOG2E, F32) for hd in range(N_HEADS_A)], axis=0)
    p, denom = _sink_softmax(jnp.where(valid, sc + bias, NEG), sink)
    p = p.astype(BF16)
    o_all = (lax.dot_general(p[:, 0:STATE_KEYS], vt_all, nt, preferred_element_type=F32)
             + jnp.dot(p[:, STATE_KEYS:], vnew.astype(BF16), preferred_element_type=F32)) / denom
    for j in range(GQA_GROUP):
        blk = jnp.where(olane < HEAD_DIM, o_all[j * SAMPLE_ROWS:(j + 1) * SAMPLE_ROWS, :],
                        o_all[(j + GQA_GROUP) * SAMPLE_ROWS:(j + GQA_GROUP + 1) * SAMPLE_ROWS, :])
        for t in range(DEC_SEQ):
            attn_f32[pl.ds(t * DEC_BATCH + row0, SEQ_GROUP), j * LANES:(j + 1) * LANES] = (
                blk[t * SEQ_GROUP:(t + 1) * SEQ_GROUP, :])

    lane = lax.broadcasted_iota(jnp.int32, (KV_WIDTH, WINDOW), 1)
    for new, old_ref, out_ref in ((knew, sk_ref, nk_ref), (vnew, sv_ref, nv_ref)):
        new_t = new.T
        for b in range(SEQ_GROUP):
            tail = pltpu.roll(new_t, (WINDOW - DEC_SEQ - b * NEW_SLOTS) % WINDOW, 1)
            out_ref[seq0 + b] = jnp.where(
                lane >= WINDOW - DEC_SEQ, tail, pltpu.roll(old_ref[seq0 + b], WINDOW - DEC_SEQ, 1))
    return carry


def _even_sample(x, gain, w_in, w_out, w_pool, pool_scale, rel_bias, sinks, state_k, state_v, state_pool_t):
    bucket = jnp.asarray(_sample_bucket_table())
    smem = pl.BlockSpec(memory_space=pltpu.SMEM)
    sample_block = PROMPT_TOK // SAMPLE_TOK
    return pl.pallas_call(
        _even_sample_kernel,
        out_shape=(
            jax.ShapeDtypeStruct((N_TOK, D_MODEL), F32),
            jax.ShapeDtypeStruct((DEC_BATCH, KV_WIDTH, WINDOW), F32),
            jax.ShapeDtypeStruct((DEC_BATCH, KV_WIDTH, WINDOW), F32),
            jax.ShapeDtypeStruct((POOL_BUF, DEC_BATCH, POOL_WIDTH), F32),
        ),
        grid=(N_SEQ_GROUPS // GROUPS_PER_STEP,),
        in_specs=[
            pl.BlockSpec((SAMPLE_TOK, D_MODEL), lambda g: (sample_block, 0)),
            _resident((1, D_MODEL)),
            _resident((D_MODEL, IN_EVEN)),
            _resident((ATTN_WIDTH + POOL_WIDTH, D_MODEL)),
            _resident((N_POOL_GROUPS, POOL_GROUP, POOL_GROUP)),
            _resident((1, POOL_WIDTH)),
            _resident((SAMPLE_ROWS, SAMPLE_KEYS)),
            smem,
            smem,
            pl.BlockSpec((GROUPS_PER_STEP * SEQ_GROUP, KV_WIDTH, WINDOW), lambda g: (g, 0, 0)),
            pl.BlockSpec((GROUPS_PER_STEP * SEQ_GROUP, KV_WIDTH, WINDOW), lambda g: (g, 0, 0)),
            _resident((POOL_BUF, DEC_BATCH, POOL_WIDTH)),
        ],
        out_specs=(
            pl.BlockSpec((SAMPLE_TOK, D_MODEL), lambda g: (sample_block, 0)),
            pl.BlockSpec((GROUPS_PER_STEP * SEQ_GROUP, KV_WIDTH, WINDOW), lambda g: (g, 0, 0)),
            pl.BlockSpec((GROUPS_PER_STEP * SEQ_GROUP, KV_WIDTH, WINDOW), lambda g: (g, 0, 0)),
            pl.BlockSpec((POOL_BUF, DEC_BATCH, POOL_WIDTH), lambda g: (0, 0, 0)),
        ),
        scratch_shapes=[
            pltpu.VMEM((NEW_SLOTS * DEC_BATCH, KV_WIDTH), F32),
            pltpu.VMEM((NEW_SLOTS * DEC_BATCH, KV_WIDTH), F32),
            pltpu.VMEM((SAMPLE_TOK, POOL_WIDTH), F32),
            pltpu.VMEM((N_HEADS_A, SAMPLE_TOK, LANES), F32),
            pltpu.VMEM((N_HEADS_A, SAMPLE_ROWS, SAMPLE_KEYS), F32),
            pltpu.VMEM((SAMPLE_TOK, ATTN_WIDTH), F32),
            pltpu.VMEM((SAMPLE_TOK, ATTN_WIDTH + POOL_WIDTH), BF16),
        ],
        input_output_aliases={0: 0},
        compiler_params=pltpu.CompilerParams(
            dimension_semantics=("arbitrary",), vmem_limit_bytes=VMEM_LIMIT_BYTES),
        name="even_mixer_sample",
    )(x, gain.reshape(1, D_MODEL), w_in, w_out, w_pool, pool_scale.reshape(1, POOL_WIDTH), bucket,
      rel_bias, sinks, state_k, state_v, state_pool_t)


def _sgu_front(x, g_ref, win_ref, gv_ref):
    h = _rms(x, g_ref[...]).astype(BF16)
    uv = jax.nn.gelu(jnp.dot(h, win_ref[...], preferred_element_type=F32))
    return uv[:, 0:SGU_WIDTH], _rms(uv[:, SGU_WIDTH:], gv_ref[...])


def _odd_prompt_kernel(x_ref, g_ref, win_ref, gv_ref, ws_ref, bs_ref, wout_ref, o_ref, gated_ref):
    x = x_ref[...]
    u, v = _sgu_front(x, g_ref, win_ref, gv_ref)
    vb = v.astype(BF16)
    r = lax.broadcasted_iota(jnp.int32, (CHUNK, CHUNK), 0)
    c = lax.broadcasted_iota(jnp.int32, (CHUNK, CHUNK), 1)
    for gi in range(SGU_GROUPS):
        w = jnp.where(r >= c, ws_ref[gi], 0.0).astype(BF16)
        bias = jnp.concatenate([bs_ref[gi]] * (SGU_GROUP_W // LANES), axis=1)
        lanes = slice(gi * SGU_GROUP_W, (gi + 1) * SGU_GROUP_W)
        for ci in range(ODD_TM // CHUNK):
            rows = slice(ci * CHUNK, (ci + 1) * CHUNK)
            mixed = jnp.dot(w, vb[rows, lanes], preferred_element_type=F32) + bias
            gated_ref[rows, lanes] = (u[rows, lanes] * mixed).astype(BF16)
    o_ref[...] = x + jnp.dot(gated_ref[...], wout_ref[...], preferred_element_type=F32)


def _odd_prompt(x, gain, w_in, g_v, w_s, b_s, w_out):
    bias = jnp.broadcast_to(b_s[:, :, None], (SGU_GROUPS, CHUNK, LANES))
    return pl.pallas_call(
        _odd_prompt_kernel,
        out_shape=jax.ShapeDtypeStruct((N_TOK, D_MODEL), F32),
        grid=(PROMPT_TOK // ODD_TM,),
        in_specs=[
            pl.BlockSpec((ODD_TM, D_MODEL), lambda i: (i, 0)),
            _resident((1, D_MODEL)),
            _resident((D_MODEL, 2 * SGU_WIDTH)),
            _resident((1, SGU_WIDTH)),
            _resident((SGU_GROUPS, CHUNK, CHUNK)),
            _resident((SGU_GROUPS, CHUNK, LANES)),
            _resident((SGU_WIDTH, D_MODEL)),
        ],
        out_specs=pl.BlockSpec((ODD_TM, D_MODEL), lambda i: (i, 0)),
        scratch_shapes=[pltpu.VMEM((ODD_TM, SGU_WIDTH), BF16)],
        input_output_aliases={0: 0},
        compiler_params=pltpu.CompilerParams(
            dimension_semantics=("arbitrary",), vmem_limit_bytes=VMEM_LIMIT_BYTES),
        name="odd_mixer_prompt",
    )(x, gain.reshape(1, D_MODEL), w_in, g_v.reshape(1, SGU_WIDTH), w_s, bias, w_out)


def _odd_sample_kernel(x_ref, g_ref, win_ref, gv_ref, coef_ref, bias_ref, wout_ref, o_ref, v_ref):
    x = x_ref[...]
    u, v = _sgu_front(x, g_ref, win_ref, gv_ref)
    for t in range(DEC_SEQ):
        v_ref[:, t, :] = v[t * DEC_BATCH:(t + 1) * DEC_BATCH, :]
    gated = []
    for t in range(DEC_SEQ):
        mixed = bias_ref[t:t + 1, :]
        for s in range(t + 1):
            mixed = mixed + coef_ref[t, s:s + 1, :] * v[s * DEC_BATCH:(s + 1) * DEC_BATCH, :]
        gated.append((u[t * DEC_BATCH:(t + 1) * DEC_BATCH, :] * mixed).astype(BF16))
    o_ref[...] = x + jnp.dot(jnp.concatenate(gated, axis=0), wout_ref[...], preferred_element_type=F32)


def _odd_sample(x, gain, w_in, g_v, w_s, b_s, w_out):
    coef = jnp.repeat(jnp.transpose(w_s[:, :DEC_SEQ, :DEC_SEQ], (1, 2, 0)), SGU_GROUP_W, axis=-1)
    bias = jnp.repeat(b_s[:, :DEC_SEQ].T, SGU_GROUP_W, axis=-1)
    sample_block = PROMPT_TOK // SAMPLE_TOK
    return pl.pallas_call(
        _odd_sample_kernel,
        out_shape=(
            jax.ShapeDtypeStruct((N_TOK, D_MODEL), F32),
            jax.ShapeDtypeStruct((DEC_BATCH, DEC_SEQ, SGU_WIDTH), F32),
        ),
        grid=(1,),
        in_specs=[
            pl.BlockSpec((SAMPLE_TOK, D_MODEL), lambda i: (sample_block, 0)),
            _resident((1, D_MODEL)),
            _resident((D_MODEL, 2 * SGU_WIDTH)),
            _resident((1, SGU_WIDTH)),
            _resident((DEC_SEQ, DEC_SEQ, SGU_WIDTH)),
            _resident((DEC_SEQ, SGU_WIDTH)),
            _resident((SGU_WIDTH, D_MODEL)),
        ],
        out_specs=(
            pl.BlockSpec((SAMPLE_TOK, D_MODEL), lambda i: (sample_block, 0)),
            pl.BlockSpec((DEC_BATCH, DEC_SEQ, SGU_WIDTH), lambda i: (0, 0, 0)),
        ),
        input_output_aliases={0: 0},
        compiler_params=pltpu.CompilerParams(
            dimension_semantics=("arbitrary",), vmem_limit_bytes=VMEM_LIMIT_BYTES),
        name="odd_mixer_sample",
    )(x, gain.reshape(1, D_MODEL), w_in, g_v.reshape(1, SGU_WIDTH), coef, bias, w_out)


def _window_to_lanes(w):
    return jnp.transpose(w, (0, 2, 3, 1)).reshape(DEC_BATCH, KV_WIDTH, WINDOW)


def _window_from_lanes(w):
    return jnp.transpose(w.reshape(DEC_BATCH, N_KV_A, HEAD_DIM, WINDOW), (0, 3, 1, 2))


def kernel(x_prompt, x_sample, state_win_k, state_win_v, state_pool, rel_bias, norm_gains, final_gain,
           ffn_gate, ffn_up, ffn_down, w_in_even, w_out_even, attn_sinks, w_pool, pool_scale,
           w_in_odd, sgu_norm, w_spatial, b_spatial, w_out_odd):
    xp = x_prompt.reshape(PROMPT_TOK, D_MODEL)

    kp_l, vp_l, pp_l, ks_l, vs_l, ps_l, sv_l = [], [], [], [], [], [], []
    x = None
    y_prompt = y_sample = None
    for l in range(DEPTH):
        fa = (norm_gains[l, 0], ffn_gate, ffn_up, ffn_down, l, 0)
        fb = (norm_gains[l, 2], ffn_gate, ffn_up, ffn_down, l, 1)
        if l == 0:
            x = _ffn(xp, *fa, tail_in=x_sample)
        else:
            x = _ffn(x, *fa)
        g1 = norm_gains[l, 1]
        if l % 2 == 0:
            e = l // 2
            wq = w_in_even[e][:, :ATTN_WIDTH].reshape(D_MODEL, N_KV_A, GQA_GROUP, HEAD_DIM)
            wq = jnp.swapaxes(wq, 1, 2).reshape(D_MODEL, ATTN_WIDTH)
            wo = w_out_even[e][:ATTN_WIDTH].reshape(N_KV_A, GQA_GROUP, HEAD_DIM, D_MODEL)
            wo = jnp.swapaxes(wo, 0, 1).reshape(ATTN_WIDTH, D_MODEL)
            w_in = jnp.concatenate([wq, w_in_even[e][:, ATTN_WIDTH:]], axis=1).astype(BF16)
            w_out = jnp.concatenate([wo, w_out_even[e][ATTN_WIDTH:]], axis=0).astype(BF16)
            wp = w_pool[e].astype(BF16)
            x, kp, vp, pp = _even_prompt(x, g1, w_in, w_out, wp, pool_scale[e], rel_bias, attn_sinks[e])
            x, nk, nv, npool = _even_sample(
                x, g1, w_in, w_out, wp, pool_scale[e], rel_bias, attn_sinks[e],
                _window_to_lanes(state_win_k[e]), _window_to_lanes(state_win_v[e]),
                jnp.swapaxes(state_pool[e], 0, 1))
            kp_l.append(kp.reshape(BATCH, WINDOW, N_KV_A, HEAD_DIM))
            vp_l.append(vp.reshape(BATCH, WINDOW, N_KV_A, HEAD_DIM))
            pp_l.append(pp[:, 2 * SUBLANES - POOL_BUF:, :])
            ks_l.append(_window_from_lanes(nk))
            vs_l.append(_window_from_lanes(nv))
            ps_l.append(jnp.swapaxes(npool, 0, 1))
        else:
            o = l // 2
            w_in, w_out = w_in_odd[o].astype(BF16), w_out_odd[o].astype(BF16)
            x = _odd_prompt(x, g1, w_in, sgu_norm[o], w_spatial[o], b_spatial[o], w_out)
            x, sv_new = _odd_sample(x, g1, w_in, sgu_norm[o], w_spatial[o], b_spatial[o], w_out)
            sv_l.append(sv_new)
        if l < DEPTH - 1:
            x = _ffn(x, *fb)
        else:
            y_prompt, y_sample = _ffn(x, *fb, final_gain=final_gain)

    return (y_prompt.reshape(BATCH, SEQ, D_MODEL), y_sample,
            jnp.stack(kp_l), jnp.stack(vp_l), jnp.stack(pp_l),
            jnp.stack(ks_l), jnp.stack(vs_l), jnp.stack(ps_l), jnp.stack(sv_l))
```

```python
import functools
import math

import numpy as np
import jax
import jax.numpy as jnp
from jax import lax
from jax.experimental import pallas as pl
from jax.experimental.pallas import tpu as pltpu

F32 = jnp.float32
BF16 = jnp.bfloat16

D_MODEL = 1024
BATCH = 8
SEQ = 2048
DEPTH = 2
DEC_BATCH = 128
DEC_SEQ = 4
PAST_LEN = 16384
N_HEADS_A = 8
N_KV_A = 2
HEAD_DIM = 64
GQA_GROUP = N_HEADS_A // N_KV_A
WINDOW = 128
ATTN_WIDTH = N_HEADS_A * HEAD_DIM
KV_WIDTH = N_KV_A * HEAD_DIM
N_BUCKETS = 32
MAX_DISTANCE = 128
POOL_WINDOWS = (2, 4, 8, 16)
N_POOL_GROUPS = len(POOL_WINDOWS)
POOL_GROUP = 128
POOL_WIDTH = N_POOL_GROUPS * POOL_GROUP
POOL_BUF = max(POOL_WINDOWS) - 1
IN_EVEN = ATTN_WIDTH + 2 * KV_WIDTH + POOL_WIDTH
CHUNK = 128
SGU_WIDTH = 1024
SGU_GROUPS = 4
SGU_GROUP_W = SGU_WIDTH // SGU_GROUPS
D_FF = 2816
EPS = 1e-6
NEG = -1e30

LANES = 128
SUBLANES = 8
VMEM_LIMIT_BYTES = 56 * 1024 * 1024

PROMPT_TOK = BATCH * SEQ
SAMPLE_TOK = DEC_BATCH * DEC_SEQ
N_TOK = PROMPT_TOK + SAMPLE_TOK

TM = SAMPLE_TOK
assert PROMPT_TOK % TM == 0
PROMPT_BLOCKS = PROMPT_TOK // TM
EVEN_TM = 1024
assert SEQ % EVEN_TM == 0 and EVEN_TM % WINDOW == 0
TILES_PER_SEQ = SEQ // EVEN_TM
QBLK_PER_TILE = EVEN_TM // WINDOW
FFN_TM = 2 * TM
assert PROMPT_TOK % FFN_TM == 0
FFN_PROMPT_STEPS = PROMPT_TOK // FFN_TM
ODD_TM = 1024
assert PROMPT_TOK % ODD_TM == 0 and ODD_TM % CHUNK == 0
FF_CHUNK = 256
N_FF_CHUNKS = D_FF // FF_CHUNK
FF_SLOTS = 3
POOL_HEAD = 2 * SUBLANES
assert all(w & (w - 1) == 0 and w <= POOL_HEAD for w in POOL_WINDOWS)
POOL_CTX = 32
SEQ_GROUP = 8
N_SEQ_GROUPS = DEC_BATCH // SEQ_GROUP
GROUPS_PER_STEP = 4
assert N_SEQ_GROUPS % GROUPS_PER_STEP == 0
STATE_KEYS = SEQ_GROUP * WINDOW
NEW_SLOTS = SUBLANES
assert DEC_SEQ <= NEW_SLOTS and SEQ_GROUP * NEW_SLOTS <= WINDOW
SAMPLE_KEYS = STATE_KEYS + WINDOW
SAMPLE_ROWS = DEC_SEQ * SEQ_GROUP


assert N_KV_A == 2 and KV_WIDTH == LANES
HEAD_ORDER = [h for j in range(GQA_GROUP) for h in (j, j + GQA_GROUP)]


def _t5_bucket_np(dist):
    n = np.maximum(dist, 0)
    max_exact = N_BUCKETS // 2
    nf = np.maximum(n, 1).astype(np.float32)
    large = max_exact + (np.log(nf / np.float32(max_exact)) / np.float32(math.log(MAX_DISTANCE / max_exact))
                         * np.float32(N_BUCKETS - max_exact)).astype(np.int32)
    large = np.minimum(large, N_BUCKETS - 1)
    return np.where(n < max_exact, n, large).astype(np.int32)


def _prompt_bucket_table():
    qi = np.arange(WINDOW)[:, None]
    c = np.arange(WINDOW)[None, :]
    dist = np.where(c > qi, qi + WINDOW - c, qi - c)
    return _t5_bucket_np(dist)


def _sample_bucket_table():
    t = (np.arange(SAMPLE_ROWS) // SEQ_GROUP)[:, None]
    b = (np.arange(SAMPLE_ROWS) % SEQ_GROUP)[:, None]
    col = np.arange(SAMPLE_KEYS)[None, :]
    is_state = col < STATE_KEYS
    slot = (col - STATE_KEYS) % NEW_SLOTS
    is_new = (col >= STATE_KEYS) & (col < STATE_KEYS + SEQ_GROUP * NEW_SLOTS) & (slot < DEC_SEQ)
    kb = np.where(is_state, col // WINDOW, (col - STATE_KEYS) // NEW_SLOTS)
    kpos = np.where(is_state, col % WINDOW, WINDOW + slot)
    dist = (WINDOW + t) - kpos
    valid = (is_state | is_new) & (kb == b) & (dist >= 0) & (dist < WINDOW)
    return np.where(valid, _t5_bucket_np(dist), -1).astype(np.int32)


def _rms(x, g):
    y = x * lax.rsqrt(jnp.mean(x * x, axis=-1, keepdims=True) + EPS)
    return y * g


LOG2E = math.log2(math.e)
Q_SCALE = HEAD_DIM ** -0.5 * LOG2E


def _bias_from_buckets(bucket, rb_ref, h):
    t = jnp.zeros(bucket.shape, F32)
    for bkt in range(N_BUCKETS):
        t = jnp.where(bucket == bkt, rb_ref[bkt, h] * LOG2E, t)
    return t


def _sink_softmax(s, sink):
    m = jnp.maximum(jnp.max(s, axis=-1, keepdims=True), sink)
    p = jnp.exp2(s - m)
    return p, jnp.sum(p, axis=-1, keepdims=True) + jnp.exp2(sink - m)


def _softmax_pv(s, valid, bias, sink, v):
    p, denom = _sink_softmax(jnp.where(valid, s + bias, NEG), sink)
    return jnp.dot(p.astype(BF16), v, preferred_element_type=F32) / denom


def _ffn_kernel(*refs, tail_in, final_split):
    refs = list(refs)
    x_ref = refs.pop(0)
    xt_ref = refs.pop(0) if tail_in else None
    g_ref, which_ref, wg_hbm, wu_hbm, wd_hbm = refs[:5]
    del refs[:5]
    layer, half = which_ref[0], which_ref[1]
    fg_ref = refs.pop(0) if final_split else None
    o_ref = refs.pop(0)
    ot_ref = refs.pop(0) if final_split else None
    wg, wu, wd, stage_in, stage_dn, sem = refs
    i = pl.program_id(0)

    def normed(x):
        return _rms(x, g_ref[...]).astype(BF16)

    def result(x, y):
        out = x + 0.5 * y
        return _rms(out, fg_ref[...]) if final_split else out

    def chunk_copies(c, slot):
        cols = pl.ds(c * FF_CHUNK, FF_CHUNK)
        return (
            pltpu.make_async_copy(wg_hbm.at[layer, half, :, cols], stage_in.at[0, slot], sem.at[0, slot]),
            pltpu.make_async_copy(wu_hbm.at[layer, half, :, cols], stage_in.at[1, slot], sem.at[1, slot]),
            pltpu.make_async_copy(wd_hbm.at[layer, half, cols, :], stage_dn.at[slot], sem.at[2, slot]),
        )

    @pl.when(i == 0)
    def _():
        for c in range(min(FF_SLOTS, N_FF_CHUNKS)):
            for cp in chunk_copies(c, c % FF_SLOTS):
                cp.start()
        if tail_in:
            x = jnp.concatenate([xt_ref[:, t, :] for t in range(DEC_SEQ)], axis=0)
        else:
            x = x_ref[0:TM, :]
        h = normed(x)
        acc = jnp.zeros((TM, D_MODEL), F32)
        for c in range(N_FF_CHUNKS):
            slot = c % FF_SLOTS
            cols = slice(c * FF_CHUNK, (c + 1) * FF_CHUNK)
            for cp in chunk_copies(c, slot):
                cp.wait()
            wg[:, cols] = stage_in[0, slot].astype(BF16)
            wu[:, cols] = stage_in[1, slot].astype(BF16)
            wd[cols, :] = stage_dn[slot].astype(BF16)
            if c + FF_SLOTS < N_FF_CHUNKS:
                for cp in chunk_copies(c + FF_SLOTS, slot):
                    cp.start()
            gate = jnp.dot(h, wg[:, cols], preferred_element_type=F32)
            up = jnp.dot(h, wu[:, cols], preferred_element_type=F32)
            act = (jax.nn.silu(gate) * up).astype(BF16)
            acc = acc + jnp.dot(act, wd[cols, :], preferred_element_type=F32)
        out = result(x, acc)
        if final_split:
            for t in range(DEC_SEQ):
                ot_ref[:, t, :] = out[t * DEC_BATCH:(t + 1) * DEC_BATCH, :]
        else:
            o_ref[0:TM, :] = out

    @pl.when(i > 0)
    def _():
        for hf in range(FFN_TM // TM):
            rows = slice(hf * TM, (hf + 1) * TM)
            x = x_ref[rows, :]
            h = normed(x)
            gate = jnp.dot(h, wg[...], preferred_element_type=F32)
            up = jnp.dot(h, wu[...], preferred_element_type=F32)
            act = (jax.nn.silu(gate) * up).astype(BF16)
            o_ref[rows, :] = result(x, jnp.dot(act, wd[...], preferred_element_type=F32))


def _resident(shape):
    return pl.BlockSpec(shape, lambda *_: (0,) * len(shape), pipeline_mode=pl.Buffered(1))


def _ffn(x_in, gain, wg, wu, wd, layer, half, *, tail_in=None, final_gain=None):
    hbm = pl.BlockSpec(memory_space=pl.ANY)
    row_block = pl.BlockSpec((FFN_TM, D_MODEL), lambda i: (jnp.where(i == 0, FFN_PROMPT_STEPS, i - 1), 0))
    prompt_block = pl.BlockSpec((FFN_TM, D_MODEL), lambda i: (jnp.maximum(i - 1, 0), 0))
    sample_block = pl.BlockSpec((DEC_BATCH, DEC_SEQ, D_MODEL), lambda i: (0, 0, 0))
    if tail_in is None:
        args, in_specs = [x_in], [row_block]
    else:
        args, in_specs = [x_in, tail_in], [prompt_block, sample_block]
    args += [gain.reshape(1, D_MODEL), jnp.array([layer, half], jnp.int32), wg, wu, wd]
    in_specs += [_resident((1, D_MODEL)), pl.BlockSpec(memory_space=pltpu.SMEM), hbm, hbm, hbm]
    if final_gain is None:
        out_shape = jax.ShapeDtypeStruct((N_TOK, D_MODEL), F32)
        out_specs = row_block
    else:
        args.append(final_gain.reshape(1, D_MODEL))
        in_specs.append(_resident((1, D_MODEL)))
        out_shape = (jax.ShapeDtypeStruct((PROMPT_TOK, D_MODEL), F32),
                     jax.ShapeDtypeStruct((DEC_BATCH, DEC_SEQ, D_MODEL), F32))
        out_specs = (prompt_block, sample_block)
    aliases = {0: 0} if (tail_in is None and final_gain is None) else {}
    return pl.pallas_call(
        functools.partial(_ffn_kernel, tail_in=tail_in is not None,
                          final_split=final_gain is not None),
        out_shape=out_shape,
        grid=(FFN_PROMPT_STEPS + 1,),
        in_specs=in_specs,
        out_specs=out_specs,
        scratch_shapes=[
            pltpu.VMEM((D_MODEL, D_FF), BF16),
            pltpu.VMEM((D_MODEL, D_FF), BF16),
            pltpu.VMEM((D_FF, D_MODEL), BF16),
            pltpu.VMEM((2, FF_SLOTS, D_MODEL, FF_CHUNK), F32),
            pltpu.VMEM((FF_SLOTS, FF_CHUNK, D_MODEL), F32),
            pltpu.SemaphoreType.DMA((3, FF_SLOTS)),
        ],
        input_output_aliases=aliases,
        compiler_params=pltpu.CompilerParams(
            dimension_semantics=("arbitrary",), vmem_limit_bytes=VMEM_LIMIT_BYTES),
        name="macaron_half",
    )(*args)


def _even_prompt_kernel(x_ref, g_ref, win_ref, wout_ref, wpool_ref, pscale_ref, bucket_ref, rb_ref, sink_ref,
                        o_ref, klast_ref, vlast_ref, plast_ref,
                        kbuf, vbuf, zb, s2b, s4b, s8b, bias_ref, cat_ref):
    b = pl.program_id(0)
    s = pl.program_id(1)
    last = pl.num_programs(1) - 1

    @pl.when((b == 0) & (s == 0))
    def _():
        bucket = bucket_ref[...]
        for h in range(N_HEADS_A):
            bias_ref[h] = _bias_from_buckets(bucket, rb_ref, h)

    @pl.when(s == 0)
    def _():
        for buf in (kbuf, vbuf):
            buf[0:WINDOW, :] = jnp.zeros((WINDOW, KV_WIDTH), BF16)
        zb[0:POOL_CTX, :] = jnp.zeros((POOL_CTX, POOL_WIDTH), F32)

    @pl.when(s > 0)
    def _():
        for buf in (kbuf, vbuf):
            buf[0:WINDOW, :] = buf[EVEN_TM:EVEN_TM + WINDOW, :]
        zb[0:POOL_CTX, :] = zb[EVEN_TM:EVEN_TM + POOL_CTX, :]

    x = x_ref[...]
    h = _rms(x, g_ref[...]).astype(BF16)
    proj = jnp.dot(h, win_ref[...], preferred_element_type=F32)
    q = proj[:, 0:ATTN_WIDTH] * Q_SCALE
    k = proj[:, ATTN_WIDTH:ATTN_WIDTH + KV_WIDTH]
    v = proj[:, ATTN_WIDTH + KV_WIDTH:ATTN_WIDTH + 2 * KV_WIDTH]
    up = proj[:, ATTN_WIDTH + 2 * KV_WIDTH:IN_EVEN]

    qlane = lax.broadcasted_iota(jnp.int32, (EVEN_TM, ATTN_WIDTH), 1) % LANES
    q_half = (jnp.where(qlane < HEAD_DIM, q, 0.0).astype(BF16),
              jnp.where(qlane >= HEAD_DIM, q, 0.0).astype(BF16))
    kbuf[WINDOW:, :] = k.astype(BF16)
    vbuf[WINDOW:, :] = v.astype(BF16)

    qrow = lax.broadcasted_iota(jnp.int32, (WINDOW, WINDOW), 0)
    kcol = lax.broadcasted_iota(jnp.int32, (WINDOW, WINDOW), 1)
    prev = kcol > qrow
    olane = lax.broadcasted_iota(jnp.int32, (WINDOW, LANES), 1)
    for i in range(QBLK_PER_TILE):
        rows = slice(i * WINDOW, (i + 1) * WINDOW)
        keys = slice(i * WINDOW, (i + 2) * WINDOW)
        qs = jnp.concatenate(
            [q_half[hd // GQA_GROUP][rows, (hd % GQA_GROUP) * LANES:(hd % GQA_GROUP + 1) * LANES]
             for hd in range(N_HEADS_A)], axis=0)
        sc_all = lax.dot_general(qs, kbuf[keys, :], (((1,), (1,)), ((), ())), preferred_element_type=F32)
        ps, denoms = [], []
        for hd in range(N_HEADS_A):
            sc = sc_all[hd * WINDOW:(hd + 1) * WINDOW, :]
            sc = jnp.where(prev, sc[:, 0:WINDOW], sc[:, WINDOW:]) + bias_ref[hd]
            if i == 0:
                sc = jnp.where(prev & (qrow >= jnp.where(s > 0, WINDOW, 0)), NEG, sc)
            p, denom = _sink_softmax(sc, sink_ref[hd] * LOG2E)
            ps.append(jnp.concatenate([jnp.where(prev, p, 0.0), jnp.where(prev, 0.0, p)], axis=1).astype(BF16))
            denoms.append(denom)
        o_all = jnp.dot(jnp.concatenate(ps, axis=0), vbuf[keys, :], preferred_element_type=F32)
        out = [o_all[hd * WINDOW:(hd + 1) * WINDOW, :] / denoms[hd] for hd in range(N_HEADS_A)]
        for j in range(GQA_GROUP):
            cat_ref[rows, j * LANES:(j + 1) * LANES] = jnp.where(
                olane < HEAD_DIM, out[j], out[j + GQA_GROUP]).astype(BF16)

    zb[POOL_CTX:, :] = up
    n = EVEN_TM + POOL_CTX
    s2b[8:, :] = zb[8:n, :] + zb[7:n - 1, :]
    s4b[16:, :] = s2b[16:n, POOL_GROUP:] + s2b[14:n - 2, POOL_GROUP:]
    s8b[24:, :] = s4b[24:n, POOL_GROUP:] + s4b[20:n - 4, POOL_GROUP:]
    s16 = s8b[32:n, POOL_GROUP:] + s8b[24:n - 8, POOL_GROUP:]
    sums = (s2b[POOL_CTX:, 0:POOL_GROUP], s4b[POOL_CTX:, 0:POOL_GROUP], s8b[POOL_CTX:, 0:POOL_GROUP], s16)
    pos1 = lax.broadcasted_iota(jnp.int32, (POOL_HEAD, POOL_GROUP), 0) + s * EVEN_TM + 1
    for g, w in enumerate(POOL_WINDOWS):
        lanes = slice(g * POOL_GROUP, (g + 1) * POOL_GROUP)
        cnt = jnp.minimum(pos1, w).astype(F32)
        mean = jnp.concatenate([sums[g][0:POOL_HEAD] / cnt, sums[g][POOL_HEAD:] * (1.0 / w)], axis=0)
        d = (mean - up[:, lanes]).astype(BF16)
        y = jnp.dot(d, wpool_ref[g], preferred_element_type=F32) * pscale_ref[:, lanes]
        cat_ref[:, ATTN_WIDTH + g * POOL_GROUP:ATTN_WIDTH + (g + 1) * POOL_GROUP] = y.astype(BF16)

    o_ref[...] = x + jnp.dot(cat_ref[...], wout_ref[...], preferred_element_type=F32)

    @pl.when(s == last)
    def _():
        klast_ref[0] = k[EVEN_TM - WINDOW:, :]
        vlast_ref[0] = v[EVEN_TM - WINDOW:, :]
        plast_ref[0] = up[EVEN_TM - 2 * SUBLANES:, :]


def _even_prompt(x, gain, w_in, w_out, w_pool, pool_scale, rel_bias, sinks):
    bucket = jnp.asarray(_prompt_bucket_table())
    smem = pl.BlockSpec(memory_space=pltpu.SMEM)
    return pl.pallas_call(
        _even_prompt_kernel,
        out_shape=(
            jax.ShapeDtypeStruct((N_TOK, D_MODEL), F32),
            jax.ShapeDtypeStruct((BATCH, WINDOW, KV_WIDTH), F32),
            jax.ShapeDtypeStruct((BATCH, WINDOW, KV_WIDTH), F32),
            jax.ShapeDtypeStruct((BATCH, 2 * SUBLANES, POOL_WIDTH), F32),
        ),
        grid=(BATCH, TILES_PER_SEQ),
        in_specs=[
            pl.BlockSpec((EVEN_TM, D_MODEL), lambda b, s: (b * TILES_PER_SEQ + s, 0)),
            _resident((1, D_MODEL)),
            _resident((D_MODEL, IN_EVEN)),
            _resident((ATTN_WIDTH + POOL_WIDTH, D_MODEL)),
            _resident((N_POOL_GROUPS, POOL_GROUP, POOL_GROUP)),
            _resident((1, POOL_WIDTH)),
            _resident((WINDOW, WINDOW)),
            smem,
            smem,
        ],
        out_specs=(
            pl.BlockSpec((EVEN_TM, D_MODEL), lambda b, s: (b * TILES_PER_SEQ + s, 0)),
            pl.BlockSpec((1, WINDOW, KV_WIDTH), lambda b, s: (b, 0, 0)),
            pl.BlockSpec((1, WINDOW, KV_WIDTH), lambda b, s: (b, 0, 0)),
            pl.BlockSpec((1, 2 * SUBLANES, POOL_WIDTH), lambda b, s: (b, 0, 0)),
        ),
        scratch_shapes=[
            pltpu.VMEM((WINDOW + EVEN_TM, KV_WIDTH), BF16),
            pltpu.VMEM((WINDOW + EVEN_TM, KV_WIDTH), BF16),
            pltpu.VMEM((POOL_CTX + EVEN_TM, POOL_WIDTH), F32),
            pltpu.VMEM((POOL_CTX + EVEN_TM, POOL_WIDTH), F32),
            pltpu.VMEM((POOL_CTX + EVEN_TM, POOL_WIDTH - POOL_GROUP), F32),
            pltpu.VMEM((POOL_CTX + EVEN_TM, POOL_WIDTH - 2 * POOL_GROUP), F32),
            pltpu.VMEM((N_HEADS_A, WINDOW, WINDOW), F32),
            pltpu.VMEM((EVEN_TM, ATTN_WIDTH + POOL_WIDTH), BF16),
        ],
        input_output_aliases={0: 0},
        compiler_params=pltpu.CompilerParams(
            dimension_semantics=("arbitrary", "arbitrary"), vmem_limit_bytes=VMEM_LIMIT_BYTES),
        name="even_mixer_prompt",
    )(x, gain.reshape(1, D_MODEL), w_in, w_out, w_pool, pool_scale.reshape(1, POOL_WIDTH), bucket,
      rel_bias, sinks)


def _even_sample_kernel(x_ref, g_ref, win_ref, wout_ref, wpool_ref, pscale_ref, bucket_ref, rb_ref, sink_ref,
                        sk_ref, sv_ref, spool_ref,
                        o_ref, nk_ref, nv_ref, npool_ref,
                        knew_ref, vnew_ref, upnew_ref, qsel, bias_ref, attn_f32, cat_ref):
    g = pl.program_id(0)
    last = pl.num_programs(0) - 1

    @pl.when(g == 0)
    def _():
        bucket = bucket_ref[...]
        for h in range(N_HEADS_A):
            bias_ref[h] = _bias_from_buckets(bucket, rb_ref, h)
        x = x_ref[...]
        h = _rms(x, g_ref[...]).astype(BF16)
        proj = jnp.dot(h, win_ref[...], preferred_element_type=F32)
        q = proj[:, 0:ATTN_WIDTH] * Q_SCALE
        zeros = jnp.zeros(((NEW_SLOTS - DEC_SEQ) * DEC_BATCH, KV_WIDTH), F32)
        knew_ref[0:SAMPLE_TOK, :] = proj[:, ATTN_WIDTH:ATTN_WIDTH + KV_WIDTH]
        vnew_ref[0:SAMPLE_TOK, :] = proj[:, ATTN_WIDTH + KV_WIDTH:ATTN_WIDTH + 2 * KV_WIDTH]
        knew_ref[SAMPLE_TOK:, :] = zeros
        vnew_ref[SAMPLE_TOK:, :] = zeros
        upnew_ref[...] = proj[:, ATTN_WIDTH + 2 * KV_WIDTH:IN_EVEN]
        qlane = lax.broadcasted_iota(jnp.int32, (SAMPLE_TOK, LANES), 1)
        for hd in range(N_HEADS_A):
            j, kvh = hd % GQA_GROUP, hd // GQA_GROUP
            keep = (qlane < HEAD_DIM) if kvh == 0 else (qlane >= HEAD_DIM)
            qsel[hd] = jnp.where(keep, q[:, j * LANES:(j + 1) * LANES], 0.0)

    lax.fori_loop(0, GROUPS_PER_STEP, functools.partial(
        _even_sample_group, g, bucket_ref, sink_ref, sk_ref, sv_ref, nk_ref, nv_ref,
        knew_ref, vnew_ref, qsel, bias_ref, attn_f32), 0)

    @pl.when(g == last)
    def _():
        cat_ref[:, 0:ATTN_WIDTH] = attn_f32[...].astype(BF16)
        for r in range(POOL_BUF):
            if r < POOL_BUF - DEC_SEQ:
                npool_ref[r] = spool_ref[r + DEC_SEQ]
            else:
                t = r - (POOL_BUF - DEC_SEQ)
                npool_ref[r] = upnew_ref[t * DEC_BATCH:(t + 1) * DEC_BATCH, :]
        for t in range(DEC_SEQ):
            rows = slice(t * DEC_BATCH, (t + 1) * DEC_BATCH)
            for gi, w in enumerate(POOL_WINDOWS):
                lanes = slice(gi * POOL_GROUP, (gi + 1) * POOL_GROUP)
                acc = upnew_ref[rows, lanes]
                for back in range(1, w):
                    tt = t - back
                    if tt >= 0:
                        acc = acc + upnew_ref[tt * DEC_BATCH:(tt + 1) * DEC_BATCH, lanes]
                    else:
                        acc = acc + spool_ref[POOL_BUF + tt, :, lanes]
                cnt = float(min(PAST_LEN + t + 1, w))
                d = (acc / cnt - upnew_ref[rows, lanes]).astype(BF16)
                y = jnp.dot(d, wpool_ref[gi], preferred_element_type=F32) * pscale_ref[:, lanes]
                cat_ref[rows, ATTN_WIDTH + gi * POOL_GROUP:ATTN_WIDTH + (gi + 1) * POOL_GROUP] = y.astype(BF16)
        o_ref[...] = x_ref[...] + jnp.dot(cat_ref[...], wout_ref[...], preferred_element_type=F32)


def _even_sample_group(g, bucket_ref, sink_ref, sk_ref, sv_ref, nk_ref, nv_ref,
                       knew_ref, vnew_ref, qsel, bias_ref, attn_f32, gg, carry):
    olane = lax.broadcasted_iota(jnp.int32, (SAMPLE_ROWS, LANES), 1)
    seq0 = gg * SEQ_GROUP
    row0 = pl.multiple_of((g * GROUPS_PER_STEP + gg) * SEQ_GROUP, SEQ_GROUP)
    pad = [jnp.zeros((WINDOW - SEQ_GROUP * NEW_SLOTS, KV_WIDTH), F32)]
    knew = jnp.concatenate(
        [knew_ref[pl.ds(row0 + b, NEW_SLOTS, stride=DEC_BATCH), :] for b in range(SEQ_GROUP)] + pad, axis=0)
    vnew = jnp.concatenate(
        [vnew_ref[pl.ds(row0 + b, NEW_SLOTS, stride=DEC_BATCH), :] for b in range(SEQ_GROUP)] + pad, axis=0)
    kt_all = jnp.concatenate([sk_ref[seq0 + b] for b in range(SEQ_GROUP)], axis=1).astype(BF16)
    vt_all = jnp.concatenate([sv_ref[seq0 + b] for b in range(SEQ_GROUP)], axis=1).astype(BF16)

    qg = jnp.concatenate(
        [qsel[hd, pl.ds(t * DEC_BATCH + row0, SEQ_GROUP), :] for hd in range(N_HEADS_A) for t in range(DEC_SEQ)],
        axis=0).astype(BF16)
    nt = (((1,), (1,)), ((), ()))
    sc = jnp.concatenate(
        [jnp.dot(qg, kt_all, preferred_element_type=F32),
         lax.dot_general(qg, knew.astype(BF16), nt, preferred_element_type=F32)], axis=1)
    valid = jnp.concatenate([bucket_ref[...]] * N_HEADS_A, axis=0) >= 0
    bias = bias_ref[...].reshape(N_HEADS_A * SAMPLE_ROWS, SAMPLE_KEYS)
    sink = jnp.concatenate(
        [jnp.full((SAMPLE_ROWS, 1), sink_ref[hd] * LOG2E, F32) for hd in range(N_HEADS_A)], axis=0)
    p, denom = _sink_softmax(jnp.where(valid, sc + bias, NEG), sink)
    p = p.astype(BF16)
    o_all = (lax.dot_general(p[:, 0:STATE_KEYS], vt_all, nt, preferred_element_type=F32)
             + jnp.dot(p[:, STATE_KEYS:], vnew.astype(BF16), preferred_element_type=F32)) / denom
    for j in range(GQA_GROUP):
        blk = jnp.where(olane < HEAD_DIM, o_all[j * SAMPLE_ROWS:(j + 1) * SAMPLE_ROWS, :],
                        o_all[(j + GQA_GROUP) * SAMPLE_ROWS:(j + GQA_GROUP + 1) * SAMPLE_ROWS, :])
        for t in range(DEC_SEQ):
            attn_f32[pl.ds(t * DEC_BATCH + row0, SEQ_GROUP), j * LANES:(j + 1) * LANES] = (
                blk[t * SEQ_GROUP:(t + 1) * SEQ_GROUP, :])

    lane = lax.broadcasted_iota(jnp.int32, (KV_WIDTH, WINDOW), 1)
    for new, old_ref, out_ref in ((knew, sk_ref, nk_ref), (vnew, sv_ref, nv_ref)):
        new_t = new.T
        for b in range(SEQ_GROUP):
            tail = pltpu.roll(new_t, (WINDOW - DEC_SEQ - b * NEW_SLOTS) % WINDOW, 1)
            out_ref[seq0 + b] = jnp.where(
                lane >= WINDOW - DEC_SEQ, tail, pltpu.roll(old_ref[seq0 + b], WINDOW - DEC_SEQ, 1))
    return carry


def _even_sample(x, gain, w_in, w_out, w_pool, pool_scale, rel_bias, sinks, state_k, state_v, state_pool_t):
    bucket = jnp.asarray(_sample_bucket_table())
    smem = pl.BlockSpec(memory_space=pltpu.SMEM)
    sample_block = PROMPT_TOK // SAMPLE_TOK
    return pl.pallas_call(
        _even_sample_kernel,
        out_shape=(
            jax.ShapeDtypeStruct((N_TOK, D_MODEL), F32),
            jax.ShapeDtypeStruct((DEC_BATCH, KV_WIDTH, WINDOW), F32),
            jax.ShapeDtypeStruct((DEC_BATCH, KV_WIDTH, WINDOW), F32),
            jax.ShapeDtypeStruct((POOL_BUF, DEC_BATCH, POOL_WIDTH), F32),
        ),
        grid=(N_SEQ_GROUPS // GROUPS_PER_STEP,),
        in_specs=[
            pl.BlockSpec((SAMPLE_TOK, D_MODEL), lambda g: (sample_block, 0)),
            _resident((1, D_MODEL)),
            _resident((D_MODEL, IN_EVEN)),
            _resident((ATTN_WIDTH + POOL_WIDTH, D_MODEL)),
            _resident((N_POOL_GROUPS, POOL_GROUP, POOL_GROUP)),
            _resident((1, POOL_WIDTH)),
            _resident((SAMPLE_ROWS, SAMPLE_KEYS)),
            smem,
            smem,
            pl.BlockSpec((GROUPS_PER_STEP * SEQ_GROUP, KV_WIDTH, WINDOW), lambda g: (g, 0, 0)),
            pl.BlockSpec((GROUPS_PER_STEP * SEQ_GROUP, KV_WIDTH, WINDOW), lambda g: (g, 0, 0)),
            _resident((POOL_BUF, DEC_BATCH, POOL_WIDTH)),
        ],
        out_specs=(
            pl.BlockSpec((SAMPLE_TOK, D_MODEL), lambda g: (sample_block, 0)),
            pl.BlockSpec((GROUPS_PER_STEP * SEQ_GROUP, KV_WIDTH, WINDOW), lambda g: (g, 0, 0)),
            pl.BlockSpec((GROUPS_PER_STEP * SEQ_GROUP, KV_WIDTH, WINDOW), lambda g: (g, 0, 0)),
            pl.BlockSpec((POOL_BUF, DEC_BATCH, POOL_WIDTH), lambda g: (0, 0, 0)),
        ),
        scratch_shapes=[
            pltpu.VMEM((NEW_SLOTS * DEC_BATCH, KV_WIDTH), F32),
            pltpu.VMEM((NEW_SLOTS * DEC_BATCH, KV_WIDTH), F32),
            pltpu.VMEM((SAMPLE_TOK, POOL_WIDTH), F32),
            pltpu.VMEM((N_HEADS_A, SAMPLE_TOK, LANES), F32),
            pltpu.VMEM((N_HEADS_A, SAMPLE_ROWS, SAMPLE_KEYS), F32),
            pltpu.VMEM((SAMPLE_TOK, ATTN_WIDTH), F32),
            pltpu.VMEM((SAMPLE_TOK, ATTN_WIDTH + POOL_WIDTH), BF16),
        ],
        input_output_aliases={0: 0},
        compiler_params=pltpu.CompilerParams(
            dimension_semantics=("arbitrary",), vmem_limit_bytes=VMEM_LIMIT_BYTES),
        name="even_mixer_sample",
    )(x, gain.reshape(1, D_MODEL), w_in, w_out, w_pool, pool_scale.reshape(1, POOL_WIDTH), bucket,
      rel_bias, sinks, state_k, state_v, state_pool_t)


def _sgu_front(x, g_ref, win_ref, gv_ref):
    h = _rms(x, g_ref[...]).astype(BF16)
    uv = jax.nn.gelu(jnp.dot(h, win_ref[...], preferred_element_type=F32))
    return uv[:, 0:SGU_WIDTH], _rms(uv[:, SGU_WIDTH:], gv_ref[...])


def _odd_prompt_kernel(x_ref, g_ref, win_ref, gv_ref, ws_ref, bs_ref, wout_ref, o_ref, gated_ref):
    x = x_ref[...]
    u, v = _sgu_front(x, g_ref, win_ref, gv_ref)
    vb = v.astype(BF16)
    r = lax.broadcasted_iota(jnp.int32, (CHUNK, CHUNK), 0)
    c = lax.broadcasted_iota(jnp.int32, (CHUNK, CHUNK), 1)
    for gi in range(SGU_GROUPS):
        w = jnp.where(r >= c, ws_ref[gi], 0.0).astype(BF16)
        bias = jnp.concatenate([bs_ref[gi]] * (SGU_GROUP_W // LANES), axis=1)
        lanes = slice(gi * SGU_GROUP_W, (gi + 1) * SGU_GROUP_W)
        for ci in range(ODD_TM // CHUNK):
            rows = slice(ci * CHUNK, (ci + 1) * CHUNK)
            mixed = jnp.dot(w, vb[rows, lanes], preferred_element_type=F32) + bias
            gated_ref[rows, lanes] = (u[rows, lanes] * mixed).astype(BF16)
    o_ref[...] = x + jnp.dot(gated_ref[...], wout_ref[...], preferred_element_type=F32)


def _odd_prompt(x, gain, w_in, g_v, w_s, b_s, w_out):
    bias = jnp.broadcast_to(b_s[:, :, None], (SGU_GROUPS, CHUNK, LANES))
    return pl.pallas_call(
        _odd_prompt_kernel,
        out_shape=jax.ShapeDtypeStruct((N_TOK, D_MODEL), F32),
        grid=(PROMPT_TOK // ODD_TM,),
        in_specs=[
            pl.BlockSpec((ODD_TM, D_MODEL), lambda i: (i, 0)),
            _resident((1, D_MODEL)),
            _resident((D_MODEL, 2 * SGU_WIDTH)),
            _resident((1, SGU_WIDTH)),
            _resident((SGU_GROUPS, CHUNK, CHUNK)),
            _resident((SGU_GROUPS, CHUNK, LANES)),
            _resident((SGU_WIDTH, D_MODEL)),
        ],
        out_specs=pl.BlockSpec((ODD_TM, D_MODEL), lambda i: (i, 0)),
        scratch_shapes=[pltpu.VMEM((ODD_TM, SGU_WIDTH), BF16)],
        input_output_aliases={0: 0},
        compiler_params=pltpu.CompilerParams(
            dimension_semantics=("arbitrary",), vmem_limit_bytes=VMEM_LIMIT_BYTES),
        name="odd_mixer_prompt",
    )(x, gain.reshape(1, D_MODEL), w_in, g_v.reshape(1, SGU_WIDTH), w_s, bias, w_out)


def _odd_sample_kernel(x_ref, g_ref, win_ref, gv_ref, coef_ref, bias_ref, wout_ref, o_ref, v_ref):
    x = x_ref[...]
    u, v = _sgu_front(x, g_ref, win_ref, gv_ref)
    for t in range(DEC_SEQ):
        v_ref[:, t, :] = v[t * DEC_BATCH:(t + 1) * DEC_BATCH, :]
    gated = []
    for t in range(DEC_SEQ):
        mixed = bias_ref[t:t + 1, :]
        for s in range(t + 1):
            mixed = mixed + coef_ref[t, s:s + 1, :] * v[s * DEC_BATCH:(s + 1) * DEC_BATCH, :]
        gated.append((u[t * DEC_BATCH:(t + 1) * DEC_BATCH, :] * mixed).astype(BF16))
    o_ref[...] = x + jnp.dot(jnp.concatenate(gated, axis=0), wout_ref[...], preferred_element_type=F32)


def _odd_sample(x, gain, w_in, g_v, w_s, b_s, w_out):
    coef = jnp.repeat(jnp.transpose(w_s[:, :DEC_SEQ, :DEC_SEQ], (1, 2, 0)), SGU_GROUP_W, axis=-1)
    bias = jnp.repeat(b_s[:, :DEC_SEQ].T, SGU_GROUP_W, axis=-1)
    sample_block = PROMPT_TOK // SAMPLE_TOK
    return pl.pallas_call(
        _odd_sample_kernel,
        out_shape=(
            jax.ShapeDtypeStruct((N_TOK, D_MODEL), F32),
            jax.ShapeDtypeStruct((DEC_BATCH, DEC_SEQ, SGU_WIDTH), F32),
        ),
        grid=(1,),
        in_specs=[
            pl.BlockSpec((SAMPLE_TOK, D_MODEL), lambda i: (sample_block, 0)),
            _resident((1, D_MODEL)),
            _resident((D_MODEL, 2 * SGU_WIDTH)),
            _resident((1, SGU_WIDTH)),
            _resident((DEC_SEQ, DEC_SEQ, SGU_WIDTH)),
            _resident((DEC_SEQ, SGU_WIDTH)),
            _resident((SGU_WIDTH, D_MODEL)),
        ],
        out_specs=(
            pl.BlockSpec((SAMPLE_TOK, D_MODEL), lambda i: (sample_block, 0)),
            pl.BlockSpec((DEC_BATCH, DEC_SEQ, SGU_WIDTH), lambda i: (0, 0, 0)),
        ),
        input_output_aliases={0: 0},
        compiler_params=pltpu.CompilerParams(
            dimension_semantics=("arbitrary",), vmem_limit_bytes=VMEM_LIMIT_BYTES),
        name="odd_mixer_sample",
    )(x, gain.reshape(1, D_MODEL), w_in, g_v.reshape(1, SGU_WIDTH), coef, bias, w_out)


def _window_to_lanes(w):
    return jnp.transpose(w, (0, 2, 3, 1)).reshape(DEC_BATCH, KV_WIDTH, WINDOW)


def _window_from_lanes(w):
    return jnp.transpose(w.reshape(DEC_BATCH, N_KV_A, HEAD_DIM, WINDOW), (0, 3, 1, 2))


def kernel(x_prompt, x_sample, state_win_k, state_win_v, state_pool, rel_bias, norm_gains, final_gain,
           ffn_gate, ffn_up, ffn_down, w_in_even, w_out_even, attn_sinks, w_pool, pool_scale,
           w_in_odd, sgu_norm, w_spatial, b_spatial, w_out_odd):
    xp = x_prompt.reshape(PROMPT_TOK, D_MODEL)

    kp_l, vp_l, pp_l, ks_l, vs_l, ps_l, sv_l = [], [], [], [], [], [], []
    x = None
    y_prompt = y_sample = None
    for l in range(DEPTH):
        fa = (norm_gains[l, 0], ffn_gate, ffn_up, ffn_down, l, 0)
        fb = (norm_gains[l, 2], ffn_gate, ffn_up, ffn_down, l, 1)
        if l == 0:
            x = _ffn(xp, *fa, tail_in=x_sample)
        else:
            x = _ffn(x, *fa)
        g1 = norm_gains[l, 1]
        if l % 2 == 0:
            e = l // 2
            wq = w_in_even[e][:, :ATTN_WIDTH].reshape(D_MODEL, N_KV_A, GQA_GROUP, HEAD_DIM)
            wq = jnp.swapaxes(wq, 1, 2).reshape(D_MODEL, ATTN_WIDTH)
            wo = w_out_even[e][:ATTN_WIDTH].reshape(N_KV_A, GQA_GROUP, HEAD_DIM, D_MODEL)
            wo = jnp.swapaxes(wo, 0, 1).reshape(ATTN_WIDTH, D_MODEL)
            w_in = jnp.concatenate([wq, w_in_even[e][:, ATTN_WIDTH:]], axis=1).astype(BF16)
            w_out = jnp.concatenate([wo, w_out_even[e][ATTN_WIDTH:]], axis=0).astype(BF16)
            wp = w_pool[e].astype(BF16)
            x, kp, vp, pp = _even_prompt(x, g1, w_in, w_out, wp, pool_scale[e], rel_bias, attn_sinks[e])
            x, nk, nv, npool = _even_sample(
                x, g1, w_in, w_out, wp, pool_scale[e], rel_bias, attn_sinks[e],
                _window_to_lanes(state_win_k[e]), _window_to_lanes(state_win_v[e]),
                jnp.swapaxes(state_pool[e], 0, 1))
            kp_l.append(kp.reshape(BATCH, WINDOW, N_KV_A, HEAD_DIM))
            vp_l.append(vp.reshape(BATCH, WINDOW, N_KV_A, HEAD_DIM))
            pp_l.append(pp[:, 2 * SUBLANES - POOL_BUF:, :])
            ks_l.append(_window_from_lanes(nk))
            vs_l.append(_window_from_lanes(nv))
            ps_l.append(jnp.swapaxes(npool, 0, 1))
        else:
            o = l // 2
            w_in, w_out = w_in_odd[o].astype(BF16), w_out_odd[o].astype(BF16)
            x = _odd_prompt(x, g1, w_in, sgu_norm[o], w_spatial[o], b_spatial[o], w_out)
            x, sv_new = _odd_sample(x, g1, w_in, sgu_norm[o], w_spatial[o], b_spatial[o], w_out)
            sv_l.append(sv_new)
        if l < DEPTH - 1:
            x = _ffn(x, *fb)
        else:
            y_prompt, y_sample = _ffn(x, *fb, final_gain=final_gain)

    return (y_prompt.reshape(BATCH, SEQ, D_MODEL), y_sample,
            jnp.stack(kp_l), jnp.stack(vp_l), jnp.stack(pp_l),
            jnp.stack(ks_l), jnp.stack(vs_l), jnp.stack(ps_l), jnp.stack(sv_l))
```

```python
import functools
import math

import numpy as np
import jax
import jax.numpy as jnp
from jax import lax
from jax.experimental import pallas as pl
from jax.experimental.pallas import tpu as pltpu

F32 = jnp.float32
BF16 = jnp.bfloat16

D_MODEL = 1024
BATCH = 8
SEQ = 2048
DEPTH = 2
DEC_BATCH = 128
DEC_SEQ = 4
PAST_LEN = 16384
N_HEADS_A = 8
N_KV_A = 2
HEAD_DIM = 64
GQA_GROUP = N_HEADS_A // N_KV_A
WINDOW = 128
ATTN_WIDTH = N_HEADS_A * HEAD_DIM
KV_WIDTH = N_KV_A * HEAD_DIM
N_BUCKETS = 32
MAX_DISTANCE = 128
POOL_WINDOWS = (2, 4, 8, 16)
N_POOL_GROUPS = len(POOL_WINDOWS)
POOL_GROUP = 128
POOL_WIDTH = N_POOL_GROUPS * POOL_GROUP
POOL_BUF = max(POOL_WINDOWS) - 1
IN_EVEN = ATTN_WIDTH + 2 * KV_WIDTH + POOL_WIDTH
CHUNK = 128
SGU_WIDTH = 1024
SGU_GROUPS = 4
SGU_GROUP_W = SGU_WIDTH // SGU_GROUPS
D_FF = 2816
EPS = 1e-6
NEG = -1e30

LANES = 128
SUBLANES = 8
VMEM_LIMIT_BYTES = 56 * 1024 * 1024

PROMPT_TOK = BATCH * SEQ
SAMPLE_TOK = DEC_BATCH * DEC_SEQ
N_TOK = PROMPT_TOK + SAMPLE_TOK

TM = SAMPLE_TOK
assert PROMPT_TOK % TM == 0
PROMPT_BLOCKS = PROMPT_TOK // TM
EVEN_TM = 1024
assert SEQ % EVEN_TM == 0 and EVEN_TM % WINDOW == 0
TILES_PER_SEQ = SEQ // EVEN_TM
QBLK_PER_TILE = EVEN_TM // WINDOW
FFN_TM = 2 * TM
assert PROMPT_TOK % FFN_TM == 0
FFN_PROMPT_STEPS = PROMPT_TOK // FFN_TM
ODD_TM = 1024
assert PROMPT_TOK % ODD_TM == 0 and ODD_TM % CHUNK == 0
FF_CHUNK = 256
N_FF_CHUNKS = D_FF // FF_CHUNK
FF_SLOTS = 3
POOL_HEAD = 2 * SUBLANES
assert all(w & (w - 1) == 0 and w <= POOL_HEAD for w in POOL_WINDOWS)
POOL_CTX = 32
SEQ_GROUP = 8
N_SEQ_GROUPS = DEC_BATCH // SEQ_GROUP
GROUPS_PER_STEP = 4
assert N_SEQ_GROUPS % GROUPS_PER_STEP == 0
STATE_KEYS = SEQ_GROUP * WINDOW
NEW_SLOTS = SUBLANES
assert DEC_SEQ <= NEW_SLOTS and SEQ_GROUP * NEW_SLOTS <= WINDOW
SAMPLE_KEYS = STATE_KEYS + WINDOW
SAMPLE_ROWS = DEC_SEQ * SEQ_GROUP


assert N_KV_A == 2 and KV_WIDTH == LANES
HEAD_ORDER = [h for j in range(GQA_GROUP) for h in (j, j + GQA_GROUP)]


def _t5_bucket_np(dist):
    n = np.maximum(dist, 0)
    max_exact = N_BUCKETS // 2
    nf = np.maximum(n, 1).astype(np.float32)
    large = max_exact + (np.log(nf / np.float32(max_exact)) / np.float32(math.log(MAX_DISTANCE / max_exact))
                         * np.float32(N_BUCKETS - max_exact)).astype(np.int32)
    large = np.minimum(large, N_BUCKETS - 1)
    return np.where(n < max_exact, n, large).astype(np.int32)


def _prompt_bucket_table():
    qi = np.arange(WINDOW)[:, None]
    c = np.arange(WINDOW)[None, :]
    dist = np.where(c > qi, qi + WINDOW - c, qi - c)
    return _t5_bucket_np(dist)


def _sample_bucket_table():
    t = (np.arange(SAMPLE_ROWS) // SEQ_GROUP)[:, None]
    b = (np.arange(SAMPLE_ROWS) % SEQ_GROUP)[:, None]
    col = np.arange(SAMPLE_KEYS)[None, :]
    is_state = col < STATE_KEYS
    slot = (col - STATE_KEYS) % NEW_SLOTS
    is_new = (col >= STATE_KEYS) & (col < STATE_KEYS + SEQ_GROUP * NEW_SLOTS) & (slot < DEC_SEQ)
    kb = np.where(is_state, col // WINDOW, (col - STATE_KEYS) // NEW_SLOTS)
    kpos = np.where(is_state, col % WINDOW, WINDOW + slot)
    dist = (WINDOW + t) - kpos
    valid = (is_state | is_new) & (kb == b) & (dist >= 0) & (dist < WINDOW)
    return np.where(valid, _t5_bucket_np(dist), -1).astype(np.int32)


def _rms(x, g):
    y = x * lax.rsqrt(jnp.mean(x * x, axis=-1, keepdims=True) + EPS)
    return y * g


LOG2E = math.log2(math.e)
Q_SCALE = HEAD_DIM ** -0.5 * LOG2E


def _bias_from_buckets(bucket, rb_ref, h):
    t = jnp.zeros(bucket.shape, F32)
    for bkt in range(N_BUCKETS):
        t = jnp.where(bucket == bkt, rb_ref[bkt, h] * LOG2E, t)
    return t


def _sink_softmax(s, sink):
    m = jnp.maximum(jnp.max(s, axis=-1, keepdims=True), sink)
    p = jnp.exp2(s - m)
    return p, jnp.sum(p, axis=-1, keepdims=True) + jnp.exp2(sink - m)


def _softmax_pv(s, valid, bias, sink, v):
    p, denom = _sink_softmax(jnp.where(valid, s + bias, NEG), sink)
    return jnp.dot(p.astype(BF16), v, preferred_element_type=F32) / denom


def _ffn_kernel(*refs, layer, half, tail_in, final_split):
    refs = list(refs)
    x_ref = refs.pop(0)
    xt_ref = refs.pop(0) if tail_in else None
    g_ref, wg_hbm, wu_hbm, wd_hbm = refs[:4]
    del refs[:4]
    fg_ref = refs.pop(0) if final_split else None
    o_ref = refs.pop(0)
    ot_ref = refs.pop(0) if final_split else None
    wg, wu, wd, stage_in, stage_dn, sem = refs
    i = pl.program_id(0)

    def normed(x):
        return _rms(x, g_ref[...]).astype(BF16)

    def result(x, y):
        out = x + 0.5 * y
        return _rms(out, fg_ref[...]) if final_split else out

    def chunk_copies(c, slot):
        cols = pl.ds(c * FF_CHUNK, FF_CHUNK)
        return (
            pltpu.make_async_copy(wg_hbm.at[layer, half, :, cols], stage_in.at[0, slot], sem.at[0, slot]),
            pltpu.make_async_copy(wu_hbm.at[layer, half, :, cols], stage_in.at[1, slot], sem.at[1, slot]),
            pltpu.make_async_copy(wd_hbm.at[layer, half, cols, :], stage_dn.at[slot], sem.at[2, slot]),
        )

    @pl.when(i == 0)
    def _():
        for c in range(min(FF_SLOTS, N_FF_CHUNKS)):
            for cp in chunk_copies(c, c % FF_SLOTS):
                cp.start()
        if tail_in:
            x = jnp.concatenate([xt_ref[:, t, :] for t in range(DEC_SEQ)], axis=0)
        else:
            x = x_ref[0:TM, :]
        h = normed(x)
        acc = jnp.zeros((TM, D_MODEL), F32)
        for c in range(N_FF_CHUNKS):
            slot = c % FF_SLOTS
            cols = slice(c * FF_CHUNK, (c + 1) * FF_CHUNK)
            for cp in chunk_copies(c, slot):
                cp.wait()
            wg[:, cols] = stage_in[0, slot].astype(BF16)
            wu[:, cols] = stage_in[1, slot].astype(BF16)
            wd[cols, :] = stage_dn[slot].astype(BF16)
            if c + FF_SLOTS < N_FF_CHUNKS:
                for cp in chunk_copies(c + FF_SLOTS, slot):
                    cp.start()
            gate = jnp.dot(h, wg[:, cols], preferred_element_type=F32)
            up = jnp.dot(h, wu[:, cols], preferred_element_type=F32)
            act = (jax.nn.silu(gate) * up).astype(BF16)
            acc = acc + jnp.dot(act, wd[cols, :], preferred_element_type=F32)
        out = result(x, acc)
        if final_split:
            for t in range(DEC_SEQ):
                ot_ref[:, t, :] = out[t * DEC_BATCH:(t + 1) * DEC_BATCH, :]
        else:
            o_ref[0:TM, :] = out

    @pl.when(i > 0)
    def _():
        for hf in range(FFN_TM // TM):
            rows = slice(hf * TM, (hf + 1) * TM)
            x = x_ref[rows, :]
            h = normed(x)
            gate = jnp.dot(h, wg[...], preferred_element_type=F32)
            up = jnp.dot(h, wu[...], preferred_element_type=F32)
            act = (jax.nn.silu(gate) * up).astype(BF16)
            o_ref[rows, :] = result(x, jnp.dot(act, wd[...], preferred_element_type=F32))


def _resident(shape):
    return pl.BlockSpec(shape, lambda *_: (0,) * len(shape), pipeline_mode=pl.Buffered(1))


def _ffn(x_in, gain, wg, wu, wd, layer, half, *, tail_in=None, final_gain=None):
    hbm = pl.BlockSpec(memory_space=pl.ANY)
    row_block = pl.BlockSpec((FFN_TM, D_MODEL), lambda i: (jnp.where(i == 0, FFN_PROMPT_STEPS, i - 1), 0))
    prompt_block = pl.BlockSpec((FFN_TM, D_MODEL), lambda i: (jnp.maximum(i - 1, 0), 0))
    sample_block = pl.BlockSpec((DEC_BATCH, DEC_SEQ, D_MODEL), lambda i: (0, 0, 0))
    if tail_in is None:
        args, in_specs = [x_in], [row_block]
    else:
        args, in_specs = [x_in, tail_in], [prompt_block, sample_block]
    args += [gain.reshape(1, D_MODEL), wg, wu, wd]
    in_specs += [_resident((1, D_MODEL)), hbm, hbm, hbm]
    if final_gain is None:
        out_shape = jax.ShapeDtypeStruct((N_TOK, D_MODEL), F32)
        out_specs = row_block
    else:
        args.append(final_gain.reshape(1, D_MODEL))
        in_specs.append(_resident((1, D_MODEL)))
        out_shape = (jax.ShapeDtypeStruct((PROMPT_TOK, D_MODEL), F32),
                     jax.ShapeDtypeStruct((DEC_BATCH, DEC_SEQ, D_MODEL), F32))
        out_specs = (prompt_block, sample_block)
    aliases = {0: 0} if (tail_in is None and final_gain is None) else {}
    return pl.pallas_call(
        functools.partial(_ffn_kernel, layer=layer, half=half, tail_in=tail_in is not None,
                          final_split=final_gain is not None),
        out_shape=out_shape,
        grid=(FFN_PROMPT_STEPS + 1,),
        in_specs=in_specs,
        out_specs=out_specs,
        scratch_shapes=[
            pltpu.VMEM((D_MODEL, D_FF), BF16),
            pltpu.VMEM((D_MODEL, D_FF), BF16),
            pltpu.VMEM((D_FF, D_MODEL), BF16),
            pltpu.VMEM((2, FF_SLOTS, D_MODEL, FF_CHUNK), F32),
            pltpu.VMEM((FF_SLOTS, FF_CHUNK, D_MODEL), F32),
            pltpu.SemaphoreType.DMA((3, FF_SLOTS)),
        ],
        input_output_aliases=aliases,
        compiler_params=pltpu.CompilerParams(
            dimension_semantics=("arbitrary",), vmem_limit_bytes=VMEM_LIMIT_BYTES),
        name="macaron_half",
    )(*args)


def _even_prompt_kernel(x_ref, g_ref, win_ref, wout_ref, wpool_ref, pscale_ref, bucket_ref, rb_ref, sink_ref,
                        o_ref, klast_ref, vlast_ref, plast_ref,
                        kbuf, vbuf, zb, s2b, s4b, s8b, bias_ref, cat_ref):
    b = pl.program_id(0)
    s = pl.program_id(1)
    last = pl.num_programs(1) - 1

    @pl.when((b == 0) & (s == 0))
    def _():
        bucket = bucket_ref[...]
        for h in range(N_HEADS_A):
            bias_ref[h] = _bias_from_buckets(bucket, rb_ref, h)

    @pl.when(s == 0)
    def _():
        for buf in (kbuf, vbuf):
            buf[0:WINDOW, :] = jnp.zeros((WINDOW, KV_WIDTH), BF16)
        zb[0:POOL_CTX, :] = jnp.zeros((POOL_CTX, POOL_WIDTH), F32)

    @pl.when(s > 0)
    def _():
        for buf in (kbuf, vbuf):
            buf[0:WINDOW, :] = buf[EVEN_TM:EVEN_TM + WINDOW, :]
        zb[0:POOL_CTX, :] = zb[EVEN_TM:EVEN_TM + POOL_CTX, :]

    x = x_ref[...]
    h = _rms(x, g_ref[...]).astype(BF16)
    proj = jnp.dot(h, win_ref[...], preferred_element_type=F32)
    q = proj[:, 0:ATTN_WIDTH] * Q_SCALE
    k = proj[:, ATTN_WIDTH:ATTN_WIDTH + KV_WIDTH]
    v = proj[:, ATTN_WIDTH + KV_WIDTH:ATTN_WIDTH + 2 * KV_WIDTH]
    up = proj[:, ATTN_WIDTH + 2 * KV_WIDTH:IN_EVEN]

    qlane = lax.broadcasted_iota(jnp.int32, (EVEN_TM, ATTN_WIDTH), 1) % LANES
    q_half = (jnp.where(qlane < HEAD_DIM, q, 0.0).astype(BF16),
              jnp.where(qlane >= HEAD_DIM, q, 0.0).astype(BF16))
    kbuf[WINDOW:, :] = k.astype(BF16)
    vbuf[WINDOW:, :] = v.astype(BF16)

    qrow = lax.broadcasted_iota(jnp.int32, (WINDOW, WINDOW), 0)
    kcol = lax.broadcasted_iota(jnp.int32, (WINDOW, WINDOW), 1)
    prev = kcol > qrow
    olane = lax.broadcasted_iota(jnp.int32, (WINDOW, LANES), 1)
    for i in range(QBLK_PER_TILE):
        rows = slice(i * WINDOW, (i + 1) * WINDOW)
        keys = slice(i * WINDOW, (i + 2) * WINDOW)
        qs = jnp.concatenate(
            [q_half[hd // GQA_GROUP][rows, (hd % GQA_GROUP) * LANES:(hd % GQA_GROUP + 1) * LANES]
             for hd in range(N_HEADS_A)], axis=0)
        sc_all = lax.dot_general(qs, kbuf[keys, :], (((1,), (1,)), ((), ())), preferred_element_type=F32)
        ps, denoms = [], []
        for hd in range(N_HEADS_A):
            sc = sc_all[hd * WINDOW:(hd + 1) * WINDOW, :]
            sc = jnp.where(prev, sc[:, 0:WINDOW], sc[:, WINDOW:]) + bias_ref[hd]
            if i == 0:
                sc = jnp.where(prev & (qrow >= jnp.where(s > 0, WINDOW, 0)), NEG, sc)
            p, denom = _sink_softmax(sc, sink_ref[hd] * LOG2E)
            ps.append(jnp.concatenate([jnp.where(prev, p, 0.0), jnp.where(prev, 0.0, p)], axis=1).astype(BF16))
            denoms.append(denom)
        o_all = jnp.dot(jnp.concatenate(ps, axis=0), vbuf[keys, :], preferred_element_type=F32)
        out = [o_all[hd * WINDOW:(hd + 1) * WINDOW, :] / denoms[hd] for hd in range(N_HEADS_A)]
        for j in range(GQA_GROUP):
            cat_ref[rows, j * LANES:(j + 1) * LANES] = jnp.where(
                olane < HEAD_DIM, out[j], out[j + GQA_GROUP]).astype(BF16)

    zb[POOL_CTX:, :] = up
    n = EVEN_TM + POOL_CTX
    s2b[8:, :] = zb[8:n, :] + zb[7:n - 1, :]
    s4b[16:, :] = s2b[16:n, POOL_GROUP:] + s2b[14:n - 2, POOL_GROUP:]
    s8b[24:, :] = s4b[24:n, POOL_GROUP:] + s4b[20:n - 4, POOL_GROUP:]
    s16 = s8b[32:n, POOL_GROUP:] + s8b[24:n - 8, POOL_GROUP:]
    sums = (s2b[POOL_CTX:, 0:POOL_GROUP], s4b[POOL_CTX:, 0:POOL_GROUP], s8b[POOL_CTX:, 0:POOL_GROUP], s16)
    pos1 = lax.broadcasted_iota(jnp.int32, (POOL_HEAD, POOL_GROUP), 0) + s * EVEN_TM + 1
    for g, w in enumerate(POOL_WINDOWS):
        lanes = slice(g * POOL_GROUP, (g + 1) * POOL_GROUP)
        cnt = jnp.minimum(pos1, w).astype(F32)
        mean = jnp.concatenate([sums[g][0:POOL_HEAD] / cnt, sums[g][POOL_HEAD:] * (1.0 / w)], axis=0)
        d = (mean - up[:, lanes]).astype(BF16)
        y = jnp.dot(d, wpool_ref[g], preferred_element_type=F32) * pscale_ref[:, lanes]
        cat_ref[:, ATTN_WIDTH + g * POOL_GROUP:ATTN_WIDTH + (g + 1) * POOL_GROUP] = y.astype(BF16)

    o_ref[...] = x + jnp.dot(cat_ref[...], wout_ref[...], preferred_element_type=F32)

    @pl.when(s == last)
    def _():
        klast_ref[0] = k[EVEN_TM - WINDOW:, :]
        vlast_ref[0] = v[EVEN_TM - WINDOW:, :]
        plast_ref[0] = up[EVEN_TM - 2 * SUBLANES:, :]


def _even_prompt(x, gain, w_in, w_out, w_pool, pool_scale, rel_bias, sinks):
    bucket = jnp.asarray(_prompt_bucket_table())
    smem = pl.BlockSpec(memory_space=pltpu.SMEM)
    return pl.pallas_call(
        _even_prompt_kernel,
        out_shape=(
            jax.ShapeDtypeStruct((N_TOK, D_MODEL), F32),
            jax.ShapeDtypeStruct((BATCH, WINDOW, KV_WIDTH), F32),
            jax.ShapeDtypeStruct((BATCH, WINDOW, KV_WIDTH), F32),
            jax.ShapeDtypeStruct((BATCH, 2 * SUBLANES, POOL_WIDTH), F32),
        ),
        grid=(BATCH, TILES_PER_SEQ),
        in_specs=[
            pl.BlockSpec((EVEN_TM, D_MODEL), lambda b, s: (b * TILES_PER_SEQ + s, 0)),
            _resident((1, D_MODEL)),
            _resident((D_MODEL, IN_EVEN)),
            _resident((ATTN_WIDTH + POOL_WIDTH, D_MODEL)),
            _resident((N_POOL_GROUPS, POOL_GROUP, POOL_GROUP)),
            _resident((1, POOL_WIDTH)),
            _resident((WINDOW, WINDOW)),
            smem,
            smem,
        ],
        out_specs=(
            pl.BlockSpec((EVEN_TM, D_MODEL), lambda b, s: (b * TILES_PER_SEQ + s, 0)),
            pl.BlockSpec((1, WINDOW, KV_WIDTH), lambda b, s: (b, 0, 0)),
            pl.BlockSpec((1, WINDOW, KV_WIDTH), lambda b, s: (b, 0, 0)),
            pl.BlockSpec((1, 2 * SUBLANES, POOL_WIDTH), lambda b, s: (b, 0, 0)),
        ),
        scratch_shapes=[
            pltpu.VMEM((WINDOW + EVEN_TM, KV_WIDTH), BF16),
            pltpu.VMEM((WINDOW + EVEN_TM, KV_WIDTH), BF16),
            pltpu.VMEM((POOL_CTX + EVEN_TM, POOL_WIDTH), F32),
            pltpu.VMEM((POOL_CTX + EVEN_TM, POOL_WIDTH), F32),
            pltpu.VMEM((POOL_CTX + EVEN_TM, POOL_WIDTH - POOL_GROUP), F32),
            pltpu.VMEM((POOL_CTX + EVEN_TM, POOL_WIDTH - 2 * POOL_GROUP), F32),
            pltpu.VMEM((N_HEADS_A, WINDOW, WINDOW), F32),
            pltpu.VMEM((EVEN_TM, ATTN_WIDTH + POOL_WIDTH), BF16),
        ],
        input_output_aliases={0: 0},
        compiler_params=pltpu.CompilerParams(
            dimension_semantics=("arbitrary", "arbitrary"), vmem_limit_bytes=VMEM_LIMIT_BYTES),
        name="even_mixer_prompt",
    )(x, gain.reshape(1, D_MODEL), w_in, w_out, w_pool, pool_scale.reshape(1, POOL_WIDTH), bucket,
      rel_bias, sinks)


def _even_sample_kernel(x_ref, g_ref, win_ref, wout_ref, wpool_ref, pscale_ref, bucket_ref, rb_ref, sink_ref,
                        sk_ref, sv_ref, spool_ref,
                        o_ref, nk_ref, nv_ref, npool_ref,
                        knew_ref, vnew_ref, upnew_ref, qsel, bias_ref, attn_f32, cat_ref):
    g = pl.program_id(0)
    last = pl.num_programs(0) - 1

    @pl.when(g == 0)
    def _():
        bucket = bucket_ref[...]
        for h in range(N_HEADS_A):
            bias_ref[h] = _bias_from_buckets(bucket, rb_ref, h)
        x = x_ref[...]
        h = _rms(x, g_ref[...]).astype(BF16)
        proj = jnp.dot(h, win_ref[...], preferred_element_type=F32)
        q = proj[:, 0:ATTN_WIDTH] * Q_SCALE
        zeros = jnp.zeros(((NEW_SLOTS - DEC_SEQ) * DEC_BATCH, KV_WIDTH), F32)
        knew_ref[0:SAMPLE_TOK, :] = proj[:, ATTN_WIDTH:ATTN_WIDTH + KV_WIDTH]
        vnew_ref[0:SAMPLE_TOK, :] = proj[:, ATTN_WIDTH + KV_WIDTH:ATTN_WIDTH + 2 * KV_WIDTH]
        knew_ref[SAMPLE_TOK:, :] = zeros
        vnew_ref[SAMPLE_TOK:, :] = zeros
        upnew_ref[...] = proj[:, ATTN_WIDTH + 2 * KV_WIDTH:IN_EVEN]
        qlane = lax.broadcasted_iota(jnp.int32, (SAMPLE_TOK, LANES), 1)
        for hd in range(N_HEADS_A):
            j, kvh = hd % GQA_GROUP, hd // GQA_GROUP
            keep = (qlane < HEAD_DIM) if kvh == 0 else (qlane >= HEAD_DIM)
            qsel[hd] = jnp.where(keep, q[:, j * LANES:(j + 1) * LANES], 0.0)

    lax.fori_loop(0, GROUPS_PER_STEP, functools.partial(
        _even_sample_group, g, bucket_ref, sink_ref, sk_ref, sv_ref, nk_ref, nv_ref,
        knew_ref, vnew_ref, qsel, bias_ref, attn_f32), 0)

    @pl.when(g == last)
    def _():
        cat_ref[:, 0:ATTN_WIDTH] = attn_f32[...].astype(BF16)
        for r in range(POOL_BUF):
            if r < POOL_BUF - DEC_SEQ:
                npool_ref[r] = spool_ref[r + DEC_SEQ]
            else:
                t = r - (POOL_BUF - DEC_SEQ)
                npool_ref[r] = upnew_ref[t * DEC_BATCH:(t + 1) * DEC_BATCH, :]
        for t in range(DEC_SEQ):
            rows = slice(t * DEC_BATCH, (t + 1) * DEC_BATCH)
            for gi, w in enumerate(POOL_WINDOWS):
                lanes = slice(gi * POOL_GROUP, (gi + 1) * POOL_GROUP)
                acc = upnew_ref[rows, lanes]
                for back in range(1, w):
                    tt = t - back
                    if tt >= 0:
                        acc = acc + upnew_ref[tt * DEC_BATCH:(tt + 1) * DEC_BATCH, lanes]
                    else:
                        acc = acc + spool_ref[POOL_BUF + tt, :, lanes]
                cnt = float(min(PAST_LEN + t + 1, w))
                d = (acc / cnt - upnew_ref[rows, lanes]).astype(BF16)
                y = jnp.dot(d, wpool_ref[gi], preferred_element_type=F32) * pscale_ref[:, lanes]
                cat_ref[rows, ATTN_WIDTH + gi * POOL_GROUP:ATTN_WIDTH + (gi + 1) * POOL_GROUP] = y.astype(BF16)
        o_ref[...] = x_ref[...] + jnp.dot(cat_ref[...], wout_ref[...], preferred_element_type=F32)


def _even_sample_group(g, bucket_ref, sink_ref, sk_ref, sv_ref, nk_ref, nv_ref,
                       knew_ref, vnew_ref, qsel, bias_ref, attn_f32, gg, carry):
    olane = lax.broadcasted_iota(jnp.int32, (SAMPLE_ROWS, LANES), 1)
    seq0 = gg * SEQ_GROUP
    row0 = pl.multiple_of((g * GROUPS_PER_STEP + gg) * SEQ_GROUP, SEQ_GROUP)
    pad = [jnp.zeros((WINDOW - SEQ_GROUP * NEW_SLOTS, KV_WIDTH), F32)]
    knew = jnp.concatenate(
        [knew_ref[pl.ds(row0 + b, NEW_SLOTS, stride=DEC_BATCH), :] for b in range(SEQ_GROUP)] + pad, axis=0)
    vnew = jnp.concatenate(
        [vnew_ref[pl.ds(row0 + b, NEW_SLOTS, stride=DEC_BATCH), :] for b in range(SEQ_GROUP)] + pad, axis=0)
    kt_all = jnp.concatenate([sk_ref[seq0 + b] for b in range(SEQ_GROUP)], axis=1).astype(BF16)
    vt_all = jnp.concatenate([sv_ref[seq0 + b] for b in range(SEQ_GROUP)], axis=1).astype(BF16)

    qg = jnp.concatenate(
        [qsel[hd, pl.ds(t * DEC_BATCH + row0, SEQ_GROUP), :] for hd in range(N_HEADS_A) for t in range(DEC_SEQ)],
        axis=0).astype(BF16)
    nt = (((1,), (1,)), ((), ()))
    sc = jnp.concatenate(
        [jnp.dot(qg, kt_all, preferred_element_type=F32),
         lax.dot_general(qg, knew.astype(BF16), nt, preferred_element_type=F32)], axis=1)
    valid = jnp.concatenate([bucket_ref[...]] * N_HEADS_A, axis=0) >= 0
    bias = bias_ref[...].reshape(N_HEADS_A * SAMPLE_ROWS, SAMPLE_KEYS)
    sink = jnp.concatenate(
        [jnp.full((SAMPLE_ROWS, 1), sink_ref[hd] * LOG2E, F32) for hd in range(N_HEADS_A)], axis=0)
    p, denom = _sink_softmax(jnp.where(valid, sc + bias, NEG), sink)
    p = p.astype(BF16)
    o_all = (lax.dot_general(p[:, 0:STATE_KEYS], vt_all, nt, preferred_element_type=F32)
             + jnp.dot(p[:, STATE_KEYS:], vnew.astype(BF16), preferred_element_type=F32)) / denom
    for j in range(GQA_GROUP):
        blk = jnp.where(olane < HEAD_DIM, o_all[j * SAMPLE_ROWS:(j + 1) * SAMPLE_ROWS, :],
                        o_all[(j + GQA_GROUP) * SAMPLE_ROWS:(j + GQA_GROUP + 1) * SAMPLE_ROWS, :])
        for t in range(DEC_SEQ):
            attn_f32[pl.ds(t * DEC_BATCH + row0, SEQ_GROUP), j * LANES:(j + 1) * LANES] = (
                blk[t * SEQ_GROUP:(t + 1) * SEQ_GROUP, :])

    lane = lax.broadcasted_iota(jnp.int32, (KV_WIDTH, WINDOW), 1)
    for new, old_ref, out_ref in ((knew, sk_ref, nk_ref), (vnew, sv_ref, nv_ref)):
        new_t = new.T
        for b in range(SEQ_GROUP):
            tail = pltpu.roll(new_t, (WINDOW - DEC_SEQ - b * NEW_SLOTS) % WINDOW, 1)
            out_ref[seq0 + b] = jnp.where(
                lane >= WINDOW - DEC_SEQ, tail, pltpu.roll(old_ref[seq0 + b], WINDOW - DEC_SEQ, 1))
    return carry


def _even_sample(x, gain, w_in, w_out, w_pool, pool_scale, rel_bias, sinks, state_k, state_v, state_pool_t):
    bucket = jnp.asarray(_sample_bucket_table())
    smem = pl.BlockSpec(memory_space=pltpu.SMEM)
    sample_block = PROMPT_TOK // SAMPLE_TOK
    return pl.pallas_call(
        _even_sample_kernel,
        out_shape=(
            jax.ShapeDtypeStruct((N_TOK, D_MODEL), F32),
            jax.ShapeDtypeStruct((DEC_BATCH, KV_WIDTH, WINDOW), F32),
            jax.ShapeDtypeStruct((DEC_BATCH, KV_WIDTH, WINDOW), F32),
            jax.ShapeDtypeStruct((POOL_BUF, DEC_BATCH, POOL_WIDTH), F32),
        ),
        grid=(N_SEQ_GROUPS // GROUPS_PER_STEP,),
        in_specs=[
            pl.BlockSpec((SAMPLE_TOK, D_MODEL), lambda g: (sample_block, 0)),
            _resident((1, D_MODEL)),
            _resident((D_MODEL, IN_EVEN)),
            _resident((ATTN_WIDTH + POOL_WIDTH, D_MODEL)),
            _resident((N_POOL_GROUPS, POOL_GROUP, POOL_GROUP)),
            _resident((1, POOL_WIDTH)),
            _resident((SAMPLE_ROWS, SAMPLE_KEYS)),
            smem,
            smem,
            pl.BlockSpec((GROUPS_PER_STEP * SEQ_GROUP, KV_WIDTH, WINDOW), lambda g: (g, 0, 0)),
            pl.BlockSpec((GROUPS_PER_STEP * SEQ_GROUP, KV_WIDTH, WINDOW), lambda g: (g, 0, 0)),
            _resident((POOL_BUF, DEC_BATCH, POOL_WIDTH)),
        ],
        out_specs=(
            pl.BlockSpec((SAMPLE_TOK, D_MODEL), lambda g: (sample_block, 0)),
            pl.BlockSpec((GROUPS_PER_STEP * SEQ_GROUP, KV_WIDTH, WINDOW), lambda g: (g, 0, 0)),
            pl.BlockSpec((GROUPS_PER_STEP * SEQ_GROUP, KV_WIDTH, WINDOW), lambda g: (g, 0, 0)),
            pl.BlockSpec((POOL_BUF, DEC_BATCH, POOL_WIDTH), lambda g: (0, 0, 0)),
        ),
        scratch_shapes=[
            pltpu.VMEM((NEW_SLOTS * DEC_BATCH, KV_WIDTH), F32),
            pltpu.VMEM((NEW_SLOTS * DEC_BATCH, KV_WIDTH), F32),
            pltpu.VMEM((SAMPLE_TOK, POOL_WIDTH), F32),
            pltpu.VMEM((N_HEADS_A, SAMPLE_TOK, LANES), F32),
            pltpu.VMEM((N_HEADS_A, SAMPLE_ROWS, SAMPLE_KEYS), F32),
            pltpu.VMEM((SAMPLE_TOK, ATTN_WIDTH), F32),
            pltpu.VMEM((SAMPLE_TOK, ATTN_WIDTH + POOL_WIDTH), BF16),
        ],
        input_output_aliases={0: 0},
        compiler_params=pltpu.CompilerParams(
            dimension_semantics=("arbitrary",), vmem_limit_bytes=VMEM_LIMIT_BYTES),
        name="even_mixer_sample",
    )(x, gain.reshape(1, D_MODEL), w_in, w_out, w_pool, pool_scale.reshape(1, POOL_WIDTH), bucket,
      rel_bias, sinks, state_k, state_v, state_pool_t)


def _sgu_front(x, g_ref, win_ref, gv_ref):
    h = _rms(x, g_ref[...]).astype(BF16)
    uv = jax.nn.gelu(jnp.dot(h, win_ref[...], preferred_element_type=F32))
    return uv[:, 0:SGU_WIDTH], _rms(uv[:, SGU_WIDTH:], gv_ref[...])


def _odd_prompt_kernel(x_ref, g_ref, win_ref, gv_ref, ws_ref, bs_ref, wout_ref, o_ref, gated_ref):
    x = x_ref[...]
    u, v = _sgu_front(x, g_ref, win_ref, gv_ref)
    vb = v.astype(BF16)
    r = lax.broadcasted_iota(jnp.int32, (CHUNK, CHUNK), 0)
    c = lax.broadcasted_iota(jnp.int32, (CHUNK, CHUNK), 1)
    for gi in range(SGU_GROUPS):
        w = jnp.where(r >= c, ws_ref[gi], 0.0).astype(BF16)
        bias = jnp.concatenate([bs_ref[gi]] * (SGU_GROUP_W // LANES), axis=1)
        lanes = slice(gi * SGU_GROUP_W, (gi + 1) * SGU_GROUP_W)
        for ci in range(ODD_TM // CHUNK):
            rows = slice(ci * CHUNK, (ci + 1) * CHUNK)
            mixed = jnp.dot(w, vb[rows, lanes], preferred_element_type=F32) + bias
            gated_ref[rows, lanes] = (u[rows, lanes] * mixed).astype(BF16)
    o_ref[...] = x + jnp.dot(gated_ref[...], wout_ref[...], preferred_element_type=F32)


def _odd_prompt(x, gain, w_in, g_v, w_s, b_s, w_out):
    bias = jnp.broadcast_to(b_s[:, :, None], (SGU_GROUPS, CHUNK, LANES))
    return pl.pallas_call(
        _odd_prompt_kernel,
        out_shape=jax.ShapeDtypeStruct((N_TOK, D_MODEL), F32),
        grid=(PROMPT_TOK // ODD_TM,),
        in_specs=[
            pl.BlockSpec((ODD_TM, D_MODEL), lambda i: (i, 0)),
            _resident((1, D_MODEL)),
            _resident((D_MODEL, 2 * SGU_WIDTH)),
            _resident((1, SGU_WIDTH)),
            _resident((SGU_GROUPS, CHUNK, CHUNK)),
            _resident((SGU_GROUPS, CHUNK, LANES)),
            _resident((SGU_WIDTH, D_MODEL)),
        ],
        out_specs=pl.BlockSpec((ODD_TM, D_MODEL), lambda i: (i, 0)),
        scratch_shapes=[pltpu.VMEM((ODD_TM, SGU_WIDTH), BF16)],
        input_output_aliases={0: 0},
        compiler_params=pltpu.CompilerParams(
            dimension_semantics=("arbitrary",), vmem_limit_bytes=VMEM_LIMIT_BYTES,
            allow_input_fusion=[False, False, True, False, False, False, True]),
        name="odd_mixer_prompt",
    )(x, gain.reshape(1, D_MODEL), w_in, g_v.reshape(1, SGU_WIDTH), w_s, bias, w_out)


def _odd_sample_kernel(x_ref, g_ref, win_ref, gv_ref, coef_ref, bias_ref, wout_ref, o_ref, v_ref):
    x = x_ref[...]
    u, v = _sgu_front(x, g_ref, win_ref, gv_ref)
    for t in range(DEC_SEQ):
        v_ref[:, t, :] = v[t * DEC_BATCH:(t + 1) * DEC_BATCH, :]
    gated = []
    for t in range(DEC_SEQ):
        mixed = bias_ref[t:t + 1, :]
        for s in range(t + 1):
            mixed = mixed + coef_ref[t, s:s + 1, :] * v[s * DEC_BATCH:(s + 1) * DEC_BATCH, :]
        gated.append((u[t * DEC_BATCH:(t + 1) * DEC_BATCH, :] * mixed).astype(BF16))
    o_ref[...] = x + jnp.dot(jnp.concatenate(gated, axis=0), wout_ref[...], preferred_element_type=F32)


def _odd_sample(x, gain, w_in, g_v, w_s, b_s, w_out):
    coef = jnp.repeat(jnp.transpose(w_s[:, :DEC_SEQ, :DEC_SEQ], (1, 2, 0)), SGU_GROUP_W, axis=-1)
    bias = jnp.repeat(b_s[:, :DEC_SEQ].T, SGU_GROUP_W, axis=-1)
    sample_block = PROMPT_TOK // SAMPLE_TOK
    return pl.pallas_call(
        _odd_sample_kernel,
        out_shape=(
            jax.ShapeDtypeStruct((N_TOK, D_MODEL), F32),
            jax.ShapeDtypeStruct((DEC_BATCH, DEC_SEQ, SGU_WIDTH), F32),
        ),
        grid=(1,),
        in_specs=[
            pl.BlockSpec((SAMPLE_TOK, D_MODEL), lambda i: (sample_block, 0)),
            _resident((1, D_MODEL)),
            _resident((D_MODEL, 2 * SGU_WIDTH)),
            _resident((1, SGU_WIDTH)),
            _resident((DEC_SEQ, DEC_SEQ, SGU_WIDTH)),
            _resident((DEC_SEQ, SGU_WIDTH)),
            _resident((SGU_WIDTH, D_MODEL)),
        ],
        out_specs=(
            pl.BlockSpec((SAMPLE_TOK, D_MODEL), lambda i: (sample_block, 0)),
            pl.BlockSpec((DEC_BATCH, DEC_SEQ, SGU_WIDTH), lambda i: (0, 0, 0)),
        ),
        input_output_aliases={0: 0},
        compiler_params=pltpu.CompilerParams(
            dimension_semantics=("arbitrary",), vmem_limit_bytes=VMEM_LIMIT_BYTES,
            allow_input_fusion=[False, False, True, False, False, False, True]),
        name="odd_mixer_sample",
    )(x, gain.reshape(1, D_MODEL), w_in, g_v.reshape(1, SGU_WIDTH), coef, bias, w_out)


def _window_to_lanes(w):
    return jnp.transpose(w, (0, 2, 3, 1)).reshape(DEC_BATCH, KV_WIDTH, WINDOW)


def _window_from_lanes(w):
    return jnp.transpose(w.reshape(DEC_BATCH, N_KV_A, HEAD_DIM, WINDOW), (0, 3, 1, 2))


def kernel(x_prompt, x_sample, state_win_k, state_win_v, state_pool, rel_bias, norm_gains, final_gain,
           ffn_gate, ffn_up, ffn_down, w_in_even, w_out_even, attn_sinks, w_pool, pool_scale,
           w_in_odd, sgu_norm, w_spatial, b_spatial, w_out_odd):
    xp = x_prompt.reshape(PROMPT_TOK, D_MODEL)

    kp_l, vp_l, pp_l, ks_l, vs_l, ps_l, sv_l = [], [], [], [], [], [], []
    x = None
    y_prompt = y_sample = None
    for l in range(DEPTH):
        fa = (norm_gains[l, 0], ffn_gate, ffn_up, ffn_down, l, 0)
        fb = (norm_gains[l, 2], ffn_gate, ffn_up, ffn_down, l, 1)
        if l == 0:
            x = _ffn(xp, *fa, tail_in=x_sample)
        else:
            x = _ffn(x, *fa)
        g1 = norm_gains[l, 1]
        if l % 2 == 0:
            e = l // 2
            wq = w_in_even[e][:, :ATTN_WIDTH].reshape(D_MODEL, N_KV_A, GQA_GROUP, HEAD_DIM)
            wq = jnp.swapaxes(wq, 1, 2).reshape(D_MODEL, ATTN_WIDTH)
            wo = w_out_even[e][:ATTN_WIDTH].reshape(N_KV_A, GQA_GROUP, HEAD_DIM, D_MODEL)
            wo = jnp.swapaxes(wo, 0, 1).reshape(ATTN_WIDTH, D_MODEL)
            w_in = jnp.concatenate([wq, w_in_even[e][:, ATTN_WIDTH:]], axis=1).astype(BF16)
            w_out = jnp.concatenate([wo, w_out_even[e][ATTN_WIDTH:]], axis=0).astype(BF16)
            wp = w_pool[e].astype(BF16)
            x, kp, vp, pp = _even_prompt(x, g1, w_in, w_out, wp, pool_scale[e], rel_bias, attn_sinks[e])
            x, nk, nv, npool = _even_sample(
                x, g1, w_in, w_out, wp, pool_scale[e], rel_bias, attn_sinks[e],
                _window_to_lanes(state_win_k[e]), _window_to_lanes(state_win_v[e]),
                jnp.swapaxes(state_pool[e], 0, 1))
            kp_l.append(kp.reshape(BATCH, WINDOW, N_KV_A, HEAD_DIM))
            vp_l.append(vp.reshape(BATCH, WINDOW, N_KV_A, HEAD_DIM))
            pp_l.append(pp[:, 2 * SUBLANES - POOL_BUF:, :])
            ks_l.append(_window_from_lanes(nk))
            vs_l.append(_window_from_lanes(nv))
            ps_l.append(jnp.swapaxes(npool, 0, 1))
        else:
            o = l // 2
            w_in, w_out = w_in_odd[o].astype(BF16), w_out_odd[o].astype(BF16)
            x = _odd_prompt(x, g1, w_in, sgu_norm[o], w_spatial[o], b_spatial[o], w_out)
            x, sv_new = _odd_sample(x, g1, w_in, sgu_norm[o], w_spatial[o], b_spatial[o], w_out)
            sv_l.append(sv_new)
        if l < DEPTH - 1:
            x = _ffn(x, *fb)
        else:
            y_prompt, y_sample = _ffn(x, *fb, final_gain=final_gain)

    return (y_prompt.reshape(BATCH, SEQ, D_MODEL), y_sample,
            jnp.stack(kp_l), jnp.stack(vp_l), jnp.stack(pp_l),
            jnp.stack(ks_l), jnp.stack(vs_l), jnp.stack(ps_l), jnp.stack(sv_l))
```

```python
import functools
import math

import numpy as np
import jax
import jax.numpy as jnp
from jax import lax
from jax.experimental import pallas as pl
from jax.experimental.pallas import tpu as pltpu

F32 = jnp.float32
BF16 = jnp.bfloat16

D_MODEL = 1024
BATCH = 8
SEQ = 2048
DEPTH = 2
DEC_BATCH = 128
DEC_SEQ = 4
PAST_LEN = 16384
N_HEADS_A = 8
N_KV_A = 2
HEAD_DIM = 64
GQA_GROUP = N_HEADS_A // N_KV_A
WINDOW = 128
ATTN_WIDTH = N_HEADS_A * HEAD_DIM
KV_WIDTH = N_KV_A * HEAD_DIM
N_BUCKETS = 32
MAX_DISTANCE = 128
POOL_WINDOWS = (2, 4, 8, 16)
N_POOL_GROUPS = len(POOL_WINDOWS)
POOL_GROUP = 128
POOL_WIDTH = N_POOL_GROUPS * POOL_GROUP
POOL_BUF = max(POOL_WINDOWS) - 1
IN_EVEN = ATTN_WIDTH + 2 * KV_WIDTH + POOL_WIDTH
CHUNK = 128
SGU_WIDTH = 1024
SGU_GROUPS = 4
SGU_GROUP_W = SGU_WIDTH // SGU_GROUPS
D_FF = 2816
EPS = 1e-6
NEG = -1e30

LANES = 128
SUBLANES = 8
MIB = 1024 * 1024
VMEM_LIMIT_BYTES = 56 * MIB
EVEN_PROMPT_VMEM = 42 * MIB
EVEN_SAMPLE_VMEM = 44 * MIB
ODD_PROMPT_VMEM = 36 * MIB
ODD_SAMPLE_VMEM = 20 * MIB

PROMPT_TOK = BATCH * SEQ
SAMPLE_TOK = DEC_BATCH * DEC_SEQ
N_TOK = PROMPT_TOK + SAMPLE_TOK

TM = SAMPLE_TOK
EVEN_TM = 1024
assert SEQ % EVEN_TM == 0 and EVEN_TM % WINDOW == 0
TILES_PER_SEQ = SEQ // EVEN_TM
QBLK_PER_TILE = EVEN_TM // WINDOW
FFN_TM = 2 * TM
assert PROMPT_TOK % FFN_TM == 0
FFN_PROMPT_STEPS = PROMPT_TOK // FFN_TM
ODD_TM = 1024
assert PROMPT_TOK % ODD_TM == 0 and ODD_TM % CHUNK == 0
FF_CHUNK = 256
N_FF_CHUNKS = D_FF // FF_CHUNK
FF_SLOTS = 3
POOL_HEAD = 2 * SUBLANES
assert all(w & (w - 1) == 0 and w <= POOL_HEAD for w in POOL_WINDOWS)
POOL_CTX = 32
SEQ_GROUP = 8
N_SEQ_GROUPS = DEC_BATCH // SEQ_GROUP
GROUPS_PER_STEP = 4
assert N_SEQ_GROUPS % GROUPS_PER_STEP == 0
STATE_KEYS = SEQ_GROUP * WINDOW
NEW_SLOTS = SUBLANES
assert DEC_SEQ <= NEW_SLOTS and SEQ_GROUP * NEW_SLOTS <= WINDOW
SAMPLE_KEYS = STATE_KEYS + WINDOW
SAMPLE_ROWS = DEC_SEQ * SEQ_GROUP


assert N_KV_A == 2 and KV_WIDTH == LANES
HEAD_ORDER = [h for j in range(GQA_GROUP) for h in (j, j + GQA_GROUP)]


def _t5_bucket_np(dist):
    n = np.maximum(dist, 0)
    max_exact = N_BUCKETS // 2
    nf = np.maximum(n, 1).astype(np.float32)
    large = max_exact + (np.log(nf / np.float32(max_exact)) / np.float32(math.log(MAX_DISTANCE / max_exact))
                         * np.float32(N_BUCKETS - max_exact)).astype(np.int32)
    large = np.minimum(large, N_BUCKETS - 1)
    return np.where(n < max_exact, n, large).astype(np.int32)


def _prompt_bucket_table():
    qi = np.arange(WINDOW)[:, None]
    c = np.arange(WINDOW)[None, :]
    dist = np.where(c > qi, qi + WINDOW - c, qi - c)
    return _t5_bucket_np(dist)


def _sample_bucket_table():
    t = (np.arange(SAMPLE_ROWS) // SEQ_GROUP)[:, None]
    b = (np.arange(SAMPLE_ROWS) % SEQ_GROUP)[:, None]
    col = np.arange(SAMPLE_KEYS)[None, :]
    is_state = col < STATE_KEYS
    slot = (col - STATE_KEYS) % NEW_SLOTS
    is_new = (col >= STATE_KEYS) & (col < STATE_KEYS + SEQ_GROUP * NEW_SLOTS) & (slot < DEC_SEQ)
    kb = np.where(is_state, col // WINDOW, (col - STATE_KEYS) // NEW_SLOTS)
    kpos = np.where(is_state, col % WINDOW, WINDOW + slot)
    dist = (WINDOW + t) - kpos
    valid = (is_state | is_new) & (kb == b) & (dist >= 0) & (dist < WINDOW)
    return np.where(valid, _t5_bucket_np(dist), -1).astype(np.int32)


def _rms(x, g):
    y = x * lax.rsqrt(jnp.mean(x * x, axis=-1, keepdims=True) + EPS)
    return y * g


LOG2E = math.log2(math.e)
Q_SCALE = HEAD_DIM ** -0.5 * LOG2E


def _bias_from_buckets(bucket, rb_ref, h):
    t = jnp.zeros(bucket.shape, F32)
    for bkt in range(N_BUCKETS):
        t = jnp.where(bucket == bkt, rb_ref[bkt, h] * LOG2E, t)
    return t


def _sink_softmax(s, sink):
    m = jnp.maximum(jnp.max(s, axis=-1, keepdims=True), sink)
    p = jnp.exp2(s - m)
    return p, jnp.sum(p, axis=-1, keepdims=True) + jnp.exp2(sink - m)


def _softmax_pv(s, valid, bias, sink, v):
    p, denom = _sink_softmax(jnp.where(valid, s + bias, NEG), sink)
    return jnp.dot(p.astype(BF16), v, preferred_element_type=F32) / denom


def _ffn_kernel(*refs, layer, half, tail_in, final_split):
    refs = list(refs)
    x_ref = refs.pop(0)
    xt_ref = refs.pop(0) if tail_in else None
    g_ref, wg_hbm, wu_hbm, wd_hbm = refs[:4]
    del refs[:4]
    fg_ref = refs.pop(0) if final_split else None
    o_ref = refs.pop(0)
    ot_ref = refs.pop(0) if final_split else None
    wg, wu, wd, stage_in, stage_dn, sem = refs
    i = pl.program_id(0)

    def normed(x):
        return _rms(x, g_ref[...]).astype(BF16)

    def result(x, y):
        out = x + 0.5 * y
        return _rms(out, fg_ref[...]) if final_split else out

    def chunk_copies(c, slot):
        cols = pl.ds(c * FF_CHUNK, FF_CHUNK)
        return (
            pltpu.make_async_copy(wg_hbm.at[layer, half, :, cols], stage_in.at[0, slot], sem.at[0, slot]),
            pltpu.make_async_copy(wu_hbm.at[layer, half, :, cols], stage_in.at[1, slot], sem.at[1, slot]),
            pltpu.make_async_copy(wd_hbm.at[layer, half, cols, :], stage_dn.at[slot], sem.at[2, slot]),
        )

    @pl.when(i == 0)
    def _():
        for c in range(min(FF_SLOTS, N_FF_CHUNKS)):
            for cp in chunk_copies(c, c % FF_SLOTS):
                cp.start()
        if tail_in:
            x = jnp.concatenate([xt_ref[:, t, :] for t in range(DEC_SEQ)], axis=0)
        else:
            x = x_ref[0:TM, :]
        h = normed(x)
        acc = jnp.zeros((TM, D_MODEL), F32)
        for c in range(N_FF_CHUNKS):
            slot = c % FF_SLOTS
            cols = slice(c * FF_CHUNK, (c + 1) * FF_CHUNK)
            for cp in chunk_copies(c, slot):
                cp.wait()
            wg[:, cols] = stage_in[0, slot].astype(BF16)
            wu[:, cols] = stage_in[1, slot].astype(BF16)
            wd[cols, :] = stage_dn[slot].astype(BF16)
            if c + FF_SLOTS < N_FF_CHUNKS:
                for cp in chunk_copies(c + FF_SLOTS, slot):
                    cp.start()
            gate = jnp.dot(h, wg[:, cols], preferred_element_type=F32)
            up = jnp.dot(h, wu[:, cols], preferred_element_type=F32)
            act = (jax.nn.silu(gate) * up).astype(BF16)
            acc = acc + jnp.dot(act, wd[cols, :], preferred_element_type=F32)
        out = result(x, acc)
        if final_split:
            for t in range(DEC_SEQ):
                ot_ref[:, t, :] = out[t * DEC_BATCH:(t + 1) * DEC_BATCH, :]
        else:
            o_ref[0:TM, :] = out

    @pl.when(i > 0)
    def _():
        for hf in range(FFN_TM // TM):
            rows = slice(hf * TM, (hf + 1) * TM)
            x = x_ref[rows, :]
            h = normed(x)
            gate = jnp.dot(h, wg[...], preferred_element_type=F32)
            up = jnp.dot(h, wu[...], preferred_element_type=F32)
            act = (jax.nn.silu(gate) * up).astype(BF16)
            o_ref[rows, :] = result(x, jnp.dot(act, wd[...], preferred_element_type=F32))


def _resident(shape):
    return pl.BlockSpec(shape, lambda *_: (0,) * len(shape), pipeline_mode=pl.Buffered(1))


def _ffn(x_in, gain, wg, wu, wd, layer, half, *, tail_in=None, final_gain=None):
    hbm = pl.BlockSpec(memory_space=pl.ANY)
    row_block = pl.BlockSpec((FFN_TM, D_MODEL), lambda i: (jnp.where(i == 0, FFN_PROMPT_STEPS, i - 1), 0))
    prompt_block = pl.BlockSpec((FFN_TM, D_MODEL), lambda i: (jnp.maximum(i - 1, 0), 0))
    sample_block = pl.BlockSpec((DEC_BATCH, DEC_SEQ, D_MODEL), lambda i: (0, 0, 0))
    if tail_in is None:
        args, in_specs = [x_in], [row_block]
    else:
        args, in_specs = [x_in, tail_in], [prompt_block, sample_block]
    args += [gain.reshape(1, D_MODEL), wg, wu, wd]
    in_specs += [_resident((1, D_MODEL)), hbm, hbm, hbm]
    if final_gain is None:
        out_shape = jax.ShapeDtypeStruct((N_TOK, D_MODEL), F32)
        out_specs = row_block
    else:
        args.append(final_gain.reshape(1, D_MODEL))
        in_specs.append(_resident((1, D_MODEL)))
        out_shape = (jax.ShapeDtypeStruct((PROMPT_TOK, D_MODEL), F32),
                     jax.ShapeDtypeStruct((DEC_BATCH, DEC_SEQ, D_MODEL), F32))
        out_specs = (prompt_block, sample_block)
    aliases = {0: 0} if (tail_in is None and final_gain is None) else {}
    return pl.pallas_call(
        functools.partial(_ffn_kernel, layer=layer, half=half, tail_in=tail_in is not None,
                          final_split=final_gain is not None),
        out_shape=out_shape,
        grid=(FFN_PROMPT_STEPS + 1,),
        in_specs=in_specs,
        out_specs=out_specs,
        scratch_shapes=[
            pltpu.VMEM((D_MODEL, D_FF), BF16),
            pltpu.VMEM((D_MODEL, D_FF), BF16),
            pltpu.VMEM((D_FF, D_MODEL), BF16),
            pltpu.VMEM((2, FF_SLOTS, D_MODEL, FF_CHUNK), F32),
            pltpu.VMEM((FF_SLOTS, FF_CHUNK, D_MODEL), F32),
            pltpu.SemaphoreType.DMA((3, FF_SLOTS)),
        ],
        input_output_aliases=aliases,
        compiler_params=pltpu.CompilerParams(
            dimension_semantics=("arbitrary",), vmem_limit_bytes=VMEM_LIMIT_BYTES),
        name="macaron_half",
    )(*args)


def _even_prompt_kernel(x_ref, g_ref, win_ref, wout_ref, wpool_ref, pscale_ref, bucket_ref, rb_ref, sink_ref,
                        o_ref, klast_ref, vlast_ref, plast_ref,
                        kbuf, vbuf, zb, s2b, s4b, s8b, bias_ref, cat_ref):
    b = pl.program_id(0)
    s = pl.program_id(1)
    last = pl.num_programs(1) - 1

    @pl.when((b == 0) & (s == 0))
    def _():
        bucket = bucket_ref[...]
        for h in range(N_HEADS_A):
            bias_ref[h] = _bias_from_buckets(bucket, rb_ref, h)

    @pl.when(s == 0)
    def _():
        for buf in (kbuf, vbuf):
            buf[0:WINDOW, :] = jnp.zeros((WINDOW, KV_WIDTH), BF16)
        zb[0:POOL_CTX, :] = jnp.zeros((POOL_CTX, POOL_WIDTH), F32)

    @pl.when(s > 0)
    def _():
        for buf in (kbuf, vbuf):
            buf[0:WINDOW, :] = buf[EVEN_TM:EVEN_TM + WINDOW, :]
        zb[0:POOL_CTX, :] = zb[EVEN_TM:EVEN_TM + POOL_CTX, :]

    x = x_ref[...]
    h = _rms(x, g_ref[...]).astype(BF16)
    proj = jnp.dot(h, win_ref[...], preferred_element_type=F32)
    q = proj[:, 0:ATTN_WIDTH] * Q_SCALE
    k = proj[:, ATTN_WIDTH:ATTN_WIDTH + KV_WIDTH]
    v = proj[:, ATTN_WIDTH + KV_WIDTH:ATTN_WIDTH + 2 * KV_WIDTH]
    up = proj[:, ATTN_WIDTH + 2 * KV_WIDTH:IN_EVEN]

    qlane = lax.broadcasted_iota(jnp.int32, (EVEN_TM, ATTN_WIDTH), 1) % LANES
    q_half = (jnp.where(qlane < HEAD_DIM, q, 0.0).astype(BF16),
              jnp.where(qlane >= HEAD_DIM, q, 0.0).astype(BF16))
    kbuf[WINDOW:, :] = k.astype(BF16)
    vbuf[WINDOW:, :] = v.astype(BF16)

    qrow = lax.broadcasted_iota(jnp.int32, (WINDOW, WINDOW), 0)
    kcol = lax.broadcasted_iota(jnp.int32, (WINDOW, WINDOW), 1)
    prev = kcol > qrow
    olane = lax.broadcasted_iota(jnp.int32, (WINDOW, LANES), 1)
    for i in range(QBLK_PER_TILE):
        rows = slice(i * WINDOW, (i + 1) * WINDOW)
        keys = slice(i * WINDOW, (i + 2) * WINDOW)
        qs = jnp.concatenate(
            [q_half[hd // GQA_GROUP][rows, (hd % GQA_GROUP) * LANES:(hd % GQA_GROUP + 1) * LANES]
             for hd in range(N_HEADS_A)], axis=0)
        sc_all = lax.dot_general(qs, kbuf[keys, :], (((1,), (1,)), ((), ())), preferred_element_type=F32)
        ps, denoms = [], []
        for hd in range(N_HEADS_A):
            sc = sc_all[hd * WINDOW:(hd + 1) * WINDOW, :]
            sc = jnp.where(prev, sc[:, 0:WINDOW], sc[:, WINDOW:]) + bias_ref[hd]
            if i == 0:
                sc = jnp.where(prev & (qrow >= jnp.where(s > 0, WINDOW, 0)), NEG, sc)
            p, denom = _sink_softmax(sc, sink_ref[hd] * LOG2E)
            ps.append(jnp.concatenate([jnp.where(prev, p, 0.0), jnp.where(prev, 0.0, p)], axis=1).astype(BF16))
            denoms.append(denom)
        o_all = jnp.dot(jnp.concatenate(ps, axis=0), vbuf[keys, :], preferred_element_type=F32)
        out = [o_all[hd * WINDOW:(hd + 1) * WINDOW, :] / denoms[hd] for hd in range(N_HEADS_A)]
        for j in range(GQA_GROUP):
            cat_ref[rows, j * LANES:(j + 1) * LANES] = jnp.where(
                olane < HEAD_DIM, out[j], out[j + GQA_GROUP]).astype(BF16)

    zb[POOL_CTX:, :] = up
    n = EVEN_TM + POOL_CTX
    s2b[8:, :] = zb[8:n, :] + zb[7:n - 1, :]
    s4b[16:, :] = s2b[16:n, POOL_GROUP:] + s2b[14:n - 2, POOL_GROUP:]
    s8b[24:, :] = s4b[24:n, POOL_GROUP:] + s4b[20:n - 4, POOL_GROUP:]
    s16 = s8b[32:n, POOL_GROUP:] + s8b[24:n - 8, POOL_GROUP:]
    sums = (s2b[POOL_CTX:, 0:POOL_GROUP], s4b[POOL_CTX:, 0:POOL_GROUP], s8b[POOL_CTX:, 0:POOL_GROUP], s16)
    pos1 = lax.broadcasted_iota(jnp.int32, (POOL_HEAD, POOL_GROUP), 0) + s * EVEN_TM + 1
    for g, w in enumerate(POOL_WINDOWS):
        lanes = slice(g * POOL_GROUP, (g + 1) * POOL_GROUP)
        cnt = jnp.minimum(pos1, w).astype(F32)
        mean = jnp.concatenate([sums[g][0:POOL_HEAD] / cnt, sums[g][POOL_HEAD:] * (1.0 / w)], axis=0)
        d = (mean - up[:, lanes]).astype(BF16)
        y = jnp.dot(d, wpool_ref[g], preferred_element_type=F32) * pscale_ref[:, lanes]
        cat_ref[:, ATTN_WIDTH + g * POOL_GROUP:ATTN_WIDTH + (g + 1) * POOL_GROUP] = y.astype(BF16)

    o_ref[...] = x + jnp.dot(cat_ref[...], wout_ref[...], preferred_element_type=F32)

    @pl.when(s == last)
    def _():
        klast_ref[0] = k[EVEN_TM - WINDOW:, :]
        vlast_ref[0] = v[EVEN_TM - WINDOW:, :]
        plast_ref[0] = up[EVEN_TM - 2 * SUBLANES:, :]


def _even_prompt(x, gain, w_in, w_out, w_pool, pool_scale, rel_bias, sinks):
    bucket = jnp.asarray(_prompt_bucket_table())
    smem = pl.BlockSpec(memory_space=pltpu.SMEM)
    return pl.pallas_call(
        _even_prompt_kernel,
        out_shape=(
            jax.ShapeDtypeStruct((N_TOK, D_MODEL), F32),
            jax.ShapeDtypeStruct((BATCH, WINDOW, KV_WIDTH), F32),
            jax.ShapeDtypeStruct((BATCH, WINDOW, KV_WIDTH), F32),
            jax.ShapeDtypeStruct((BATCH, 2 * SUBLANES, POOL_WIDTH), F32),
        ),
        grid=(BATCH, TILES_PER_SEQ),
        in_specs=[
            pl.BlockSpec((EVEN_TM, D_MODEL), lambda b, s: (b * TILES_PER_SEQ + s, 0)),
            _resident((1, D_MODEL)),
            _resident((D_MODEL, IN_EVEN)),
            _resident((ATTN_WIDTH + POOL_WIDTH, D_MODEL)),
            _resident((N_POOL_GROUPS, POOL_GROUP, POOL_GROUP)),
            _resident((1, POOL_WIDTH)),
            _resident((WINDOW, WINDOW)),
            smem,
            smem,
        ],
        out_specs=(
            pl.BlockSpec((EVEN_TM, D_MODEL), lambda b, s: (b * TILES_PER_SEQ + s, 0)),
            pl.BlockSpec((1, WINDOW, KV_WIDTH), lambda b, s: (b, 0, 0)),
            pl.BlockSpec((1, WINDOW, KV_WIDTH), lambda b, s: (b, 0, 0)),
            pl.BlockSpec((1, 2 * SUBLANES, POOL_WIDTH), lambda b, s: (b, 0, 0)),
        ),
        scratch_shapes=[
            pltpu.VMEM((WINDOW + EVEN_TM, KV_WIDTH), BF16),
            pltpu.VMEM((WINDOW + EVEN_TM, KV_WIDTH), BF16),
            pltpu.VMEM((POOL_CTX + EVEN_TM, POOL_WIDTH), F32),
            pltpu.VMEM((POOL_CTX + EVEN_TM, POOL_WIDTH), F32),
            pltpu.VMEM((POOL_CTX + EVEN_TM, POOL_WIDTH - POOL_GROUP), F32),
            pltpu.VMEM((POOL_CTX + EVEN_TM, POOL_WIDTH - 2 * POOL_GROUP), F32),
            pltpu.VMEM((N_HEADS_A, WINDOW, WINDOW), F32),
            pltpu.VMEM((EVEN_TM, ATTN_WIDTH + POOL_WIDTH), BF16),
        ],
        input_output_aliases={0: 0},
        compiler_params=pltpu.CompilerParams(
            dimension_semantics=("arbitrary", "arbitrary"), vmem_limit_bytes=EVEN_PROMPT_VMEM),
        name="even_mixer_prompt",
    )(x, gain.reshape(1, D_MODEL), w_in, w_out, w_pool, pool_scale.reshape(1, POOL_WIDTH), bucket,
      rel_bias, sinks)


def _even_sample_kernel(x_ref, g_ref, win_ref, wout_ref, wpool_ref, pscale_ref, bucket_ref, rb_ref, sink_ref,
                        sk_ref, sv_ref, spool_ref,
                        o_ref, nk_ref, nv_ref, npool_ref,
                        knew_ref, vnew_ref, upnew_ref, qsel, bias_ref, attn_f32, cat_ref):
    g = pl.program_id(0)
    last = pl.num_programs(0) - 1

    @pl.when(g == 0)
    def _():
        bucket = bucket_ref[...]
        for h in range(N_HEADS_A):
            bias_ref[h] = _bias_from_buckets(bucket, rb_ref, h)
        x = x_ref[...]
        h = _rms(x, g_ref[...]).astype(BF16)
        proj = jnp.dot(h, win_ref[...], preferred_element_type=F32)
        q = proj[:, 0:ATTN_WIDTH] * Q_SCALE
        zeros = jnp.zeros(((NEW_SLOTS - DEC_SEQ) * DEC_BATCH, KV_WIDTH), F32)
        knew_ref[0:SAMPLE_TOK, :] = proj[:, ATTN_WIDTH:ATTN_WIDTH + KV_WIDTH]
        vnew_ref[0:SAMPLE_TOK, :] = proj[:, ATTN_WIDTH + KV_WIDTH:ATTN_WIDTH + 2 * KV_WIDTH]
        knew_ref[SAMPLE_TOK:, :] = zeros
        vnew_ref[SAMPLE_TOK:, :] = zeros
        upnew_ref[...] = proj[:, ATTN_WIDTH + 2 * KV_WIDTH:IN_EVEN]
        qlane = lax.broadcasted_iota(jnp.int32, (SAMPLE_TOK, LANES), 1)
        for hd in range(N_HEADS_A):
            j, kvh = hd % GQA_GROUP, hd // GQA_GROUP
            keep = (qlane < HEAD_DIM) if kvh == 0 else (qlane >= HEAD_DIM)
            qsel[hd] = jnp.where(keep, q[:, j * LANES:(j + 1) * LANES], 0.0)

    lax.fori_loop(0, GROUPS_PER_STEP, functools.partial(
        _even_sample_group, g, bucket_ref, sink_ref, sk_ref, sv_ref, nk_ref, nv_ref,
        knew_ref, vnew_ref, qsel, bias_ref, attn_f32), 0)

    @pl.when(g == last)
    def _():
        cat_ref[:, 0:ATTN_WIDTH] = attn_f32[...].astype(BF16)
        for r in range(POOL_BUF):
            if r < POOL_BUF - DEC_SEQ:
                npool_ref[r] = spool_ref[r + DEC_SEQ]
            else:
                t = r - (POOL_BUF - DEC_SEQ)
                npool_ref[r] = upnew_ref[t * DEC_BATCH:(t + 1) * DEC_BATCH, :]
        for t in range(DEC_SEQ):
            rows = slice(t * DEC_BATCH, (t + 1) * DEC_BATCH)
            for gi, w in enumerate(POOL_WINDOWS):
                lanes = slice(gi * POOL_GROUP, (gi + 1) * POOL_GROUP)
                acc = upnew_ref[rows, lanes]
                for back in range(1, w):
                    tt = t - back
                    if tt >= 0:
                        acc = acc + upnew_ref[tt * DEC_BATCH:(tt + 1) * DEC_BATCH, lanes]
                    else:
                        acc = acc + spool_ref[POOL_BUF + tt, :, lanes]
                cnt = float(min(PAST_LEN + t + 1, w))
                d = (acc / cnt - upnew_ref[rows, lanes]).astype(BF16)
                y = jnp.dot(d, wpool_ref[gi], preferred_element_type=F32) * pscale_ref[:, lanes]
                cat_ref[rows, ATTN_WIDTH + gi * POOL_GROUP:ATTN_WIDTH + (gi + 1) * POOL_GROUP] = y.astype(BF16)
        o_ref[...] = x_ref[...] + jnp.dot(cat_ref[...], wout_ref[...], preferred_element_type=F32)


def _even_sample_group(g, bucket_ref, sink_ref, sk_ref, sv_ref, nk_ref, nv_ref,
                       knew_ref, vnew_ref, qsel, bias_ref, attn_f32, gg, carry):
    olane = lax.broadcasted_iota(jnp.int32, (SAMPLE_ROWS, LANES), 1)
    seq0 = gg * SEQ_GROUP
    row0 = pl.multiple_of((g * GROUPS_PER_STEP + gg) * SEQ_GROUP, SEQ_GROUP)
    pad = [jnp.zeros((WINDOW - SEQ_GROUP * NEW_SLOTS, KV_WIDTH), F32)]
    knew = jnp.concatenate(
        [knew_ref[pl.ds(row0 + b, NEW_SLOTS, stride=DEC_BATCH), :] for b in range(SEQ_GROUP)] + pad, axis=0)
    vnew = jnp.concatenate(
        [vnew_ref[pl.ds(row0 + b, NEW_SLOTS, stride=DEC_BATCH), :] for b in range(SEQ_GROUP)] + pad, axis=0)
    kt_all = jnp.concatenate([sk_ref[seq0 + b] for b in range(SEQ_GROUP)], axis=1).astype(BF16)
    vt_all = jnp.concatenate([sv_ref[seq0 + b] for b in range(SEQ_GROUP)], axis=1).astype(BF16)

    qg = jnp.concatenate(
        [qsel[hd, pl.ds(t * DEC_BATCH + row0, SEQ_GROUP), :] for hd in range(N_HEADS_A) for t in range(DEC_SEQ)],
        axis=0).astype(BF16)
    nt = (((1,), (1,)), ((), ()))
    sc = jnp.concatenate(
        [jnp.dot(qg, kt_all, preferred_element_type=F32),
         lax.dot_general(qg, knew.astype(BF16), nt, preferred_element_type=F32)], axis=1)
    valid = jnp.concatenate([bucket_ref[...]] * N_HEADS_A, axis=0) >= 0
    bias = bias_ref[...].reshape(N_HEADS_A * SAMPLE_ROWS, SAMPLE_KEYS)
    sink = jnp.concatenate(
        [jnp.full((SAMPLE_ROWS, 1), sink_ref[hd] * LOG2E, F32) for hd in range(N_HEADS_A)], axis=0)
    p, denom = _sink_softmax(jnp.where(valid, sc + bias, NEG), sink)
    p = p.astype(BF16)
    o_all = (lax.dot_general(p[:, 0:STATE_KEYS], vt_all, nt, preferred_element_type=F32)
             + jnp.dot(p[:, STATE_KEYS:], vnew.astype(BF16), preferred_element_type=F32)) / denom
    for j in range(GQA_GROUP):
        blk = jnp.where(olane < HEAD_DIM, o_all[j * SAMPLE_ROWS:(j + 1) * SAMPLE_ROWS, :],
                        o_all[(j + GQA_GROUP) * SAMPLE_ROWS:(j + GQA_GROUP + 1) * SAMPLE_ROWS, :])
        for t in range(DEC_SEQ):
            attn_f32[pl.ds(t * DEC_BATCH + row0, SEQ_GROUP), j * LANES:(j + 1) * LANES] = (
                blk[t * SEQ_GROUP:(t + 1) * SEQ_GROUP, :])

    lane = lax.broadcasted_iota(jnp.int32, (KV_WIDTH, WINDOW), 1)
    for new, old_ref, out_ref in ((knew, sk_ref, nk_ref), (vnew, sv_ref, nv_ref)):
        new_t = new.T
        for b in range(SEQ_GROUP):
            tail = pltpu.roll(new_t, (WINDOW - DEC_SEQ - b * NEW_SLOTS) % WINDOW, 1)
            out_ref[seq0 + b] = jnp.where(
                lane >= WINDOW - DEC_SEQ, tail, pltpu.roll(old_ref[seq0 + b], WINDOW - DEC_SEQ, 1))
    return carry


def _even_sample(x, gain, w_in, w_out, w_pool, pool_scale, rel_bias, sinks, state_k, state_v, state_pool_t):
    bucket = jnp.asarray(_sample_bucket_table())
    smem = pl.BlockSpec(memory_space=pltpu.SMEM)
    sample_block = PROMPT_TOK // SAMPLE_TOK
    return pl.pallas_call(
        _even_sample_kernel,
        out_shape=(
            jax.ShapeDtypeStruct((N_TOK, D_MODEL), F32),
            jax.ShapeDtypeStruct((DEC_BATCH, KV_WIDTH, WINDOW), F32),
            jax.ShapeDtypeStruct((DEC_BATCH, KV_WIDTH, WINDOW), F32),
            jax.ShapeDtypeStruct((POOL_BUF, DEC_BATCH, POOL_WIDTH), F32),
        ),
        grid=(N_SEQ_GROUPS // GROUPS_PER_STEP,),
        in_specs=[
            pl.BlockSpec((SAMPLE_TOK, D_MODEL), lambda g: (sample_block, 0)),
            _resident((1, D_MODEL)),
            _resident((D_MODEL, IN_EVEN)),
            _resident((ATTN_WIDTH + POOL_WIDTH, D_MODEL)),
            _resident((N_POOL_GROUPS, POOL_GROUP, POOL_GROUP)),
            _resident((1, POOL_WIDTH)),
            _resident((SAMPLE_ROWS, SAMPLE_KEYS)),
            smem,
            smem,
            pl.BlockSpec((GROUPS_PER_STEP * SEQ_GROUP, KV_WIDTH, WINDOW), lambda g: (g, 0, 0)),
            pl.BlockSpec((GROUPS_PER_STEP * SEQ_GROUP, KV_WIDTH, WINDOW), lambda g: (g, 0, 0)),
            _resident((POOL_BUF, DEC_BATCH, POOL_WIDTH)),
        ],
        out_specs=(
            pl.BlockSpec((SAMPLE_TOK, D_MODEL), lambda g: (sample_block, 0)),
            pl.BlockSpec((GROUPS_PER_STEP * SEQ_GROUP, KV_WIDTH, WINDOW), lambda g: (g, 0, 0)),
            pl.BlockSpec((GROUPS_PER_STEP * SEQ_GROUP, KV_WIDTH, WINDOW), lambda g: (g, 0, 0)),
            pl.BlockSpec((POOL_BUF, DEC_BATCH, POOL_WIDTH), lambda g: (0, 0, 0)),
        ),
        scratch_shapes=[
            pltpu.VMEM((NEW_SLOTS * DEC_BATCH, KV_WIDTH), F32),
            pltpu.VMEM((NEW_SLOTS * DEC_BATCH, KV_WIDTH), F32),
            pltpu.VMEM((SAMPLE_TOK, POOL_WIDTH), F32),
            pltpu.VMEM((N_HEADS_A, SAMPLE_TOK, LANES), F32),
            pltpu.VMEM((N_HEADS_A, SAMPLE_ROWS, SAMPLE_KEYS), F32),
            pltpu.VMEM((SAMPLE_TOK, ATTN_WIDTH), F32),
            pltpu.VMEM((SAMPLE_TOK, ATTN_WIDTH + POOL_WIDTH), BF16),
        ],
        input_output_aliases={0: 0},
        compiler_params=pltpu.CompilerParams(
            dimension_semantics=("arbitrary",), vmem_limit_bytes=EVEN_SAMPLE_VMEM),
        name="even_mixer_sample",
    )(x, gain.reshape(1, D_MODEL), w_in, w_out, w_pool, pool_scale.reshape(1, POOL_WIDTH), bucket,
      rel_bias, sinks, state_k, state_v, state_pool_t)


def _sgu_front(x, g_ref, win_ref, gv_ref):
    h = _rms(x, g_ref[...]).astype(BF16)
    uv = jax.nn.gelu(jnp.dot(h, win_ref[...], preferred_element_type=F32))
    return uv[:, 0:SGU_WIDTH], _rms(uv[:, SGU_WIDTH:], gv_ref[...])


def _odd_prompt_kernel(x_ref, g_ref, win_ref, gv_ref, ws_ref, bs_ref, wout_ref, o_ref, gated_ref):
    x = x_ref[...]
    u, v = _sgu_front(x, g_ref, win_ref, gv_ref)
    vb = v.astype(BF16)
    r = lax.broadcasted_iota(jnp.int32, (CHUNK, CHUNK), 0)
    c = lax.broadcasted_iota(jnp.int32, (CHUNK, CHUNK), 1)
    for gi in range(SGU_GROUPS):
        w = jnp.where(r >= c, ws_ref[gi], 0.0).astype(BF16)
        bias = jnp.concatenate([bs_ref[gi]] * (SGU_GROUP_W // LANES), axis=1)
        lanes = slice(gi * SGU_GROUP_W, (gi + 1) * SGU_GROUP_W)
        for ci in range(ODD_TM // CHUNK):
            rows = slice(ci * CHUNK, (ci + 1) * CHUNK)
            mixed = jnp.dot(w, vb[rows, lanes], preferred_element_type=F32) + bias
            gated_ref[rows, lanes] = (u[rows, lanes] * mixed).astype(BF16)
    o_ref[...] = x + jnp.dot(gated_ref[...], wout_ref[...], preferred_element_type=F32)


def _odd_prompt(x, gain, w_in, g_v, w_s, b_s, w_out):
    bias = jnp.broadcast_to(b_s[:, :, None], (SGU_GROUPS, CHUNK, LANES))
    return pl.pallas_call(
        _odd_prompt_kernel,
        out_shape=jax.ShapeDtypeStruct((N_TOK, D_MODEL), F32),
        grid=(PROMPT_TOK // ODD_TM,),
        in_specs=[
            pl.BlockSpec((ODD_TM, D_MODEL), lambda i: (i, 0)),
            _resident((1, D_MODEL)),
            _resident((D_MODEL, 2 * SGU_WIDTH)),
            _resident((1, SGU_WIDTH)),
            _resident((SGU_GROUPS, CHUNK, CHUNK)),
            _resident((SGU_GROUPS, CHUNK, LANES)),
            _resident((SGU_WIDTH, D_MODEL)),
        ],
        out_specs=pl.BlockSpec((ODD_TM, D_MODEL), lambda i: (i, 0)),
        scratch_shapes=[pltpu.VMEM((ODD_TM, SGU_WIDTH), BF16)],
        input_output_aliases={0: 0},
        compiler_params=pltpu.CompilerParams(
            dimension_semantics=("arbitrary",), vmem_limit_bytes=ODD_PROMPT_VMEM),
        name="odd_mixer_prompt",
    )(x, gain.reshape(1, D_MODEL), w_in, g_v.reshape(1, SGU_WIDTH), w_s, bias, w_out)


def _odd_sample_kernel(x_ref, g_ref, win_ref, gv_ref, coef_ref, bias_ref, wout_ref, o_ref, v_ref):
    x = x_ref[...]
    u, v = _sgu_front(x, g_ref, win_ref, gv_ref)
    for t in range(DEC_SEQ):
        v_ref[:, t, :] = v[t * DEC_BATCH:(t + 1) * DEC_BATCH, :]
    gated = []
    for t in range(DEC_SEQ):
        mixed = bias_ref[t:t + 1, :]
        for s in range(t + 1):
            mixed = mixed + coef_ref[t, s:s + 1, :] * v[s * DEC_BATCH:(s + 1) * DEC_BATCH, :]
        gated.append((u[t * DEC_BATCH:(t + 1) * DEC_BATCH, :] * mixed).astype(BF16))
    o_ref[...] = x + jnp.dot(jnp.concatenate(gated, axis=0), wout_ref[...], preferred_element_type=F32)


def _odd_sample(x, gain, w_in, g_v, w_s, b_s, w_out):
    coef = jnp.repeat(jnp.transpose(w_s[:, :DEC_SEQ, :DEC_SEQ], (1, 2, 0)), SGU_GROUP_W, axis=-1)
    bias = jnp.repeat(b_s[:, :DEC_SEQ].T, SGU_GROUP_W, axis=-1)
    sample_block = PROMPT_TOK // SAMPLE_TOK
    return pl.pallas_call(
        _odd_sample_kernel,
        out_shape=(
            jax.ShapeDtypeStruct((N_TOK, D_MODEL), F32),
            jax.ShapeDtypeStruct((DEC_BATCH, DEC_SEQ, SGU_WIDTH), F32),
        ),
        grid=(1,),
        in_specs=[
            pl.BlockSpec((SAMPLE_TOK, D_MODEL), lambda i: (sample_block, 0)),
            _resident((1, D_MODEL)),
            _resident((D_MODEL, 2 * SGU_WIDTH)),
            _resident((1, SGU_WIDTH)),
            _resident((DEC_SEQ, DEC_SEQ, SGU_WIDTH)),
            _resident((DEC_SEQ, SGU_WIDTH)),
            _resident((SGU_WIDTH, D_MODEL)),
        ],
        out_specs=(
            pl.BlockSpec((SAMPLE_TOK, D_MODEL), lambda i: (sample_block, 0)),
            pl.BlockSpec((DEC_BATCH, DEC_SEQ, SGU_WIDTH), lambda i: (0, 0, 0)),
        ),
        input_output_aliases={0: 0},
        compiler_params=pltpu.CompilerParams(
            dimension_semantics=("arbitrary",), vmem_limit_bytes=ODD_SAMPLE_VMEM),
        name="odd_mixer_sample",
    )(x, gain.reshape(1, D_MODEL), w_in, g_v.reshape(1, SGU_WIDTH), coef, bias, w_out)


def _window_to_lanes(w):
    return jnp.transpose(w, (0, 2, 3, 1)).reshape(DEC_BATCH, KV_WIDTH, WINDOW)


def _window_from_lanes(w):
    return jnp.transpose(w.reshape(DEC_BATCH, N_KV_A, HEAD_DIM, WINDOW), (0, 3, 1, 2))


def kernel(x_prompt, x_sample, state_win_k, state_win_v, state_pool, rel_bias, norm_gains, final_gain,
           ffn_gate, ffn_up, ffn_down, w_in_even, w_out_even, attn_sinks, w_pool, pool_scale,
           w_in_odd, sgu_norm, w_spatial, b_spatial, w_out_odd):
    xp = x_prompt.reshape(PROMPT_TOK, D_MODEL)

    kp_l, vp_l, pp_l, ks_l, vs_l, ps_l, sv_l = [], [], [], [], [], [], []
    x = None
    y_prompt = y_sample = None
    for l in range(DEPTH):
        fa = (norm_gains[l, 0], ffn_gate, ffn_up, ffn_down, l, 0)
        fb = (norm_gains[l, 2], ffn_gate, ffn_up, ffn_down, l, 1)
        if l == 0:
            x = _ffn(xp, *fa, tail_in=x_sample)
        else:
            x = _ffn(x, *fa)
        g1 = norm_gains[l, 1]
        if l % 2 == 0:
            e = l // 2
            wq = w_in_even[e][:, :ATTN_WIDTH].reshape(D_MODEL, N_KV_A, GQA_GROUP, HEAD_DIM)
            wq = jnp.swapaxes(wq, 1, 2).reshape(D_MODEL, ATTN_WIDTH)
            wo = w_out_even[e][:ATTN_WIDTH].reshape(N_KV_A, GQA_GROUP, HEAD_DIM, D_MODEL)
            wo = jnp.swapaxes(wo, 0, 1).reshape(ATTN_WIDTH, D_MODEL)
            w_in = jnp.concatenate([wq, w_in_even[e][:, ATTN_WIDTH:]], axis=1).astype(BF16)
            w_out = jnp.concatenate([wo, w_out_even[e][ATTN_WIDTH:]], axis=0).astype(BF16)
            wp = w_pool[e].astype(BF16)
            x, kp, vp, pp = _even_prompt(x, g1, w_in, w_out, wp, pool_scale[e], rel_bias, attn_sinks[e])
            x, nk, nv, npool = _even_sample(
                x, g1, w_in, w_out, wp, pool_scale[e], rel_bias, attn_sinks[e],
                _window_to_lanes(state_win_k[e]), _window_to_lanes(state_win_v[e]),
                jnp.swapaxes(state_pool[e], 0, 1))
            kp_l.append(kp.reshape(BATCH, WINDOW, N_KV_A, HEAD_DIM))
            vp_l.append(vp.reshape(BATCH, WINDOW, N_KV_A, HEAD_DIM))
            pp_l.append(pp[:, 2 * SUBLANES - POOL_BUF:, :])
            ks_l.append(_window_from_lanes(nk))
            vs_l.append(_window_from_lanes(nv))
            ps_l.append(jnp.swapaxes(npool, 0, 1))
        else:
            o = l // 2
            w_in, w_out = w_in_odd[o].astype(BF16), w_out_odd[o].astype(BF16)
            x = _odd_prompt(x, g1, w_in, sgu_norm[o], w_spatial[o], b_spatial[o], w_out)
            x, sv_new = _odd_sample(x, g1, w_in, sgu_norm[o], w_spatial[o], b_spatial[o], w_out)
            sv_l.append(sv_new)
        if l < DEPTH - 1:
            x = _ffn(x, *fb)
        else:
            y_prompt, y_sample = _ffn(x, *fb, final_gain=final_gain)

    return (y_prompt.reshape(BATCH, SEQ, D_MODEL), y_sample,
            jnp.stack(kp_l), jnp.stack(vp_l), jnp.stack(pp_l),
            jnp.stack(ks_l), jnp.stack(vs_l), jnp.stack(ps_l), jnp.stack(sv_l))
```

```python
import functools
import math

import numpy as np
import jax
import jax.numpy as jnp
from jax import lax
from jax.experimental import pallas as pl
from jax.experimental.pallas import tpu as pltpu

F32 = jnp.float32
BF16 = jnp.bfloat16

D_MODEL = 1024
BATCH = 8
SEQ = 2048
DEPTH = 2
DEC_BATCH = 128
DEC_SEQ = 4
PAST_LEN = 16384
N_HEADS_A = 8
N_KV_A = 2
HEAD_DIM = 64
GQA_GROUP = N_HEADS_A // N_KV_A
WINDOW = 128
ATTN_WIDTH = N_HEADS_A * HEAD_DIM
KV_WIDTH = N_KV_A * HEAD_DIM
N_BUCKETS = 32
MAX_DISTANCE = 128
POOL_WINDOWS = (2, 4, 8, 16)
N_POOL_GROUPS = len(POOL_WINDOWS)
POOL_GROUP = 128
POOL_WIDTH = N_POOL_GROUPS * POOL_GROUP
POOL_BUF = max(POOL_WINDOWS) - 1
IN_EVEN = ATTN_WIDTH + 2 * KV_WIDTH + POOL_WIDTH
CHUNK = 128
SGU_WIDTH = 1024
SGU_GROUPS = 4
SGU_GROUP_W = SGU_WIDTH // SGU_GROUPS
D_FF = 2816
EPS = 1e-6
NEG = -1e30

LANES = 128
SUBLANES = 8
MIB = 1024 * 1024
VMEM_LIMIT_BYTES = 56 * MIB
EVEN_PROMPT_VMEM = 42 * MIB
EVEN_SAMPLE_VMEM = 44 * MIB
ODD_PROMPT_VMEM = 36 * MIB
ODD_SAMPLE_VMEM = 20 * MIB

PROMPT_TOK = BATCH * SEQ
SAMPLE_TOK = DEC_BATCH * DEC_SEQ
N_TOK = PROMPT_TOK + SAMPLE_TOK

TM = SAMPLE_TOK
EVEN_TM = 1024
assert SEQ % EVEN_TM == 0 and EVEN_TM % WINDOW == 0
TILES_PER_SEQ = SEQ // EVEN_TM
QBLK_PER_TILE = EVEN_TM // WINDOW
FFN_TM = 2 * TM
assert PROMPT_TOK % FFN_TM == 0
FFN_PROMPT_STEPS = PROMPT_TOK // FFN_TM
ODD_TM = 1024
assert PROMPT_TOK % ODD_TM == 0 and ODD_TM % CHUNK == 0
FF_CHUNK = 256
N_FF_CHUNKS = D_FF // FF_CHUNK
FF_SLOTS = 3
POOL_HEAD = 2 * SUBLANES
assert all(w & (w - 1) == 0 and w <= POOL_HEAD for w in POOL_WINDOWS)
POOL_CTX = 32
SEQ_GROUP = 8
N_SEQ_GROUPS = DEC_BATCH // SEQ_GROUP
GROUPS_PER_STEP = 4
assert N_SEQ_GROUPS % GROUPS_PER_STEP == 0
STATE_KEYS = SEQ_GROUP * WINDOW
NEW_SLOTS = SUBLANES
assert DEC_SEQ <= NEW_SLOTS and SEQ_GROUP * NEW_SLOTS <= WINDOW
SAMPLE_KEYS = STATE_KEYS + WINDOW
SAMPLE_ROWS = DEC_SEQ * SEQ_GROUP


assert N_KV_A == 2 and KV_WIDTH == LANES
HEAD_ORDER = [h for j in range(GQA_GROUP) for h in (j, j + GQA_GROUP)]


def _t5_bucket_np(dist):
    n = np.maximum(dist, 0)
    max_exact = N_BUCKETS // 2
    nf = np.maximum(n, 1).astype(np.float32)
    large = max_exact + (np.log(nf / np.float32(max_exact)) / np.float32(math.log(MAX_DISTANCE / max_exact))
                         * np.float32(N_BUCKETS - max_exact)).astype(np.int32)
    large = np.minimum(large, N_BUCKETS - 1)
    return np.where(n < max_exact, n, large).astype(np.int32)


def _prompt_bucket_table():
    qi = np.arange(WINDOW)[:, None]
    c = np.arange(WINDOW)[None, :]
    dist = np.where(c > qi, qi + WINDOW - c, qi - c)
    return _t5_bucket_np(dist)


def _sample_bucket_table():
    t = (np.arange(SAMPLE_ROWS) // SEQ_GROUP)[:, None]
    b = (np.arange(SAMPLE_ROWS) % SEQ_GROUP)[:, None]
    col = np.arange(SAMPLE_KEYS)[None, :]
    is_state = col < STATE_KEYS
    slot = (col - STATE_KEYS) % NEW_SLOTS
    is_new = (col >= STATE_KEYS) & (col < STATE_KEYS + SEQ_GROUP * NEW_SLOTS) & (slot < DEC_SEQ)
    kb = np.where(is_state, col // WINDOW, (col - STATE_KEYS) // NEW_SLOTS)
    kpos = np.where(is_state, col % WINDOW, WINDOW + slot)
    dist = (WINDOW + t) - kpos
    valid = (is_state | is_new) & (kb == b) & (dist >= 0) & (dist < WINDOW)
    return np.where(valid, _t5_bucket_np(dist), -1).astype(np.int32)


def _rms(x, g):
    y = x * lax.rsqrt(jnp.mean(x * x, axis=-1, keepdims=True) + EPS)
    return y * g


LOG2E = math.log2(math.e)
Q_SCALE = HEAD_DIM ** -0.5 * LOG2E


def _bias_from_buckets(bucket, rb_ref, h):
    t = jnp.zeros(bucket.shape, F32)
    for bkt in range(N_BUCKETS):
        t = jnp.where(bucket == bkt, rb_ref[bkt, h] * LOG2E, t)
    return t


def _sink_softmax(s, sink):
    m = jnp.maximum(jnp.max(s, axis=-1, keepdims=True), sink)
    p = jnp.exp2(s - m)
    return p, jnp.sum(p, axis=-1, keepdims=True) + jnp.exp2(sink - m)


def _softmax_pv(s, valid, bias, sink, v):
    p, denom = _sink_softmax(jnp.where(valid, s + bias, NEG), sink)
    return jnp.dot(p.astype(BF16), v, preferred_element_type=F32) / denom


def _ffn_kernel(*refs, layer, half, tail_in, final_split, unroll_halves):
    refs = list(refs)
    x_ref = refs.pop(0)
    xt_ref = refs.pop(0) if tail_in else None
    g_ref, wg_hbm, wu_hbm, wd_hbm = refs[:4]
    del refs[:4]
    fg_ref = refs.pop(0) if final_split else None
    o_ref = refs.pop(0)
    ot_ref = refs.pop(0) if final_split else None
    wg, wu, wd, stage_in, stage_dn, sem = refs
    i = pl.program_id(0)

    def normed(x):
        return _rms(x, g_ref[...]).astype(BF16)

    def result(x, y):
        out = x + 0.5 * y
        return _rms(out, fg_ref[...]) if final_split else out

    def chunk_copies(c, slot):
        cols = pl.ds(c * FF_CHUNK, FF_CHUNK)
        return (
            pltpu.make_async_copy(wg_hbm.at[layer, half, :, cols], stage_in.at[0, slot], sem.at[0, slot]),
            pltpu.make_async_copy(wu_hbm.at[layer, half, :, cols], stage_in.at[1, slot], sem.at[1, slot]),
            pltpu.make_async_copy(wd_hbm.at[layer, half, cols, :], stage_dn.at[slot], sem.at[2, slot]),
        )

    @pl.when(i == 0)
    def _():
        for c in range(min(FF_SLOTS, N_FF_CHUNKS)):
            for cp in chunk_copies(c, c % FF_SLOTS):
                cp.start()
        if tail_in:
            x = jnp.concatenate([xt_ref[:, t, :] for t in range(DEC_SEQ)], axis=0)
        else:
            x = x_ref[0:TM, :]
        h = normed(x)
        acc = jnp.zeros((TM, D_MODEL), F32)
        for c in range(N_FF_CHUNKS):
            slot = c % FF_SLOTS
            cols = slice(c * FF_CHUNK, (c + 1) * FF_CHUNK)
            for cp in chunk_copies(c, slot):
                cp.wait()
            wg[:, cols] = stage_in[0, slot].astype(BF16)
            wu[:, cols] = stage_in[1, slot].astype(BF16)
            wd[cols, :] = stage_dn[slot].astype(BF16)
            if c + FF_SLOTS < N_FF_CHUNKS:
                for cp in chunk_copies(c + FF_SLOTS, slot):
                    cp.start()
            gate = jnp.dot(h, wg[:, cols], preferred_element_type=F32)
            up = jnp.dot(h, wu[:, cols], preferred_element_type=F32)
            act = (jax.nn.silu(gate) * up).astype(BF16)
            acc = acc + jnp.dot(act, wd[cols, :], preferred_element_type=F32)
        out = result(x, acc)
        if final_split:
            for t in range(DEC_SEQ):
                ot_ref[:, t, :] = out[t * DEC_BATCH:(t + 1) * DEC_BATCH, :]
        else:
            o_ref[0:TM, :] = out

    def half_block(rows):
        x = x_ref[rows, :]
        h = normed(x)
        gate = jnp.dot(h, wg[...], preferred_element_type=F32)
        up = jnp.dot(h, wu[...], preferred_element_type=F32)
        act = (jax.nn.silu(gate) * up).astype(BF16)
        o_ref[rows, :] = result(x, jnp.dot(act, wd[...], preferred_element_type=F32))

    @pl.when(i > 0)
    def _():
        if unroll_halves:
            for hf in range(FFN_TM // TM):
                half_block(slice(hf * TM, (hf + 1) * TM))
        else:
            def body(hf, carry):
                half_block(pl.ds(pl.multiple_of(hf * TM, TM), TM))
                return carry
            lax.fori_loop(0, FFN_TM // TM, body, 0)


def _resident(shape):
    return pl.BlockSpec(shape, lambda *_: (0,) * len(shape), pipeline_mode=pl.Buffered(1))


def _ffn(x_in, gain, wg, wu, wd, layer, half, *, tail_in=None, final_gain=None):
    hbm = pl.BlockSpec(memory_space=pl.ANY)
    row_block = pl.BlockSpec((FFN_TM, D_MODEL), lambda i: (jnp.where(i == 0, FFN_PROMPT_STEPS, i - 1), 0))
    prompt_block = pl.BlockSpec((FFN_TM, D_MODEL), lambda i: (jnp.maximum(i - 1, 0), 0))
    sample_block = pl.BlockSpec((DEC_BATCH, DEC_SEQ, D_MODEL), lambda i: (0, 0, 0))
    if tail_in is None:
        args, in_specs = [x_in], [row_block]
    else:
        args, in_specs = [x_in, tail_in], [prompt_block, sample_block]
    args += [gain.reshape(1, D_MODEL), wg, wu, wd]
    in_specs += [_resident((1, D_MODEL)), hbm, hbm, hbm]
    if final_gain is None:
        out_shape = jax.ShapeDtypeStruct((N_TOK, D_MODEL), F32)
        out_specs = row_block
    else:
        args.append(final_gain.reshape(1, D_MODEL))
        in_specs.append(_resident((1, D_MODEL)))
        out_shape = (jax.ShapeDtypeStruct((PROMPT_TOK, D_MODEL), F32),
                     jax.ShapeDtypeStruct((DEC_BATCH, DEC_SEQ, D_MODEL), F32))
        out_specs = (prompt_block, sample_block)
    aliases = {0: 0} if (tail_in is None and final_gain is None) else {}
    return pl.pallas_call(
        functools.partial(_ffn_kernel, layer=layer, half=half, tail_in=tail_in is not None,
                          final_split=final_gain is not None, unroll_halves=(layer == 0 or half == 1)),
        out_shape=out_shape,
        grid=(FFN_PROMPT_STEPS + 1,),
        in_specs=in_specs,
        out_specs=out_specs,
        scratch_shapes=[
            pltpu.VMEM((D_MODEL, D_FF), BF16),
            pltpu.VMEM((D_MODEL, D_FF), BF16),
            pltpu.VMEM((D_FF, D_MODEL), BF16),
            pltpu.VMEM((2, FF_SLOTS, D_MODEL, FF_CHUNK), F32),
            pltpu.VMEM((FF_SLOTS, FF_CHUNK, D_MODEL), F32),
            pltpu.SemaphoreType.DMA((3, FF_SLOTS)),
        ],
        input_output_aliases=aliases,
        compiler_params=pltpu.CompilerParams(
            dimension_semantics=("arbitrary",), vmem_limit_bytes=VMEM_LIMIT_BYTES),
        name="macaron_half",
    )(*args)


def _even_prompt_kernel(x_ref, g_ref, win_ref, wout_ref, wpool_ref, pscale_ref, bucket_ref, rb_ref, sink_ref,
                        o_ref, klast_ref, vlast_ref, plast_ref,
                        kbuf, vbuf, zb, s2b, s4b, s8b, bias_ref, cat_ref):
    b = pl.program_id(0)
    s = pl.program_id(1)
    last = pl.num_programs(1) - 1

    @pl.when((b == 0) & (s == 0))
    def _():
        bucket = bucket_ref[...]
        for h in range(N_HEADS_A):
            bias_ref[h] = _bias_from_buckets(bucket, rb_ref, h)

    @pl.when(s == 0)
    def _():
        for buf in (kbuf, vbuf):
            buf[0:WINDOW, :] = jnp.zeros((WINDOW, KV_WIDTH), BF16)
        zb[0:POOL_CTX, :] = jnp.zeros((POOL_CTX, POOL_WIDTH), F32)

    @pl.when(s > 0)
    def _():
        for buf in (kbuf, vbuf):
            buf[0:WINDOW, :] = buf[EVEN_TM:EVEN_TM + WINDOW, :]
        zb[0:POOL_CTX, :] = zb[EVEN_TM:EVEN_TM + POOL_CTX, :]

    x = x_ref[...]
    h = _rms(x, g_ref[...]).astype(BF16)
    proj = jnp.dot(h, win_ref[...], preferred_element_type=F32)
    q = proj[:, 0:ATTN_WIDTH] * Q_SCALE
    k = proj[:, ATTN_WIDTH:ATTN_WIDTH + KV_WIDTH]
    v = proj[:, ATTN_WIDTH + KV_WIDTH:ATTN_WIDTH + 2 * KV_WIDTH]
    up = proj[:, ATTN_WIDTH + 2 * KV_WIDTH:IN_EVEN]

    qlane = lax.broadcasted_iota(jnp.int32, (EVEN_TM, ATTN_WIDTH), 1) % LANES
    q_half = (jnp.where(qlane < HEAD_DIM, q, 0.0).astype(BF16),
              jnp.where(qlane >= HEAD_DIM, q, 0.0).astype(BF16))
    kbuf[WINDOW:, :] = k.astype(BF16)
    vbuf[WINDOW:, :] = v.astype(BF16)

    qrow = lax.broadcasted_iota(jnp.int32, (WINDOW, WINDOW), 0)
    kcol = lax.broadcasted_iota(jnp.int32, (WINDOW, WINDOW), 1)
    prev = kcol > qrow
    olane = lax.broadcasted_iota(jnp.int32, (WINDOW, LANES), 1)
    for i in range(QBLK_PER_TILE):
        rows = slice(i * WINDOW, (i + 1) * WINDOW)
        keys = slice(i * WINDOW, (i + 2) * WINDOW)
        qs = jnp.concatenate(
            [q_half[hd // GQA_GROUP][rows, (hd % GQA_GROUP) * LANES:(hd % GQA_GROUP + 1) * LANES]
             for hd in range(N_HEADS_A)], axis=0)
        sc_all = lax.dot_general(qs, kbuf[keys, :], (((1,), (1,)), ((), ())), preferred_element_type=F32)
        ps, denoms = [], []
        for hd in range(N_HEADS_A):
            sc = sc_all[hd * WINDOW:(hd + 1) * WINDOW, :]
            sc = jnp.where(prev, sc[:, 0:WINDOW], sc[:, WINDOW:]) + bias_ref[hd]
            if i == 0:
                sc = jnp.where(prev & (qrow >= jnp.where(s > 0, WINDOW, 0)), NEG, sc)
            p, denom = _sink_softmax(sc, sink_ref[hd] * LOG2E)
            ps.append(jnp.concatenate([jnp.where(prev, p, 0.0), jnp.where(prev, 0.0, p)], axis=1).astype(BF16))
            denoms.append(denom)
        o_all = jnp.dot(jnp.concatenate(ps, axis=0), vbuf[keys, :], preferred_element_type=F32)
        out = [o_all[hd * WINDOW:(hd + 1) * WINDOW, :] / denoms[hd] for hd in range(N_HEADS_A)]
        for j in range(GQA_GROUP):
            cat_ref[rows, j * LANES:(j + 1) * LANES] = jnp.where(
                olane < HEAD_DIM, out[j], out[j + GQA_GROUP]).astype(BF16)

    zb[POOL_CTX:, :] = up
    n = EVEN_TM + POOL_CTX
    s2b[8:, :] = zb[8:n, :] + zb[7:n - 1, :]
    s4b[16:, :] = s2b[16:n, POOL_GROUP:] + s2b[14:n - 2, POOL_GROUP:]
    s8b[24:, :] = s4b[24:n, POOL_GROUP:] + s4b[20:n - 4, POOL_GROUP:]
    s16 = s8b[32:n, POOL_GROUP:] + s8b[24:n - 8, POOL_GROUP:]
    sums = (s2b[POOL_CTX:, 0:POOL_GROUP], s4b[POOL_CTX:, 0:POOL_GROUP], s8b[POOL_CTX:, 0:POOL_GROUP], s16)
    pos1 = lax.broadcasted_iota(jnp.int32, (POOL_HEAD, POOL_GROUP), 0) + s * EVEN_TM + 1
    for g, w in enumerate(POOL_WINDOWS):
        lanes = slice(g * POOL_GROUP, (g + 1) * POOL_GROUP)
        cnt = jnp.minimum(pos1, w).astype(F32)
        mean = jnp.concatenate([sums[g][0:POOL_HEAD] / cnt, sums[g][POOL_HEAD:] * (1.0 / w)], axis=0)
        d = (mean - up[:, lanes]).astype(BF16)
        y = jnp.dot(d, wpool_ref[g], preferred_element_type=F32) * pscale_ref[:, lanes]
        cat_ref[:, ATTN_WIDTH + g * POOL_GROUP:ATTN_WIDTH + (g + 1) * POOL_GROUP] = y.astype(BF16)

    o_ref[...] = x + jnp.dot(cat_ref[...], wout_ref[...], preferred_element_type=F32)

    @pl.when(s == last)
    def _():
        klast_ref[0] = k[EVEN_TM - WINDOW:, :]
        vlast_ref[0] = v[EVEN_TM - WINDOW:, :]
        plast_ref[0] = up[EVEN_TM - 2 * SUBLANES:, :]


def _even_prompt(x, gain, w_in, w_out, w_pool, pool_scale, rel_bias, sinks):
    bucket = jnp.asarray(_prompt_bucket_table())
    smem = pl.BlockSpec(memory_space=pltpu.SMEM)
    return pl.pallas_call(
        _even_prompt_kernel,
        out_shape=(
            jax.ShapeDtypeStruct((N_TOK, D_MODEL), F32),
            jax.ShapeDtypeStruct((BATCH, WINDOW, KV_WIDTH), F32),
            jax.ShapeDtypeStruct((BATCH, WINDOW, KV_WIDTH), F32),
            jax.ShapeDtypeStruct((BATCH, 2 * SUBLANES, POOL_WIDTH), F32),
        ),
        grid=(BATCH, TILES_PER_SEQ),
        in_specs=[
            pl.BlockSpec((EVEN_TM, D_MODEL), lambda b, s: (b * TILES_PER_SEQ + s, 0)),
            _resident((1, D_MODEL)),
            _resident((D_MODEL, IN_EVEN)),
            _resident((ATTN_WIDTH + POOL_WIDTH, D_MODEL)),
            _resident((N_POOL_GROUPS, POOL_GROUP, POOL_GROUP)),
            _resident((1, POOL_WIDTH)),
            _resident((WINDOW, WINDOW)),
            smem,
            smem,
        ],
        out_specs=(
            pl.BlockSpec((EVEN_TM, D_MODEL), lambda b, s: (b * TILES_PER_SEQ + s, 0)),
            pl.BlockSpec((1, WINDOW, KV_WIDTH), lambda b, s: (b, 0, 0)),
            pl.BlockSpec((1, WINDOW, KV_WIDTH), lambda b, s: (b, 0, 0)),
            pl.BlockSpec((1, 2 * SUBLANES, POOL_WIDTH), lambda b, s: (b, 0, 0)),
        ),
        scratch_shapes=[
            pltpu.VMEM((WINDOW + EVEN_TM, KV_WIDTH), BF16),
            pltpu.VMEM((WINDOW + EVEN_TM, KV_WIDTH), BF16),
            pltpu.VMEM((POOL_CTX + EVEN_TM, POOL_WIDTH), F32),
            pltpu.VMEM((POOL_CTX + EVEN_TM, POOL_WIDTH), F32),
            pltpu.VMEM((POOL_CTX + EVEN_TM, POOL_WIDTH - POOL_GROUP), F32),
            pltpu.VMEM((POOL_CTX + EVEN_TM, POOL_WIDTH - 2 * POOL_GROUP), F32),
            pltpu.VMEM((N_HEADS_A, WINDOW, WINDOW), F32),
            pltpu.VMEM((EVEN_TM, ATTN_WIDTH + POOL_WIDTH), BF16),
        ],
        input_output_aliases={0: 0},
        compiler_params=pltpu.CompilerParams(
            dimension_semantics=("arbitrary", "arbitrary"), vmem_limit_bytes=EVEN_PROMPT_VMEM),
        name="even_mixer_prompt",
    )(x, gain.reshape(1, D_MODEL), w_in, w_out, w_pool, pool_scale.reshape(1, POOL_WIDTH), bucket,
      rel_bias, sinks)


def _even_sample_kernel(x_ref, g_ref, win_ref, wout_ref, wpool_ref, pscale_ref, bucket_ref, rb_ref, sink_ref,
                        sk_ref, sv_ref, spool_ref,
                        o_ref, nk_ref, nv_ref, npool_ref,
                        knew_ref, vnew_ref, upnew_ref, qsel, bias_ref, attn_f32, cat_ref):
    g = pl.program_id(0)
    last = pl.num_programs(0) - 1

    @pl.when(g == 0)
    def _():
        bucket = bucket_ref[...]
        for h in range(N_HEADS_A):
            bias_ref[h] = _bias_from_buckets(bucket, rb_ref, h)
        x = x_ref[...]
        h = _rms(x, g_ref[...]).astype(BF16)
        proj = jnp.dot(h, win_ref[...], preferred_element_type=F32)
        q = proj[:, 0:ATTN_WIDTH] * Q_SCALE
        zeros = jnp.zeros(((NEW_SLOTS - DEC_SEQ) * DEC_BATCH, KV_WIDTH), F32)
        knew_ref[0:SAMPLE_TOK, :] = proj[:, ATTN_WIDTH:ATTN_WIDTH + KV_WIDTH]
        vnew_ref[0:SAMPLE_TOK, :] = proj[:, ATTN_WIDTH + KV_WIDTH:ATTN_WIDTH + 2 * KV_WIDTH]
        knew_ref[SAMPLE_TOK:, :] = zeros
        vnew_ref[SAMPLE_TOK:, :] = zeros
        upnew_ref[...] = proj[:, ATTN_WIDTH + 2 * KV_WIDTH:IN_EVEN]
        qlane = lax.broadcasted_iota(jnp.int32, (SAMPLE_TOK, LANES), 1)
        for hd in range(N_HEADS_A):
            j, kvh = hd % GQA_GROUP, hd // GQA_GROUP
            keep = (qlane < HEAD_DIM) if kvh == 0 else (qlane >= HEAD_DIM)
            qsel[hd] = jnp.where(keep, q[:, j * LANES:(j + 1) * LANES], 0.0)

    lax.fori_loop(0, GROUPS_PER_STEP, functools.partial(
        _even_sample_group, g, bucket_ref, sink_ref, sk_ref, sv_ref, nk_ref, nv_ref,
        knew_ref, vnew_ref, qsel, bias_ref, attn_f32), 0)

    @pl.when(g == last)
    def _():
        cat_ref[:, 0:ATTN_WIDTH] = attn_f32[...].astype(BF16)
        for r in range(POOL_BUF):
            if r < POOL_BUF - DEC_SEQ:
                npool_ref[r] = spool_ref[r + DEC_SEQ]
            else:
                t = r - (POOL_BUF - DEC_SEQ)
                npool_ref[r] = upnew_ref[t * DEC_BATCH:(t + 1) * DEC_BATCH, :]
        for t in range(DEC_SEQ):
            rows = slice(t * DEC_BATCH, (t + 1) * DEC_BATCH)
            for gi, w in enumerate(POOL_WINDOWS):
                lanes = slice(gi * POOL_GROUP, (gi + 1) * POOL_GROUP)
                acc = upnew_ref[rows, lanes]
                for back in range(1, w):
                    tt = t - back
                    if tt >= 0:
                        acc = acc + upnew_ref[tt * DEC_BATCH:(tt + 1) * DEC_BATCH, lanes]
                    else:
                        acc = acc + spool_ref[POOL_BUF + tt, :, lanes]
                cnt = float(min(PAST_LEN + t + 1, w))
                d = (acc / cnt - upnew_ref[rows, lanes]).astype(BF16)
                y = jnp.dot(d, wpool_ref[gi], preferred_element_type=F32) * pscale_ref[:, lanes]
                cat_ref[rows, ATTN_WIDTH + gi * POOL_GROUP:ATTN_WIDTH + (gi + 1) * POOL_GROUP] = y.astype(BF16)
        o_ref[...] = x_ref[...] + jnp.dot(cat_ref[...], wout_ref[...], preferred_element_type=F32)


def _even_sample_group(g, bucket_ref, sink_ref, sk_ref, sv_ref, nk_ref, nv_ref,
                       knew_ref, vnew_ref, qsel, bias_ref, attn_f32, gg, carry):
    olane = lax.broadcasted_iota(jnp.int32, (SAMPLE_ROWS, LANES), 1)
    seq0 = gg * SEQ_GROUP
    row0 = pl.multiple_of((g * GROUPS_PER_STEP + gg) * SEQ_GROUP, SEQ_GROUP)
    pad = [jnp.zeros((WINDOW - SEQ_GROUP * NEW_SLOTS, KV_WIDTH), F32)]
    knew = jnp.concatenate(
        [knew_ref[pl.ds(row0 + b, NEW_SLOTS, stride=DEC_BATCH), :] for b in range(SEQ_GROUP)] + pad, axis=0)
    vnew = jnp.concatenate(
        [vnew_ref[pl.ds(row0 + b, NEW_SLOTS, stride=DEC_BATCH), :] for b in range(SEQ_GROUP)] + pad, axis=0)
    kt_all = jnp.concatenate([sk_ref[seq0 + b] for b in range(SEQ_GROUP)], axis=1).astype(BF16)
    vt_all = jnp.concatenate([sv_ref[seq0 + b] for b in range(SEQ_GROUP)], axis=1).astype(BF16)

    qg = jnp.concatenate(
        [qsel[hd, pl.ds(t * DEC_BATCH + row0, SEQ_GROUP), :] for hd in range(N_HEADS_A) for t in range(DEC_SEQ)],
        axis=0).astype(BF16)
    nt = (((1,), (1,)), ((), ()))
    sc = jnp.concatenate(
        [jnp.dot(qg, kt_all, preferred_element_type=F32),
         lax.dot_general(qg, knew.astype(BF16), nt, preferred_element_type=F32)], axis=1)
    valid = jnp.concatenate([bucket_ref[...]] * N_HEADS_A, axis=0) >= 0
    bias = bias_ref[...].reshape(N_HEADS_A * SAMPLE_ROWS, SAMPLE_KEYS)
    sink = jnp.concatenate(
        [jnp.full((SAMPLE_ROWS, 1), sink_ref[hd] * LOG2E, F32) for hd in range(N_HEADS_A)], axis=0)
    p, denom = _sink_softmax(jnp.where(valid, sc + bias, NEG), sink)
    p = p.astype(BF16)
    o_all = (lax.dot_general(p[:, 0:STATE_KEYS], vt_all, nt, preferred_element_type=F32)
             + jnp.dot(p[:, STATE_KEYS:], vnew.astype(BF16), preferred_element_type=F32)) / denom
    for j in range(GQA_GROUP):
        blk = jnp.where(olane < HEAD_DIM, o_all[j * SAMPLE_ROWS:(j + 1) * SAMPLE_ROWS, :],
                        o_all[(j + GQA_GROUP) * SAMPLE_ROWS:(j + GQA_GROUP + 1) * SAMPLE_ROWS, :])
        for t in range(DEC_SEQ):
            attn_f32[pl.ds(t * DEC_BATCH + row0, SEQ_GROUP), j * LANES:(j + 1) * LANES] = (
                blk[t * SEQ_GROUP:(t + 1) * SEQ_GROUP, :])

    lane = lax.broadcasted_iota(jnp.int32, (KV_WIDTH, WINDOW), 1)
    for new, old_ref, out_ref in ((knew, sk_ref, nk_ref), (vnew, sv_ref, nv_ref)):
        new_t = new.T
        for b in range(SEQ_GROUP):
            tail = pltpu.roll(new_t, (WINDOW - DEC_SEQ - b * NEW_SLOTS) % WINDOW, 1)
            out_ref[seq0 + b] = jnp.where(
                lane >= WINDOW - DEC_SEQ, tail, pltpu.roll(old_ref[seq0 + b], WINDOW - DEC_SEQ, 1))
    return carry


def _even_sample(x, gain, w_in, w_out, w_pool, pool_scale, rel_bias, sinks, state_k, state_v, state_pool_t):
    bucket = jnp.asarray(_sample_bucket_table())
    smem = pl.BlockSpec(memory_space=pltpu.SMEM)
    sample_block = PROMPT_TOK // SAMPLE_TOK
    return pl.pallas_call(
        _even_sample_kernel,
        out_shape=(
            jax.ShapeDtypeStruct((N_TOK, D_MODEL), F32),
            jax.ShapeDtypeStruct((DEC_BATCH, KV_WIDTH, WINDOW), F32),
            jax.ShapeDtypeStruct((DEC_BATCH, KV_WIDTH, WINDOW), F32),
            jax.ShapeDtypeStruct((POOL_BUF, DEC_BATCH, POOL_WIDTH), F32),
        ),
        grid=(N_SEQ_GROUPS // GROUPS_PER_STEP,),
        in_specs=[
            pl.BlockSpec((SAMPLE_TOK, D_MODEL), lambda g: (sample_block, 0)),
            _resident((1, D_MODEL)),
            _resident((D_MODEL, IN_EVEN)),
            _resident((ATTN_WIDTH + POOL_WIDTH, D_MODEL)),
            _resident((N_POOL_GROUPS, POOL_GROUP, POOL_GROUP)),
            _resident((1, POOL_WIDTH)),
            _resident((SAMPLE_ROWS, SAMPLE_KEYS)),
            smem,
            smem,
            pl.BlockSpec((GROUPS_PER_STEP * SEQ_GROUP, KV_WIDTH, WINDOW), lambda g: (g, 0, 0)),
            pl.BlockSpec((GROUPS_PER_STEP * SEQ_GROUP, KV_WIDTH, WINDOW), lambda g: (g, 0, 0)),
            _resident((POOL_BUF, DEC_BATCH, POOL_WIDTH)),
        ],
        out_specs=(
            pl.BlockSpec((SAMPLE_TOK, D_MODEL), lambda g: (sample_block, 0)),
            pl.BlockSpec((GROUPS_PER_STEP * SEQ_GROUP, KV_WIDTH, WINDOW), lambda g: (g, 0, 0)),
            pl.BlockSpec((GROUPS_PER_STEP * SEQ_GROUP, KV_WIDTH, WINDOW), lambda g: (g, 0, 0)),
            pl.BlockSpec((POOL_BUF, DEC_BATCH, POOL_WIDTH), lambda g: (0, 0, 0)),
        ),
        scratch_shapes=[
            pltpu.VMEM((NEW_SLOTS * DEC_BATCH, KV_WIDTH), F32),
            pltpu.VMEM((NEW_SLOTS * DEC_BATCH, KV_WIDTH), F32),
            pltpu.VMEM((SAMPLE_TOK, POOL_WIDTH), F32),
            pltpu.VMEM((N_HEADS_A, SAMPLE_TOK, LANES), F32),
            pltpu.VMEM((N_HEADS_A, SAMPLE_ROWS, SAMPLE_KEYS), F32),
            pltpu.VMEM((SAMPLE_TOK, ATTN_WIDTH), F32),
            pltpu.VMEM((SAMPLE_TOK, ATTN_WIDTH + POOL_WIDTH), BF16),
        ],
        input_output_aliases={0: 0},
        compiler_params=pltpu.CompilerParams(
            dimension_semantics=("arbitrary",), vmem_limit_bytes=EVEN_SAMPLE_VMEM),
        name="even_mixer_sample",
    )(x, gain.reshape(1, D_MODEL), w_in, w_out, w_pool, pool_scale.reshape(1, POOL_WIDTH), bucket,
      rel_bias, sinks, state_k, state_v, state_pool_t)


def _sgu_front(x, g_ref, win_ref, gv_ref):
    h = _rms(x, g_ref[...]).astype(BF16)
    uv = jax.nn.gelu(jnp.dot(h, win_ref[...], preferred_element_type=F32))
    return uv[:, 0:SGU_WIDTH], _rms(uv[:, SGU_WIDTH:], gv_ref[...])


def _odd_prompt_kernel(x_ref, g_ref, win_ref, gv_ref, ws_ref, bs_ref, wout_ref, o_ref, gated_ref):
    x = x_ref[...]
    u, v = _sgu_front(x, g_ref, win_ref, gv_ref)
    vb = v.astype(BF16)
    r = lax.broadcasted_iota(jnp.int32, (CHUNK, CHUNK), 0)
    c = lax.broadcasted_iota(jnp.int32, (CHUNK, CHUNK), 1)
    for gi in range(SGU_GROUPS):
        w = jnp.where(r >= c, ws_ref[gi], 0.0).astype(BF16)
        bias = jnp.concatenate([bs_ref[gi]] * (SGU_GROUP_W // LANES), axis=1)
        lanes = slice(gi * SGU_GROUP_W, (gi + 1) * SGU_GROUP_W)
        for ci in range(ODD_TM // CHUNK):
            rows = slice(ci * CHUNK, (ci + 1) * CHUNK)
            mixed = jnp.dot(w, vb[rows, lanes], preferred_element_type=F32) + bias
            gated_ref[rows, lanes] = (u[rows, lanes] * mixed).astype(BF16)
    o_ref[...] = x + jnp.dot(gated_ref[...], wout_ref[...], preferred_element_type=F32)


def _odd_prompt(x, gain, w_in, g_v, w_s, b_s, w_out):
    bias = jnp.broadcast_to(b_s[:, :, None], (SGU_GROUPS, CHUNK, LANES))
    return pl.pallas_call(
        _odd_prompt_kernel,
        out_shape=jax.ShapeDtypeStruct((N_TOK, D_MODEL), F32),
        grid=(PROMPT_TOK // ODD_TM,),
        in_specs=[
            pl.BlockSpec((ODD_TM, D_MODEL), lambda i: (i, 0)),
            _resident((1, D_MODEL)),
            _resident((D_MODEL, 2 * SGU_WIDTH)),
            _resident((1, SGU_WIDTH)),
            _resident((SGU_GROUPS, CHUNK, CHUNK)),
            _resident((SGU_GROUPS, CHUNK, LANES)),
            _resident((SGU_WIDTH, D_MODEL)),
        ],
        out_specs=pl.BlockSpec((ODD_TM, D_MODEL), lambda i: (i, 0)),
        scratch_shapes=[pltpu.VMEM((ODD_TM, SGU_WIDTH), BF16)],
        input_output_aliases={0: 0},
        compiler_params=pltpu.CompilerParams(
            dimension_semantics=("arbitrary",), vmem_limit_bytes=ODD_PROMPT_VMEM),
        name="odd_mixer_prompt",
    )(x, gain.reshape(1, D_MODEL), w_in, g_v.reshape(1, SGU_WIDTH), w_s, bias, w_out)


def _odd_sample_kernel(x_ref, g_ref, win_ref, gv_ref, coef_ref, bias_ref, wout_ref, o_ref, v_ref):
    x = x_ref[...]
    u, v = _sgu_front(x, g_ref, win_ref, gv_ref)
    for t in range(DEC_SEQ):
        v_ref[:, t, :] = v[t * DEC_BATCH:(t + 1) * DEC_BATCH, :]
    gated = []
    for t in range(DEC_SEQ):
        mixed = bias_ref[t:t + 1, :]
        for s in range(t + 1):
            mixed = mixed + coef_ref[t, s:s + 1, :] * v[s * DEC_BATCH:(s + 1) * DEC_BATCH, :]
        gated.append((u[t * DEC_BATCH:(t + 1) * DEC_BATCH, :] * mixed).astype(BF16))
    o_ref[...] = x + jnp.dot(jnp.concatenate(gated, axis=0), wout_ref[...], preferred_element_type=F32)


def _odd_sample(x, gain, w_in, g_v, w_s, b_s, w_out):
    coef = jnp.repeat(jnp.transpose(w_s[:, :DEC_SEQ, :DEC_SEQ], (1, 2, 0)), SGU_GROUP_W, axis=-1)
    bias = jnp.repeat(b_s[:, :DEC_SEQ].T, SGU_GROUP_W, axis=-1)
    sample_block = PROMPT_TOK // SAMPLE_TOK
    return pl.pallas_call(
        _odd_sample_kernel,
        out_shape=(
            jax.ShapeDtypeStruct((N_TOK, D_MODEL), F32),
            jax.ShapeDtypeStruct((DEC_BATCH, DEC_SEQ, SGU_WIDTH), F32),
        ),
        grid=(1,),
        in_specs=[
            pl.BlockSpec((SAMPLE_TOK, D_MODEL), lambda i: (sample_block, 0)),
            _resident((1, D_MODEL)),
            _resident((D_MODEL, 2 * SGU_WIDTH)),
            _resident((1, SGU_WIDTH)),
            _resident((DEC_SEQ, DEC_SEQ, SGU_WIDTH)),
            _resident((DEC_SEQ, SGU_WIDTH)),
            _resident((SGU_WIDTH, D_MODEL)),
        ],
        out_specs=(
            pl.BlockSpec((SAMPLE_TOK, D_MODEL), lambda i: (sample_block, 0)),
            pl.BlockSpec((DEC_BATCH, DEC_SEQ, SGU_WIDTH), lambda i: (0, 0, 0)),
        ),
        input_output_aliases={0: 0},
        compiler_params=pltpu.CompilerParams(
            dimension_semantics=("arbitrary",), vmem_limit_bytes=ODD_SAMPLE_VMEM),
        name="odd_mixer_sample",
    )(x, gain.reshape(1, D_MODEL), w_in, g_v.reshape(1, SGU_WIDTH), coef, bias, w_out)


def _window_to_lanes(w):
    return jnp.transpose(w, (0, 2, 3, 1)).reshape(DEC_BATCH, KV_WIDTH, WINDOW)


def _window_from_lanes(w):
    return jnp.transpose(w.reshape(DEC_BATCH, N_KV_A, HEAD_DIM, WINDOW), (0, 3, 1, 2))


def kernel(x_prompt, x_sample, state_win_k, state_win_v, state_pool, rel_bias, norm_gains, final_gain,
           ffn_gate, ffn_up, ffn_down, w_in_even, w_out_even, attn_sinks, w_pool, pool_scale,
           w_in_odd, sgu_norm, w_spatial, b_spatial, w_out_odd):
    xp = x_prompt.reshape(PROMPT_TOK, D_MODEL)

    kp_l, vp_l, pp_l, ks_l, vs_l, ps_l, sv_l = [], [], [], [], [], [], []
    x = None
    y_prompt = y_sample = None
    for l in range(DEPTH):
        fa = (norm_gains[l, 0], ffn_gate, ffn_up, ffn_down, l, 0)
        fb = (norm_gains[l, 2], ffn_gate, ffn_up, ffn_down, l, 1)
        if l == 0:
            x = _ffn(xp, *fa, tail_in=x_sample)
        else:
            x = _ffn(x, *fa)
        g1 = norm_gains[l, 1]
        if l % 2 == 0:
            e = l // 2
            wq = w_in_even[e][:, :ATTN_WIDTH].reshape(D_MODEL, N_KV_A, GQA_GROUP, HEAD_DIM)
            wq = jnp.swapaxes(wq, 1, 2).reshape(D_MODEL, ATTN_WIDTH)
            wo = w_out_even[e][:ATTN_WIDTH].reshape(N_KV_A, GQA_GROUP, HEAD_DIM, D_MODEL)
            wo = jnp.swapaxes(wo, 0, 1).reshape(ATTN_WIDTH, D_MODEL)
            w_in = jnp.concatenate([wq, w_in_even[e][:, ATTN_WIDTH:]], axis=1).astype(BF16)
            w_out = jnp.concatenate([wo, w_out_even[e][ATTN_WIDTH:]], axis=0).astype(BF16)
            wp = w_pool[e].astype(BF16)
            x, kp, vp, pp = _even_prompt(x, g1, w_in, w_out, wp, pool_scale[e], rel_bias, attn_sinks[e])
            x, nk, nv, npool = _even_sample(
                x, g1, w_in, w_out, wp, pool_scale[e], rel_bias, attn_sinks[e],
                _window_to_lanes(state_win_k[e]), _window_to_lanes(state_win_v[e]),
                jnp.swapaxes(state_pool[e], 0, 1))
            kp_l.append(kp.reshape(BATCH, WINDOW, N_KV_A, HEAD_DIM))
            vp_l.append(vp.reshape(BATCH, WINDOW, N_KV_A, HEAD_DIM))
            pp_l.append(pp[:, 2 * SUBLANES - POOL_BUF:, :])
            ks_l.append(_window_from_lanes(nk))
            vs_l.append(_window_from_lanes(nv))
            ps_l.append(jnp.swapaxes(npool, 0, 1))
        else:
            o = l // 2
            w_in, w_out = w_in_odd[o].astype(BF16), w_out_odd[o].astype(BF16)
            x = _odd_prompt(x, g1, w_in, sgu_norm[o], w_spatial[o], b_spatial[o], w_out)
            x, sv_new = _odd_sample(x, g1, w_in, sgu_norm[o], w_spatial[o], b_spatial[o], w_out)
            sv_l.append(sv_new)
        if l < DEPTH - 1:
            x = _ffn(x, *fb)
        else:
            y_prompt, y_sample = _ffn(x, *fb, final_gain=final_gain)

    return (y_prompt.reshape(BATCH, SEQ, D_MODEL), y_sample,
            jnp.stack(kp_l), jnp.stack(vp_l), jnp.stack(pp_l),
            jnp.stack(ks_l), jnp.stack(vs_l), jnp.stack(ps_l), jnp.stack(sv_l))
```

```python
import functools
import math

import numpy as np
import jax
import jax.numpy as jnp
from jax import lax
from jax.experimental import pallas as pl
from jax.experimental.pallas import tpu as pltpu

F32 = jnp.float32
BF16 = jnp.bfloat16

D_MODEL = 1024
BATCH = 8
SEQ = 2048
DEPTH = 2
DEC_BATCH = 128
DEC_SEQ = 4
PAST_LEN = 16384
N_HEADS_A = 8
N_KV_A = 2
HEAD_DIM = 64
GQA_GROUP = N_HEADS_A // N_KV_A
WINDOW = 128
ATTN_WIDTH = N_HEADS_A * HEAD_DIM
KV_WIDTH = N_KV_A * HEAD_DIM
N_BUCKETS = 32
MAX_DISTANCE = 128
POOL_WINDOWS = (2, 4, 8, 16)
N_POOL_GROUPS = len(POOL_WINDOWS)
POOL_GROUP = 128
POOL_WIDTH = N_POOL_GROUPS * POOL_GROUP
POOL_BUF = max(POOL_WINDOWS) - 1
IN_EVEN = ATTN_WIDTH + 2 * KV_WIDTH + POOL_WIDTH
CHUNK = 128
SGU_WIDTH = 1024
SGU_GROUPS = 4
SGU_GROUP_W = SGU_WIDTH // SGU_GROUPS
D_FF = 2816
EPS = 1e-6
NEG = -1e30

LANES = 128
SUBLANES = 8
MIB = 1024 * 1024
VMEM_LIMIT_BYTES = 56 * MIB
EVEN_PROMPT_VMEM = 42 * MIB
EVEN_SAMPLE_VMEM = 44 * MIB
ODD_PROMPT_VMEM = 36 * MIB
ODD_SAMPLE_VMEM = 20 * MIB

PROMPT_TOK = BATCH * SEQ
SAMPLE_TOK = DEC_BATCH * DEC_SEQ
N_TOK = PROMPT_TOK + SAMPLE_TOK

TM = SAMPLE_TOK
EVEN_TM = 1024
assert SEQ % EVEN_TM == 0 and EVEN_TM % WINDOW == 0
TILES_PER_SEQ = SEQ // EVEN_TM
QBLK_PER_TILE = EVEN_TM // WINDOW
FFN_TM = 2 * TM
assert PROMPT_TOK % FFN_TM == 0
FFN_PROMPT_STEPS = PROMPT_TOK // FFN_TM
ODD_TM = 1024
assert PROMPT_TOK % ODD_TM == 0 and ODD_TM % CHUNK == 0
FF_CHUNK = 256
N_FF_CHUNKS = D_FF // FF_CHUNK
FF_SLOTS = 3
POOL_HEAD = 2 * SUBLANES
assert all(w & (w - 1) == 0 and w <= POOL_HEAD for w in POOL_WINDOWS)
POOL_CTX = 32
SEQ_GROUP = 8
N_SEQ_GROUPS = DEC_BATCH // SEQ_GROUP
GROUPS_PER_STEP = 4
assert N_SEQ_GROUPS % GROUPS_PER_STEP == 0
STATE_KEYS = SEQ_GROUP * WINDOW
NEW_SLOTS = SUBLANES
assert DEC_SEQ <= NEW_SLOTS and SEQ_GROUP * NEW_SLOTS <= WINDOW
SAMPLE_KEYS = STATE_KEYS + WINDOW
SAMPLE_ROWS = DEC_SEQ * SEQ_GROUP


assert N_KV_A == 2 and KV_WIDTH == LANES
HEAD_ORDER = [h for j in range(GQA_GROUP) for h in (j, j + GQA_GROUP)]


def _t5_bucket_np(dist):
    n = np.maximum(dist, 0)
    max_exact = N_BUCKETS // 2
    nf = np.maximum(n, 1).astype(np.float32)
    large = max_exact + (np.log(nf / np.float32(max_exact)) / np.float32(math.log(MAX_DISTANCE / max_exact))
                         * np.float32(N_BUCKETS - max_exact)).astype(np.int32)
    large = np.minimum(large, N_BUCKETS - 1)
    return np.where(n < max_exact, n, large).astype(np.int32)


def _prompt_bucket_table():
    qi = np.arange(WINDOW)[:, None]
    c = np.arange(WINDOW)[None, :]
    dist = np.where(c > qi, qi + WINDOW - c, qi - c)
    return _t5_bucket_np(dist)


def _sample_bucket_table():
    t = (np.arange(SAMPLE_ROWS) // SEQ_GROUP)[:, None]
    b = (np.arange(SAMPLE_ROWS) % SEQ_GROUP)[:, None]
    col = np.arange(SAMPLE_KEYS)[None, :]
    is_state = col < STATE_KEYS
    slot = (col - STATE_KEYS) % NEW_SLOTS
    is_new = (col >= STATE_KEYS) & (col < STATE_KEYS + SEQ_GROUP * NEW_SLOTS) & (slot < DEC_SEQ)
    kb = np.where(is_state, col // WINDOW, (col - STATE_KEYS) // NEW_SLOTS)
    kpos = np.where(is_state, col % WINDOW, WINDOW + slot)
    dist = (WINDOW + t) - kpos
    valid = (is_state | is_new) & (kb == b) & (dist >= 0) & (dist < WINDOW)
    return np.where(valid, _t5_bucket_np(dist), -1).astype(np.int32)


def _rms(x, g):
    y = x * lax.rsqrt(jnp.mean(x * x, axis=-1, keepdims=True) + EPS)
    return y * g


LOG2E = math.log2(math.e)
Q_SCALE = HEAD_DIM ** -0.5 * LOG2E


def _bias_from_buckets(bucket, rb_ref, h):
    t = jnp.zeros(bucket.shape, F32)
    for bkt in range(N_BUCKETS):
        t = jnp.where(bucket == bkt, rb_ref[bkt, h] * LOG2E, t)
    return t


def _sink_softmax(s, sink):
    m = jnp.maximum(jnp.max(s, axis=-1, keepdims=True), sink)
    p = jnp.exp2(s - m)
    return p, jnp.sum(p, axis=-1, keepdims=True) + jnp.exp2(sink - m)


def _softmax_pv(s, valid, bias, sink, v):
    p, denom = _sink_softmax(jnp.where(valid, s + bias, NEG), sink)
    return jnp.dot(p.astype(BF16), v, preferred_element_type=F32) / denom


def _ffn_kernel(*refs, layer, half, tail_in, final_split, unroll_halves):
    refs = list(refs)
    x_ref = refs.pop(0)
    xt_ref = refs.pop(0) if tail_in else None
    g_ref, wg_hbm, wu_hbm, wd_hbm = refs[:4]
    del refs[:4]
    fg_ref = refs.pop(0) if final_split else None
    o_ref = refs.pop(0)
    ot_ref = refs.pop(0) if final_split else None
    wg, wu, wd, stage_in, stage_dn, sem = refs
    i = pl.program_id(0)

    def normed(x):
        return _rms(x, g_ref[...]).astype(BF16)

    def result(x, y):
        out = x + 0.5 * y
        return _rms(out, fg_ref[...]) if final_split else out

    def chunk_copies(c, slot):
        cols = pl.ds(c * FF_CHUNK, FF_CHUNK)
        return (
            pltpu.make_async_copy(wg_hbm.at[layer, half, :, cols], stage_in.at[0, slot], sem.at[0, slot]),
            pltpu.make_async_copy(wu_hbm.at[layer, half, :, cols], stage_in.at[1, slot], sem.at[1, slot]),
            pltpu.make_async_copy(wd_hbm.at[layer, half, cols, :], stage_dn.at[slot], sem.at[2, slot]),
        )

    @pl.when(i == 0)
    def _():
        for c in range(min(FF_SLOTS, N_FF_CHUNKS)):
            for cp in chunk_copies(c, c % FF_SLOTS):
                cp.start()
        if tail_in:
            x = jnp.concatenate([xt_ref[:, t, :] for t in range(DEC_SEQ)], axis=0)
        else:
            x = x_ref[0:TM, :]
        h = normed(x)
        acc = jnp.zeros((TM, D_MODEL), F32)
        for c in range(N_FF_CHUNKS):
            slot = c % FF_SLOTS
            cols = slice(c * FF_CHUNK, (c + 1) * FF_CHUNK)
            for cp in chunk_copies(c, slot):
                cp.wait()
            wg[:, cols] = stage_in[0, slot].astype(BF16)
            wu[:, cols] = stage_in[1, slot].astype(BF16)
            wd[cols, :] = stage_dn[slot].astype(BF16)
            if c + FF_SLOTS < N_FF_CHUNKS:
                for cp in chunk_copies(c + FF_SLOTS, slot):
                    cp.start()
            gate = jnp.dot(h, wg[:, cols], preferred_element_type=F32)
            up = jnp.dot(h, wu[:, cols], preferred_element_type=F32)
            act = (jax.nn.silu(gate) * up).astype(BF16)
            acc = acc + jnp.dot(act, wd[cols, :], preferred_element_type=F32)
        out = result(x, acc)
        if final_split:
            for t in range(DEC_SEQ):
                ot_ref[:, t, :] = out[t * DEC_BATCH:(t + 1) * DEC_BATCH, :]
        else:
            o_ref[0:TM, :] = out

    def half_block(rows):
        x = x_ref[rows, :]
        h = normed(x)
        gate = jnp.dot(h, wg[...], preferred_element_type=F32)
        up = jnp.dot(h, wu[...], preferred_element_type=F32)
        act = (jax.nn.silu(gate) * up).astype(BF16)
        o_ref[rows, :] = result(x, jnp.dot(act, wd[...], preferred_element_type=F32))

    @pl.when(i > 0)
    def _():
        if unroll_halves:
            for hf in range(FFN_TM // TM):
                half_block(slice(hf * TM, (hf + 1) * TM))
        else:
            def body(hf, carry):
                half_block(pl.ds(pl.multiple_of(hf * TM, TM), TM))
                return carry
            lax.fori_loop(0, FFN_TM // TM, body, 0)


def _resident(shape):
    return pl.BlockSpec(shape, lambda *_: (0,) * len(shape), pipeline_mode=pl.Buffered(1))


def _ffn(x_in, gain, wg, wu, wd, layer, half, *, tail_in=None, final_gain=None):
    hbm = pl.BlockSpec(memory_space=pl.ANY)
    row_block = pl.BlockSpec((FFN_TM, D_MODEL), lambda i: (jnp.where(i == 0, FFN_PROMPT_STEPS, i - 1), 0))
    prompt_block = pl.BlockSpec((FFN_TM, D_MODEL), lambda i: (jnp.maximum(i - 1, 0), 0))
    sample_block = pl.BlockSpec((DEC_BATCH, DEC_SEQ, D_MODEL), lambda i: (0, 0, 0))
    if tail_in is None:
        args, in_specs = [x_in], [row_block]
    else:
        args, in_specs = [x_in, tail_in], [prompt_block, sample_block]
    args += [gain.reshape(1, D_MODEL), wg, wu, wd]
    in_specs += [_resident((1, D_MODEL)), hbm, hbm, hbm]
    if final_gain is None:
        out_shape = jax.ShapeDtypeStruct((N_TOK, D_MODEL), F32)
        out_specs = row_block
    else:
        args.append(final_gain.reshape(1, D_MODEL))
        in_specs.append(_resident((1, D_MODEL)))
        out_shape = (jax.ShapeDtypeStruct((PROMPT_TOK, D_MODEL), F32),
                     jax.ShapeDtypeStruct((DEC_BATCH, DEC_SEQ, D_MODEL), F32))
        out_specs = (prompt_block, sample_block)
    aliases = {0: 0} if (tail_in is None and final_gain is None) else {}
    return pl.pallas_call(
        functools.partial(_ffn_kernel, layer=layer, half=half, tail_in=tail_in is not None,
                          final_split=final_gain is not None, unroll_halves=(layer == 0 or half == 1)),
        out_shape=out_shape,
        grid=(FFN_PROMPT_STEPS + 1,),
        in_specs=in_specs,
        out_specs=out_specs,
        scratch_shapes=[
            pltpu.VMEM((D_MODEL, D_FF), BF16),
            pltpu.VMEM((D_MODEL, D_FF), BF16),
            pltpu.VMEM((D_FF, D_MODEL), BF16),
            pltpu.VMEM((2, FF_SLOTS, D_MODEL, FF_CHUNK), F32),
            pltpu.VMEM((FF_SLOTS, FF_CHUNK, D_MODEL), F32),
            pltpu.SemaphoreType.DMA((3, FF_SLOTS)),
        ],
        input_output_aliases=aliases,
        compiler_params=pltpu.CompilerParams(
            dimension_semantics=("arbitrary",), vmem_limit_bytes=VMEM_LIMIT_BYTES),
        name="macaron_half",
    )(*args)


def _even_prompt_kernel(x_ref, g_ref, win_ref, wout_ref, wpool_ref, pscale_ref, bucket_ref, rb_ref, sink_ref,
                        o_ref, klast_ref, vlast_ref, plast_ref,
                        kbuf, vbuf, zb, s2b, s4b, s8b, bias_ref, cat_ref):
    b = pl.program_id(0)
    s = pl.program_id(1)
    last = pl.num_programs(1) - 1

    @pl.when((b == 0) & (s == 0))
    def _():
        bucket = bucket_ref[...]
        for h in range(N_HEADS_A):
            bias_ref[h] = _bias_from_buckets(bucket, rb_ref, h)

    @pl.when(s == 0)
    def _():
        for buf in (kbuf, vbuf):
            buf[0:WINDOW, :] = jnp.zeros((WINDOW, KV_WIDTH), BF16)
        zb[0:POOL_CTX, :] = jnp.zeros((POOL_CTX, POOL_WIDTH), F32)

    @pl.when(s > 0)
    def _():
        for buf in (kbuf, vbuf):
            buf[0:WINDOW, :] = buf[EVEN_TM:EVEN_TM + WINDOW, :]
        zb[0:POOL_CTX, :] = zb[EVEN_TM:EVEN_TM + POOL_CTX, :]

    x = x_ref[...]
    h = _rms(x, g_ref[...]).astype(BF16)
    proj = jnp.dot(h, win_ref[...], preferred_element_type=F32)
    q = proj[:, 0:ATTN_WIDTH] * Q_SCALE
    k = proj[:, ATTN_WIDTH:ATTN_WIDTH + KV_WIDTH]
    v = proj[:, ATTN_WIDTH + KV_WIDTH:ATTN_WIDTH + 2 * KV_WIDTH]
    up = proj[:, ATTN_WIDTH + 2 * KV_WIDTH:IN_EVEN]

    qlane = lax.broadcasted_iota(jnp.int32, (EVEN_TM, ATTN_WIDTH), 1) % LANES
    q_half = (jnp.where(qlane < HEAD_DIM, q, 0.0).astype(BF16),
              jnp.where(qlane >= HEAD_DIM, q, 0.0).astype(BF16))
    kbuf[WINDOW:, :] = k.astype(BF16)
    vbuf[WINDOW:, :] = v.astype(BF16)

    qrow = lax.broadcasted_iota(jnp.int32, (WINDOW, WINDOW), 0)
    kcol = lax.broadcasted_iota(jnp.int32, (WINDOW, WINDOW), 1)
    prev = kcol > qrow
    olane = lax.broadcasted_iota(jnp.int32, (WINDOW, LANES), 1)
    for i in range(QBLK_PER_TILE):
        rows = slice(i * WINDOW, (i + 1) * WINDOW)
        keys = slice(i * WINDOW, (i + 2) * WINDOW)
        qs = jnp.concatenate(
            [q_half[hd // GQA_GROUP][rows, (hd % GQA_GROUP) * LANES:(hd % GQA_GROUP + 1) * LANES]
             for hd in range(N_HEADS_A)], axis=0)
        sc_all = lax.dot_general(qs, kbuf[keys, :], (((1,), (1,)), ((), ())), preferred_element_type=F32)
        ps, denoms = [], []
        for hd in range(N_HEADS_A):
            sc = sc_all[hd * WINDOW:(hd + 1) * WINDOW, :]
            sc = jnp.where(prev, sc[:, 0:WINDOW], sc[:, WINDOW:]) + bias_ref[hd]
            if i == 0:
                sc = jnp.where(prev & (qrow >= jnp.where(s > 0, WINDOW, 0)), NEG, sc)
            p, denom = _sink_softmax(sc, sink_ref[hd] * LOG2E)
            ps.append(jnp.concatenate([jnp.where(prev, p, 0.0), jnp.where(prev, 0.0, p)], axis=1).astype(BF16))
            denoms.append(denom)
        o_all = jnp.dot(jnp.concatenate(ps, axis=0), vbuf[keys, :], preferred_element_type=F32)
        out = [o_all[hd * WINDOW:(hd + 1) * WINDOW, :] / denoms[hd] for hd in range(N_HEADS_A)]
        for j in range(GQA_GROUP):
            cat_ref[rows, j * LANES:(j + 1) * LANES] = jnp.where(
                olane < HEAD_DIM, out[j], out[j + GQA_GROUP]).astype(BF16)

    zb[POOL_CTX:, :] = up
    n = EVEN_TM + POOL_CTX
    s2b[8:, :] = zb[8:n, :] + zb[7:n - 1, :]
    s4b[16:, :] = s2b[16:n, POOL_GROUP:] + s2b[14:n - 2, POOL_GROUP:]
    s8b[24:, :] = s4b[24:n, POOL_GROUP:] + s4b[20:n - 4, POOL_GROUP:]
    s16 = s8b[32:n, POOL_GROUP:] + s8b[24:n - 8, POOL_GROUP:]
    sums = (s2b[POOL_CTX:, 0:POOL_GROUP], s4b[POOL_CTX:, 0:POOL_GROUP], s8b[POOL_CTX:, 0:POOL_GROUP], s16)
    pos1 = lax.broadcasted_iota(jnp.int32, (POOL_HEAD, POOL_GROUP), 0) + s * EVEN_TM + 1
    for g, w in enumerate(POOL_WINDOWS):
        lanes = slice(g * POOL_GROUP, (g + 1) * POOL_GROUP)
        cnt = jnp.minimum(pos1, w).astype(F32)
        mean = jnp.concatenate([sums[g][0:POOL_HEAD] / cnt, sums[g][POOL_HEAD:] * (1.0 / w)], axis=0)
        d = (mean - up[:, lanes]).astype(BF16)
        y = jnp.dot(d, wpool_ref[g], preferred_element_type=F32) * pscale_ref[:, lanes]
        cat_ref[:, ATTN_WIDTH + g * POOL_GROUP:ATTN_WIDTH + (g + 1) * POOL_GROUP] = y.astype(BF16)

    o_ref[...] = x + jnp.dot(cat_ref[...], wout_ref[...], preferred_element_type=F32)

    @pl.when(s == last)
    def _():
        klast_ref[0] = k[EVEN_TM - WINDOW:, :]
        vlast_ref[0] = v[EVEN_TM - WINDOW:, :]
        plast_ref[0] = up[EVEN_TM - 2 * SUBLANES:, :]


def _even_prompt(x, gain, w_in, w_out, w_pool, pool_scale, rel_bias, sinks):
    bucket = jnp.asarray(_prompt_bucket_table())
    smem = pl.BlockSpec(memory_space=pltpu.SMEM)
    return pl.pallas_call(
        _even_prompt_kernel,
        out_shape=(
            jax.ShapeDtypeStruct((N_TOK, D_MODEL), F32),
            jax.ShapeDtypeStruct((BATCH, WINDOW, KV_WIDTH), F32),
            jax.ShapeDtypeStruct((BATCH, WINDOW, KV_WIDTH), F32),
            jax.ShapeDtypeStruct((BATCH, 2 * SUBLANES, POOL_WIDTH), F32),
        ),
        grid=(BATCH, TILES_PER_SEQ),
        in_specs=[
            pl.BlockSpec((EVEN_TM, D_MODEL), lambda b, s: (b * TILES_PER_SEQ + s, 0)),
            _resident((1, D_MODEL)),
            _resident((D_MODEL, IN_EVEN)),
            _resident((ATTN_WIDTH + POOL_WIDTH, D_MODEL)),
            _resident((N_POOL_GROUPS, POOL_GROUP, POOL_GROUP)),
            _resident((1, POOL_WIDTH)),
            _resident((WINDOW, WINDOW)),
            smem,
            smem,
        ],
        out_specs=(
            pl.BlockSpec((EVEN_TM, D_MODEL), lambda b, s: (b * TILES_PER_SEQ + s, 0)),
            pl.BlockSpec((1, WINDOW, KV_WIDTH), lambda b, s: (b, 0, 0)),
            pl.BlockSpec((1, WINDOW, KV_WIDTH), lambda b, s: (b, 0, 0)),
            pl.BlockSpec((1, 2 * SUBLANES, POOL_WIDTH), lambda b, s: (b, 0, 0)),
        ),
        scratch_shapes=[
            pltpu.VMEM((WINDOW + EVEN_TM, KV_WIDTH), BF16),
            pltpu.VMEM((WINDOW + EVEN_TM, KV_WIDTH), BF16),
            pltpu.VMEM((POOL_CTX + EVEN_TM, POOL_WIDTH), F32),
            pltpu.VMEM((POOL_CTX + EVEN_TM, POOL_WIDTH), F32),
            pltpu.VMEM((POOL_CTX + EVEN_TM, POOL_WIDTH - POOL_GROUP), F32),
            pltpu.VMEM((POOL_CTX + EVEN_TM, POOL_WIDTH - 2 * POOL_GROUP), F32),
            pltpu.VMEM((N_HEADS_A, WINDOW, WINDOW), F32),
            pltpu.VMEM((EVEN_TM, ATTN_WIDTH + POOL_WIDTH), BF16),
        ],
        input_output_aliases={0: 0},
        compiler_params=pltpu.CompilerParams(
            dimension_semantics=("arbitrary", "arbitrary"), vmem_limit_bytes=EVEN_PROMPT_VMEM),
        name="even_mixer_prompt",
    )(x, gain.reshape(1, D_MODEL), w_in, w_out, w_pool, pool_scale.reshape(1, POOL_WIDTH), bucket,
      rel_bias, sinks)


def _even_sample_kernel(x_ref, g_ref, win_ref, wout_ref, wpool_ref, pscale_ref, bucket_ref, rb_ref, sink_ref,
                        sk_ref, sv_ref, spool_ref,
                        o_ref, nk_ref, nv_ref, npool_ref,
                        knew_ref, vnew_ref, upnew_ref, qsel, bias_ref, attn_f32, cat_ref):
    g = pl.program_id(0)
    last = pl.num_programs(0) - 1

    @pl.when(g == 0)
    def _():
        bucket = bucket_ref[...]
        for h in range(N_HEADS_A):
            bias_ref[h] = _bias_from_buckets(bucket, rb_ref, h)
        x = x_ref[...]
        h = _rms(x, g_ref[...]).astype(BF16)
        proj = jnp.dot(h, win_ref[...], preferred_element_type=F32)
        q = proj[:, 0:ATTN_WIDTH] * Q_SCALE
        zeros = jnp.zeros(((NEW_SLOTS - DEC_SEQ) * DEC_BATCH, KV_WIDTH), F32)
        knew_ref[0:SAMPLE_TOK, :] = proj[:, ATTN_WIDTH:ATTN_WIDTH + KV_WIDTH]
        vnew_ref[0:SAMPLE_TOK, :] = proj[:, ATTN_WIDTH + KV_WIDTH:ATTN_WIDTH + 2 * KV_WIDTH]
        knew_ref[SAMPLE_TOK:, :] = zeros
        vnew_ref[SAMPLE_TOK:, :] = zeros
        upnew_ref[...] = proj[:, ATTN_WIDTH + 2 * KV_WIDTH:IN_EVEN]
        qlane = lax.broadcasted_iota(jnp.int32, (SAMPLE_TOK, LANES), 1)
        for hd in range(N_HEADS_A):
            j, kvh = hd % GQA_GROUP, hd // GQA_GROUP
            keep = (qlane < HEAD_DIM) if kvh == 0 else (qlane >= HEAD_DIM)
            qsel[hd] = jnp.where(keep, q[:, j * LANES:(j + 1) * LANES], 0.0)

    lax.fori_loop(0, GROUPS_PER_STEP, functools.partial(
        _even_sample_group, g, bucket_ref, sink_ref, sk_ref, sv_ref, nk_ref, nv_ref,
        knew_ref, vnew_ref, qsel, bias_ref, attn_f32), 0)

    @pl.when(g == last)
    def _():
        cat_ref[:, 0:ATTN_WIDTH] = attn_f32[...].astype(BF16)
        for r in range(POOL_BUF):
            if r < POOL_BUF - DEC_SEQ:
                npool_ref[r] = spool_ref[r + DEC_SEQ]
            else:
                t = r - (POOL_BUF - DEC_SEQ)
                npool_ref[r] = upnew_ref[t * DEC_BATCH:(t + 1) * DEC_BATCH, :]
        for t in range(DEC_SEQ):
            rows = slice(t * DEC_BATCH, (t + 1) * DEC_BATCH)
            for gi, w in enumerate(POOL_WINDOWS):
                lanes = slice(gi * POOL_GROUP, (gi + 1) * POOL_GROUP)
                acc = upnew_ref[rows, lanes]
                for back in range(1, w):
                    tt = t - back
                    if tt >= 0:
                        acc = acc + upnew_ref[tt * DEC_BATCH:(tt + 1) * DEC_BATCH, lanes]
                    else:
                        acc = acc + spool_ref[POOL_BUF + tt, :, lanes]
                cnt = float(min(PAST_LEN + t + 1, w))
                d = (acc / cnt - upnew_ref[rows, lanes]).astype(BF16)
                y = jnp.dot(d, wpool_ref[gi], preferred_element_type=F32) * pscale_ref[:, lanes]
                cat_ref[rows, ATTN_WIDTH + gi * POOL_GROUP:ATTN_WIDTH + (gi + 1) * POOL_GROUP] = y.astype(BF16)
        o_ref[...] = x_ref[...] + jnp.dot(cat_ref[...], wout_ref[...], preferred_element_type=F32)


def _even_sample_group(g, bucket_ref, sink_ref, sk_ref, sv_ref, nk_ref, nv_ref,
                       knew_ref, vnew_ref, qsel, bias_ref, attn_f32, gg, carry):
    olane = lax.broadcasted_iota(jnp.int32, (SAMPLE_ROWS, LANES), 1)
    seq0 = gg * SEQ_GROUP
    row0 = pl.multiple_of((g * GROUPS_PER_STEP + gg) * SEQ_GROUP, SEQ_GROUP)
    pad = [jnp.zeros((WINDOW - SEQ_GROUP * NEW_SLOTS, KV_WIDTH), F32)]
    knew = jnp.concatenate(
        [knew_ref[pl.ds(row0 + b, NEW_SLOTS, stride=DEC_BATCH), :] for b in range(SEQ_GROUP)] + pad, axis=0)
    vnew = jnp.concatenate(
        [vnew_ref[pl.ds(row0 + b, NEW_SLOTS, stride=DEC_BATCH), :] for b in range(SEQ_GROUP)] + pad, axis=0)
    kt_all = jnp.concatenate([sk_ref[seq0 + b] for b in range(SEQ_GROUP)], axis=1).astype(BF16)
    vt_all = jnp.concatenate([sv_ref[seq0 + b] for b in range(SEQ_GROUP)], axis=1).astype(BF16)

    qg = jnp.concatenate(
        [qsel[hd, pl.ds(t * DEC_BATCH + row0, SEQ_GROUP), :] for hd in range(N_HEADS_A) for t in range(DEC_SEQ)],
        axis=0).astype(BF16)
    nt = (((1,), (1,)), ((), ()))
    sc = jnp.concatenate(
        [jnp.dot(qg, kt_all, preferred_element_type=F32),
         lax.dot_general(qg, knew.astype(BF16), nt, preferred_element_type=F32)], axis=1)
    valid = jnp.concatenate([bucket_ref[...]] * N_HEADS_A, axis=0) >= 0
    bias = bias_ref[...].reshape(N_HEADS_A * SAMPLE_ROWS, SAMPLE_KEYS)
    sink = jnp.concatenate(
        [jnp.full((SAMPLE_ROWS, 1), sink_ref[hd] * LOG2E, F32) for hd in range(N_HEADS_A)], axis=0)
    p, denom = _sink_softmax(jnp.where(valid, sc + bias, NEG), sink)
    p = p.astype(BF16)
    o_all = (lax.dot_general(p[:, 0:STATE_KEYS], vt_all, nt, preferred_element_type=F32)
             + jnp.dot(p[:, STATE_KEYS:], vnew.astype(BF16), preferred_element_type=F32)) / denom
    for j in range(GQA_GROUP):
        blk = jnp.where(olane < HEAD_DIM, o_all[j * SAMPLE_ROWS:(j + 1) * SAMPLE_ROWS, :],
                        o_all[(j + GQA_GROUP) * SAMPLE_ROWS:(j + GQA_GROUP + 1) * SAMPLE_ROWS, :])
        for t in range(DEC_SEQ):
            attn_f32[pl.ds(t * DEC_BATCH + row0, SEQ_GROUP), j * LANES:(j + 1) * LANES] = (
                blk[t * SEQ_GROUP:(t + 1) * SEQ_GROUP, :])

    lane = lax.broadcasted_iota(jnp.int32, (KV_WIDTH, WINDOW), 1)
    for new, old_ref, out_ref in ((knew, sk_ref, nk_ref), (vnew, sv_ref, nv_ref)):
        new_t = new.T
        for b in range(SEQ_GROUP):
            tail = pltpu.roll(new_t, (WINDOW - DEC_SEQ - b * NEW_SLOTS) % WINDOW, 1)
            out_ref[seq0 + b] = jnp.where(
                lane >= WINDOW - DEC_SEQ, tail, pltpu.roll(old_ref[seq0 + b], WINDOW - DEC_SEQ, 1))
    return carry


def _even_sample(x, gain, w_in, w_out, w_pool, pool_scale, rel_bias, sinks, state_k, state_v, state_pool_t):
    bucket = jnp.asarray(_sample_bucket_table())
    smem = pl.BlockSpec(memory_space=pltpu.SMEM)
    sample_block = PROMPT_TOK // SAMPLE_TOK
    return pl.pallas_call(
        _even_sample_kernel,
        out_shape=(
            jax.ShapeDtypeStruct((N_TOK, D_MODEL), F32),
            jax.ShapeDtypeStruct((DEC_BATCH, KV_WIDTH, WINDOW), F32),
            jax.ShapeDtypeStruct((DEC_BATCH, KV_WIDTH, WINDOW), F32),
            jax.ShapeDtypeStruct((POOL_BUF, DEC_BATCH, POOL_WIDTH), F32),
        ),
        grid=(N_SEQ_GROUPS // GROUPS_PER_STEP,),
        in_specs=[
            pl.BlockSpec((SAMPLE_TOK, D_MODEL), lambda g: (sample_block, 0)),
            _resident((1, D_MODEL)),
            _resident((D_MODEL, IN_EVEN)),
            _resident((ATTN_WIDTH + POOL_WIDTH, D_MODEL)),
            _resident((N_POOL_GROUPS, POOL_GROUP, POOL_GROUP)),
            _resident((1, POOL_WIDTH)),
            _resident((SAMPLE_ROWS, SAMPLE_KEYS)),
            smem,
            smem,
            pl.BlockSpec((GROUPS_PER_STEP * SEQ_GROUP, KV_WIDTH, WINDOW), lambda g: (g, 0, 0)),
            pl.BlockSpec((GROUPS_PER_STEP * SEQ_GROUP, KV_WIDTH, WINDOW), lambda g: (g, 0, 0)),
            _resident((POOL_BUF, DEC_BATCH, POOL_WIDTH)),
        ],
        out_specs=(
            pl.BlockSpec((SAMPLE_TOK, D_MODEL), lambda g: (sample_block, 0)),
            pl.BlockSpec((GROUPS_PER_STEP * SEQ_GROUP, KV_WIDTH, WINDOW), lambda g: (g, 0, 0)),
            pl.BlockSpec((GROUPS_PER_STEP * SEQ_GROUP, KV_WIDTH, WINDOW), lambda g: (g, 0, 0)),
            pl.BlockSpec((POOL_BUF, DEC_BATCH, POOL_WIDTH), lambda g: (0, 0, 0)),
        ),
        scratch_shapes=[
            pltpu.VMEM((NEW_SLOTS * DEC_BATCH, KV_WIDTH), F32),
            pltpu.VMEM((NEW_SLOTS * DEC_BATCH, KV_WIDTH), F32),
            pltpu.VMEM((SAMPLE_TOK, POOL_WIDTH), F32),
            pltpu.VMEM((N_HEADS_A, SAMPLE_TOK, LANES), F32),
            pltpu.VMEM((N_HEADS_A, SAMPLE_ROWS, SAMPLE_KEYS), F32),
            pltpu.VMEM((SAMPLE_TOK, ATTN_WIDTH), F32),
            pltpu.VMEM((SAMPLE_TOK, ATTN_WIDTH + POOL_WIDTH), BF16),
        ],
        input_output_aliases={0: 0},
        compiler_params=pltpu.CompilerParams(
            dimension_semantics=("arbitrary",), vmem_limit_bytes=EVEN_SAMPLE_VMEM),
        name="even_mixer_sample",
    )(x, gain.reshape(1, D_MODEL), w_in, w_out, w_pool, pool_scale.reshape(1, POOL_WIDTH), bucket,
      rel_bias, sinks, state_k, state_v, state_pool_t)


def _sgu_front(x, g_ref, win_ref, gv_ref):
    h = _rms(x, g_ref[...]).astype(BF16)
    uv = jax.nn.gelu(jnp.dot(h, win_ref[...], preferred_element_type=F32))
    return uv[:, 0:SGU_WIDTH], _rms(uv[:, SGU_WIDTH:], gv_ref[...])


def _odd_prompt_kernel(x_ref, g_ref, win_ref, gv_ref, ws_ref, bs_ref, wout_ref, o_ref, gated_ref):
    x = x_ref[...]
    u, v = _sgu_front(x, g_ref, win_ref, gv_ref)
    vb = v.astype(BF16)
    r = lax.broadcasted_iota(jnp.int32, (CHUNK, CHUNK), 0)
    c = lax.broadcasted_iota(jnp.int32, (CHUNK, CHUNK), 1)
    for gi in range(SGU_GROUPS):
        w = jnp.where(r >= c, ws_ref[gi], 0.0).astype(BF16)
        bias = jnp.concatenate([bs_ref[gi]] * (SGU_GROUP_W // LANES), axis=1)
        lanes = slice(gi * SGU_GROUP_W, (gi + 1) * SGU_GROUP_W)
        for ci in range(ODD_TM // CHUNK):
            rows = slice(ci * CHUNK, (ci + 1) * CHUNK)
            mixed = jnp.dot(w, vb[rows, lanes], preferred_element_type=F32) + bias
            gated_ref[rows, lanes] = (u[rows, lanes] * mixed).astype(BF16)
    o_ref[...] = x + jnp.dot(gated_ref[...], wout_ref[...], preferred_element_type=F32)


def _odd_prompt(x, gain, w_in, g_v, w_s, b_s, w_out):
    bias = jnp.broadcast_to(b_s[:, :, None], (SGU_GROUPS, CHUNK, LANES))
    return pl.pallas_call(
        _odd_prompt_kernel,
        out_shape=jax.ShapeDtypeStruct((N_TOK, D_MODEL), F32),
        grid=(PROMPT_TOK // ODD_TM,),
        in_specs=[
            pl.BlockSpec((ODD_TM, D_MODEL), lambda i: (i, 0)),
            _resident((1, D_MODEL)),
            _resident((D_MODEL, 2 * SGU_WIDTH)),
            _resident((1, SGU_WIDTH)),
            _resident((SGU_GROUPS, CHUNK, CHUNK)),
            _resident((SGU_GROUPS, CHUNK, LANES)),
            _resident((SGU_WIDTH, D_MODEL)),
        ],
        out_specs=pl.BlockSpec((ODD_TM, D_MODEL), lambda i: (i, 0)),
        scratch_shapes=[pltpu.VMEM((ODD_TM, SGU_WIDTH), BF16)],
        input_output_aliases={0: 0},
        compiler_params=pltpu.CompilerParams(
            dimension_semantics=("arbitrary",), vmem_limit_bytes=ODD_PROMPT_VMEM),
        name="odd_mixer_prompt",
    )(x, gain.reshape(1, D_MODEL), w_in, g_v.reshape(1, SGU_WIDTH), w_s, bias, w_out)


def _odd_sample_kernel(x_ref, g_ref, win_ref, gv_ref, coef_ref, bias_ref, wout_ref, o_ref, v_ref):
    x = x_ref[...]
    u, v = _sgu_front(x, g_ref, win_ref, gv_ref)
    for t in range(DEC_SEQ):
        v_ref[:, t, :] = v[t * DEC_BATCH:(t + 1) * DEC_BATCH, :]
    gated = []
    for t in range(DEC_SEQ):
        mixed = bias_ref[t:t + 1, :]
        for s in range(t + 1):
            mixed = mixed + coef_ref[t, s:s + 1, :] * v[s * DEC_BATCH:(s + 1) * DEC_BATCH, :]
        gated.append((u[t * DEC_BATCH:(t + 1) * DEC_BATCH, :] * mixed).astype(BF16))
    o_ref[...] = x + jnp.dot(jnp.concatenate(gated, axis=0), wout_ref[...], preferred_element_type=F32)


def _odd_sample(x, gain, w_in, g_v, w_s, b_s, w_out):
    coef = jnp.repeat(jnp.transpose(w_s[:, :DEC_SEQ, :DEC_SEQ], (1, 2, 0)), SGU_GROUP_W, axis=-1)
    bias = jnp.repeat(b_s[:, :DEC_SEQ].T, SGU_GROUP_W, axis=-1)
    sample_block = PROMPT_TOK // SAMPLE_TOK
    return pl.pallas_call(
        _odd_sample_kernel,
        out_shape=(
            jax.ShapeDtypeStruct((N_TOK, D_MODEL), F32),
            jax.ShapeDtypeStruct((DEC_BATCH, DEC_SEQ, SGU_WIDTH), F32),
        ),
        grid=(1,),
        in_specs=[
            pl.BlockSpec((SAMPLE_TOK, D_MODEL), lambda i: (sample_block, 0)),
            _resident((1, D_MODEL)),
            _resident((D_MODEL, 2 * SGU_WIDTH)),
            _resident((1, SGU_WIDTH)),
            _resident((DEC_SEQ, DEC_SEQ, SGU_WIDTH)),
            _resident((DEC_SEQ, SGU_WIDTH)),
            _resident((SGU_WIDTH, D_MODEL)),
        ],
        out_specs=(
            pl.BlockSpec((SAMPLE_TOK, D_MODEL), lambda i: (sample_block, 0)),
            pl.BlockSpec((DEC_BATCH, DEC_SEQ, SGU_WIDTH), lambda i: (0, 0, 0)),
        ),
        input_output_aliases={0: 0},
        compiler_params=pltpu.CompilerParams(
            dimension_semantics=("arbitrary",), vmem_limit_bytes=ODD_SAMPLE_VMEM),
        name="odd_mixer_sample",
    )(x, gain.reshape(1, D_MODEL), w_in, g_v.reshape(1, SGU_WIDTH), coef, bias, w_out)


def _window_to_lanes(w):
    return jnp.transpose(w, (0, 2, 3, 1)).reshape(DEC_BATCH, KV_WIDTH, WINDOW)


def _window_from_lanes(w):
    return jnp.transpose(w.reshape(DEC_BATCH, N_KV_A, HEAD_DIM, WINDOW), (0, 3, 1, 2))


def kernel(x_prompt, x_sample, state_win_k, state_win_v, state_pool, rel_bias, norm_gains, final_gain,
           ffn_gate, ffn_up, ffn_down, w_in_even, w_out_even, attn_sinks, w_pool, pool_scale,
           w_in_odd, sgu_norm, w_spatial, b_spatial, w_out_odd):
    xp = x_prompt.reshape(PROMPT_TOK, D_MODEL)

    kp_l, vp_l, pp_l, ks_l, vs_l, ps_l, sv_l = [], [], [], [], [], [], []
    x = None
    y_prompt = y_sample = None
    for l in range(DEPTH):
        fa = (norm_gains[l, 0], ffn_gate, ffn_up, ffn_down, l, 0)
        fb = (norm_gains[l, 2], ffn_gate, ffn_up, ffn_down, l, 1)
        if l == 0:
            x = _ffn(xp, *fa, tail_in=x_sample)
        else:
            x = _ffn(x, *fa)
        g1 = norm_gains[l, 1]
        if l % 2 == 0:
            e = l // 2
            wq = w_in_even[e][:, :ATTN_WIDTH].reshape(D_MODEL, N_KV_A, GQA_GROUP, HEAD_DIM)
            wq = jnp.swapaxes(wq, 1, 2).reshape(D_MODEL, ATTN_WIDTH)
            wo = w_out_even[e][:ATTN_WIDTH].reshape(N_KV_A, GQA_GROUP, HEAD_DIM, D_MODEL)
            wo = jnp.swapaxes(wo, 0, 1).reshape(ATTN_WIDTH, D_MODEL)
            w_in = jnp.concatenate([wq, w_in_even[e][:, ATTN_WIDTH:]], axis=1).astype(BF16)
            w_out = jnp.concatenate([wo, w_out_even[e][ATTN_WIDTH:]], axis=0).astype(BF16)
            wp = w_pool[e].astype(BF16)
            x, nk, nv, npool = _even_sample(
                x, g1, w_in, w_out, wp, pool_scale[e], rel_bias, attn_sinks[e],
                _window_to_lanes(state_win_k[e]), _window_to_lanes(state_win_v[e]),
                jnp.swapaxes(state_pool[e], 0, 1))
            x, kp, vp, pp = _even_prompt(x, g1, w_in, w_out, wp, pool_scale[e], rel_bias, attn_sinks[e])
            kp_l.append(kp.reshape(BATCH, WINDOW, N_KV_A, HEAD_DIM))
            vp_l.append(vp.reshape(BATCH, WINDOW, N_KV_A, HEAD_DIM))
            pp_l.append(pp[:, 2 * SUBLANES - POOL_BUF:, :])
            ks_l.append(_window_from_lanes(nk))
            vs_l.append(_window_from_lanes(nv))
            ps_l.append(jnp.swapaxes(npool, 0, 1))
        else:
            o = l // 2
            w_in, w_out = w_in_odd[o].astype(BF16), w_out_odd[o].astype(BF16)
            x, sv_new = _odd_sample(x, g1, w_in, sgu_norm[o], w_spatial[o], b_spatial[o], w_out)
            x = _odd_prompt(x, g1, w_in, sgu_norm[o], w_spatial[o], b_spatial[o], w_out)
            sv_l.append(sv_new)
        if l < DEPTH - 1:
            x = _ffn(x, *fb)
        else:
            y_prompt, y_sample = _ffn(x, *fb, final_gain=final_gain)

    return (y_prompt.reshape(BATCH, SEQ, D_MODEL), y_sample,
            jnp.stack(kp_l), jnp.stack(vp_l), jnp.stack(pp_l),
            jnp.stack(ks_l), jnp.stack(vs_l), jnp.stack(ps_l), jnp.stack(sv_l))
```

```python
import functools
import math

import numpy as np
import jax
import jax.numpy as jnp
from jax import lax
from jax.experimental import pallas as pl
from jax.experimental.pallas import tpu as pltpu

F32 = jnp.float32
BF16 = jnp.bfloat16

D_MODEL = 1024
BATCH = 8
SEQ = 2048
DEPTH = 2
DEC_BATCH = 128
DEC_SEQ = 4
PAST_LEN = 16384
N_HEADS_A = 8
N_KV_A = 2
HEAD_DIM = 64
GQA_GROUP = N_HEADS_A // N_KV_A
WINDOW = 128
ATTN_WIDTH = N_HEADS_A * HEAD_DIM
KV_WIDTH = N_KV_A * HEAD_DIM
N_BUCKETS = 32
MAX_DISTANCE = 128
POOL_WINDOWS = (2, 4, 8, 16)
N_POOL_GROUPS = len(POOL_WINDOWS)
POOL_GROUP = 128
POOL_WIDTH = N_POOL_GROUPS * POOL_GROUP
POOL_BUF = max(POOL_WINDOWS) - 1
IN_EVEN = ATTN_WIDTH + 2 * KV_WIDTH + POOL_WIDTH
CHUNK = 128
SGU_WIDTH = 1024
SGU_GROUPS = 4
SGU_GROUP_W = SGU_WIDTH // SGU_GROUPS
D_FF = 2816
EPS = 1e-6
NEG = -1e30

LANES = 128
SUBLANES = 8
MIB = 1024 * 1024
VMEM_LIMIT_BYTES = 56 * MIB
EVEN_PROMPT_VMEM = 42 * MIB
EVEN_SAMPLE_VMEM = 44 * MIB
ODD_VMEM = 36 * MIB

PROMPT_TOK = BATCH * SEQ
SAMPLE_TOK = DEC_BATCH * DEC_SEQ
N_TOK = PROMPT_TOK + SAMPLE_TOK

TM = SAMPLE_TOK
EVEN_TM = 1024
assert SEQ % EVEN_TM == 0 and EVEN_TM % WINDOW == 0
TILES_PER_SEQ = SEQ // EVEN_TM
QBLK_PER_TILE = EVEN_TM // WINDOW
FFN_TM = 2 * TM
assert PROMPT_TOK % FFN_TM == 0
FFN_PROMPT_STEPS = PROMPT_TOK // FFN_TM
ODD_TM = 1024
assert PROMPT_TOK % ODD_TM == 0 and ODD_TM % CHUNK == 0
ODD_STEPS = PROMPT_TOK // ODD_TM
X_RING = 3
assert ODD_STEPS >= X_RING and SAMPLE_TOK <= ODD_TM
FF_CHUNK = 256
N_FF_CHUNKS = D_FF // FF_CHUNK
FF_SLOTS = 3
POOL_HEAD = 2 * SUBLANES
assert all(w & (w - 1) == 0 and w <= POOL_HEAD for w in POOL_WINDOWS)
POOL_CTX = 32
SEQ_GROUP = 8
N_SEQ_GROUPS = DEC_BATCH // SEQ_GROUP
GROUPS_PER_STEP = 4
assert N_SEQ_GROUPS % GROUPS_PER_STEP == 0
STATE_KEYS = SEQ_GROUP * WINDOW
NEW_SLOTS = SUBLANES
assert DEC_SEQ <= NEW_SLOTS and SEQ_GROUP * NEW_SLOTS <= WINDOW
SAMPLE_KEYS = STATE_KEYS + WINDOW
SAMPLE_ROWS = DEC_SEQ * SEQ_GROUP


assert N_KV_A == 2 and KV_WIDTH == LANES
HEAD_ORDER = [h for j in range(GQA_GROUP) for h in (j, j + GQA_GROUP)]


def _t5_bucket_np(dist):
    n = np.maximum(dist, 0)
    max_exact = N_BUCKETS // 2
    nf = np.maximum(n, 1).astype(np.float32)
    large = max_exact + (np.log(nf / np.float32(max_exact)) / np.float32(math.log(MAX_DISTANCE / max_exact))
                         * np.float32(N_BUCKETS - max_exact)).astype(np.int32)
    large = np.minimum(large, N_BUCKETS - 1)
    return np.where(n < max_exact, n, large).astype(np.int32)


def _prompt_bucket_table():
    qi = np.arange(WINDOW)[:, None]
    c = np.arange(WINDOW)[None, :]
    dist = np.where(c > qi, qi + WINDOW - c, qi - c)
    return _t5_bucket_np(dist)


def _sample_bucket_table():
    t = (np.arange(SAMPLE_ROWS) // SEQ_GROUP)[:, None]
    b = (np.arange(SAMPLE_ROWS) % SEQ_GROUP)[:, None]
    col = np.arange(SAMPLE_KEYS)[None, :]
    is_state = col < STATE_KEYS
    slot = (col - STATE_KEYS) % NEW_SLOTS
    is_new = (col >= STATE_KEYS) & (col < STATE_KEYS + SEQ_GROUP * NEW_SLOTS) & (slot < DEC_SEQ)
    kb = np.where(is_state, col // WINDOW, (col - STATE_KEYS) // NEW_SLOTS)
    kpos = np.where(is_state, col % WINDOW, WINDOW + slot)
    dist = (WINDOW + t) - kpos
    valid = (is_state | is_new) & (kb == b) & (dist >= 0) & (dist < WINDOW)
    return np.where(valid, _t5_bucket_np(dist), -1).astype(np.int32)


def _rms(x, g):
    y = x * lax.rsqrt(jnp.mean(x * x, axis=-1, keepdims=True) + EPS)
    return y * g


LOG2E = math.log2(math.e)
Q_SCALE = HEAD_DIM ** -0.5 * LOG2E


def _bias_from_buckets(bucket, rb_ref, h):
    t = jnp.zeros(bucket.shape, F32)
    for bkt in range(N_BUCKETS):
        t = jnp.where(bucket == bkt, rb_ref[bkt, h] * LOG2E, t)
    return t


def _sink_softmax(s, sink):
    m = jnp.maximum(jnp.max(s, axis=-1, keepdims=True), sink)
    p = jnp.exp2(s - m)
    return p, jnp.sum(p, axis=-1, keepdims=True) + jnp.exp2(sink - m)


def _softmax_pv(s, valid, bias, sink, v):
    p, denom = _sink_softmax(jnp.where(valid, s + bias, NEG), sink)
    return jnp.dot(p.astype(BF16), v, preferred_element_type=F32) / denom


def _ffn_kernel(*refs, layer, half, tail_in, final_split, unroll_halves):
    refs = list(refs)
    x_ref = refs.pop(0)
    xt_ref = refs.pop(0) if tail_in else None
    g_ref, wg_hbm, wu_hbm, wd_hbm = refs[:4]
    del refs[:4]
    fg_ref = refs.pop(0) if final_split else None
    o_ref = refs.pop(0)
    ot_ref = refs.pop(0) if final_split else None
    wg, wu, wd, stage_in, stage_dn, sem = refs
    i = pl.program_id(0)

    def normed(x):
        return _rms(x, g_ref[...]).astype(BF16)

    def result(x, y):
        out = x + 0.5 * y
        return _rms(out, fg_ref[...]) if final_split else out

    def chunk_copies(c, slot):
        cols = pl.ds(c * FF_CHUNK, FF_CHUNK)
        return (
            pltpu.make_async_copy(wg_hbm.at[layer, half, :, cols], stage_in.at[0, slot], sem.at[0, slot]),
            pltpu.make_async_copy(wu_hbm.at[layer, half, :, cols], stage_in.at[1, slot], sem.at[1, slot]),
            pltpu.make_async_copy(wd_hbm.at[layer, half, cols, :], stage_dn.at[slot], sem.at[2, slot]),
        )

    @pl.when(i == 0)
    def _():
        for c in range(min(FF_SLOTS, N_FF_CHUNKS)):
            for cp in chunk_copies(c, c % FF_SLOTS):
                cp.start()
        if tail_in:
            x = jnp.concatenate([xt_ref[:, t, :] for t in range(DEC_SEQ)], axis=0)
        else:
            x = x_ref[0:TM, :]
        h = normed(x)
        acc = jnp.zeros((TM, D_MODEL), F32)
        for c in range(N_FF_CHUNKS):
            slot = c % FF_SLOTS
            cols = slice(c * FF_CHUNK, (c + 1) * FF_CHUNK)
            for cp in chunk_copies(c, slot):
                cp.wait()
            wg[:, cols] = stage_in[0, slot].astype(BF16)
            wu[:, cols] = stage_in[1, slot].astype(BF16)
            wd[cols, :] = stage_dn[slot].astype(BF16)
            if c + FF_SLOTS < N_FF_CHUNKS:
                for cp in chunk_copies(c + FF_SLOTS, slot):
                    cp.start()
            gate = jnp.dot(h, wg[:, cols], preferred_element_type=F32)
            up = jnp.dot(h, wu[:, cols], preferred_element_type=F32)
            act = (jax.nn.silu(gate) * up).astype(BF16)
            acc = acc + jnp.dot(act, wd[cols, :], preferred_element_type=F32)
        out = result(x, acc)
        if final_split:
            for t in range(DEC_SEQ):
                ot_ref[:, t, :] = out[t * DEC_BATCH:(t + 1) * DEC_BATCH, :]
        else:
            o_ref[0:TM, :] = out

    def half_block(rows):
        x = x_ref[rows, :]
        h = normed(x)
        gate = jnp.dot(h, wg[...], preferred_element_type=F32)
        up = jnp.dot(h, wu[...], preferred_element_type=F32)
        act = (jax.nn.silu(gate) * up).astype(BF16)
        o_ref[rows, :] = result(x, jnp.dot(act, wd[...], preferred_element_type=F32))

    @pl.when(i > 0)
    def _():
        if unroll_halves:
            for hf in range(FFN_TM // TM):
                half_block(slice(hf * TM, (hf + 1) * TM))
        else:
            def body(hf, carry):
                half_block(pl.ds(pl.multiple_of(hf * TM, TM), TM))
                return carry
            lax.fori_loop(0, FFN_TM // TM, body, 0)


def _resident(shape):
    return pl.BlockSpec(shape, lambda *_: (0,) * len(shape), pipeline_mode=pl.Buffered(1))


def _ffn(x_in, gain, wg, wu, wd, layer, half, *, tail_in=None, final_gain=None):
    hbm = pl.BlockSpec(memory_space=pl.ANY)
    row_block = pl.BlockSpec((FFN_TM, D_MODEL), lambda i: (jnp.where(i == 0, FFN_PROMPT_STEPS, i - 1), 0))
    prompt_block = pl.BlockSpec((FFN_TM, D_MODEL), lambda i: (jnp.maximum(i - 1, 0), 0))
    sample_block = pl.BlockSpec((DEC_BATCH, DEC_SEQ, D_MODEL), lambda i: (0, 0, 0))
    if tail_in is None:
        args, in_specs = [x_in], [row_block]
    else:
        args, in_specs = [x_in, tail_in], [prompt_block, sample_block]
    args += [gain.reshape(1, D_MODEL), wg, wu, wd]
    in_specs += [_resident((1, D_MODEL)), hbm, hbm, hbm]
    if final_gain is None:
        out_shape = jax.ShapeDtypeStruct((N_TOK, D_MODEL), F32)
        out_specs = row_block
    else:
        args.append(final_gain.reshape(1, D_MODEL))
        in_specs.append(_resident((1, D_MODEL)))
        out_shape = (jax.ShapeDtypeStruct((PROMPT_TOK, D_MODEL), F32),
                     jax.ShapeDtypeStruct((DEC_BATCH, DEC_SEQ, D_MODEL), F32))
        out_specs = (prompt_block, sample_block)
    aliases = {0: 0} if (tail_in is None and final_gain is None) else {}
    return pl.pallas_call(
        functools.partial(_ffn_kernel, layer=layer, half=half, tail_in=tail_in is not None,
                          final_split=final_gain is not None, unroll_halves=(layer == 0 or half == 1)),
        out_shape=out_shape,
        grid=(FFN_PROMPT_STEPS + 1,),
        in_specs=in_specs,
        out_specs=out_specs,
        scratch_shapes=[
            pltpu.VMEM((D_MODEL, D_FF), BF16),
            pltpu.VMEM((D_MODEL, D_FF), BF16),
            pltpu.VMEM((D_FF, D_MODEL), BF16),
            pltpu.VMEM((2, FF_SLOTS, D_MODEL, FF_CHUNK), F32),
            pltpu.VMEM((FF_SLOTS, FF_CHUNK, D_MODEL), F32),
            pltpu.SemaphoreType.DMA((3, FF_SLOTS)),
        ],
        input_output_aliases=aliases,
        compiler_params=pltpu.CompilerParams(
            dimension_semantics=("arbitrary",), vmem_limit_bytes=VMEM_LIMIT_BYTES),
        name="macaron_half",
    )(*args)


def _even_prompt_kernel(x_ref, g_ref, win_ref, wout_ref, wpool_ref, pscale_ref, bucket_ref, rb_ref, sink_ref,
                        o_ref, klast_ref, vlast_ref, plast_ref,
                        kbuf, vbuf, zb, s2b, s4b, s8b, bias_ref, cat_ref):
    b = pl.program_id(0)
    s = pl.program_id(1)
    last = pl.num_programs(1) - 1

    @pl.when((b == 0) & (s == 0))
    def _():
        bucket = bucket_ref[...]
        for h in range(N_HEADS_A):
            bias_ref[h] = _bias_from_buckets(bucket, rb_ref, h)

    @pl.when(s == 0)
    def _():
        for buf in (kbuf, vbuf):
            buf[0:WINDOW, :] = jnp.zeros((WINDOW, KV_WIDTH), BF16)
        zb[0:POOL_CTX, :] = jnp.zeros((POOL_CTX, POOL_WIDTH), F32)

    @pl.when(s > 0)
    def _():
        for buf in (kbuf, vbuf):
            buf[0:WINDOW, :] = buf[EVEN_TM:EVEN_TM + WINDOW, :]
        zb[0:POOL_CTX, :] = zb[EVEN_TM:EVEN_TM + POOL_CTX, :]

    x = x_ref[...]
    h = _rms(x, g_ref[...]).astype(BF16)
    proj = jnp.dot(h, win_ref[...], preferred_element_type=F32)
    q = proj[:, 0:ATTN_WIDTH] * Q_SCALE
    k = proj[:, ATTN_WIDTH:ATTN_WIDTH + KV_WIDTH]
    v = proj[:, ATTN_WIDTH + KV_WIDTH:ATTN_WIDTH + 2 * KV_WIDTH]
    up = proj[:, ATTN_WIDTH + 2 * KV_WIDTH:IN_EVEN]

    qlane = lax.broadcasted_iota(jnp.int32, (EVEN_TM, ATTN_WIDTH), 1) % LANES
    q_half = (jnp.where(qlane < HEAD_DIM, q, 0.0).astype(BF16),
              jnp.where(qlane >= HEAD_DIM, q, 0.0).astype(BF16))
    kbuf[WINDOW:, :] = k.astype(BF16)
    vbuf[WINDOW:, :] = v.astype(BF16)

    qrow = lax.broadcasted_iota(jnp.int32, (WINDOW, WINDOW), 0)
    kcol = lax.broadcasted_iota(jnp.int32, (WINDOW, WINDOW), 1)
    prev = kcol > qrow
    olane = lax.broadcasted_iota(jnp.int32, (WINDOW, LANES), 1)
    for i in range(QBLK_PER_TILE):
        rows = slice(i * WINDOW, (i + 1) * WINDOW)
        keys = slice(i * WINDOW, (i + 2) * WINDOW)
        qs = jnp.concatenate(
            [q_half[hd // GQA_GROUP][rows, (hd % GQA_GROUP) * LANES:(hd % GQA_GROUP + 1) * LANES]
             for hd in range(N_HEADS_A)], axis=0)
        sc_all = lax.dot_general(qs, kbuf[keys, :], (((1,), (1,)), ((), ())), preferred_element_type=F32)
        ps, denoms = [], []
        for hd in range(N_HEADS_A):
            sc = sc_all[hd * WINDOW:(hd + 1) * WINDOW, :]
            sc = jnp.where(prev, sc[:, 0:WINDOW], sc[:, WINDOW:]) + bias_ref[hd]
            if i == 0:
                sc = jnp.where(prev & (qrow >= jnp.where(s > 0, WINDOW, 0)), NEG, sc)
            p, denom = _sink_softmax(sc, sink_ref[hd] * LOG2E)
            ps.append(jnp.concatenate([jnp.where(prev, p, 0.0), jnp.where(prev, 0.0, p)], axis=1).astype(BF16))
            denoms.append(denom)
        o_all = jnp.dot(jnp.concatenate(ps, axis=0), vbuf[keys, :], preferred_element_type=F32)
        out = [o_all[hd * WINDOW:(hd + 1) * WINDOW, :] / denoms[hd] for hd in range(N_HEADS_A)]
        for j in range(GQA_GROUP):
            cat_ref[rows, j * LANES:(j + 1) * LANES] = jnp.where(
                olane < HEAD_DIM, out[j], out[j + GQA_GROUP]).astype(BF16)

    zb[POOL_CTX:, :] = up
    n = EVEN_TM + POOL_CTX
    s2b[8:, :] = zb[8:n, :] + zb[7:n - 1, :]
    s4b[16:, :] = s2b[16:n, POOL_GROUP:] + s2b[14:n - 2, POOL_GROUP:]
    s8b[24:, :] = s4b[24:n, POOL_GROUP:] + s4b[20:n - 4, POOL_GROUP:]
    s16 = s8b[32:n, POOL_GROUP:] + s8b[24:n - 8, POOL_GROUP:]
    sums = (s2b[POOL_CTX:, 0:POOL_GROUP], s4b[POOL_CTX:, 0:POOL_GROUP], s8b[POOL_CTX:, 0:POOL_GROUP], s16)
    pos1 = lax.broadcasted_iota(jnp.int32, (POOL_HEAD, POOL_GROUP), 0) + s * EVEN_TM + 1
    for g, w in enumerate(POOL_WINDOWS):
        lanes = slice(g * POOL_GROUP, (g + 1) * POOL_GROUP)
        cnt = jnp.minimum(pos1, w).astype(F32)
        mean = jnp.concatenate([sums[g][0:POOL_HEAD] / cnt, sums[g][POOL_HEAD:] * (1.0 / w)], axis=0)
        d = (mean - up[:, lanes]).astype(BF16)
        y = jnp.dot(d, wpool_ref[g], preferred_element_type=F32) * pscale_ref[:, lanes]
        cat_ref[:, ATTN_WIDTH + g * POOL_GROUP:ATTN_WIDTH + (g + 1) * POOL_GROUP] = y.astype(BF16)

    o_ref[...] = x + jnp.dot(cat_ref[...], wout_ref[...], preferred_element_type=F32)

    @pl.when(s == last)
    def _():
        klast_ref[0] = k[EVEN_TM - WINDOW:, :]
        vlast_ref[0] = v[EVEN_TM - WINDOW:, :]
        plast_ref[0] = up[EVEN_TM - 2 * SUBLANES:, :]


def _even_prompt(x, gain, w_in, w_out, w_pool, pool_scale, rel_bias, sinks):
    bucket = jnp.asarray(_prompt_bucket_table())
    smem = pl.BlockSpec(memory_space=pltpu.SMEM)
    return pl.pallas_call(
        _even_prompt_kernel,
        out_shape=(
            jax.ShapeDtypeStruct((N_TOK, D_MODEL), F32),
            jax.ShapeDtypeStruct((BATCH, WINDOW, KV_WIDTH), F32),
            jax.ShapeDtypeStruct((BATCH, WINDOW, KV_WIDTH), F32),
            jax.ShapeDtypeStruct((BATCH, 2 * SUBLANES, POOL_WIDTH), F32),
        ),
        grid=(BATCH, TILES_PER_SEQ),
        in_specs=[
            pl.BlockSpec((EVEN_TM, D_MODEL), lambda b, s: (b * TILES_PER_SEQ + s, 0)),
            _resident((1, D_MODEL)),
            _resident((D_MODEL, IN_EVEN)),
            _resident((ATTN_WIDTH + POOL_WIDTH, D_MODEL)),
            _resident((N_POOL_GROUPS, POOL_GROUP, POOL_GROUP)),
            _resident((1, POOL_WIDTH)),
            _resident((WINDOW, WINDOW)),
            smem,
            smem,
        ],
        out_specs=(
            pl.BlockSpec((EVEN_TM, D_MODEL), lambda b, s: (b * TILES_PER_SEQ + s, 0)),
            pl.BlockSpec((1, WINDOW, KV_WIDTH), lambda b, s: (b, 0, 0)),
            pl.BlockSpec((1, WINDOW, KV_WIDTH), lambda b, s: (b, 0, 0)),
            pl.BlockSpec((1, 2 * SUBLANES, POOL_WIDTH), lambda b, s: (b, 0, 0)),
        ),
        scratch_shapes=[
            pltpu.VMEM((WINDOW + EVEN_TM, KV_WIDTH), BF16),
            pltpu.VMEM((WINDOW + EVEN_TM, KV_WIDTH), BF16),
            pltpu.VMEM((POOL_CTX + EVEN_TM, POOL_WIDTH), F32),
            pltpu.VMEM((POOL_CTX + EVEN_TM, POOL_WIDTH), F32),
            pltpu.VMEM((POOL_CTX + EVEN_TM, POOL_WIDTH - POOL_GROUP), F32),
            pltpu.VMEM((POOL_CTX + EVEN_TM, POOL_WIDTH - 2 * POOL_GROUP), F32),
            pltpu.VMEM((N_HEADS_A, WINDOW, WINDOW), F32),
            pltpu.VMEM((EVEN_TM, ATTN_WIDTH + POOL_WIDTH), BF16),
        ],
        input_output_aliases={0: 0},
        compiler_params=pltpu.CompilerParams(
            dimension_semantics=("arbitrary", "arbitrary"), vmem_limit_bytes=EVEN_PROMPT_VMEM),
        name="even_mixer_prompt",
    )(x, gain.reshape(1, D_MODEL), w_in, w_out, w_pool, pool_scale.reshape(1, POOL_WIDTH), bucket,
      rel_bias, sinks)


def _even_sample_kernel(x_ref, g_ref, win_ref, wout_ref, wpool_ref, pscale_ref, bucket_ref, rb_ref, sink_ref,
                        sk_ref, sv_ref, spool_ref,
                        o_ref, nk_ref, nv_ref, npool_ref,
                        knew_ref, vnew_ref, upnew_ref, qsel, bias_ref, attn_f32, cat_ref):
    g = pl.program_id(0)
    last = pl.num_programs(0) - 1

    @pl.when(g == 0)
    def _():
        bucket = bucket_ref[...]
        for h in range(N_HEADS_A):
            bias_ref[h] = _bias_from_buckets(bucket, rb_ref, h)
        x = x_ref[...]
        h = _rms(x, g_ref[...]).astype(BF16)
        proj = jnp.dot(h, win_ref[...], preferred_element_type=F32)
        q = proj[:, 0:ATTN_WIDTH] * Q_SCALE
        zeros = jnp.zeros(((NEW_SLOTS - DEC_SEQ) * DEC_BATCH, KV_WIDTH), F32)
        knew_ref[0:SAMPLE_TOK, :] = proj[:, ATTN_WIDTH:ATTN_WIDTH + KV_WIDTH]
        vnew_ref[0:SAMPLE_TOK, :] = proj[:, ATTN_WIDTH + KV_WIDTH:ATTN_WIDTH + 2 * KV_WIDTH]
        knew_ref[SAMPLE_TOK:, :] = zeros
        vnew_ref[SAMPLE_TOK:, :] = zeros
        upnew_ref[...] = proj[:, ATTN_WIDTH + 2 * KV_WIDTH:IN_EVEN]
        qlane = lax.broadcasted_iota(jnp.int32, (SAMPLE_TOK, LANES), 1)
        for hd in range(N_HEADS_A):
            j, kvh = hd % GQA_GROUP, hd // GQA_GROUP
            keep = (qlane < HEAD_DIM) if kvh == 0 else (qlane >= HEAD_DIM)
            qsel[hd] = jnp.where(keep, q[:, j * LANES:(j + 1) * LANES], 0.0)

    lax.fori_loop(0, GROUPS_PER_STEP, functools.partial(
        _even_sample_group, g, bucket_ref, sink_ref, sk_ref, sv_ref, nk_ref, nv_ref,
        knew_ref, vnew_ref, qsel, bias_ref, attn_f32), 0)

    @pl.when(g == last)
    def _():
        cat_ref[:, 0:ATTN_WIDTH] = attn_f32[...].astype(BF16)
        for r in range(POOL_BUF):
            if r < POOL_BUF - DEC_SEQ:
                npool_ref[r] = spool_ref[r + DEC_SEQ]
            else:
                t = r - (POOL_BUF - DEC_SEQ)
                npool_ref[r] = upnew_ref[t * DEC_BATCH:(t + 1) * DEC_BATCH, :]
        for t in range(DEC_SEQ):
            rows = slice(t * DEC_BATCH, (t + 1) * DEC_BATCH)
            for gi, w in enumerate(POOL_WINDOWS):
                lanes = slice(gi * POOL_GROUP, (gi + 1) * POOL_GROUP)
                acc = upnew_ref[rows, lanes]
                for back in range(1, w):
                    tt = t - back
                    if tt >= 0:
                        acc = acc + upnew_ref[tt * DEC_BATCH:(tt + 1) * DEC_BATCH, lanes]
                    else:
                        acc = acc + spool_ref[POOL_BUF + tt, :, lanes]
                cnt = float(min(PAST_LEN + t + 1, w))
                d = (acc / cnt - upnew_ref[rows, lanes]).astype(BF16)
                y = jnp.dot(d, wpool_ref[gi], preferred_element_type=F32) * pscale_ref[:, lanes]
                cat_ref[rows, ATTN_WIDTH + gi * POOL_GROUP:ATTN_WIDTH + (gi + 1) * POOL_GROUP] = y.astype(BF16)
        o_ref[...] = x_ref[...] + jnp.dot(cat_ref[...], wout_ref[...], preferred_element_type=F32)


def _even_sample_group(g, bucket_ref, sink_ref, sk_ref, sv_ref, nk_ref, nv_ref,
                       knew_ref, vnew_ref, qsel, bias_ref, attn_f32, gg, carry):
    olane = lax.broadcasted_iota(jnp.int32, (SAMPLE_ROWS, LANES), 1)
    seq0 = gg * SEQ_GROUP
    row0 = pl.multiple_of((g * GROUPS_PER_STEP + gg) * SEQ_GROUP, SEQ_GROUP)
    pad = [jnp.zeros((WINDOW - SEQ_GROUP * NEW_SLOTS, KV_WIDTH), F32)]
    knew = jnp.concatenate(
        [knew_ref[pl.ds(row0 + b, NEW_SLOTS, stride=DEC_BATCH), :] for b in range(SEQ_GROUP)] + pad, axis=0)
    vnew = jnp.concatenate(
        [vnew_ref[pl.ds(row0 + b, NEW_SLOTS, stride=DEC_BATCH), :] for b in range(SEQ_GROUP)] + pad, axis=0)
    kt_all = jnp.concatenate([sk_ref[seq0 + b] for b in range(SEQ_GROUP)], axis=1).astype(BF16)
    vt_all = jnp.concatenate([sv_ref[seq0 + b] for b in range(SEQ_GROUP)], axis=1).astype(BF16)

    qg = jnp.concatenate(
        [qsel[hd, pl.ds(t * DEC_BATCH + row0, SEQ_GROUP), :] for hd in range(N_HEADS_A) for t in range(DEC_SEQ)],
        axis=0).astype(BF16)
    nt = (((1,), (1,)), ((), ()))
    sc = jnp.concatenate(
        [jnp.dot(qg, kt_all, preferred_element_type=F32),
         lax.dot_general(qg, knew.astype(BF16), nt, preferred_element_type=F32)], axis=1)
    valid = jnp.concatenate([bucket_ref[...]] * N_HEADS_A, axis=0) >= 0
    bias = bias_ref[...].reshape(N_HEADS_A * SAMPLE_ROWS, SAMPLE_KEYS)
    sink = jnp.concatenate(
        [jnp.full((SAMPLE_ROWS, 1), sink_ref[hd] * LOG2E, F32) for hd in range(N_HEADS_A)], axis=0)
    p, denom = _sink_softmax(jnp.where(valid, sc + bias, NEG), sink)
    p = p.astype(BF16)
    o_all = (lax.dot_general(p[:, 0:STATE_KEYS], vt_all, nt, preferred_element_type=F32)
             + jnp.dot(p[:, STATE_KEYS:], vnew.astype(BF16), preferred_element_type=F32)) / denom
    for j in range(GQA_GROUP):
        blk = jnp.where(olane < HEAD_DIM, o_all[j * SAMPLE_ROWS:(j + 1) * SAMPLE_ROWS, :],
                        o_all[(j + GQA_GROUP) * SAMPLE_ROWS:(j + GQA_GROUP + 1) * SAMPLE_ROWS, :])
        for t in range(DEC_SEQ):
            attn_f32[pl.ds(t * DEC_BATCH + row0, SEQ_GROUP), j * LANES:(j + 1) * LANES] = (
                blk[t * SEQ_GROUP:(t + 1) * SEQ_GROUP, :])

    lane = lax.broadcasted_iota(jnp.int32, (KV_WIDTH, WINDOW), 1)
    for new, old_ref, out_ref in ((knew, sk_ref, nk_ref), (vnew, sv_ref, nv_ref)):
        new_t = new.T
        for b in range(SEQ_GROUP):
            tail = pltpu.roll(new_t, (WINDOW - DEC_SEQ - b * NEW_SLOTS) % WINDOW, 1)
            out_ref[seq0 + b] = jnp.where(
                lane >= WINDOW - DEC_SEQ, tail, pltpu.roll(old_ref[seq0 + b], WINDOW - DEC_SEQ, 1))
    return carry


def _even_sample(x, gain, w_in, w_out, w_pool, pool_scale, rel_bias, sinks, state_k, state_v, state_pool_t):
    bucket = jnp.asarray(_sample_bucket_table())
    smem = pl.BlockSpec(memory_space=pltpu.SMEM)
    sample_block = PROMPT_TOK // SAMPLE_TOK
    return pl.pallas_call(
        _even_sample_kernel,
        out_shape=(
            jax.ShapeDtypeStruct((N_TOK, D_MODEL), F32),
            jax.ShapeDtypeStruct((DEC_BATCH, KV_WIDTH, WINDOW), F32),
            jax.ShapeDtypeStruct((DEC_BATCH, KV_WIDTH, WINDOW), F32),
            jax.ShapeDtypeStruct((POOL_BUF, DEC_BATCH, POOL_WIDTH), F32),
        ),
        grid=(N_SEQ_GROUPS // GROUPS_PER_STEP,),
        in_specs=[
            pl.BlockSpec((SAMPLE_TOK, D_MODEL), lambda g: (sample_block, 0)),
            _resident((1, D_MODEL)),
            _resident((D_MODEL, IN_EVEN)),
            _resident((ATTN_WIDTH + POOL_WIDTH, D_MODEL)),
            _resident((N_POOL_GROUPS, POOL_GROUP, POOL_GROUP)),
            _resident((1, POOL_WIDTH)),
            _resident((SAMPLE_ROWS, SAMPLE_KEYS)),
            smem,
            smem,
            pl.BlockSpec((GROUPS_PER_STEP * SEQ_GROUP, KV_WIDTH, WINDOW), lambda g: (g, 0, 0)),
            pl.BlockSpec((GROUPS_PER_STEP * SEQ_GROUP, KV_WIDTH, WINDOW), lambda g: (g, 0, 0)),
            _resident((POOL_BUF, DEC_BATCH, POOL_WIDTH)),
        ],
        out_specs=(
            pl.BlockSpec((SAMPLE_TOK, D_MODEL), lambda g: (sample_block, 0)),
            pl.BlockSpec((GROUPS_PER_STEP * SEQ_GROUP, KV_WIDTH, WINDOW), lambda g: (g, 0, 0)),
            pl.BlockSpec((GROUPS_PER_STEP * SEQ_GROUP, KV_WIDTH, WINDOW), lambda g: (g, 0, 0)),
            pl.BlockSpec((POOL_BUF, DEC_BATCH, POOL_WIDTH), lambda g: (0, 0, 0)),
        ),
        scratch_shapes=[
            pltpu.VMEM((NEW_SLOTS * DEC_BATCH, KV_WIDTH), F32),
            pltpu.VMEM((NEW_SLOTS * DEC_BATCH, KV_WIDTH), F32),
            pltpu.VMEM((SAMPLE_TOK, POOL_WIDTH), F32),
            pltpu.VMEM((N_HEADS_A, SAMPLE_TOK, LANES), F32),
            pltpu.VMEM((N_HEADS_A, SAMPLE_ROWS, SAMPLE_KEYS), F32),
            pltpu.VMEM((SAMPLE_TOK, ATTN_WIDTH), F32),
            pltpu.VMEM((SAMPLE_TOK, ATTN_WIDTH + POOL_WIDTH), BF16),
        ],
        input_output_aliases={0: 0},
        compiler_params=pltpu.CompilerParams(
            dimension_semantics=("arbitrary",), vmem_limit_bytes=EVEN_SAMPLE_VMEM),
        name="even_mixer_sample",
    )(x, gain.reshape(1, D_MODEL), w_in, w_out, w_pool, pool_scale.reshape(1, POOL_WIDTH), bucket,
      rel_bias, sinks, state_k, state_v, state_pool_t)


def _sgu_front(x, g_ref, win_ref, gv_ref):
    h = _rms(x, g_ref[...]).astype(BF16)
    uv = jax.nn.gelu(jnp.dot(h, win_ref[...], preferred_element_type=F32))
    return uv[:, 0:SGU_WIDTH], _rms(uv[:, SGU_WIDTH:], gv_ref[...])


def _odd_kernel(x_hbm, g_ref, win_ref, gv_ref, ws_ref, bs_ref, coef_ref, sbias_ref, wout_ref,
                o_ref, v_ref, gated_ref, xbuf, xsem):
    i = pl.program_id(0)
    sample_slot = ODD_STEPS % X_RING

    def x_copy(blk, slot):
        rows = pl.ds(pl.multiple_of(blk * ODD_TM, ODD_TM), ODD_TM)
        return pltpu.make_async_copy(x_hbm.at[rows, :], xbuf.at[slot], xsem.at[slot])

    def sample_copy():
        return pltpu.make_async_copy(x_hbm.at[pl.ds(PROMPT_TOK, SAMPLE_TOK), :],
                                     xbuf.at[sample_slot, pl.ds(0, SAMPLE_TOK), :], xsem.at[sample_slot])

    @pl.when(i == 0)
    def _():
        for blk in range(X_RING - 1):
            x_copy(blk, blk).start()

    ahead = i + (X_RING - 1)

    @pl.when(ahead < ODD_STEPS)
    def _():
        x_copy(ahead, ahead % X_RING).start()

    @pl.when(ahead == ODD_STEPS)
    def _():
        sample_copy().start()

    @pl.when(i < ODD_STEPS)
    def _():
        slot = i % X_RING
        x_copy(i, slot).wait()
        x = xbuf[slot]
        u, v = _sgu_front(x, g_ref, win_ref, gv_ref)
        vb = v.astype(BF16)
        r = lax.broadcasted_iota(jnp.int32, (CHUNK, CHUNK), 0)
        c = lax.broadcasted_iota(jnp.int32, (CHUNK, CHUNK), 1)
        for gi in range(SGU_GROUPS):
            w = jnp.where(r >= c, ws_ref[gi], 0.0).astype(BF16)
            bias = jnp.concatenate([bs_ref[gi]] * (SGU_GROUP_W // LANES), axis=1)
            lanes = slice(gi * SGU_GROUP_W, (gi + 1) * SGU_GROUP_W)
            for ci in range(ODD_TM // CHUNK):
                rows = slice(ci * CHUNK, (ci + 1) * CHUNK)
                mixed = jnp.dot(w, vb[rows, lanes], preferred_element_type=F32) + bias
                gated_ref[rows, lanes] = (u[rows, lanes] * mixed).astype(BF16)
        o_ref[...] = x + jnp.dot(gated_ref[...], wout_ref[...], preferred_element_type=F32)

    @pl.when(i == ODD_STEPS)
    def _():
        sample_copy().wait()
        x = xbuf[sample_slot, 0:SAMPLE_TOK, :]
        u, v = _sgu_front(x, g_ref, win_ref, gv_ref)
        for t in range(DEC_SEQ):
            v_ref[:, t, :] = v[t * DEC_BATCH:(t + 1) * DEC_BATCH, :]
        gated = []
        for t in range(DEC_SEQ):
            mixed = sbias_ref[t:t + 1, :]
            for s in range(t + 1):
                mixed = mixed + coef_ref[t, s:s + 1, :] * v[s * DEC_BATCH:(s + 1) * DEC_BATCH, :]
            gated.append((u[t * DEC_BATCH:(t + 1) * DEC_BATCH, :] * mixed).astype(BF16))
        o_ref[0:SAMPLE_TOK, :] = x + jnp.dot(
            jnp.concatenate(gated, axis=0), wout_ref[...], preferred_element_type=F32)


def _odd_mixer(x, gain, w_in, g_v, w_s, b_s, w_out):
    bias = jnp.broadcast_to(b_s[:, :, None], (SGU_GROUPS, CHUNK, LANES))
    coef = jnp.repeat(jnp.transpose(w_s[:, :DEC_SEQ, :DEC_SEQ], (1, 2, 0)), SGU_GROUP_W, axis=-1)
    sbias = jnp.repeat(b_s[:, :DEC_SEQ].T, SGU_GROUP_W, axis=-1)
    return pl.pallas_call(
        _odd_kernel,
        out_shape=(
            jax.ShapeDtypeStruct((N_TOK, D_MODEL), F32),
            jax.ShapeDtypeStruct((DEC_BATCH, DEC_SEQ, SGU_WIDTH), F32),
        ),
        grid=(ODD_STEPS + 1,),
        in_specs=[
            pl.BlockSpec(memory_space=pl.ANY),
            _resident((1, D_MODEL)),
            _resident((D_MODEL, 2 * SGU_WIDTH)),
            _resident((1, SGU_WIDTH)),
            _resident((SGU_GROUPS, CHUNK, CHUNK)),
            _resident((SGU_GROUPS, CHUNK, LANES)),
            _resident((DEC_SEQ, DEC_SEQ, SGU_WIDTH)),
            _resident((DEC_SEQ, SGU_WIDTH)),
            _resident((SGU_WIDTH, D_MODEL)),
        ],
        out_specs=(
            pl.BlockSpec((ODD_TM, D_MODEL), lambda i: (i, 0)),
            pl.BlockSpec((DEC_BATCH, DEC_SEQ, SGU_WIDTH), lambda i: (0, 0, 0)),
        ),
        scratch_shapes=[pltpu.VMEM((ODD_TM, SGU_WIDTH), BF16),
                        pltpu.VMEM((X_RING, ODD_TM, D_MODEL), F32),
                        pltpu.SemaphoreType.DMA((X_RING,))],
        compiler_params=pltpu.CompilerParams(
            dimension_semantics=("arbitrary",), vmem_limit_bytes=ODD_VMEM),
        name="odd_mixer",
    )(x, gain.reshape(1, D_MODEL), w_in, g_v.reshape(1, SGU_WIDTH), w_s, bias, coef, sbias, w_out)


def _window_to_lanes(w):
    return jnp.transpose(w, (0, 2, 3, 1)).reshape(DEC_BATCH, KV_WIDTH, WINDOW)


def _window_from_lanes(w):
    return jnp.transpose(w.reshape(DEC_BATCH, N_KV_A, HEAD_DIM, WINDOW), (0, 3, 1, 2))


def kernel(x_prompt, x_sample, state_win_k, state_win_v, state_pool, rel_bias, norm_gains, final_gain,
           ffn_gate, ffn_up, ffn_down, w_in_even, w_out_even, attn_sinks, w_pool, pool_scale,
           w_in_odd, sgu_norm, w_spatial, b_spatial, w_out_odd):
    xp = x_prompt.reshape(PROMPT_TOK, D_MODEL)

    kp_l, vp_l, pp_l, ks_l, vs_l, ps_l, sv_l = [], [], [], [], [], [], []
    x = None
    y_prompt = y_sample = None
    for l in range(DEPTH):
        fa = (norm_gains[l, 0], ffn_gate, ffn_up, ffn_down, l, 0)
        fb = (norm_gains[l, 2], ffn_gate, ffn_up, ffn_down, l, 1)
        if l == 0:
            x = _ffn(xp, *fa, tail_in=x_sample)
        else:
            x = _ffn(x, *fa)
        g1 = norm_gains[l, 1]
        if l % 2 == 0:
            e = l // 2
            wq = w_in_even[e][:, :ATTN_WIDTH].reshape(D_MODEL, N_KV_A, GQA_GROUP, HEAD_DIM)
            wq = jnp.swapaxes(wq, 1, 2).reshape(D_MODEL, ATTN_WIDTH)
            wo = w_out_even[e][:ATTN_WIDTH].reshape(N_KV_A, GQA_GROUP, HEAD_DIM, D_MODEL)
            wo = jnp.swapaxes(wo, 0, 1).reshape(ATTN_WIDTH, D_MODEL)
            w_in = jnp.concatenate([wq, w_in_even[e][:, ATTN_WIDTH:]], axis=1).astype(BF16)
            w_out = jnp.concatenate([wo, w_out_even[e][ATTN_WIDTH:]], axis=0).astype(BF16)
            wp = w_pool[e].astype(BF16)
            x, kp, vp, pp = _even_prompt(x, g1, w_in, w_out, wp, pool_scale[e], rel_bias, attn_sinks[e])
            x, nk, nv, npool = _even_sample(
                x, g1, w_in, w_out, wp, pool_scale[e], rel_bias, attn_sinks[e],
                _window_to_lanes(state_win_k[e]), _window_to_lanes(state_win_v[e]),
                jnp.swapaxes(state_pool[e], 0, 1))
            kp_l.append(kp.reshape(BATCH, WINDOW, N_KV_A, HEAD_DIM))
            vp_l.append(vp.reshape(BATCH, WINDOW, N_KV_A, HEAD_DIM))
            pp_l.append(pp[:, 2 * SUBLANES - POOL_BUF:, :])
            ks_l.append(_window_from_lanes(nk))
            vs_l.append(_window_from_lanes(nv))
            ps_l.append(jnp.swapaxes(npool, 0, 1))
        else:
            o = l // 2
            w_in, w_out = w_in_odd[o].astype(BF16), w_out_odd[o].astype(BF16)
            x, sv_new = _odd_mixer(x, g1, w_in, sgu_norm[o], w_spatial[o], b_spatial[o], w_out)
            sv_l.append(sv_new)
        if l < DEPTH - 1:
            x = _ffn(x, *fb)
        else:
            y_prompt, y_sample = _ffn(x, *fb, final_gain=final_gain)

    return (y_prompt.reshape(BATCH, SEQ, D_MODEL), y_sample,
            jnp.stack(kp_l), jnp.stack(vp_l), jnp.stack(pp_l),
            jnp.stack(ks_l), jnp.stack(vs_l), jnp.stack(ps_l), jnp.stack(sv_l))
```

```python
import functools
import math

import numpy as np
import jax
import jax.numpy as jnp
from jax import lax
from jax.experimental import pallas as pl
from jax.experimental.pallas import tpu as pltpu

F32 = jnp.float32
BF16 = jnp.bfloat16

D_MODEL = 1024
BATCH = 8
SEQ = 2048
DEPTH = 2
DEC_BATCH = 128
DEC_SEQ = 4
PAST_LEN = 16384
N_HEADS_A = 8
N_KV_A = 2
HEAD_DIM = 64
GQA_GROUP = N_HEADS_A // N_KV_A
WINDOW = 128
ATTN_WIDTH = N_HEADS_A * HEAD_DIM
KV_WIDTH = N_KV_A * HEAD_DIM
N_BUCKETS = 32
MAX_DISTANCE = 128
POOL_WINDOWS = (2, 4, 8, 16)
N_POOL_GROUPS = len(POOL_WINDOWS)
POOL_GROUP = 128
POOL_WIDTH = N_POOL_GROUPS * POOL_GROUP
POOL_BUF = max(POOL_WINDOWS) - 1
IN_EVEN = ATTN_WIDTH + 2 * KV_WIDTH + POOL_WIDTH
CHUNK = 128
SGU_WIDTH = 1024
SGU_GROUPS = 4
SGU_GROUP_W = SGU_WIDTH // SGU_GROUPS
D_FF = 2816
EPS = 1e-6
NEG = -1e30

LANES = 128
SUBLANES = 8
MIB = 1024 * 1024
VMEM_LIMIT_BYTES = 56 * MIB
EVEN_PROMPT_VMEM = 42 * MIB
EVEN_SAMPLE_VMEM = 44 * MIB
ODD_VMEM = 38 * MIB

PROMPT_TOK = BATCH * SEQ
SAMPLE_TOK = DEC_BATCH * DEC_SEQ
N_TOK = PROMPT_TOK + SAMPLE_TOK

TM = SAMPLE_TOK
EVEN_TM = 1024
assert SEQ % EVEN_TM == 0 and EVEN_TM % WINDOW == 0
TILES_PER_SEQ = SEQ // EVEN_TM
QBLK_PER_TILE = EVEN_TM // WINDOW
FFN_TM = 2 * TM
assert PROMPT_TOK % FFN_TM == 0
FFN_PROMPT_STEPS = PROMPT_TOK // FFN_TM
ODD_TM = 1024
assert PROMPT_TOK % ODD_TM == 0 and ODD_TM % CHUNK == 0
ODD_STEPS = PROMPT_TOK // ODD_TM
assert SAMPLE_TOK <= ODD_TM
FF_CHUNK = 256
N_FF_CHUNKS = D_FF // FF_CHUNK
FF_SLOTS = 3
POOL_HEAD = 2 * SUBLANES
assert all(w & (w - 1) == 0 and w <= POOL_HEAD for w in POOL_WINDOWS)
POOL_CTX = 32
SEQ_GROUP = 8
N_SEQ_GROUPS = DEC_BATCH // SEQ_GROUP
GROUPS_PER_STEP = 4
assert N_SEQ_GROUPS % GROUPS_PER_STEP == 0
STATE_KEYS = SEQ_GROUP * WINDOW
NEW_SLOTS = SUBLANES
assert DEC_SEQ <= NEW_SLOTS and SEQ_GROUP * NEW_SLOTS <= WINDOW
SAMPLE_KEYS = STATE_KEYS + WINDOW
SAMPLE_ROWS = DEC_SEQ * SEQ_GROUP


assert N_KV_A == 2 and KV_WIDTH == LANES
HEAD_ORDER = [h for j in range(GQA_GROUP) for h in (j, j + GQA_GROUP)]


def _t5_bucket_np(dist):
    n = np.maximum(dist, 0)
    max_exact = N_BUCKETS // 2
    nf = np.maximum(n, 1).astype(np.float32)
    large = max_exact + (np.log(nf / np.float32(max_exact)) / np.float32(math.log(MAX_DISTANCE / max_exact))
                         * np.float32(N_BUCKETS - max_exact)).astype(np.int32)
    large = np.minimum(large, N_BUCKETS - 1)
    return np.where(n < max_exact, n, large).astype(np.int32)


def _prompt_bucket_table():
    qi = np.arange(WINDOW)[:, None]
    c = np.arange(WINDOW)[None, :]
    dist = np.where(c > qi, qi + WINDOW - c, qi - c)
    return _t5_bucket_np(dist)


def _sample_bucket_table():
    t = (np.arange(SAMPLE_ROWS) // SEQ_GROUP)[:, None]
    b = (np.arange(SAMPLE_ROWS) % SEQ_GROUP)[:, None]
    col = np.arange(SAMPLE_KEYS)[None, :]
    is_state = col < STATE_KEYS
    slot = (col - STATE_KEYS) % NEW_SLOTS
    is_new = (col >= STATE_KEYS) & (col < STATE_KEYS + SEQ_GROUP * NEW_SLOTS) & (slot < DEC_SEQ)
    kb = np.where(is_state, col // WINDOW, (col - STATE_KEYS) // NEW_SLOTS)
    kpos = np.where(is_state, col % WINDOW, WINDOW + slot)
    dist = (WINDOW + t) - kpos
    valid = (is_state | is_new) & (kb == b) & (dist >= 0) & (dist < WINDOW)
    return np.where(valid, _t5_bucket_np(dist), -1).astype(np.int32)


def _rms(x, g):
    y = x * lax.rsqrt(jnp.mean(x * x, axis=-1, keepdims=True) + EPS)
    return y * g


LOG2E = math.log2(math.e)
Q_SCALE = HEAD_DIM ** -0.5 * LOG2E


def _bias_from_buckets(bucket, rb_ref, h):
    t = jnp.zeros(bucket.shape, F32)
    for bkt in range(N_BUCKETS):
        t = jnp.where(bucket == bkt, rb_ref[bkt, h] * LOG2E, t)
    return t


def _sink_softmax(s, sink):
    m = jnp.maximum(jnp.max(s, axis=-1, keepdims=True), sink)
    p = jnp.exp2(s - m)
    return p, jnp.sum(p, axis=-1, keepdims=True) + jnp.exp2(sink - m)


def _softmax_pv(s, valid, bias, sink, v):
    p, denom = _sink_softmax(jnp.where(valid, s + bias, NEG), sink)
    return jnp.dot(p.astype(BF16), v, preferred_element_type=F32) / denom


def _ffn_kernel(*refs, layer, half, tail_in, final_split, unroll_halves):
    refs = list(refs)
    x_ref = refs.pop(0)
    xt_ref = refs.pop(0) if tail_in else None
    g_ref, wg_hbm, wu_hbm, wd_hbm = refs[:4]
    del refs[:4]
    fg_ref = refs.pop(0) if final_split else None
    o_ref = refs.pop(0)
    ot_ref = refs.pop(0) if final_split else None
    wg, wu, wd, stage_in, stage_dn, sem = refs
    i = pl.program_id(0)

    def normed(x):
        return _rms(x, g_ref[...]).astype(BF16)

    def result(x, y):
        out = x + 0.5 * y
        return _rms(out, fg_ref[...]) if final_split else out

    def chunk_copies(c, slot):
        cols = pl.ds(c * FF_CHUNK, FF_CHUNK)
        return (
            pltpu.make_async_copy(wg_hbm.at[layer, half, :, cols], stage_in.at[0, slot], sem.at[0, slot]),
            pltpu.make_async_copy(wu_hbm.at[layer, half, :, cols], stage_in.at[1, slot], sem.at[1, slot]),
            pltpu.make_async_copy(wd_hbm.at[layer, half, cols, :], stage_dn.at[slot], sem.at[2, slot]),
        )

    @pl.when(i == 0)
    def _():
        for c in range(min(FF_SLOTS, N_FF_CHUNKS)):
            for cp in chunk_copies(c, c % FF_SLOTS):
                cp.start()
        if tail_in:
            x = jnp.concatenate([xt_ref[:, t, :] for t in range(DEC_SEQ)], axis=0)
        else:
            x = x_ref[0:TM, :]
        h = normed(x)
        acc = jnp.zeros((TM, D_MODEL), F32)
        for c in range(N_FF_CHUNKS):
            slot = c % FF_SLOTS
            cols = slice(c * FF_CHUNK, (c + 1) * FF_CHUNK)
            for cp in chunk_copies(c, slot):
                cp.wait()
            wg[:, cols] = stage_in[0, slot].astype(BF16)
            wu[:, cols] = stage_in[1, slot].astype(BF16)
            wd[cols, :] = stage_dn[slot].astype(BF16)
            if c + FF_SLOTS < N_FF_CHUNKS:
                for cp in chunk_copies(c + FF_SLOTS, slot):
                    cp.start()
            gate = jnp.dot(h, wg[:, cols], preferred_element_type=F32)
            up = jnp.dot(h, wu[:, cols], preferred_element_type=F32)
            act = (jax.nn.silu(gate) * up).astype(BF16)
            acc = acc + jnp.dot(act, wd[cols, :], preferred_element_type=F32)
        out = result(x, acc)
        if final_split:
            for t in range(DEC_SEQ):
                ot_ref[:, t, :] = out[t * DEC_BATCH:(t + 1) * DEC_BATCH, :]
        else:
            o_ref[0:TM, :] = out

    def half_block(rows):
        x = x_ref[rows, :]
        h = normed(x)
        gate = jnp.dot(h, wg[...], preferred_element_type=F32)
        up = jnp.dot(h, wu[...], preferred_element_type=F32)
        act = (jax.nn.silu(gate) * up).astype(BF16)
        o_ref[rows, :] = result(x, jnp.dot(act, wd[...], preferred_element_type=F32))

    @pl.when(i > 0)
    def _():
        if unroll_halves:
            for hf in range(FFN_TM // TM):
                half_block(slice(hf * TM, (hf + 1) * TM))
        else:
            def body(hf, carry):
                half_block(pl.ds(pl.multiple_of(hf * TM, TM), TM))
                return carry
            lax.fori_loop(0, FFN_TM // TM, body, 0)


def _resident(shape):
    return pl.BlockSpec(shape, lambda *_: (0,) * len(shape), pipeline_mode=pl.Buffered(1))


def _ffn(x_in, gain, wg, wu, wd, layer, half, *, tail_in=None, final_gain=None):
    hbm = pl.BlockSpec(memory_space=pl.ANY)
    row_block = pl.BlockSpec((FFN_TM, D_MODEL), lambda i: (jnp.where(i == 0, FFN_PROMPT_STEPS, i - 1), 0))
    prompt_block = pl.BlockSpec((FFN_TM, D_MODEL), lambda i: (jnp.maximum(i - 1, 0), 0))
    sample_block = pl.BlockSpec((DEC_BATCH, DEC_SEQ, D_MODEL), lambda i: (0, 0, 0))
    if tail_in is None:
        args, in_specs = [x_in], [row_block]
    else:
        args, in_specs = [x_in, tail_in], [prompt_block, sample_block]
    args += [gain.reshape(1, D_MODEL), wg, wu, wd]
    in_specs += [_resident((1, D_MODEL)), hbm, hbm, hbm]
    if final_gain is None:
        out_shape = jax.ShapeDtypeStruct((N_TOK, D_MODEL), F32)
        out_specs = row_block
    else:
        args.append(final_gain.reshape(1, D_MODEL))
        in_specs.append(_resident((1, D_MODEL)))
        out_shape = (jax.ShapeDtypeStruct((PROMPT_TOK, D_MODEL), F32),
                     jax.ShapeDtypeStruct((DEC_BATCH, DEC_SEQ, D_MODEL), F32))
        out_specs = (prompt_block, sample_block)
    aliases = {0: 0} if (tail_in is None and final_gain is None) else {}
    return pl.pallas_call(
        functools.partial(_ffn_kernel, layer=layer, half=half, tail_in=tail_in is not None,
                          final_split=final_gain is not None, unroll_halves=(layer == 0 or half == 1)),
        out_shape=out_shape,
        grid=(FFN_PROMPT_STEPS + 1,),
        in_specs=in_specs,
        out_specs=out_specs,
        scratch_shapes=[
            pltpu.VMEM((D_MODEL, D_FF), BF16),
            pltpu.VMEM((D_MODEL, D_FF), BF16),
            pltpu.VMEM((D_FF, D_MODEL), BF16),
            pltpu.VMEM((2, FF_SLOTS, D_MODEL, FF_CHUNK), F32),
            pltpu.VMEM((FF_SLOTS, FF_CHUNK, D_MODEL), F32),
            pltpu.SemaphoreType.DMA((3, FF_SLOTS)),
        ],
        input_output_aliases=aliases,
        compiler_params=pltpu.CompilerParams(
            dimension_semantics=("arbitrary",), vmem_limit_bytes=VMEM_LIMIT_BYTES),
        name="macaron_half",
    )(*args)


def _even_prompt_kernel(x_ref, g_ref, win_ref, wout_ref, wpool_ref, pscale_ref, bucket_ref, rb_ref, sink_ref,
                        o_ref, klast_ref, vlast_ref, plast_ref,
                        kbuf, vbuf, zb, s2b, s4b, s8b, bias_ref, cat_ref):
    b = pl.program_id(0)
    s = pl.program_id(1)
    last = pl.num_programs(1) - 1

    @pl.when((b == 0) & (s == 0))
    def _():
        bucket = bucket_ref[...]
        for h in range(N_HEADS_A):
            bias_ref[h] = _bias_from_buckets(bucket, rb_ref, h)

    @pl.when(s == 0)
    def _():
        for buf in (kbuf, vbuf):
            buf[0:WINDOW, :] = jnp.zeros((WINDOW, KV_WIDTH), BF16)
        zb[0:POOL_CTX, :] = jnp.zeros((POOL_CTX, POOL_WIDTH), F32)

    @pl.when(s > 0)
    def _():
        for buf in (kbuf, vbuf):
            buf[0:WINDOW, :] = buf[EVEN_TM:EVEN_TM + WINDOW, :]
        zb[0:POOL_CTX, :] = zb[EVEN_TM:EVEN_TM + POOL_CTX, :]

    x = x_ref[...]
    h = _rms(x, g_ref[...]).astype(BF16)
    proj = jnp.dot(h, win_ref[...], preferred_element_type=F32)
    q = proj[:, 0:ATTN_WIDTH] * Q_SCALE
    k = proj[:, ATTN_WIDTH:ATTN_WIDTH + KV_WIDTH]
    v = proj[:, ATTN_WIDTH + KV_WIDTH:ATTN_WIDTH + 2 * KV_WIDTH]
    up = proj[:, ATTN_WIDTH + 2 * KV_WIDTH:IN_EVEN]

    qlane = lax.broadcasted_iota(jnp.int32, (EVEN_TM, ATTN_WIDTH), 1) % LANES
    q_half = (jnp.where(qlane < HEAD_DIM, q, 0.0).astype(BF16),
              jnp.where(qlane >= HEAD_DIM, q, 0.0).astype(BF16))
    kbuf[WINDOW:, :] = k.astype(BF16)
    vbuf[WINDOW:, :] = v.astype(BF16)

    qrow = lax.broadcasted_iota(jnp.int32, (WINDOW, WINDOW), 0)
    kcol = lax.broadcasted_iota(jnp.int32, (WINDOW, WINDOW), 1)
    prev = kcol > qrow
    olane = lax.broadcasted_iota(jnp.int32, (WINDOW, LANES), 1)
    for i in range(QBLK_PER_TILE):
        rows = slice(i * WINDOW, (i + 1) * WINDOW)
        keys = slice(i * WINDOW, (i + 2) * WINDOW)
        qs = jnp.concatenate(
            [q_half[hd // GQA_GROUP][rows, (hd % GQA_GROUP) * LANES:(hd % GQA_GROUP + 1) * LANES]
             for hd in range(N_HEADS_A)], axis=0)
        sc_all = lax.dot_general(qs, kbuf[keys, :], (((1,), (1,)), ((), ())), preferred_element_type=F32)
        ps, denoms = [], []
        for hd in range(N_HEADS_A):
            sc = sc_all[hd * WINDOW:(hd + 1) * WINDOW, :]
            sc = jnp.where(prev, sc[:, 0:WINDOW], sc[:, WINDOW:]) + bias_ref[hd]
            if i == 0:
                sc = jnp.where(prev & (qrow >= jnp.where(s > 0, WINDOW, 0)), NEG, sc)
            p, denom = _sink_softmax(sc, sink_ref[hd] * LOG2E)
            ps.append(jnp.concatenate([jnp.where(prev, p, 0.0), jnp.where(prev, 0.0, p)], axis=1).astype(BF16))
            denoms.append(denom)
        o_all = jnp.dot(jnp.concatenate(ps, axis=0), vbuf[keys, :], preferred_element_type=F32)
        out = [o_all[hd * WINDOW:(hd + 1) * WINDOW, :] / denoms[hd] for hd in range(N_HEADS_A)]
        for j in range(GQA_GROUP):
            cat_ref[rows, j * LANES:(j + 1) * LANES] = jnp.where(
                olane < HEAD_DIM, out[j], out[j + GQA_GROUP]).astype(BF16)

    zb[POOL_CTX:, :] = up
    n = EVEN_TM + POOL_CTX
    s2b[8:, :] = zb[8:n, :] + zb[7:n - 1, :]
    s4b[16:, :] = s2b[16:n, POOL_GROUP:] + s2b[14:n - 2, POOL_GROUP:]
    s8b[24:, :] = s4b[24:n, POOL_GROUP:] + s4b[20:n - 4, POOL_GROUP:]
    s16 = s8b[32:n, POOL_GROUP:] + s8b[24:n - 8, POOL_GROUP:]
    sums = (s2b[POOL_CTX:, 0:POOL_GROUP], s4b[POOL_CTX:, 0:POOL_GROUP], s8b[POOL_CTX:, 0:POOL_GROUP], s16)
    pos1 = lax.broadcasted_iota(jnp.int32, (POOL_HEAD, POOL_GROUP), 0) + s * EVEN_TM + 1
    for g, w in enumerate(POOL_WINDOWS):
        lanes = slice(g * POOL_GROUP, (g + 1) * POOL_GROUP)
        cnt = jnp.minimum(pos1, w).astype(F32)
        mean = jnp.concatenate([sums[g][0:POOL_HEAD] / cnt, sums[g][POOL_HEAD:] * (1.0 / w)], axis=0)
        d = (mean - up[:, lanes]).astype(BF16)
        y = jnp.dot(d, wpool_ref[g], preferred_element_type=F32) * pscale_ref[:, lanes]
        cat_ref[:, ATTN_WIDTH + g * POOL_GROUP:ATTN_WIDTH + (g + 1) * POOL_GROUP] = y.astype(BF16)

    o_ref[...] = x + jnp.dot(cat_ref[...], wout_ref[...], preferred_element_type=F32)

    @pl.when(s == last)
    def _():
        klast_ref[0] = k[EVEN_TM - WINDOW:, :]
        vlast_ref[0] = v[EVEN_TM - WINDOW:, :]
        plast_ref[0] = up[EVEN_TM - 2 * SUBLANES:, :]


def _even_prompt(x, gain, w_in, w_out, w_pool, pool_scale, rel_bias, sinks):
    bucket = jnp.asarray(_prompt_bucket_table())
    smem = pl.BlockSpec(memory_space=pltpu.SMEM)
    return pl.pallas_call(
        _even_prompt_kernel,
        out_shape=(
            jax.ShapeDtypeStruct((N_TOK, D_MODEL), F32),
            jax.ShapeDtypeStruct((BATCH, WINDOW, KV_WIDTH), F32),
            jax.ShapeDtypeStruct((BATCH, WINDOW, KV_WIDTH), F32),
            jax.ShapeDtypeStruct((BATCH, 2 * SUBLANES, POOL_WIDTH), F32),
        ),
        grid=(BATCH, TILES_PER_SEQ),
        in_specs=[
            pl.BlockSpec((EVEN_TM, D_MODEL), lambda b, s: (b * TILES_PER_SEQ + s, 0)),
            _resident((1, D_MODEL)),
            _resident((D_MODEL, IN_EVEN)),
            _resident((ATTN_WIDTH + POOL_WIDTH, D_MODEL)),
            _resident((N_POOL_GROUPS, POOL_GROUP, POOL_GROUP)),
            _resident((1, POOL_WIDTH)),
            _resident((WINDOW, WINDOW)),
            smem,
            smem,
        ],
        out_specs=(
            pl.BlockSpec((EVEN_TM, D_MODEL), lambda b, s: (b * TILES_PER_SEQ + s, 0)),
            pl.BlockSpec((1, WINDOW, KV_WIDTH), lambda b, s: (b, 0, 0)),
            pl.BlockSpec((1, WINDOW, KV_WIDTH), lambda b, s: (b, 0, 0)),
            pl.BlockSpec((1, 2 * SUBLANES, POOL_WIDTH), lambda b, s: (b, 0, 0)),
        ),
        scratch_shapes=[
            pltpu.VMEM((WINDOW + EVEN_TM, KV_WIDTH), BF16),
            pltpu.VMEM((WINDOW + EVEN_TM, KV_WIDTH), BF16),
            pltpu.VMEM((POOL_CTX + EVEN_TM, POOL_WIDTH), F32),
            pltpu.VMEM((POOL_CTX + EVEN_TM, POOL_WIDTH), F32),
            pltpu.VMEM((POOL_CTX + EVEN_TM, POOL_WIDTH - POOL_GROUP), F32),
            pltpu.VMEM((POOL_CTX + EVEN_TM, POOL_WIDTH - 2 * POOL_GROUP), F32),
            pltpu.VMEM((N_HEADS_A, WINDOW, WINDOW), F32),
            pltpu.VMEM((EVEN_TM, ATTN_WIDTH + POOL_WIDTH), BF16),
        ],
        input_output_aliases={0: 0},
        compiler_params=pltpu.CompilerParams(
            dimension_semantics=("arbitrary", "arbitrary"), vmem_limit_bytes=EVEN_PROMPT_VMEM),
        name="even_mixer_prompt",
    )(x, gain.reshape(1, D_MODEL), w_in, w_out, w_pool, pool_scale.reshape(1, POOL_WIDTH), bucket,
      rel_bias, sinks)


def _even_sample_kernel(x_ref, g_ref, win_ref, wout_ref, wpool_ref, pscale_ref, bucket_ref, rb_ref, sink_ref,
                        sk_ref, sv_ref, spool_ref,
                        o_ref, nk_ref, nv_ref, npool_ref,
                        knew_ref, vnew_ref, upnew_ref, qsel, bias_ref, attn_f32, cat_ref):
    g = pl.program_id(0)
    last = pl.num_programs(0) - 1

    @pl.when(g == 0)
    def _():
        bucket = bucket_ref[...]
        for h in range(N_HEADS_A):
            bias_ref[h] = _bias_from_buckets(bucket, rb_ref, h)
        x = x_ref[...]
        h = _rms(x, g_ref[...]).astype(BF16)
        proj = jnp.dot(h, win_ref[...], preferred_element_type=F32)
        q = proj[:, 0:ATTN_WIDTH] * Q_SCALE
        zeros = jnp.zeros(((NEW_SLOTS - DEC_SEQ) * DEC_BATCH, KV_WIDTH), F32)
        knew_ref[0:SAMPLE_TOK, :] = proj[:, ATTN_WIDTH:ATTN_WIDTH + KV_WIDTH]
        vnew_ref[0:SAMPLE_TOK, :] = proj[:, ATTN_WIDTH + KV_WIDTH:ATTN_WIDTH + 2 * KV_WIDTH]
        knew_ref[SAMPLE_TOK:, :] = zeros
        vnew_ref[SAMPLE_TOK:, :] = zeros
        upnew_ref[...] = proj[:, ATTN_WIDTH + 2 * KV_WIDTH:IN_EVEN]
        qlane = lax.broadcasted_iota(jnp.int32, (SAMPLE_TOK, LANES), 1)
        for hd in range(N_HEADS_A):
            j, kvh = hd % GQA_GROUP, hd // GQA_GROUP
            keep = (qlane < HEAD_DIM) if kvh == 0 else (qlane >= HEAD_DIM)
            qsel[hd] = jnp.where(keep, q[:, j * LANES:(j + 1) * LANES], 0.0)

    lax.fori_loop(0, GROUPS_PER_STEP, functools.partial(
        _even_sample_group, g, bucket_ref, sink_ref, sk_ref, sv_ref, nk_ref, nv_ref,
        knew_ref, vnew_ref, qsel, bias_ref, attn_f32), 0)

    @pl.when(g == last)
    def _():
        cat_ref[:, 0:ATTN_WIDTH] = attn_f32[...].astype(BF16)
        for r in range(POOL_BUF):
            if r < POOL_BUF - DEC_SEQ:
                npool_ref[r] = spool_ref[r + DEC_SEQ]
            else:
                t = r - (POOL_BUF - DEC_SEQ)
                npool_ref[r] = upnew_ref[t * DEC_BATCH:(t + 1) * DEC_BATCH, :]
        for t in range(DEC_SEQ):
            rows = slice(t * DEC_BATCH, (t + 1) * DEC_BATCH)
            for gi, w in enumerate(POOL_WINDOWS):
                lanes = slice(gi * POOL_GROUP, (gi + 1) * POOL_GROUP)
                acc = upnew_ref[rows, lanes]
                for back in range(1, w):
                    tt = t - back
                    if tt >= 0:
                        acc = acc + upnew_ref[tt * DEC_BATCH:(tt + 1) * DEC_BATCH, lanes]
                    else:
                        acc = acc + spool_ref[POOL_BUF + tt, :, lanes]
                cnt = float(min(PAST_LEN + t + 1, w))
                d = (acc / cnt - upnew_ref[rows, lanes]).astype(BF16)
                y = jnp.dot(d, wpool_ref[gi], preferred_element_type=F32) * pscale_ref[:, lanes]
                cat_ref[rows, ATTN_WIDTH + gi * POOL_GROUP:ATTN_WIDTH + (gi + 1) * POOL_GROUP] = y.astype(BF16)
        o_ref[...] = x_ref[...] + jnp.dot(cat_ref[...], wout_ref[...], preferred_element_type=F32)


def _even_sample_group(g, bucket_ref, sink_ref, sk_ref, sv_ref, nk_ref, nv_ref,
                       knew_ref, vnew_ref, qsel, bias_ref, attn_f32, gg, carry):
    olane = lax.broadcasted_iota(jnp.int32, (SAMPLE_ROWS, LANES), 1)
    seq0 = gg * SEQ_GROUP
    row0 = pl.multiple_of((g * GROUPS_PER_STEP + gg) * SEQ_GROUP, SEQ_GROUP)
    pad = [jnp.zeros((WINDOW - SEQ_GROUP * NEW_SLOTS, KV_WIDTH), F32)]
    knew = jnp.concatenate(
        [knew_ref[pl.ds(row0 + b, NEW_SLOTS, stride=DEC_BATCH), :] for b in range(SEQ_GROUP)] + pad, axis=0)
    vnew = jnp.concatenate(
        [vnew_ref[pl.ds(row0 + b, NEW_SLOTS, stride=DEC_BATCH), :] for b in range(SEQ_GROUP)] + pad, axis=0)
    kt_all = jnp.concatenate([sk_ref[seq0 + b] for b in range(SEQ_GROUP)], axis=1).astype(BF16)
    vt_all = jnp.concatenate([sv_ref[seq0 + b] for b in range(SEQ_GROUP)], axis=1).astype(BF16)

    qg = jnp.concatenate(
        [qsel[hd, pl.ds(t * DEC_BATCH + row0, SEQ_GROUP), :] for hd in range(N_HEADS_A) for t in range(DEC_SEQ)],
        axis=0).astype(BF16)
    nt = (((1,), (1,)), ((), ()))
    sc = jnp.concatenate(
        [jnp.dot(qg, kt_all, preferred_element_type=F32),
         lax.dot_general(qg, knew.astype(BF16), nt, preferred_element_type=F32)], axis=1)
    valid = jnp.concatenate([bucket_ref[...]] * N_HEADS_A, axis=0) >= 0
    bias = bias_ref[...].reshape(N_HEADS_A * SAMPLE_ROWS, SAMPLE_KEYS)
    sink = jnp.concatenate(
        [jnp.full((SAMPLE_ROWS, 1), sink_ref[hd] * LOG2E, F32) for hd in range(N_HEADS_A)], axis=0)
    p, denom = _sink_softmax(jnp.where(valid, sc + bias, NEG), sink)
    p = p.astype(BF16)
    o_all = (lax.dot_general(p[:, 0:STATE_KEYS], vt_all, nt, preferred_element_type=F32)
             + jnp.dot(p[:, STATE_KEYS:], vnew.astype(BF16), preferred_element_type=F32)) / denom
    for j in range(GQA_GROUP):
        blk = jnp.where(olane < HEAD_DIM, o_all[j * SAMPLE_ROWS:(j + 1) * SAMPLE_ROWS, :],
                        o_all[(j + GQA_GROUP) * SAMPLE_ROWS:(j + GQA_GROUP + 1) * SAMPLE_ROWS, :])
        for t in range(DEC_SEQ):
            attn_f32[pl.ds(t * DEC_BATCH + row0, SEQ_GROUP), j * LANES:(j + 1) * LANES] = (
                blk[t * SEQ_GROUP:(t + 1) * SEQ_GROUP, :])

    lane = lax.broadcasted_iota(jnp.int32, (KV_WIDTH, WINDOW), 1)
    for new, old_ref, out_ref in ((knew, sk_ref, nk_ref), (vnew, sv_ref, nv_ref)):
        new_t = new.T
        for b in range(SEQ_GROUP):
            tail = pltpu.roll(new_t, (WINDOW - DEC_SEQ - b * NEW_SLOTS) % WINDOW, 1)
            out_ref[seq0 + b] = jnp.where(
                lane >= WINDOW - DEC_SEQ, tail, pltpu.roll(old_ref[seq0 + b], WINDOW - DEC_SEQ, 1))
    return carry


def _even_sample(x, gain, w_in, w_out, w_pool, pool_scale, rel_bias, sinks, state_k, state_v, state_pool_t):
    bucket = jnp.asarray(_sample_bucket_table())
    smem = pl.BlockSpec(memory_space=pltpu.SMEM)
    sample_block = PROMPT_TOK // SAMPLE_TOK
    return pl.pallas_call(
        _even_sample_kernel,
        out_shape=(
            jax.ShapeDtypeStruct((N_TOK, D_MODEL), F32),
            jax.ShapeDtypeStruct((DEC_BATCH, KV_WIDTH, WINDOW), F32),
            jax.ShapeDtypeStruct((DEC_BATCH, KV_WIDTH, WINDOW), F32),
            jax.ShapeDtypeStruct((POOL_BUF, DEC_BATCH, POOL_WIDTH), F32),
        ),
        grid=(N_SEQ_GROUPS // GROUPS_PER_STEP,),
        in_specs=[
            pl.BlockSpec((SAMPLE_TOK, D_MODEL), lambda g: (sample_block, 0)),
            _resident((1, D_MODEL)),
            _resident((D_MODEL, IN_EVEN)),
            _resident((ATTN_WIDTH + POOL_WIDTH, D_MODEL)),
            _resident((N_POOL_GROUPS, POOL_GROUP, POOL_GROUP)),
            _resident((1, POOL_WIDTH)),
            _resident((SAMPLE_ROWS, SAMPLE_KEYS)),
            smem,
            smem,
            pl.BlockSpec((GROUPS_PER_STEP * SEQ_GROUP, KV_WIDTH, WINDOW), lambda g: (g, 0, 0)),
            pl.BlockSpec((GROUPS_PER_STEP * SEQ_GROUP, KV_WIDTH, WINDOW), lambda g: (g, 0, 0)),
            _resident((POOL_BUF, DEC_BATCH, POOL_WIDTH)),
        ],
        out_specs=(
            pl.BlockSpec((SAMPLE_TOK, D_MODEL), lambda g: (sample_block, 0)),
            pl.BlockSpec((GROUPS_PER_STEP * SEQ_GROUP, KV_WIDTH, WINDOW), lambda g: (g, 0, 0)),
            pl.BlockSpec((GROUPS_PER_STEP * SEQ_GROUP, KV_WIDTH, WINDOW), lambda g: (g, 0, 0)),
            pl.BlockSpec((POOL_BUF, DEC_BATCH, POOL_WIDTH), lambda g: (0, 0, 0)),
        ),
        scratch_shapes=[
            pltpu.VMEM((NEW_SLOTS * DEC_BATCH, KV_WIDTH), F32),
            pltpu.VMEM((NEW_SLOTS * DEC_BATCH, KV_WIDTH), F32),
            pltpu.VMEM((SAMPLE_TOK, POOL_WIDTH), F32),
            pltpu.VMEM((N_HEADS_A, SAMPLE_TOK, LANES), F32),
            pltpu.VMEM((N_HEADS_A, SAMPLE_ROWS, SAMPLE_KEYS), F32),
            pltpu.VMEM((SAMPLE_TOK, ATTN_WIDTH), F32),
            pltpu.VMEM((SAMPLE_TOK, ATTN_WIDTH + POOL_WIDTH), BF16),
        ],
        input_output_aliases={0: 0},
        compiler_params=pltpu.CompilerParams(
            dimension_semantics=("arbitrary",), vmem_limit_bytes=EVEN_SAMPLE_VMEM),
        name="even_mixer_sample",
    )(x, gain.reshape(1, D_MODEL), w_in, w_out, w_pool, pool_scale.reshape(1, POOL_WIDTH), bucket,
      rel_bias, sinks, state_k, state_v, state_pool_t)


def _sgu_front(x, g_ref, win_ref, gv_ref):
    h = _rms(x, g_ref[...]).astype(BF16)
    uv = jax.nn.gelu(jnp.dot(h, win_ref[...], preferred_element_type=F32))
    return uv[:, 0:SGU_WIDTH], _rms(uv[:, SGU_WIDTH:], gv_ref[...])


def _odd_kernel(x_ref, g_ref, win_ref, gv_ref, ws_ref, bs_ref, coef_ref, sbias_ref, wout_ref,
                o_ref, v_ref, gated_ref):
    i = pl.program_id(0)

    @pl.when(i < ODD_STEPS)
    def _():
        x = x_ref[...]
        u, v = _sgu_front(x, g_ref, win_ref, gv_ref)
        vb = v.astype(BF16)
        r = lax.broadcasted_iota(jnp.int32, (CHUNK, CHUNK), 0)
        c = lax.broadcasted_iota(jnp.int32, (CHUNK, CHUNK), 1)
        for gi in range(SGU_GROUPS):
            w = jnp.where(r >= c, ws_ref[gi], 0.0).astype(BF16)
            bias = jnp.concatenate([bs_ref[gi]] * (SGU_GROUP_W // LANES), axis=1)
            lanes = slice(gi * SGU_GROUP_W, (gi + 1) * SGU_GROUP_W)
            for ci in range(ODD_TM // CHUNK):
                rows = slice(ci * CHUNK, (ci + 1) * CHUNK)
                mixed = jnp.dot(w, vb[rows, lanes], preferred_element_type=F32) + bias
                gated_ref[rows, lanes] = (u[rows, lanes] * mixed).astype(BF16)
        o_ref[...] = x + jnp.dot(gated_ref[...], wout_ref[...], preferred_element_type=F32)

    @pl.when(i == ODD_STEPS)
    def _():
        x = x_ref[0:SAMPLE_TOK, :]
        u, v = _sgu_front(x, g_ref, win_ref, gv_ref)
        for t in range(DEC_SEQ):
            v_ref[:, t, :] = v[t * DEC_BATCH:(t + 1) * DEC_BATCH, :]
        gated = []
        for t in range(DEC_SEQ):
            mixed = sbias_ref[t:t + 1, :]
            for s in range(t + 1):
                mixed = mixed + coef_ref[t, s:s + 1, :] * v[s * DEC_BATCH:(s + 1) * DEC_BATCH, :]
            gated.append((u[t * DEC_BATCH:(t + 1) * DEC_BATCH, :] * mixed).astype(BF16))
        o_ref[0:SAMPLE_TOK, :] = x + jnp.dot(
            jnp.concatenate(gated, axis=0), wout_ref[...], preferred_element_type=F32)


def _odd_mixer(x, gain, w_in, g_v, w_s, b_s, w_out):
    bias = jnp.broadcast_to(b_s[:, :, None], (SGU_GROUPS, CHUNK, LANES))
    coef = jnp.repeat(jnp.transpose(w_s[:, :DEC_SEQ, :DEC_SEQ], (1, 2, 0)), SGU_GROUP_W, axis=-1)
    sbias = jnp.repeat(b_s[:, :DEC_SEQ].T, SGU_GROUP_W, axis=-1)
    return pl.pallas_call(
        _odd_kernel,
        out_shape=(
            jax.ShapeDtypeStruct((N_TOK, D_MODEL), F32),
            jax.ShapeDtypeStruct((DEC_BATCH, DEC_SEQ, SGU_WIDTH), F32),
        ),
        grid=(ODD_STEPS + 1,),
        in_specs=[
            pl.BlockSpec((ODD_TM, D_MODEL), lambda i: (i, 0)),
            _resident((1, D_MODEL)),
            _resident((D_MODEL, 2 * SGU_WIDTH)),
            _resident((1, SGU_WIDTH)),
            _resident((SGU_GROUPS, CHUNK, CHUNK)),
            _resident((SGU_GROUPS, CHUNK, LANES)),
            _resident((DEC_SEQ, DEC_SEQ, SGU_WIDTH)),
            _resident((DEC_SEQ, SGU_WIDTH)),
            _resident((SGU_WIDTH, D_MODEL)),
        ],
        out_specs=(
            pl.BlockSpec((ODD_TM, D_MODEL), lambda i: (i, 0)),
            pl.BlockSpec((DEC_BATCH, DEC_SEQ, SGU_WIDTH), lambda i: (0, 0, 0)),
        ),
        scratch_shapes=[pltpu.VMEM((ODD_TM, SGU_WIDTH), BF16)],
        input_output_aliases={0: 0},
        compiler_params=pltpu.CompilerParams(
            dimension_semantics=("arbitrary",), vmem_limit_bytes=ODD_VMEM),
        name="odd_mixer",
    )(x, gain.reshape(1, D_MODEL), w_in, g_v.reshape(1, SGU_WIDTH), w_s, bias, coef, sbias, w_out)


def _window_to_lanes(w):
    return jnp.transpose(w, (0, 2, 3, 1)).reshape(DEC_BATCH, KV_WIDTH, WINDOW)


def _window_from_lanes(w):
    return jnp.transpose(w.reshape(DEC_BATCH, N_KV_A, HEAD_DIM, WINDOW), (0, 3, 1, 2))


def kernel(x_prompt, x_sample, state_win_k, state_win_v, state_pool, rel_bias, norm_gains, final_gain,
           ffn_gate, ffn_up, ffn_down, w_in_even, w_out_even, attn_sinks, w_pool, pool_scale,
           w_in_odd, sgu_norm, w_spatial, b_spatial, w_out_odd):
    xp = x_prompt.reshape(PROMPT_TOK, D_MODEL)

    kp_l, vp_l, pp_l, ks_l, vs_l, ps_l, sv_l = [], [], [], [], [], [], []
    x = None
    y_prompt = y_sample = None
    for l in range(DEPTH):
        fa = (norm_gains[l, 0], ffn_gate, ffn_up, ffn_down, l, 0)
        fb = (norm_gains[l, 2], ffn_gate, ffn_up, ffn_down, l, 1)
        if l == 0:
            x = _ffn(xp, *fa, tail_in=x_sample)
        else:
            x = _ffn(x, *fa)
        g1 = norm_gains[l, 1]
        if l % 2 == 0:
            e = l // 2
            wq = w_in_even[e][:, :ATTN_WIDTH].reshape(D_MODEL, N_KV_A, GQA_GROUP, HEAD_DIM)
            wq = jnp.swapaxes(wq, 1, 2).reshape(D_MODEL, ATTN_WIDTH)
            wo = w_out_even[e][:ATTN_WIDTH].reshape(N_KV_A, GQA_GROUP, HEAD_DIM, D_MODEL)
            wo = jnp.swapaxes(wo, 0, 1).reshape(ATTN_WIDTH, D_MODEL)
            w_in = jnp.concatenate([wq, w_in_even[e][:, ATTN_WIDTH:]], axis=1).astype(BF16)
            w_out = jnp.concatenate([wo, w_out_even[e][ATTN_WIDTH:]], axis=0).astype(BF16)
            wp = w_pool[e].astype(BF16)
            x, kp, vp, pp = _even_prompt(x, g1, w_in, w_out, wp, pool_scale[e], rel_bias, attn_sinks[e])
            x, nk, nv, npool = _even_sample(
                x, g1, w_in, w_out, wp, pool_scale[e], rel_bias, attn_sinks[e],
                _window_to_lanes(state_win_k[e]), _window_to_lanes(state_win_v[e]),
                jnp.swapaxes(state_pool[e], 0, 1))
            kp_l.append(kp.reshape(BATCH, WINDOW, N_KV_A, HEAD_DIM))
            vp_l.append(vp.reshape(BATCH, WINDOW, N_KV_A, HEAD_DIM))
            pp_l.append(pp[:, 2 * SUBLANES - POOL_BUF:, :])
            ks_l.append(_window_from_lanes(nk))
            vs_l.append(_window_from_lanes(nv))
            ps_l.append(jnp.swapaxes(npool, 0, 1))
        else:
            o = l // 2
            w_in, w_out = w_in_odd[o].astype(BF16), w_out_odd[o].astype(BF16)
            x, sv_new = _odd_mixer(x, g1, w_in, sgu_norm[o], w_spatial[o], b_spatial[o], w_out)
            sv_l.append(sv_new)
        if l < DEPTH - 1:
            x = _ffn(x, *fb)
        else:
            y_prompt, y_sample = _ffn(x, *fb, final_gain=final_gain)

    return (y_prompt.reshape(BATCH, SEQ, D_MODEL), y_sample,
            jnp.stack(kp_l), jnp.stack(vp_l), jnp.stack(pp_l),
            jnp.stack(ks_l), jnp.stack(vs_l), jnp.stack(ps_l), jnp.stack(sv_l))
```

```python
import functools
import math

import numpy as np
import jax
import jax.numpy as jnp
from jax import lax
from jax.experimental import pallas as pl
from jax.experimental.pallas import tpu as pltpu

F32 = jnp.float32
BF16 = jnp.bfloat16

D_MODEL = 1024
BATCH = 8
SEQ = 2048
DEPTH = 2
DEC_BATCH = 128
DEC_SEQ = 4
PAST_LEN = 16384
N_HEADS_A = 8
N_KV_A = 2
HEAD_DIM = 64
GQA_GROUP = N_HEADS_A // N_KV_A
WINDOW = 128
ATTN_WIDTH = N_HEADS_A * HEAD_DIM
KV_WIDTH = N_KV_A * HEAD_DIM
N_BUCKETS = 32
MAX_DISTANCE = 128
POOL_WINDOWS = (2, 4, 8, 16)
N_POOL_GROUPS = len(POOL_WINDOWS)
POOL_GROUP = 128
POOL_WIDTH = N_POOL_GROUPS * POOL_GROUP
POOL_BUF = max(POOL_WINDOWS) - 1
IN_EVEN = ATTN_WIDTH + 2 * KV_WIDTH + POOL_WIDTH
CHUNK = 128
SGU_WIDTH = 1024
SGU_GROUPS = 4
SGU_GROUP_W = SGU_WIDTH // SGU_GROUPS
D_FF = 2816
EPS = 1e-6
NEG = -1e30

LANES = 128
SUBLANES = 8
MIB = 1024 * 1024
VMEM_LIMIT_BYTES = 56 * MIB
EVEN_PROMPT_VMEM = 42 * MIB
EVEN_SAMPLE_VMEM = 44 * MIB
ODD_VMEM = 38 * MIB

PROMPT_TOK = BATCH * SEQ
SAMPLE_TOK = DEC_BATCH * DEC_SEQ
N_TOK = PROMPT_TOK + SAMPLE_TOK

TM = SAMPLE_TOK
EVEN_TM = 1024
assert SEQ % EVEN_TM == 0 and EVEN_TM % WINDOW == 0
TILES_PER_SEQ = SEQ // EVEN_TM
QBLK_PER_TILE = EVEN_TM // WINDOW
FFN_TM = 2 * TM
assert PROMPT_TOK % FFN_TM == 0
FFN_PROMPT_STEPS = PROMPT_TOK // FFN_TM
ODD_TM = 1024
assert PROMPT_TOK % ODD_TM == 0 and ODD_TM % CHUNK == 0
ODD_STEPS = PROMPT_TOK // ODD_TM
assert SAMPLE_TOK <= ODD_TM
FF_CHUNK = 256
N_FF_CHUNKS = D_FF // FF_CHUNK
FF_SLOTS = 3
POOL_HEAD = 2 * SUBLANES
assert all(w & (w - 1) == 0 and w <= POOL_HEAD for w in POOL_WINDOWS)
POOL_CTX = 32
SEQ_GROUP = 8
N_SEQ_GROUPS = DEC_BATCH // SEQ_GROUP
GROUPS_PER_STEP = 4
assert N_SEQ_GROUPS % GROUPS_PER_STEP == 0
STATE_KEYS = SEQ_GROUP * WINDOW
NEW_SLOTS = SUBLANES
assert DEC_SEQ <= NEW_SLOTS and SEQ_GROUP * NEW_SLOTS <= WINDOW
SAMPLE_KEYS = STATE_KEYS + WINDOW
SAMPLE_ROWS = DEC_SEQ * SEQ_GROUP


assert N_KV_A == 2 and KV_WIDTH == LANES
HEAD_ORDER = [h for j in range(GQA_GROUP) for h in (j, j + GQA_GROUP)]


def _t5_bucket_np(dist):
    n = np.maximum(dist, 0)
    max_exact = N_BUCKETS // 2
    nf = np.maximum(n, 1).astype(np.float32)
    large = max_exact + (np.log(nf / np.float32(max_exact)) / np.float32(math.log(MAX_DISTANCE / max_exact))
                         * np.float32(N_BUCKETS - max_exact)).astype(np.int32)
    large = np.minimum(large, N_BUCKETS - 1)
    return np.where(n < max_exact, n, large).astype(np.int32)


def _prompt_bucket_table():
    qi = np.arange(WINDOW)[:, None]
    c = np.arange(WINDOW)[None, :]
    dist = np.where(c > qi, qi + WINDOW - c, qi - c)
    return _t5_bucket_np(dist)


def _sample_bucket_table():
    t = (np.arange(SAMPLE_ROWS) // SEQ_GROUP)[:, None]
    b = (np.arange(SAMPLE_ROWS) % SEQ_GROUP)[:, None]
    col = np.arange(SAMPLE_KEYS)[None, :]
    is_state = col < STATE_KEYS
    slot = (col - STATE_KEYS) % NEW_SLOTS
    is_new = (col >= STATE_KEYS) & (col < STATE_KEYS + SEQ_GROUP * NEW_SLOTS) & (slot < DEC_SEQ)
    kb = np.where(is_state, col // WINDOW, (col - STATE_KEYS) // NEW_SLOTS)
    kpos = np.where(is_state, col % WINDOW, WINDOW + slot)
    dist = (WINDOW + t) - kpos
    valid = (is_state | is_new) & (kb == b) & (dist >= 0) & (dist < WINDOW)
    return np.where(valid, _t5_bucket_np(dist), -1).astype(np.int32)


def _rms(x, g):
    y = x * lax.rsqrt(jnp.mean(x * x, axis=-1, keepdims=True) + EPS)
    return y * g


LOG2E = math.log2(math.e)
Q_SCALE = HEAD_DIM ** -0.5 * LOG2E


def _bias_from_buckets(bucket, rb_ref, h):
    t = jnp.zeros(bucket.shape, F32)
    for bkt in range(N_BUCKETS):
        t = jnp.where(bucket == bkt, rb_ref[bkt, h] * LOG2E, t)
    return t


def _sink_softmax(s, sink):
    m = jnp.maximum(jnp.max(s, axis=-1, keepdims=True), sink)
    p = jnp.exp2(s - m)
    return p, jnp.sum(p, axis=-1, keepdims=True) + jnp.exp2(sink - m)


def _softmax_pv(s, valid, bias, sink, v):
    p, denom = _sink_softmax(jnp.where(valid, s + bias, NEG), sink)
    return jnp.dot(p.astype(BF16), v, preferred_element_type=F32) / denom


def _ffn_kernel(*refs, layer, half, tail_in, final_split, unroll_halves):
    refs = list(refs)
    x_ref = refs.pop(0)
    xt_ref = refs.pop(0) if tail_in else None
    g_ref, wg_hbm, wu_hbm, wd_hbm = refs[:4]
    del refs[:4]
    fg_ref = refs.pop(0) if final_split else None
    o_ref = refs.pop(0)
    ot_ref = refs.pop(0) if final_split else None
    wg, wu, wd, stage_in, stage_dn, sem = refs
    i = pl.program_id(0)

    def normed(x):
        return _rms(x, g_ref[...]).astype(BF16)

    def result(x, y):
        out = x + 0.5 * y
        return _rms(out, fg_ref[...]) if final_split else out

    def chunk_copies(c, slot):
        cols = pl.ds(c * FF_CHUNK, FF_CHUNK)
        return (
            pltpu.make_async_copy(wg_hbm.at[layer, half, :, cols], stage_in.at[0, slot], sem.at[0, slot]),
            pltpu.make_async_copy(wu_hbm.at[layer, half, :, cols], stage_in.at[1, slot], sem.at[1, slot]),
            pltpu.make_async_copy(wd_hbm.at[layer, half, cols, :], stage_dn.at[slot], sem.at[2, slot]),
        )

    @pl.when(i == 0)
    def _():
        for c in range(min(FF_SLOTS, N_FF_CHUNKS)):
            for cp in chunk_copies(c, c % FF_SLOTS):
                cp.start()
        if tail_in:
            x = jnp.concatenate([xt_ref[:, t, :] for t in range(DEC_SEQ)], axis=0)
        else:
            x = x_ref[0:TM, :]
        h = normed(x)
        acc = jnp.zeros((TM, D_MODEL), F32)
        for c in range(N_FF_CHUNKS):
            slot = c % FF_SLOTS
            cols = slice(c * FF_CHUNK, (c + 1) * FF_CHUNK)
            for cp in chunk_copies(c, slot):
                cp.wait()
            wg[:, cols] = stage_in[0, slot].astype(BF16)
            wu[:, cols] = stage_in[1, slot].astype(BF16)
            wd[cols, :] = stage_dn[slot].astype(BF16)
            if c + FF_SLOTS < N_FF_CHUNKS:
                for cp in chunk_copies(c + FF_SLOTS, slot):
                    cp.start()
            gate = jnp.dot(h, wg[:, cols], preferred_element_type=F32)
            up = jnp.dot(h, wu[:, cols], preferred_element_type=F32)
            act = (jax.nn.silu(gate) * up).astype(BF16)
            acc = acc + jnp.dot(act, wd[cols, :], preferred_element_type=F32)
        out = result(x, acc)
        if final_split:
            for t in range(DEC_SEQ):
                ot_ref[:, t, :] = out[t * DEC_BATCH:(t + 1) * DEC_BATCH, :]
        else:
            o_ref[0:TM, :] = out

    def half_block(rows):
        x = x_ref[rows, :]
        h = normed(x)
        gate = jnp.dot(h, wg[...], preferred_element_type=F32)
        up = jnp.dot(h, wu[...], preferred_element_type=F32)
        act = (jax.nn.silu(gate) * up).astype(BF16)
        o_ref[rows, :] = result(x, jnp.dot(act, wd[...], preferred_element_type=F32))

    @pl.when(i > 0)
    def _():
        if unroll_halves:
            for hf in range(FFN_TM // TM):
                half_block(slice(hf * TM, (hf + 1) * TM))
        else:
            def body(hf, carry):
                half_block(pl.ds(pl.multiple_of(hf * TM, TM), TM))
                return carry
            lax.fori_loop(0, FFN_TM // TM, body, 0)


def _resident(shape):
    return pl.BlockSpec(shape, lambda *_: (0,) * len(shape), pipeline_mode=pl.Buffered(1))


def _ffn(x_in, gain, wg, wu, wd, layer, half, *, tail_in=None, final_gain=None):
    hbm = pl.BlockSpec(memory_space=pl.ANY)
    row_block = pl.BlockSpec((FFN_TM, D_MODEL), lambda i: (jnp.where(i == 0, FFN_PROMPT_STEPS, i - 1), 0))
    prompt_block = pl.BlockSpec((FFN_TM, D_MODEL), lambda i: (jnp.maximum(i - 1, 0), 0))
    sample_block = pl.BlockSpec((DEC_BATCH, DEC_SEQ, D_MODEL), lambda i: (0, 0, 0))
    if tail_in is None:
        args, in_specs = [x_in], [row_block]
    else:
        args, in_specs = [x_in, tail_in], [prompt_block, sample_block]
    args += [gain.reshape(1, D_MODEL), wg, wu, wd]
    in_specs += [_resident((1, D_MODEL)), hbm, hbm, hbm]
    if final_gain is None:
        out_shape = jax.ShapeDtypeStruct((N_TOK, D_MODEL), F32)
        out_specs = row_block
    else:
        args.append(final_gain.reshape(1, D_MODEL))
        in_specs.append(_resident((1, D_MODEL)))
        out_shape = (jax.ShapeDtypeStruct((PROMPT_TOK, D_MODEL), F32),
                     jax.ShapeDtypeStruct((DEC_BATCH, DEC_SEQ, D_MODEL), F32))
        out_specs = (prompt_block, sample_block)
    aliases = {0: 0} if (tail_in is None and final_gain is None) else {}
    return pl.pallas_call(
        functools.partial(_ffn_kernel, layer=layer, half=half, tail_in=tail_in is not None,
                          final_split=final_gain is not None, unroll_halves=(layer == 0 or half == 1)),
        out_shape=out_shape,
        grid=(FFN_PROMPT_STEPS + 1,),
        in_specs=in_specs,
        out_specs=out_specs,
        scratch_shapes=[
            pltpu.VMEM((D_MODEL, D_FF), BF16),
            pltpu.VMEM((D_MODEL, D_FF), BF16),
            pltpu.VMEM((D_FF, D_MODEL), BF16),
            pltpu.VMEM((2, FF_SLOTS, D_MODEL, FF_CHUNK), F32),
            pltpu.VMEM((FF_SLOTS, FF_CHUNK, D_MODEL), F32),
            pltpu.SemaphoreType.DMA((3, FF_SLOTS)),
        ],
        input_output_aliases=aliases,
        compiler_params=pltpu.CompilerParams(
            dimension_semantics=("arbitrary",), vmem_limit_bytes=VMEM_LIMIT_BYTES),
        name="macaron_half",
    )(*args)


def _even_prompt_kernel(x_ref, g_ref, win_ref, wout_ref, wpool_ref, pscale_ref, bucket_ref, rb_ref, sink_ref,
                        o_ref, klast_ref, vlast_ref, plast_ref,
                        kbuf, vbuf, zb, s2b, s4b, s8b, bias_ref, cat_ref):
    b = pl.program_id(0)
    s = pl.program_id(1)
    last = pl.num_programs(1) - 1

    @pl.when((b == 0) & (s == 0))
    def _():
        bucket = bucket_ref[...]
        for h in range(N_HEADS_A):
            bias_ref[h] = _bias_from_buckets(bucket, rb_ref, h)

    @pl.when(s == 0)
    def _():
        for buf in (kbuf, vbuf):
            buf[0:WINDOW, :] = jnp.zeros((WINDOW, KV_WIDTH), BF16)
        zb[0:POOL_CTX, :] = jnp.zeros((POOL_CTX, POOL_WIDTH), F32)

    @pl.when(s > 0)
    def _():
        for buf in (kbuf, vbuf):
            buf[0:WINDOW, :] = buf[EVEN_TM:EVEN_TM + WINDOW, :]
        zb[0:POOL_CTX, :] = zb[EVEN_TM:EVEN_TM + POOL_CTX, :]

    x = x_ref[...]
    h = _rms(x, g_ref[...]).astype(BF16)
    proj = jnp.dot(h, win_ref[...], preferred_element_type=F32)
    q = proj[:, 0:ATTN_WIDTH] * Q_SCALE
    k = proj[:, ATTN_WIDTH:ATTN_WIDTH + KV_WIDTH]
    v = proj[:, ATTN_WIDTH + KV_WIDTH:ATTN_WIDTH + 2 * KV_WIDTH]
    up = proj[:, ATTN_WIDTH + 2 * KV_WIDTH:IN_EVEN]

    qlane = lax.broadcasted_iota(jnp.int32, (EVEN_TM, ATTN_WIDTH), 1) % LANES
    q_half = (jnp.where(qlane < HEAD_DIM, q, 0.0).astype(BF16),
              jnp.where(qlane >= HEAD_DIM, q, 0.0).astype(BF16))
    kbuf[WINDOW:, :] = k.astype(BF16)
    vbuf[WINDOW:, :] = v.astype(BF16)

    qrow = lax.broadcasted_iota(jnp.int32, (WINDOW, WINDOW), 0)
    kcol = lax.broadcasted_iota(jnp.int32, (WINDOW, WINDOW), 1)
    prev = kcol > qrow
    olane = lax.broadcasted_iota(jnp.int32, (WINDOW, LANES), 1)
    for i in range(QBLK_PER_TILE):
        rows = slice(i * WINDOW, (i + 1) * WINDOW)
        keys = slice(i * WINDOW, (i + 2) * WINDOW)
        qs = jnp.concatenate(
            [q_half[hd // GQA_GROUP][rows, (hd % GQA_GROUP) * LANES:(hd % GQA_GROUP + 1) * LANES]
             for hd in range(N_HEADS_A)], axis=0)
        sc_all = lax.dot_general(qs, kbuf[keys, :], (((1,), (1,)), ((), ())), preferred_element_type=F32)
        ps, denoms = [], []
        for hd in range(N_HEADS_A):
            sc = sc_all[hd * WINDOW:(hd + 1) * WINDOW, :]
            sc = jnp.where(prev, sc[:, 0:WINDOW], sc[:, WINDOW:]) + bias_ref[hd]
            if i == 0:
                sc = jnp.where(prev & (qrow >= jnp.where(s > 0, WINDOW, 0)), NEG, sc)
            p, denom = _sink_softmax(sc, sink_ref[hd] * LOG2E)
            ps.append(jnp.concatenate([jnp.where(prev, p, 0.0), jnp.where(prev, 0.0, p)], axis=1).astype(BF16))
            denoms.append(denom)
        o_all = jnp.dot(jnp.concatenate(ps, axis=0), vbuf[keys, :], preferred_element_type=F32)
        out = [o_all[hd * WINDOW:(hd + 1) * WINDOW, :] / denoms[hd] for hd in range(N_HEADS_A)]
        for j in range(GQA_GROUP):
            cat_ref[rows, j * LANES:(j + 1) * LANES] = jnp.where(
                olane < HEAD_DIM, out[j], out[j + GQA_GROUP]).astype(BF16)

    zb[POOL_CTX:, :] = up
    n = EVEN_TM + POOL_CTX
    s2b[8:, :] = zb[8:n, :] + zb[7:n - 1, :]
    s4b[16:, :] = s2b[16:n, POOL_GROUP:] + s2b[14:n - 2, POOL_GROUP:]
    s8b[24:, :] = s4b[24:n, POOL_GROUP:] + s4b[20:n - 4, POOL_GROUP:]
    s16 = s8b[32:n, POOL_GROUP:] + s8b[24:n - 8, POOL_GROUP:]
    sums = (s2b[POOL_CTX:, 0:POOL_GROUP], s4b[POOL_CTX:, 0:POOL_GROUP], s8b[POOL_CTX:, 0:POOL_GROUP], s16)
    pos1 = lax.broadcasted_iota(jnp.int32, (POOL_HEAD, POOL_GROUP), 0) + s * EVEN_TM + 1
    for g, w in enumerate(POOL_WINDOWS):
        lanes = slice(g * POOL_GROUP, (g + 1) * POOL_GROUP)
        cnt = jnp.minimum(pos1, w).astype(F32)
        mean = jnp.concatenate([sums[g][0:POOL_HEAD] / cnt, sums[g][POOL_HEAD:] * (1.0 / w)], axis=0)
        d = (mean - up[:, lanes]).astype(BF16)
        y = jnp.dot(d, wpool_ref[g], preferred_element_type=F32) * pscale_ref[:, lanes]
        cat_ref[:, ATTN_WIDTH + g * POOL_GROUP:ATTN_WIDTH + (g + 1) * POOL_GROUP] = y.astype(BF16)

    o_ref[...] = x + jnp.dot(cat_ref[...], wout_ref[...], preferred_element_type=F32)

    @pl.when(s == last)
    def _():
        klast_ref[0] = k[EVEN_TM - WINDOW:, :]
        vlast_ref[0] = v[EVEN_TM - WINDOW:, :]
        plast_ref[0] = up[EVEN_TM - 2 * SUBLANES:, :]


def _even_prompt(x, gain, w_in, w_out, w_pool, pool_scale, rel_bias, sinks):
    bucket = jnp.asarray(_prompt_bucket_table())
    smem = pl.BlockSpec(memory_space=pltpu.SMEM)
    return pl.pallas_call(
        _even_prompt_kernel,
        out_shape=(
            jax.ShapeDtypeStruct((N_TOK, D_MODEL), F32),
            jax.ShapeDtypeStruct((BATCH, WINDOW, KV_WIDTH), F32),
            jax.ShapeDtypeStruct((BATCH, WINDOW, KV_WIDTH), F32),
            jax.ShapeDtypeStruct((BATCH, 2 * SUBLANES, POOL_WIDTH), F32),
        ),
        grid=(BATCH, TILES_PER_SEQ),
        in_specs=[
            pl.BlockSpec((EVEN_TM, D_MODEL), lambda b, s: (b * TILES_PER_SEQ + s, 0)),
            _resident((1, D_MODEL)),
            _resident((D_MODEL, IN_EVEN)),
            _resident((ATTN_WIDTH + POOL_WIDTH, D_MODEL)),
            _resident((N_POOL_GROUPS, POOL_GROUP, POOL_GROUP)),
            _resident((1, POOL_WIDTH)),
            _resident((WINDOW, WINDOW)),
            smem,
            smem,
        ],
        out_specs=(
            pl.BlockSpec((EVEN_TM, D_MODEL), lambda b, s: (b * TILES_PER_SEQ + s, 0)),
            pl.BlockSpec((1, WINDOW, KV_WIDTH), lambda b, s: (b, 0, 0)),
            pl.BlockSpec((1, WINDOW, KV_WIDTH), lambda b, s: (b, 0, 0)),
            pl.BlockSpec((1, 2 * SUBLANES, POOL_WIDTH), lambda b, s: (b, 0, 0)),
        ),
        scratch_shapes=[
            pltpu.VMEM((WINDOW + EVEN_TM, KV_WIDTH), BF16),
            pltpu.VMEM((WINDOW + EVEN_TM, KV_WIDTH), BF16),
            pltpu.VMEM((POOL_CTX + EVEN_TM, POOL_WIDTH), F32),
            pltpu.VMEM((POOL_CTX + EVEN_TM, POOL_WIDTH), F32),
            pltpu.VMEM((POOL_CTX + EVEN_TM, POOL_WIDTH - POOL_GROUP), F32),
            pltpu.VMEM((POOL_CTX + EVEN_TM, POOL_WIDTH - 2 * POOL_GROUP), F32),
            pltpu.VMEM((N_HEADS_A, WINDOW, WINDOW), F32),
            pltpu.VMEM((EVEN_TM, ATTN_WIDTH + POOL_WIDTH), BF16),
        ],
        input_output_aliases={0: 0},
        compiler_params=pltpu.CompilerParams(
            dimension_semantics=("arbitrary", "arbitrary"), vmem_limit_bytes=EVEN_PROMPT_VMEM),
        name="even_mixer_prompt",
    )(x, gain.reshape(1, D_MODEL), w_in, w_out, w_pool, pool_scale.reshape(1, POOL_WIDTH), bucket,
      rel_bias, sinks)


def _even_sample_kernel(x_ref, g_ref, win_ref, wout_ref, wpool_ref, pscale_ref, bucket_ref, rb_ref, sink_ref,
                        sk_ref, sv_ref, spool_ref,
                        o_ref, nk_ref, nv_ref, npool_ref,
                        knew_ref, vnew_ref, upnew_ref, qsel, bias_ref, attn_f32, cat_ref):
    g = pl.program_id(0)
    last = pl.num_programs(0) - 1

    @pl.when(g == 0)
    def _():
        bucket = bucket_ref[...]
        for h in range(N_HEADS_A):
            bias_ref[h] = _bias_from_buckets(bucket, rb_ref, h)
        x = x_ref[...]
        h = _rms(x, g_ref[...]).astype(BF16)
        proj = jnp.dot(h, win_ref[...], preferred_element_type=F32)
        q = proj[:, 0:ATTN_WIDTH] * Q_SCALE
        zeros = jnp.zeros(((NEW_SLOTS - DEC_SEQ) * DEC_BATCH, KV_WIDTH), F32)
        knew_ref[0:SAMPLE_TOK, :] = proj[:, ATTN_WIDTH:ATTN_WIDTH + KV_WIDTH]
        vnew_ref[0:SAMPLE_TOK, :] = proj[:, ATTN_WIDTH + KV_WIDTH:ATTN_WIDTH + 2 * KV_WIDTH]
        knew_ref[SAMPLE_TOK:, :] = zeros
        vnew_ref[SAMPLE_TOK:, :] = zeros
        upnew_ref[...] = proj[:, ATTN_WIDTH + 2 * KV_WIDTH:IN_EVEN]
        qlane = lax.broadcasted_iota(jnp.int32, (SAMPLE_TOK, LANES), 1)
        for hd in range(N_HEADS_A):
            j, kvh = hd % GQA_GROUP, hd // GQA_GROUP
            keep = (qlane < HEAD_DIM) if kvh == 0 else (qlane >= HEAD_DIM)
            qsel[hd] = jnp.where(keep, q[:, j * LANES:(j + 1) * LANES], 0.0)

    lax.fori_loop(0, GROUPS_PER_STEP, functools.partial(
        _even_sample_group, g, bucket_ref, sink_ref, sk_ref, sv_ref, nk_ref, nv_ref,
        knew_ref, vnew_ref, qsel, bias_ref, attn_f32), 0)

    @pl.when(g == last)
    def _():
        cat_ref[:, 0:ATTN_WIDTH] = attn_f32[...].astype(BF16)
        for r in range(POOL_BUF):
            if r < POOL_BUF - DEC_SEQ:
                npool_ref[r] = spool_ref[r + DEC_SEQ]
            else:
                t = r - (POOL_BUF - DEC_SEQ)
                npool_ref[r] = upnew_ref[t * DEC_BATCH:(t + 1) * DEC_BATCH, :]
        for t in range(DEC_SEQ):
            rows = slice(t * DEC_BATCH, (t + 1) * DEC_BATCH)
            for gi, w in enumerate(POOL_WINDOWS):
                lanes = slice(gi * POOL_GROUP, (gi + 1) * POOL_GROUP)
                acc = upnew_ref[rows, lanes]
                for back in range(1, w):
                    tt = t - back
                    if tt >= 0:
                        acc = acc + upnew_ref[tt * DEC_BATCH:(tt + 1) * DEC_BATCH, lanes]
                    else:
                        acc = acc + spool_ref[POOL_BUF + tt, :, lanes]
                cnt = float(min(PAST_LEN + t + 1, w))
                d = (acc / cnt - upnew_ref[rows, lanes]).astype(BF16)
                y = jnp.dot(d, wpool_ref[gi], preferred_element_type=F32) * pscale_ref[:, lanes]
                cat_ref[rows, ATTN_WIDTH + gi * POOL_GROUP:ATTN_WIDTH + (gi + 1) * POOL_GROUP] = y.astype(BF16)
        o_ref[...] = x_ref[...] + jnp.dot(cat_ref[...], wout_ref[...], preferred_element_type=F32)


def _even_sample_group(g, bucket_ref, sink_ref, sk_ref, sv_ref, nk_ref, nv_ref,
                       knew_ref, vnew_ref, qsel, bias_ref, attn_f32, gg, carry):
    olane = lax.broadcasted_iota(jnp.int32, (SAMPLE_ROWS, LANES), 1)
    seq0 = gg * SEQ_GROUP
    row0 = pl.multiple_of((g * GROUPS_PER_STEP + gg) * SEQ_GROUP, SEQ_GROUP)
    pad = [jnp.zeros((WINDOW - SEQ_GROUP * NEW_SLOTS, KV_WIDTH), F32)]
    knew = jnp.concatenate(
        [knew_ref[pl.ds(row0 + b, NEW_SLOTS, stride=DEC_BATCH), :] for b in range(SEQ_GROUP)] + pad, axis=0)
    vnew = jnp.concatenate(
        [vnew_ref[pl.ds(row0 + b, NEW_SLOTS, stride=DEC_BATCH), :] for b in range(SEQ_GROUP)] + pad, axis=0)
    kt_all = jnp.concatenate([sk_ref[seq0 + b] for b in range(SEQ_GROUP)], axis=1).astype(BF16)
    vt_all = jnp.concatenate([sv_ref[seq0 + b] for b in range(SEQ_GROUP)], axis=1).astype(BF16)

    qg = jnp.concatenate(
        [qsel[hd, pl.ds(t * DEC_BATCH + row0, SEQ_GROUP), :] for hd in range(N_HEADS_A) for t in range(DEC_SEQ)],
        axis=0).astype(BF16)
    nt = (((1,), (1,)), ((), ()))
    sc = jnp.concatenate(
        [jnp.dot(qg, kt_all, preferred_element_type=F32),
         lax.dot_general(qg, knew.astype(BF16), nt, preferred_element_type=F32)], axis=1)
    valid = jnp.concatenate([bucket_ref[...]] * N_HEADS_A, axis=0) >= 0
    bias = bias_ref[...].reshape(N_HEADS_A * SAMPLE_ROWS, SAMPLE_KEYS)
    sink = jnp.concatenate(
        [jnp.full((SAMPLE_ROWS, 1), sink_ref[hd] * LOG2E, F32) for hd in range(N_HEADS_A)], axis=0)
    p, denom = _sink_softmax(jnp.where(valid, sc + bias, NEG), sink)
    p = p.astype(BF16)
    o_all = (lax.dot_general(p[:, 0:STATE_KEYS], vt_all, nt, preferred_element_type=F32)
             + jnp.dot(p[:, STATE_KEYS:], vnew.astype(BF16), preferred_element_type=F32)) / denom
    for j in range(GQA_GROUP):
        blk = jnp.where(olane < HEAD_DIM, o_all[j * SAMPLE_ROWS:(j + 1) * SAMPLE_ROWS, :],
                        o_all[(j + GQA_GROUP) * SAMPLE_ROWS:(j + GQA_GROUP + 1) * SAMPLE_ROWS, :])
        for t in range(DEC_SEQ):
            attn_f32[pl.ds(t * DEC_BATCH + row0, SEQ_GROUP), j * LANES:(j + 1) * LANES] = (
                blk[t * SEQ_GROUP:(t + 1) * SEQ_GROUP, :])

    lane = lax.broadcasted_iota(jnp.int32, (KV_WIDTH, WINDOW), 1)
    for new, old_ref, out_ref in ((knew, sk_ref, nk_ref), (vnew, sv_ref, nv_ref)):
        new_t = new.T
        for b in range(SEQ_GROUP):
            tail = pltpu.roll(new_t, (WINDOW - DEC_SEQ - b * NEW_SLOTS) % WINDOW, 1)
            out_ref[seq0 + b] = jnp.where(
                lane >= WINDOW - DEC_SEQ, tail, pltpu.roll(old_ref[seq0 + b], WINDOW - DEC_SEQ, 1))
    return carry


def _even_sample(x, gain, w_in, w_out, w_pool, pool_scale, rel_bias, sinks, state_k, state_v, state_pool_t):
    bucket = jnp.asarray(_sample_bucket_table())
    smem = pl.BlockSpec(memory_space=pltpu.SMEM)
    sample_block = PROMPT_TOK // SAMPLE_TOK
    return pl.pallas_call(
        _even_sample_kernel,
        out_shape=(
            jax.ShapeDtypeStruct((N_TOK, D_MODEL), F32),
            jax.ShapeDtypeStruct((DEC_BATCH, KV_WIDTH, WINDOW), F32),
            jax.ShapeDtypeStruct((DEC_BATCH, KV_WIDTH, WINDOW), F32),
            jax.ShapeDtypeStruct((POOL_BUF, DEC_BATCH, POOL_WIDTH), F32),
        ),
        grid=(N_SEQ_GROUPS // GROUPS_PER_STEP,),
        in_specs=[
            pl.BlockSpec((SAMPLE_TOK, D_MODEL), lambda g: (sample_block, 0)),
            _resident((1, D_MODEL)),
            _resident((D_MODEL, IN_EVEN)),
            _resident((ATTN_WIDTH + POOL_WIDTH, D_MODEL)),
            _resident((N_POOL_GROUPS, POOL_GROUP, POOL_GROUP)),
            _resident((1, POOL_WIDTH)),
            _resident((SAMPLE_ROWS, SAMPLE_KEYS)),
            smem,
            smem,
            pl.BlockSpec((GROUPS_PER_STEP * SEQ_GROUP, KV_WIDTH, WINDOW), lambda g: (g, 0, 0)),
            pl.BlockSpec((GROUPS_PER_STEP * SEQ_GROUP, KV_WIDTH, WINDOW), lambda g: (g, 0, 0)),
            _resident((POOL_BUF, DEC_BATCH, POOL_WIDTH)),
        ],
        out_specs=(
            pl.BlockSpec((SAMPLE_TOK, D_MODEL), lambda g: (sample_block, 0)),
            pl.BlockSpec((GROUPS_PER_STEP * SEQ_GROUP, KV_WIDTH, WINDOW), lambda g: (g, 0, 0)),
            pl.BlockSpec((GROUPS_PER_STEP * SEQ_GROUP, KV_WIDTH, WINDOW), lambda g: (g, 0, 0)),
            pl.BlockSpec((POOL_BUF, DEC_BATCH, POOL_WIDTH), lambda g: (0, 0, 0)),
        ),
        scratch_shapes=[
            pltpu.VMEM((NEW_SLOTS * DEC_BATCH, KV_WIDTH), F32),
            pltpu.VMEM((NEW_SLOTS * DEC_BATCH, KV_WIDTH), F32),
            pltpu.VMEM((SAMPLE_TOK, POOL_WIDTH), F32),
            pltpu.VMEM((N_HEADS_A, SAMPLE_TOK, LANES), F32),
            pltpu.VMEM((N_HEADS_A, SAMPLE_ROWS, SAMPLE_KEYS), F32),
            pltpu.VMEM((SAMPLE_TOK, ATTN_WIDTH), F32),
            pltpu.VMEM((SAMPLE_TOK, ATTN_WIDTH + POOL_WIDTH), BF16),
        ],
        input_output_aliases={0: 0},
        compiler_params=pltpu.CompilerParams(
            dimension_semantics=("arbitrary",), vmem_limit_bytes=EVEN_SAMPLE_VMEM),
        name="even_mixer_sample",
    )(x, gain.reshape(1, D_MODEL), w_in, w_out, w_pool, pool_scale.reshape(1, POOL_WIDTH), bucket,
      rel_bias, sinks, state_k, state_v, state_pool_t)


def _sgu_front(x, g_ref, win_ref, gv_ref):
    h = _rms(x, g_ref[...]).astype(BF16)
    uv = jax.nn.gelu(jnp.dot(h, win_ref[...], preferred_element_type=F32))
    return uv[:, 0:SGU_WIDTH], _rms(uv[:, SGU_WIDTH:], gv_ref[...])


def _odd_kernel(x_ref, g_ref, win_ref, gv_ref, ws_ref, bs_ref, wout_ref, o_ref, v_ref, gated_ref):
    i = pl.program_id(0)

    @pl.when(i < ODD_STEPS)
    def _():
        x = x_ref[...]
        u, v = _sgu_front(x, g_ref, win_ref, gv_ref)
        vb = v.astype(BF16)
        r = lax.broadcasted_iota(jnp.int32, (CHUNK, CHUNK), 0)
        c = lax.broadcasted_iota(jnp.int32, (CHUNK, CHUNK), 1)
        for gi in range(SGU_GROUPS):
            w = jnp.where(r >= c, ws_ref[gi], 0.0).astype(BF16)
            bias = jnp.concatenate([bs_ref[gi]] * (SGU_GROUP_W // LANES), axis=1)
            lanes = slice(gi * SGU_GROUP_W, (gi + 1) * SGU_GROUP_W)
            for ci in range(ODD_TM // CHUNK):
                rows = slice(ci * CHUNK, (ci + 1) * CHUNK)
                mixed = jnp.dot(w, vb[rows, lanes], preferred_element_type=F32) + bias
                gated_ref[rows, lanes] = (u[rows, lanes] * mixed).astype(BF16)
        o_ref[...] = x + jnp.dot(gated_ref[...], wout_ref[...], preferred_element_type=F32)

    @pl.when(i == ODD_STEPS)
    def _():
        x = x_ref[0:SAMPLE_TOK, :]
        u, v = _sgu_front(x, g_ref, win_ref, gv_ref)
        for t in range(DEC_SEQ):
            v_ref[:, t, :] = v[t * DEC_BATCH:(t + 1) * DEC_BATCH, :]
        def per_lane(pieces):
            return jnp.concatenate(
                [jnp.broadcast_to(p, (1, LANES)) for p in pieces for _ in range(SGU_GROUP_W // LANES)], axis=1)

        gated = []
        for t in range(DEC_SEQ):
            mixed = per_lane([bs_ref[gi, t:t + 1, :] for gi in range(SGU_GROUPS)])
            for s in range(t + 1):
                coef = per_lane([ws_ref[gi, t:t + 1, s:s + 1] for gi in range(SGU_GROUPS)])
                mixed = mixed + coef * v[s * DEC_BATCH:(s + 1) * DEC_BATCH, :]
            gated.append((u[t * DEC_BATCH:(t + 1) * DEC_BATCH, :] * mixed).astype(BF16))
        o_ref[0:SAMPLE_TOK, :] = x + jnp.dot(
            jnp.concatenate(gated, axis=0), wout_ref[...], preferred_element_type=F32)


def _odd_mixer(x, gain, w_in, g_v, w_s, b_s, w_out):
    bias = jnp.broadcast_to(b_s[:, :, None], (SGU_GROUPS, CHUNK, LANES))
    return pl.pallas_call(
        _odd_kernel,
        out_shape=(
            jax.ShapeDtypeStruct((N_TOK, D_MODEL), F32),
            jax.ShapeDtypeStruct((DEC_BATCH, DEC_SEQ, SGU_WIDTH), F32),
        ),
        grid=(ODD_STEPS + 1,),
        in_specs=[
            pl.BlockSpec((ODD_TM, D_MODEL), lambda i: (i, 0)),
            _resident((1, D_MODEL)),
            _resident((D_MODEL, 2 * SGU_WIDTH)),
            _resident((1, SGU_WIDTH)),
            _resident((SGU_GROUPS, CHUNK, CHUNK)),
            _resident((SGU_GROUPS, CHUNK, LANES)),
            _resident((SGU_WIDTH, D_MODEL)),
        ],
        out_specs=(
            pl.BlockSpec((ODD_TM, D_MODEL), lambda i: (i, 0)),
            pl.BlockSpec((DEC_BATCH, DEC_SEQ, SGU_WIDTH), lambda i: (0, 0, 0)),
        ),
        scratch_shapes=[pltpu.VMEM((ODD_TM, SGU_WIDTH), BF16)],
        input_output_aliases={0: 0},
        compiler_params=pltpu.CompilerParams(
            dimension_semantics=("arbitrary",), vmem_limit_bytes=ODD_VMEM),
        name="odd_mixer",
    )(x, gain.reshape(1, D_MODEL), w_in, g_v.reshape(1, SGU_WIDTH), w_s, bias, w_out)


def _window_to_lanes(w):
    return jnp.transpose(w, (0, 2, 3, 1)).reshape(DEC_BATCH, KV_WIDTH, WINDOW)


def _window_from_lanes(w):
    return jnp.transpose(w.reshape(DEC_BATCH, N_KV_A, HEAD_DIM, WINDOW), (0, 3, 1, 2))


def kernel(x_prompt, x_sample, state_win_k, state_win_v, state_pool, rel_bias, norm_gains, final_gain,
           ffn_gate, ffn_up, ffn_down, w_in_even, w_out_even, attn_sinks, w_pool, pool_scale,
           w_in_odd, sgu_norm, w_spatial, b_spatial, w_out_odd):
    xp = x_prompt.reshape(PROMPT_TOK, D_MODEL)

    kp_l, vp_l, pp_l, ks_l, vs_l, ps_l, sv_l = [], [], [], [], [], [], []
    x = None
    y_prompt = y_sample = None
    for l in range(DEPTH):
        fa = (norm_gains[l, 0], ffn_gate, ffn_up, ffn_down, l, 0)
        fb = (norm_gains[l, 2], ffn_gate, ffn_up, ffn_down, l, 1)
        if l == 0:
            x = _ffn(xp, *fa, tail_in=x_sample)
        else:
            x = _ffn(x, *fa)
        g1 = norm_gains[l, 1]
        if l % 2 == 0:
            e = l // 2
            wq = w_in_even[e][:, :ATTN_WIDTH].reshape(D_MODEL, N_KV_A, GQA_GROUP, HEAD_DIM)
            wq = jnp.swapaxes(wq, 1, 2).reshape(D_MODEL, ATTN_WIDTH)
            wo = w_out_even[e][:ATTN_WIDTH].reshape(N_KV_A, GQA_GROUP, HEAD_DIM, D_MODEL)
            wo = jnp.swapaxes(wo, 0, 1).reshape(ATTN_WIDTH, D_MODEL)
            w_in = jnp.concatenate([wq, w_in_even[e][:, ATTN_WIDTH:]], axis=1).astype(BF16)
            w_out = jnp.concatenate([wo, w_out_even[e][ATTN_WIDTH:]], axis=0).astype(BF16)
            wp = w_pool[e].astype(BF16)
            x, kp, vp, pp = _even_prompt(x, g1, w_in, w_out, wp, pool_scale[e], rel_bias, attn_sinks[e])
            x, nk, nv, npool = _even_sample(
                x, g1, w_in, w_out, wp, pool_scale[e], rel_bias, attn_sinks[e],
                _window_to_lanes(state_win_k[e]), _window_to_lanes(state_win_v[e]),
                jnp.swapaxes(state_pool[e], 0, 1))
            kp_l.append(kp.reshape(BATCH, WINDOW, N_KV_A, HEAD_DIM))
            vp_l.append(vp.reshape(BATCH, WINDOW, N_KV_A, HEAD_DIM))
            pp_l.append(pp[:, 2 * SUBLANES - POOL_BUF:, :])
            ks_l.append(_window_from_lanes(nk))
            vs_l.append(_window_from_lanes(nv))
            ps_l.append(jnp.swapaxes(npool, 0, 1))
        else:
            o = l // 2
            w_in, w_out = w_in_odd[o].astype(BF16), w_out_odd[o].astype(BF16)
            x, sv_new = _odd_mixer(x, g1, w_in, sgu_norm[o], w_spatial[o], b_spatial[o], w_out)
            sv_l.append(sv_new)
        if l < DEPTH - 1:
            x = _ffn(x, *fb)
        else:
            y_prompt, y_sample = _ffn(x, *fb, final_gain=final_gain)

    return (y_prompt.reshape(BATCH, SEQ, D_MODEL), y_sample,
            jnp.stack(kp_l), jnp.stack(vp_l), jnp.stack(pp_l),
            jnp.stack(ks_l), jnp.stack(vs_l), jnp.stack(ps_l), jnp.stack(sv_l))
```

```python
import functools
import math

import numpy as np
import jax
import jax.numpy as jnp
from jax import lax
from jax.experimental import pallas as pl
from jax.experimental.pallas import tpu as pltpu

F32 = jnp.float32
BF16 = jnp.bfloat16

D_MODEL = 1024
BATCH = 8
SEQ = 2048
DEPTH = 2
DEC_BATCH = 128
DEC_SEQ = 4
PAST_LEN = 16384
N_HEADS_A = 8
N_KV_A = 2
HEAD_DIM = 64
GQA_GROUP = N_HEADS_A // N_KV_A
WINDOW = 128
ATTN_WIDTH = N_HEADS_A * HEAD_DIM
KV_WIDTH = N_KV_A * HEAD_DIM
N_BUCKETS = 32
MAX_DISTANCE = 128
POOL_WINDOWS = (2, 4, 8, 16)
N_POOL_GROUPS = len(POOL_WINDOWS)
POOL_GROUP = 128
POOL_WIDTH = N_POOL_GROUPS * POOL_GROUP
POOL_BUF = max(POOL_WINDOWS) - 1
IN_EVEN = ATTN_WIDTH + 2 * KV_WIDTH + POOL_WIDTH
CHUNK = 128
SGU_WIDTH = 1024
SGU_GROUPS = 4
SGU_GROUP_W = SGU_WIDTH // SGU_GROUPS
D_FF = 2816
EPS = 1e-6
NEG = -1e30

LANES = 128
SUBLANES = 8
MIB = 1024 * 1024
VMEM_LIMIT_BYTES = 56 * MIB
EVEN_PROMPT_VMEM = 42 * MIB
EVEN_SAMPLE_VMEM = 44 * MIB
ODD_VMEM = 38 * MIB

PROMPT_TOK = BATCH * SEQ
SAMPLE_TOK = DEC_BATCH * DEC_SEQ
N_TOK = PROMPT_TOK + SAMPLE_TOK

TM = SAMPLE_TOK
EVEN_TM = 1024
assert SEQ % EVEN_TM == 0 and EVEN_TM % WINDOW == 0
TILES_PER_SEQ = SEQ // EVEN_TM
QBLK_PER_TILE = EVEN_TM // WINDOW
FFN_TM = 2 * TM
assert PROMPT_TOK % FFN_TM == 0
FFN_PROMPT_STEPS = PROMPT_TOK // FFN_TM
ODD_TM = 1024
assert PROMPT_TOK % ODD_TM == 0 and ODD_TM % CHUNK == 0
ODD_STEPS = PROMPT_TOK // ODD_TM
assert SAMPLE_TOK <= ODD_TM
FF_CHUNK = 256
N_FF_CHUNKS = D_FF // FF_CHUNK
FF_SLOTS = 3
POOL_HEAD = 2 * SUBLANES
assert all(w & (w - 1) == 0 and w <= POOL_HEAD for w in POOL_WINDOWS)
POOL_CTX = 32
SEQ_GROUP = 8
N_SEQ_GROUPS = DEC_BATCH // SEQ_GROUP
GROUPS_PER_STEP = 4
assert N_SEQ_GROUPS % GROUPS_PER_STEP == 0
STATE_KEYS = SEQ_GROUP * WINDOW
NEW_SLOTS = SUBLANES
assert DEC_SEQ <= NEW_SLOTS and SEQ_GROUP * NEW_SLOTS <= WINDOW
SAMPLE_KEYS = STATE_KEYS + WINDOW
SAMPLE_ROWS = DEC_SEQ * SEQ_GROUP


assert N_KV_A == 2 and KV_WIDTH == LANES
HEAD_ORDER = [h for j in range(GQA_GROUP) for h in (j, j + GQA_GROUP)]


def _t5_bucket_np(dist):
    n = np.maximum(dist, 0)
    max_exact = N_BUCKETS // 2
    nf = np.maximum(n, 1).astype(np.float32)
    large = max_exact + (np.log(nf / np.float32(max_exact)) / np.float32(math.log(MAX_DISTANCE / max_exact))
                         * np.float32(N_BUCKETS - max_exact)).astype(np.int32)
    large = np.minimum(large, N_BUCKETS - 1)
    return np.where(n < max_exact, n, large).astype(np.int32)


def _prompt_bucket_table():
    qi = np.arange(WINDOW)[:, None]
    c = np.arange(WINDOW)[None, :]
    dist = np.where(c > qi, qi + WINDOW - c, qi - c)
    return _t5_bucket_np(dist)


def _sample_bucket_table():
    t = (np.arange(SAMPLE_ROWS) // SEQ_GROUP)[:, None]
    b = (np.arange(SAMPLE_ROWS) % SEQ_GROUP)[:, None]
    col = np.arange(SAMPLE_KEYS)[None, :]
    is_state = col < STATE_KEYS
    slot = (col - STATE_KEYS) % NEW_SLOTS
    is_new = (col >= STATE_KEYS) & (col < STATE_KEYS + SEQ_GROUP * NEW_SLOTS) & (slot < DEC_SEQ)
    kb = np.where(is_state, col // WINDOW, (col - STATE_KEYS) // NEW_SLOTS)
    kpos = np.where(is_state, col % WINDOW, WINDOW + slot)
    dist = (WINDOW + t) - kpos
    valid = (is_state | is_new) & (kb == b) & (dist >= 0) & (dist < WINDOW)
    return np.where(valid, _t5_bucket_np(dist), -1).astype(np.int32)


def _rms(x, g):
    y = x * lax.rsqrt(jnp.mean(x * x, axis=-1, keepdims=True) + EPS)
    return y * g


LOG2E = math.log2(math.e)
Q_SCALE = HEAD_DIM ** -0.5 * LOG2E


def _bias_from_buckets(bucket, rb_ref, h):
    t = jnp.zeros(bucket.shape, F32)
    for bkt in range(N_BUCKETS):
        t = jnp.where(bucket == bkt, rb_ref[bkt, h] * LOG2E, t)
    return t


def _sink_softmax(s, sink):
    m = jnp.maximum(jnp.max(s, axis=-1, keepdims=True), sink)
    p = jnp.exp2(s - m)
    return p, jnp.sum(p, axis=-1, keepdims=True) + jnp.exp2(sink - m)


def _softmax_pv(s, valid, bias, sink, v):
    p, denom = _sink_softmax(jnp.where(valid, s + bias, NEG), sink)
    return jnp.dot(p.astype(BF16), v, preferred_element_type=F32) / denom


def _ffn_kernel(*refs, layer, half, tail_in, final_split, unroll_halves):
    refs = list(refs)
    x_ref = refs.pop(0)
    xt_ref = refs.pop(0) if tail_in else None
    g_ref, wg_hbm, wu_hbm, wd_hbm = refs[:4]
    del refs[:4]
    fg_ref = refs.pop(0) if final_split else None
    o_ref = refs.pop(0)
    ot_ref = refs.pop(0) if final_split else None
    wg, wu, wd, stage_in, stage_dn, sem = refs
    i = pl.program_id(0)

    def normed(x):
        return _rms(x, g_ref[...]).astype(BF16)

    def result(x, y):
        out = x + 0.5 * y
        return _rms(out, fg_ref[...]) if final_split else out

    def chunk_copies(c, slot):
        cols = pl.ds(c * FF_CHUNK, FF_CHUNK)
        return (
            pltpu.make_async_copy(wg_hbm.at[layer, half, :, cols], stage_in.at[0, slot], sem.at[0, slot]),
            pltpu.make_async_copy(wu_hbm.at[layer, half, :, cols], stage_in.at[1, slot], sem.at[1, slot]),
            pltpu.make_async_copy(wd_hbm.at[layer, half, cols, :], stage_dn.at[slot], sem.at[2, slot]),
        )

    @pl.when(i == 0)
    def _():
        for c in range(min(FF_SLOTS, N_FF_CHUNKS)):
            for cp in chunk_copies(c, c % FF_SLOTS):
                cp.start()
        if tail_in:
            x = jnp.concatenate([xt_ref[:, t, :] for t in range(DEC_SEQ)], axis=0)
        else:
            x = x_ref[0:TM, :]
        h = normed(x)
        acc = jnp.zeros((TM, D_MODEL), F32)
        for c in range(N_FF_CHUNKS):
            slot = c % FF_SLOTS
            cols = slice(c * FF_CHUNK, (c + 1) * FF_CHUNK)
            for cp in chunk_copies(c, slot):
                cp.wait()
            wg[:, cols] = stage_in[0, slot].astype(BF16)
            wu[:, cols] = stage_in[1, slot].astype(BF16)
            wd[cols, :] = stage_dn[slot].astype(BF16)
            if c + FF_SLOTS < N_FF_CHUNKS:
                for cp in chunk_copies(c + FF_SLOTS, slot):
                    cp.start()
            gate = jnp.dot(h, wg[:, cols], preferred_element_type=F32)
            up = jnp.dot(h, wu[:, cols], preferred_element_type=F32)
            act = (jax.nn.silu(gate) * up).astype(BF16)
            acc = acc + jnp.dot(act, wd[cols, :], preferred_element_type=F32)
        out = result(x, acc)
        if final_split:
            for t in range(DEC_SEQ):
                ot_ref[:, t, :] = out[t * DEC_BATCH:(t + 1) * DEC_BATCH, :]
        else:
            o_ref[0:TM, :] = out

    def half_block(rows):
        x = x_ref[rows, :]
        h = normed(x)
        gate = jnp.dot(h, wg[...], preferred_element_type=F32)
        up = jnp.dot(h, wu[...], preferred_element_type=F32)
        act = (jax.nn.silu(gate) * up).astype(BF16)
        o_ref[rows, :] = result(x, jnp.dot(act, wd[...], preferred_element_type=F32))

    @pl.when(i > 0)
    def _():
        if unroll_halves:
            for hf in range(FFN_TM // TM):
                half_block(slice(hf * TM, (hf + 1) * TM))
        else:
            def body(hf, carry):
                half_block(pl.ds(pl.multiple_of(hf * TM, TM), TM))
                return carry
            lax.fori_loop(0, FFN_TM // TM, body, 0)


def _resident(shape):
    return pl.BlockSpec(shape, lambda *_: (0,) * len(shape), pipeline_mode=pl.Buffered(1))


def _ffn(x_in, gain, wg, wu, wd, layer, half, *, tail_in=None, final_gain=None):
    hbm = pl.BlockSpec(memory_space=pl.ANY)
    row_block = pl.BlockSpec((FFN_TM, D_MODEL), lambda i: (jnp.where(i == 0, FFN_PROMPT_STEPS, i - 1), 0))
    prompt_block = pl.BlockSpec((FFN_TM, D_MODEL), lambda i: (jnp.maximum(i - 1, 0), 0))
    sample_block = pl.BlockSpec((DEC_BATCH, DEC_SEQ, D_MODEL), lambda i: (0, 0, 0))
    if tail_in is None:
        args, in_specs = [x_in], [row_block]
    else:
        args, in_specs = [x_in, tail_in], [prompt_block, sample_block]
    args += [gain.reshape(1, D_MODEL), wg, wu, wd]
    in_specs += [_resident((1, D_MODEL)), hbm, hbm, hbm]
    if final_gain is None:
        out_shape = jax.ShapeDtypeStruct((N_TOK, D_MODEL), F32)
        out_specs = row_block
    else:
        args.append(final_gain.reshape(1, D_MODEL))
        in_specs.append(_resident((1, D_MODEL)))
        out_shape = (jax.ShapeDtypeStruct((PROMPT_TOK, D_MODEL), F32),
                     jax.ShapeDtypeStruct((DEC_BATCH, DEC_SEQ, D_MODEL), F32))
        out_specs = (prompt_block, sample_block)
    aliases = {0: 0} if (tail_in is None and final_gain is None) else {}
    return pl.pallas_call(
        functools.partial(_ffn_kernel, layer=layer, half=half, tail_in=tail_in is not None,
                          final_split=final_gain is not None, unroll_halves=(layer == 0 or half == 1)),
        out_shape=out_shape,
        grid=(FFN_PROMPT_STEPS + 1,),
        in_specs=in_specs,
        out_specs=out_specs,
        scratch_shapes=[
            pltpu.VMEM((D_MODEL, D_FF), BF16),
            pltpu.VMEM((D_MODEL, D_FF), BF16),
            pltpu.VMEM((D_FF, D_MODEL), BF16),
            pltpu.VMEM((2, FF_SLOTS, D_MODEL, FF_CHUNK), F32),
            pltpu.VMEM((FF_SLOTS, FF_CHUNK, D_MODEL), F32),
            pltpu.SemaphoreType.DMA((3, FF_SLOTS)),
        ],
        input_output_aliases=aliases,
        compiler_params=pltpu.CompilerParams(
            dimension_semantics=("arbitrary",), vmem_limit_bytes=VMEM_LIMIT_BYTES),
        name="macaron_half",
    )(*args)


def _even_prompt_kernel(x_ref, g_ref, win_ref, wout_ref, wpool_ref, pscale_ref, bucket_ref, rb_ref, sink_ref,
                        o_ref, klast_ref, vlast_ref, plast_ref,
                        kbuf, vbuf, zb, s2b, s4b, s8b, bias_ref, cat_ref):
    b = pl.program_id(0)
    s = pl.program_id(1)
    last = pl.num_programs(1) - 1

    @pl.when((b == 0) & (s == 0))
    def _():
        bucket = bucket_ref[...]
        for h in range(N_HEADS_A):
            bias_ref[h] = _bias_from_buckets(bucket, rb_ref, h)

    @pl.when(s == 0)
    def _():
        for buf in (kbuf, vbuf):
            buf[0:WINDOW, :] = jnp.zeros((WINDOW, KV_WIDTH), BF16)
        zb[0:POOL_CTX, :] = jnp.zeros((POOL_CTX, POOL_WIDTH), F32)

    @pl.when(s > 0)
    def _():
        for buf in (kbuf, vbuf):
            buf[0:WINDOW, :] = buf[EVEN_TM:EVEN_TM + WINDOW, :]
        zb[0:POOL_CTX, :] = zb[EVEN_TM:EVEN_TM + POOL_CTX, :]

    x = x_ref[...]
    h = _rms(x, g_ref[...]).astype(BF16)
    proj = jnp.dot(h, win_ref[...], preferred_element_type=F32)
    q = proj[:, 0:ATTN_WIDTH] * Q_SCALE
    k = proj[:, ATTN_WIDTH:ATTN_WIDTH + KV_WIDTH]
    v = proj[:, ATTN_WIDTH + KV_WIDTH:ATTN_WIDTH + 2 * KV_WIDTH]
    up = proj[:, ATTN_WIDTH + 2 * KV_WIDTH:IN_EVEN]

    qlane = lax.broadcasted_iota(jnp.int32, (EVEN_TM, ATTN_WIDTH), 1) % LANES
    q_half = (jnp.where(qlane < HEAD_DIM, q, 0.0).astype(BF16),
              jnp.where(qlane >= HEAD_DIM, q, 0.0).astype(BF16))
    kbuf[WINDOW:, :] = k.astype(BF16)
    vbuf[WINDOW:, :] = v.astype(BF16)

    qrow = lax.broadcasted_iota(jnp.int32, (WINDOW, WINDOW), 0)
    kcol = lax.broadcasted_iota(jnp.int32, (WINDOW, WINDOW), 1)
    prev = kcol > qrow
    olane = lax.broadcasted_iota(jnp.int32, (WINDOW, LANES), 1)
    for i in range(QBLK_PER_TILE):
        rows = slice(i * WINDOW, (i + 1) * WINDOW)
        keys = slice(i * WINDOW, (i + 2) * WINDOW)
        qs = jnp.concatenate(
            [q_half[hd // GQA_GROUP][rows, (hd % GQA_GROUP) * LANES:(hd % GQA_GROUP + 1) * LANES]
             for hd in range(N_HEADS_A)], axis=0)
        sc_all = lax.dot_general(qs, kbuf[keys, :], (((1,), (1,)), ((), ())), preferred_element_type=F32)
        ps, denoms = [], []
        for hd in range(N_HEADS_A):
            sc = sc_all[hd * WINDOW:(hd + 1) * WINDOW, :]
            sc = jnp.where(prev, sc[:, 0:WINDOW], sc[:, WINDOW:]) + bias_ref[hd]
            if i == 0:
                sc = jnp.where(prev & (qrow >= jnp.where(s > 0, WINDOW, 0)), NEG, sc)
            p, denom = _sink_softmax(sc, sink_ref[hd] * LOG2E)
            ps.append(jnp.concatenate([jnp.where(prev, p, 0.0), jnp.where(prev, 0.0, p)], axis=1).astype(BF16))
            denoms.append(denom)
        o_all = jnp.dot(jnp.concatenate(ps, axis=0), vbuf[keys, :], preferred_element_type=F32)
        out = [o_all[hd * WINDOW:(hd + 1) * WINDOW, :] / denoms[hd] for hd in range(N_HEADS_A)]
        for j in range(GQA_GROUP):
            cat_ref[rows, j * LANES:(j + 1) * LANES] = jnp.where(
                olane < HEAD_DIM, out[j], out[j + GQA_GROUP]).astype(BF16)

    zb[POOL_CTX:, :] = up
    n = EVEN_TM + POOL_CTX
    s2b[8:, :] = zb[8:n, :] + zb[7:n - 1, :]
    s4b[16:, :] = s2b[16:n, POOL_GROUP:] + s2b[14:n - 2, POOL_GROUP:]
    s8b[24:, :] = s4b[24:n, POOL_GROUP:] + s4b[20:n - 4, POOL_GROUP:]
    s16 = s8b[32:n, POOL_GROUP:] + s8b[24:n - 8, POOL_GROUP:]
    sums = (s2b[POOL_CTX:, 0:POOL_GROUP], s4b[POOL_CTX:, 0:POOL_GROUP], s8b[POOL_CTX:, 0:POOL_GROUP], s16)
    pos1 = lax.broadcasted_iota(jnp.int32, (POOL_HEAD, POOL_GROUP), 0) + s * EVEN_TM + 1
    for g, w in enumerate(POOL_WINDOWS):
        lanes = slice(g * POOL_GROUP, (g + 1) * POOL_GROUP)
        cnt = jnp.minimum(pos1, w).astype(F32)
        mean = jnp.concatenate([sums[g][0:POOL_HEAD] / cnt, sums[g][POOL_HEAD:] * (1.0 / w)], axis=0)
        d = (mean - up[:, lanes]).astype(BF16)
        y = jnp.dot(d, wpool_ref[g], preferred_element_type=F32) * pscale_ref[:, lanes]
        cat_ref[:, ATTN_WIDTH + g * POOL_GROUP:ATTN_WIDTH + (g + 1) * POOL_GROUP] = y.astype(BF16)

    o_ref[...] = x + jnp.dot(cat_ref[...], wout_ref[...], preferred_element_type=F32)

    @pl.when(s == last)
    def _():
        klast_ref[0] = k[EVEN_TM - WINDOW:, :]
        vlast_ref[0] = v[EVEN_TM - WINDOW:, :]
        plast_ref[0] = up[EVEN_TM - 2 * SUBLANES:, :]


def _even_prompt(x, gain, w_in, w_out, w_pool, pool_scale, rel_bias, sinks):
    bucket = jnp.asarray(_prompt_bucket_table())
    smem = pl.BlockSpec(memory_space=pltpu.SMEM)
    return pl.pallas_call(
        _even_prompt_kernel,
        out_shape=(
            jax.ShapeDtypeStruct((N_TOK, D_MODEL), F32),
            jax.ShapeDtypeStruct((BATCH, WINDOW, KV_WIDTH), F32),
            jax.ShapeDtypeStruct((BATCH, WINDOW, KV_WIDTH), F32),
            jax.ShapeDtypeStruct((BATCH, 2 * SUBLANES, POOL_WIDTH), F32),
        ),
        grid=(BATCH, TILES_PER_SEQ),
        in_specs=[
            pl.BlockSpec((EVEN_TM, D_MODEL), lambda b, s: (b * TILES_PER_SEQ + s, 0)),
            _resident((1, D_MODEL)),
            _resident((D_MODEL, IN_EVEN)),
            _resident((ATTN_WIDTH + POOL_WIDTH, D_MODEL)),
            _resident((N_POOL_GROUPS, POOL_GROUP, POOL_GROUP)),
            _resident((1, POOL_WIDTH)),
            _resident((WINDOW, WINDOW)),
            smem,
            smem,
        ],
        out_specs=(
            pl.BlockSpec((EVEN_TM, D_MODEL), lambda b, s: (b * TILES_PER_SEQ + s, 0)),
            pl.BlockSpec((1, WINDOW, KV_WIDTH), lambda b, s: (b, 0, 0)),
            pl.BlockSpec((1, WINDOW, KV_WIDTH), lambda b, s: (b, 0, 0)),
            pl.BlockSpec((1, 2 * SUBLANES, POOL_WIDTH), lambda b, s: (b, 0, 0)),
        ),
        scratch_shapes=[
            pltpu.VMEM((WINDOW + EVEN_TM, KV_WIDTH), BF16),
            pltpu.VMEM((WINDOW + EVEN_TM, KV_WIDTH), BF16),
            pltpu.VMEM((POOL_CTX + EVEN_TM, POOL_WIDTH), F32),
            pltpu.VMEM((POOL_CTX + EVEN_TM, POOL_WIDTH), F32),
            pltpu.VMEM((POOL_CTX + EVEN_TM, POOL_WIDTH - POOL_GROUP), F32),
            pltpu.VMEM((POOL_CTX + EVEN_TM, POOL_WIDTH - 2 * POOL_GROUP), F32),
            pltpu.VMEM((N_HEADS_A, WINDOW, WINDOW), F32),
            pltpu.VMEM((EVEN_TM, ATTN_WIDTH + POOL_WIDTH), BF16),
        ],
        input_output_aliases={0: 0},
        compiler_params=pltpu.CompilerParams(
            dimension_semantics=("arbitrary", "arbitrary"), vmem_limit_bytes=EVEN_PROMPT_VMEM),
        name="even_mixer_prompt",
    )(x, gain.reshape(1, D_MODEL), w_in, w_out, w_pool, pool_scale.reshape(1, POOL_WIDTH), bucket,
      rel_bias, sinks)


def _even_sample_kernel(x_ref, g_ref, win_ref, wout_ref, wpool_ref, pscale_ref, bucket_ref, rb_ref, sink_ref,
                        sk_ref, sv_ref, spool_ref,
                        o_ref, nk_ref, nv_ref, npool_ref,
                        knew_ref, vnew_ref, upnew_ref, qsel, bias_ref, attn_f32, cat_ref):
    g = pl.program_id(0)
    last = pl.num_programs(0) - 1

    @pl.when(g == 0)
    def _():
        bucket = bucket_ref[...]
        for h in range(N_HEADS_A):
            bias_ref[h] = _bias_from_buckets(bucket, rb_ref, h)
        x = x_ref[...]
        h = _rms(x, g_ref[...]).astype(BF16)
        proj = jnp.dot(h, win_ref[...], preferred_element_type=F32)
        q = proj[:, 0:ATTN_WIDTH] * Q_SCALE
        zeros = jnp.zeros(((NEW_SLOTS - DEC_SEQ) * DEC_BATCH, KV_WIDTH), F32)
        knew_ref[0:SAMPLE_TOK, :] = proj[:, ATTN_WIDTH:ATTN_WIDTH + KV_WIDTH]
        vnew_ref[0:SAMPLE_TOK, :] = proj[:, ATTN_WIDTH + KV_WIDTH:ATTN_WIDTH + 2 * KV_WIDTH]
        knew_ref[SAMPLE_TOK:, :] = zeros
        vnew_ref[SAMPLE_TOK:, :] = zeros
        upnew_ref[...] = proj[:, ATTN_WIDTH + 2 * KV_WIDTH:IN_EVEN]
        qlane = lax.broadcasted_iota(jnp.int32, (SAMPLE_TOK, LANES), 1)
        for hd in range(N_HEADS_A):
            j, kvh = hd % GQA_GROUP, hd // GQA_GROUP
            keep = (qlane < HEAD_DIM) if kvh == 0 else (qlane >= HEAD_DIM)
            qsel[hd] = jnp.where(keep, q[:, j * LANES:(j + 1) * LANES], 0.0)

    lax.fori_loop(0, GROUPS_PER_STEP, functools.partial(
        _even_sample_group, g, bucket_ref, sink_ref, sk_ref, sv_ref, nk_ref, nv_ref,
        knew_ref, vnew_ref, qsel, bias_ref, attn_f32), 0)

    @pl.when(g == last)
    def _():
        cat_ref[:, 0:ATTN_WIDTH] = attn_f32[...].astype(BF16)
        for r in range(POOL_BUF):
            if r < POOL_BUF - DEC_SEQ:
                npool_ref[r] = spool_ref[r + DEC_SEQ]
            else:
                t = r - (POOL_BUF - DEC_SEQ)
                npool_ref[r] = upnew_ref[t * DEC_BATCH:(t + 1) * DEC_BATCH, :]
        for t in range(DEC_SEQ):
            rows = slice(t * DEC_BATCH, (t + 1) * DEC_BATCH)
            for gi, w in enumerate(POOL_WINDOWS):
                lanes = slice(gi * POOL_GROUP, (gi + 1) * POOL_GROUP)
                acc = upnew_ref[rows, lanes]
                for back in range(1, w):
                    tt = t - back
                    if tt >= 0:
                        acc = acc + upnew_ref[tt * DEC_BATCH:(tt + 1) * DEC_BATCH, lanes]
                    else:
                        acc = acc + spool_ref[POOL_BUF + tt, :, lanes]
                cnt = float(min(PAST_LEN + t + 1, w))
                d = (acc / cnt - upnew_ref[rows, lanes]).astype(BF16)
                y = jnp.dot(d, wpool_ref[gi], preferred_element_type=F32) * pscale_ref[:, lanes]
                cat_ref[rows, ATTN_WIDTH + gi * POOL_GROUP:ATTN_WIDTH + (gi + 1) * POOL_GROUP] = y.astype(BF16)
        o_ref[...] = x_ref[...] + jnp.dot(cat_ref[...], wout_ref[...], preferred_element_type=F32)


def _even_sample_group(g, bucket_ref, sink_ref, sk_ref, sv_ref, nk_ref, nv_ref,
                       knew_ref, vnew_ref, qsel, bias_ref, attn_f32, gg, carry):
    olane = lax.broadcasted_iota(jnp.int32, (SAMPLE_ROWS, LANES), 1)
    seq0 = gg * SEQ_GROUP
    row0 = pl.multiple_of((g * GROUPS_PER_STEP + gg) * SEQ_GROUP, SEQ_GROUP)
    pad = [jnp.zeros((WINDOW - SEQ_GROUP * NEW_SLOTS, KV_WIDTH), F32)]
    knew = jnp.concatenate(
        [knew_ref[pl.ds(row0 + b, NEW_SLOTS, stride=DEC_BATCH), :] for b in range(SEQ_GROUP)] + pad, axis=0)
    vnew = jnp.concatenate(
        [vnew_ref[pl.ds(row0 + b, NEW_SLOTS, stride=DEC_BATCH), :] for b in range(SEQ_GROUP)] + pad, axis=0)
    kt_all = jnp.concatenate([sk_ref[seq0 + b] for b in range(SEQ_GROUP)], axis=1).astype(BF16)
    vt_all = jnp.concatenate([sv_ref[seq0 + b] for b in range(SEQ_GROUP)], axis=1).astype(BF16)

    qg = jnp.concatenate(
        [qsel[hd, pl.ds(t * DEC_BATCH + row0, SEQ_GROUP), :] for hd in range(N_HEADS_A) for t in range(DEC_SEQ)],
        axis=0).astype(BF16)
    nt = (((1,), (1,)), ((), ()))
    sc = jnp.concatenate(
        [jnp.dot(qg, kt_all, preferred_element_type=F32),
         lax.dot_general(qg, knew.astype(BF16), nt, preferred_element_type=F32)], axis=1)
    valid = jnp.concatenate([bucket_ref[...]] * N_HEADS_A, axis=0) >= 0
    bias = bias_ref[...].reshape(N_HEADS_A * SAMPLE_ROWS, SAMPLE_KEYS)
    sink = jnp.concatenate(
        [jnp.full((SAMPLE_ROWS, 1), sink_ref[hd] * LOG2E, F32) for hd in range(N_HEADS_A)], axis=0)
    p, denom = _sink_softmax(jnp.where(valid, sc + bias, NEG), sink)
    p = p.astype(BF16)
    o_all = (lax.dot_general(p[:, 0:STATE_KEYS], vt_all, nt, preferred_element_type=F32)
             + jnp.dot(p[:, STATE_KEYS:], vnew.astype(BF16), preferred_element_type=F32)) / denom
    for j in range(GQA_GROUP):
        blk = jnp.where(olane < HEAD_DIM, o_all[j * SAMPLE_ROWS:(j + 1) * SAMPLE_ROWS, :],
                        o_all[(j + GQA_GROUP) * SAMPLE_ROWS:(j + GQA_GROUP + 1) * SAMPLE_ROWS, :])
        for t in range(DEC_SEQ):
            attn_f32[pl.ds(t * DEC_BATCH + row0, SEQ_GROUP), j * LANES:(j + 1) * LANES] = (
                blk[t * SEQ_GROUP:(t + 1) * SEQ_GROUP, :])

    lane = lax.broadcasted_iota(jnp.int32, (KV_WIDTH, WINDOW), 1)
    for new, old_ref, out_ref in ((knew, sk_ref, nk_ref), (vnew, sv_ref, nv_ref)):
        new_t = new.T
        for b in range(SEQ_GROUP):
            tail = pltpu.roll(new_t, (WINDOW - DEC_SEQ - b * NEW_SLOTS) % WINDOW, 1)
            out_ref[seq0 + b] = jnp.where(
                lane >= WINDOW - DEC_SEQ, tail, pltpu.roll(old_ref[seq0 + b], WINDOW - DEC_SEQ, 1))
    return carry


def _even_sample(x, gain, w_in, w_out, w_pool, pool_scale, rel_bias, sinks, state_k, state_v, state_pool_t):
    bucket = jnp.asarray(_sample_bucket_table())
    smem = pl.BlockSpec(memory_space=pltpu.SMEM)
    sample_block = PROMPT_TOK // SAMPLE_TOK
    return pl.pallas_call(
        _even_sample_kernel,
        out_shape=(
            jax.ShapeDtypeStruct((N_TOK, D_MODEL), F32),
            jax.ShapeDtypeStruct((DEC_BATCH, KV_WIDTH, WINDOW), F32),
            jax.ShapeDtypeStruct((DEC_BATCH, KV_WIDTH, WINDOW), F32),
            jax.ShapeDtypeStruct((POOL_BUF, DEC_BATCH, POOL_WIDTH), F32),
        ),
        grid=(N_SEQ_GROUPS // GROUPS_PER_STEP,),
        in_specs=[
            pl.BlockSpec((SAMPLE_TOK, D_MODEL), lambda g: (sample_block, 0)),
            _resident((1, D_MODEL)),
            _resident((D_MODEL, IN_EVEN)),
            _resident((ATTN_WIDTH + POOL_WIDTH, D_MODEL)),
            _resident((N_POOL_GROUPS, POOL_GROUP, POOL_GROUP)),
            _resident((1, POOL_WIDTH)),
            _resident((SAMPLE_ROWS, SAMPLE_KEYS)),
            smem,
            smem,
            pl.BlockSpec((GROUPS_PER_STEP * SEQ_GROUP, KV_WIDTH, WINDOW), lambda g: (g, 0, 0)),
            pl.BlockSpec((GROUPS_PER_STEP * SEQ_GROUP, KV_WIDTH, WINDOW), lambda g: (g, 0, 0)),
            _resident((POOL_BUF, DEC_BATCH, POOL_WIDTH)),
        ],
        out_specs=(
            pl.BlockSpec((SAMPLE_TOK, D_MODEL), lambda g: (sample_block, 0)),
            pl.BlockSpec((GROUPS_PER_STEP * SEQ_GROUP, KV_WIDTH, WINDOW), lambda g: (g, 0, 0)),
            pl.BlockSpec((GROUPS_PER_STEP * SEQ_GROUP, KV_WIDTH, WINDOW), lambda g: (g, 0, 0)),
            pl.BlockSpec((POOL_BUF, DEC_BATCH, POOL_WIDTH), lambda g: (0, 0, 0)),
        ),
        scratch_shapes=[
            pltpu.VMEM((NEW_SLOTS * DEC_BATCH, KV_WIDTH), F32),
            pltpu.VMEM((NEW_SLOTS * DEC_BATCH, KV_WIDTH), F32),
            pltpu.VMEM((SAMPLE_TOK, POOL_WIDTH), F32),
            pltpu.VMEM((N_HEADS_A, SAMPLE_TOK, LANES), F32),
            pltpu.VMEM((N_HEADS_A, SAMPLE_ROWS, SAMPLE_KEYS), F32),
            pltpu.VMEM((SAMPLE_TOK, ATTN_WIDTH), F32),
            pltpu.VMEM((SAMPLE_TOK, ATTN_WIDTH + POOL_WIDTH), BF16),
        ],
        input_output_aliases={0: 0},
        compiler_params=pltpu.CompilerParams(
            dimension_semantics=("arbitrary",), vmem_limit_bytes=EVEN_SAMPLE_VMEM),
        name="even_mixer_sample",
    )(x, gain.reshape(1, D_MODEL), w_in, w_out, w_pool, pool_scale.reshape(1, POOL_WIDTH), bucket,
      rel_bias, sinks, state_k, state_v, state_pool_t)


def _sgu_front(x, g_ref, win_ref, gv_ref):
    h = _rms(x, g_ref[...]).astype(BF16)
    uv = jax.nn.gelu(jnp.dot(h, win_ref[...], preferred_element_type=F32))
    return uv[:, 0:SGU_WIDTH], _rms(uv[:, SGU_WIDTH:], gv_ref[...])


def _odd_kernel(x_hbm, g_ref, win_ref, gv_ref, ws_ref, bs_ref, coef_ref, sbias_ref, wout_ref,
                o_hbm, v_ref, gated_ref):
    def prompt_step(x_ref, o_ref):
        x = x_ref[...]
        u, v = _sgu_front(x, g_ref, win_ref, gv_ref)
        vb = v.astype(BF16)
        r = lax.broadcasted_iota(jnp.int32, (CHUNK, CHUNK), 0)
        c = lax.broadcasted_iota(jnp.int32, (CHUNK, CHUNK), 1)
        for gi in range(SGU_GROUPS):
            w = jnp.where(r >= c, ws_ref[gi], 0.0).astype(BF16)
            bias = jnp.concatenate([bs_ref[gi]] * (SGU_GROUP_W // LANES), axis=1)
            lanes = slice(gi * SGU_GROUP_W, (gi + 1) * SGU_GROUP_W)
            for ci in range(ODD_TM // CHUNK):
                rows = slice(ci * CHUNK, (ci + 1) * CHUNK)
                mixed = jnp.dot(w, vb[rows, lanes], preferred_element_type=F32) + bias
                gated_ref[rows, lanes] = (u[rows, lanes] * mixed).astype(BF16)
        o_ref[...] = x + jnp.dot(gated_ref[...], wout_ref[...], preferred_element_type=F32)

    def sample_step(x_ref, o_ref):
        x = x_ref[...]
        u, v = _sgu_front(x, g_ref, win_ref, gv_ref)
        for t in range(DEC_SEQ):
            v_ref[:, t, :] = v[t * DEC_BATCH:(t + 1) * DEC_BATCH, :]
        gated = []
        for t in range(DEC_SEQ):
            mixed = sbias_ref[t:t + 1, :]
            for s in range(t + 1):
                mixed = mixed + coef_ref[t, s:s + 1, :] * v[s * DEC_BATCH:(s + 1) * DEC_BATCH, :]
            gated.append((u[t * DEC_BATCH:(t + 1) * DEC_BATCH, :] * mixed).astype(BF16))
        o_ref[...] = x + jnp.dot(
            jnp.concatenate(gated, axis=0), wout_ref[...], preferred_element_type=F32)

    prompt_rows = pl.BlockSpec((ODD_TM, D_MODEL), lambda i: (i, 0))
    pltpu.emit_pipeline(prompt_step, grid=(ODD_STEPS,), in_specs=[prompt_rows], out_specs=[prompt_rows])(
        x_hbm, o_hbm)
    sample_rows = pl.BlockSpec((SAMPLE_TOK, D_MODEL), lambda i: (PROMPT_TOK // SAMPLE_TOK, 0))
    pltpu.emit_pipeline(sample_step, grid=(1,), in_specs=[sample_rows], out_specs=[sample_rows])(
        x_hbm, o_hbm)


def _odd_mixer(x, gain, w_in, g_v, w_s, b_s, w_out):
    bias = jnp.broadcast_to(b_s[:, :, None], (SGU_GROUPS, CHUNK, LANES))
    coef = jnp.repeat(jnp.transpose(w_s[:, :DEC_SEQ, :DEC_SEQ], (1, 2, 0)), SGU_GROUP_W, axis=-1)
    sbias = jnp.repeat(b_s[:, :DEC_SEQ].T, SGU_GROUP_W, axis=-1)
    whole = pl.BlockSpec(memory_space=pltpu.VMEM)
    return pl.pallas_call(
        _odd_kernel,
        out_shape=(
            jax.ShapeDtypeStruct((N_TOK, D_MODEL), F32),
            jax.ShapeDtypeStruct((DEC_BATCH, DEC_SEQ, SGU_WIDTH), F32),
        ),
        in_specs=[pl.BlockSpec(memory_space=pl.ANY)] + [whole] * 8,
        out_specs=(pl.BlockSpec(memory_space=pl.ANY), whole),
        scratch_shapes=[pltpu.VMEM((ODD_TM, SGU_WIDTH), BF16)],
        input_output_aliases={0: 0},
        compiler_params=pltpu.CompilerParams(vmem_limit_bytes=ODD_VMEM),
        name="odd_mixer",
    )(x, gain.reshape(1, D_MODEL), w_in, g_v.reshape(1, SGU_WIDTH), w_s, bias, coef, sbias, w_out)


def _window_to_lanes(w):
    return jnp.transpose(w, (0, 2, 3, 1)).reshape(DEC_BATCH, KV_WIDTH, WINDOW)


def _window_from_lanes(w):
    return jnp.transpose(w.reshape(DEC_BATCH, N_KV_A, HEAD_DIM, WINDOW), (0, 3, 1, 2))


def kernel(x_prompt, x_sample, state_win_k, state_win_v, state_pool, rel_bias, norm_gains, final_gain,
           ffn_gate, ffn_up, ffn_down, w_in_even, w_out_even, attn_sinks, w_pool, pool_scale,
           w_in_odd, sgu_norm, w_spatial, b_spatial, w_out_odd):
    xp = x_prompt.reshape(PROMPT_TOK, D_MODEL)

    kp_l, vp_l, pp_l, ks_l, vs_l, ps_l, sv_l = [], [], [], [], [], [], []
    x = None
    y_prompt = y_sample = None
    for l in range(DEPTH):
        fa = (norm_gains[l, 0], ffn_gate, ffn_up, ffn_down, l, 0)
        fb = (norm_gains[l, 2], ffn_gate, ffn_up, ffn_down, l, 1)
        if l == 0:
            x = _ffn(xp, *fa, tail_in=x_sample)
        else:
            x = _ffn(x, *fa)
        g1 = norm_gains[l, 1]
        if l % 2 == 0:
            e = l // 2
            wq = w_in_even[e][:, :ATTN_WIDTH].reshape(D_MODEL, N_KV_A, GQA_GROUP, HEAD_DIM)
            wq = jnp.swapaxes(wq, 1, 2).reshape(D_MODEL, ATTN_WIDTH)
            wo = w_out_even[e][:ATTN_WIDTH].reshape(N_KV_A, GQA_GROUP, HEAD_DIM, D_MODEL)
            wo = jnp.swapaxes(wo, 0, 1).reshape(ATTN_WIDTH, D_MODEL)
            w_in = jnp.concatenate([wq, w_in_even[e][:, ATTN_WIDTH:]], axis=1).astype(BF16)
            w_out = jnp.concatenate([wo, w_out_even[e][ATTN_WIDTH:]], axis=0).astype(BF16)
            wp = w_pool[e].astype(BF16)
            x, kp, vp, pp = _even_prompt(x, g1, w_in, w_out, wp, pool_scale[e], rel_bias, attn_sinks[e])
            x, nk, nv, npool = _even_sample(
                x, g1, w_in, w_out, wp, pool_scale[e], rel_bias, attn_sinks[e],
                _window_to_lanes(state_win_k[e]), _window_to_lanes(state_win_v[e]),
                jnp.swapaxes(state_pool[e], 0, 1))
            kp_l.append(kp.reshape(BATCH, WINDOW, N_KV_A, HEAD_DIM))
            vp_l.append(vp.reshape(BATCH, WINDOW, N_KV_A, HEAD_DIM))
            pp_l.append(pp[:, 2 * SUBLANES - POOL_BUF:, :])
            ks_l.append(_window_from_lanes(nk))
            vs_l.append(_window_from_lanes(nv))
            ps_l.append(jnp.swapaxes(npool, 0, 1))
        else:
            o = l // 2
            w_in, w_out = w_in_odd[o].astype(BF16), w_out_odd[o].astype(BF16)
            x, sv_new = _odd_mixer(x, g1, w_in, sgu_norm[o], w_spatial[o], b_spatial[o], w_out)
            sv_l.append(sv_new)
        if l < DEPTH - 1:
            x = _ffn(x, *fb)
        else:
            y_prompt, y_sample = _ffn(x, *fb, final_gain=final_gain)

    return (y_prompt.reshape(BATCH, SEQ, D_MODEL), y_sample,
            jnp.stack(kp_l), jnp.stack(vp_l), jnp.stack(pp_l),
            jnp.stack(ks_l), jnp.stack(vs_l), jnp.stack(ps_l), jnp.stack(sv_l))
```
